```python
import math
import jax
import jax.numpy as jnp
from jax import lax
import numpy as np


D_MODEL = 2048
BATCH = 2
SEQ = 4096
DEPTH = 4

HEAD_DIM = 64
ROT_DIM = HEAD_DIM // 4
ROPE_THETA = 500000.0
Q_CHUNK = 128
NEG = -1e30
FORCE = 1e30
LN_EPS = 1e-5
SCALE = HEAD_DIM ** -0.5

MOBA_HEADS = 8
MOBA_BLOCK = 256
MOBA_TOPK = 3

NSA_HEADS = 12
NSA_KV_HEADS = 3
CMP_LEN = 32
CMP_STRIDE = 16
CMP_HIDDEN = 128
SLC_BLOCK = 64
SLC_TOPK = 16
SLC_LOCAL = 2
NSA_WINDOW = 512

DIL_CONFIGS = ((128, 1), (512, 4), (2048, 16))
DIL_HEADS_PER_GROUP = 4
DIL_HEADS = DIL_HEADS_PER_GROUP * len(DIL_CONFIGS)

N_EXPERTS = 16
N_GROUPS = 4
EXPERTS_PER_GROUP = N_EXPERTS // N_GROUPS
TOP_K = 2
EXPERT_HIDDEN = D_MODEL // 4
PLE_DIM = 256

DEEPNORM_ALPHA = (2 * DEPTH) ** 0.25
DEEPNORM_BETA = (8 * DEPTH) ** -0.25

MOBA_W = MOBA_HEADS * HEAD_DIM
NSA_QW = NSA_HEADS * HEAD_DIM
NSA_KVW = NSA_KV_HEADS * HEAD_DIM
DIL_W = DIL_HEADS * HEAD_DIM
IN_SPLITS = (MOBA_W, MOBA_W, MOBA_W, NSA_QW) + (NSA_KVW,) * 6 + (NSA_HEADS * 3, DIL_W, DIL_W, DIL_W)
IN_COLS = sum(IN_SPLITS)
SPLIT_POINTS = tuple(int(c) for c in np.cumsum(IN_SPLITS)[:-1])
MIX_OUT = MOBA_W + NSA_QW + DIL_HEADS_PER_GROUP * HEAD_DIM

kernel_name = 'hybrid_moba_nsa_dilated_grouped_moe_deepnorm'


def layer_norm(x, g, b):
    xf = x.astype(jnp.float32)
    mu = jnp.mean(xf, axis=-1, keepdims=True)
    var = jnp.mean(jnp.square(xf - mu), axis=-1, keepdims=True)
    y = (xf - mu) * lax.rsqrt(var + LN_EPS) * g.astype(jnp.float32) + b.astype(jnp.float32)
    return y.astype(x.dtype)


def split_heads(t):
    b, s, w = t.shape
    return t.reshape(b, s, w // HEAD_DIM, HEAD_DIM).transpose(0, 2, 1, 3)


def merge_heads(t):
    b, h, s, d = t.shape
    return t.transpose(0, 2, 1, 3).reshape(b, s, h * d)


def partial_rotary(t, positions):
    half = ROT_DIM // 2
    inv_freq = jnp.exp(jnp.arange(half, dtype=jnp.float32) * (-2.0 * math.log(ROPE_THETA) / ROT_DIM))
    ang = positions.astype(jnp.float32)[:, None, :, None] * inv_freq
    cos = jnp.cos(ang).astype(t.dtype)
    sin = jnp.sin(ang).astype(t.dtype)
    t1 = t[..., :half]
    t2 = t[..., half:ROT_DIM]
    return jnp.concatenate([t1 * cos - t2 * sin, t2 * cos + t1 * sin, t[..., ROT_DIM:]], axis=-1)


def banded_attention(q, k, v, n_back):
    b, hk, g, m, d = q.shape
    blk = min(Q_CHUNK, m)
    nq = -(-m // blk)
    mp = nq * blk
    nbk = -(-n_back // blk)
    qb = jnp.pad(q, ((0, 0), (0, 0), (0, 0), (0, mp - m), (0, 0))).reshape(b, hk, g, nq, blk, d)
    kv_pad = ((0, 0), (0, 0), (nbk * blk, mp - m), (0, 0))
    kp = jnp.pad(k, kv_pad).reshape(b, hk, nbk + nq, blk, d)
    vp = jnp.pad(v, kv_pad).reshape(b, hk, nbk + nq, blk, d)
    k_band = jnp.concatenate([kp[:, :, j:j + nq] for j in range(nbk + 1)], axis=3)
    v_band = jnp.concatenate([vp[:, :, j:j + nq] for j in range(nbk + 1)], axis=3)
    qpos = jnp.arange(nq)[:, None] * blk + jnp.arange(blk)[None, :]
    kpos = jnp.arange(nq)[:, None] * blk + jnp.arange((nbk + 1) * blk)[None, :] - nbk * blk
    diff = qpos[:, :, None] - kpos[:, None, :]
    ok = (diff >= 0) & (diff <= n_back) & (kpos[:, None, :] >= 0)
    sc = jnp.einsum('bkgnid,bknjd->bkgnij', qb, k_band).astype(jnp.float32) * SCALE
    sc = jnp.where(ok, sc, NEG)
    lse = jax.nn.logsumexp(sc, axis=-1)
    pr = jnp.exp(sc - lse[..., None]).astype(v.dtype)
    o = jnp.einsum('bkgnij,bknjd->bkgnid', pr, v_band)
    return o.reshape(b, hk, g, mp, d)[:, :, :, :m], lse.reshape(b, hk, g, mp)[:, :, :, :m]


def moba_attention(q, k, v):
    b, h, s, d = q.shape
    nb = -(-s // MOBA_BLOCK)
    pad = ((0, 0), (0, 0), (0, nb * MOBA_BLOCK - s), (0, 0))
    kb = jnp.pad(k, pad).reshape(b, h, nb, MOBA_BLOCK, d)
    vb = jnp.pad(v, pad).reshape(b, h, nb, MOBA_BLOCK, d)
    k_mean = jnp.mean(kb, axis=3)
    n_sel = min(MOBA_TOPK, nb - 1)
    b_ix = jnp.arange(b)[:, None, None, None]
    h_ix = jnp.arange(h)[None, :, None, None]
    blk_ids = jnp.arange(nb)

    def chunk(c):
        q0 = c * Q_CHUNK
        qc = lax.dynamic_slice_in_dim(q, q0, Q_CHUNK, axis=2)
        qpos = q0 + jnp.arange(Q_CHUNK)
        cur = q0 // MOBA_BLOCK
        k_own = lax.dynamic_index_in_dim(kb, cur, axis=2, keepdims=False)
        v_own = lax.dynamic_index_in_dim(vb, cur, axis=2, keepdims=False)
        kpos_own = cur * MOBA_BLOCK + jnp.arange(MOBA_BLOCK)
        s_own = jnp.einsum('bhqd,bhkd->bhqk', qc, k_own).astype(jnp.float32) * SCALE
        s_own = jnp.where(kpos_own[None, :] <= qpos[:, None], s_own, NEG)
        if n_sel == 0:
            pr = jax.nn.softmax(s_own, axis=-1).astype(v.dtype)
            return jnp.einsum('bhqk,bhkd->bhqd', pr, v_own)
        gate = jnp.einsum('bhqd,bhnd->bhqn', qc, k_mean).astype(jnp.float32)
        gate = jnp.where(blk_ids < cur, gate, NEG)
        _, idx = lax.top_k(gate, n_sel)
        sel_ok = idx < cur
        k_sel = kb[b_ix, h_ix, idx]
        v_sel = vb[b_ix, h_ix, idx]
        s_sel = jnp.einsum('bhqd,bhqnkd->bhqnk', qc, k_sel).astype(jnp.float32) * SCALE
        s_sel = jnp.where(sel_ok[..., None], s_sel, NEG).reshape(b, h, Q_CHUNK, n_sel * MOBA_BLOCK)
        pr = jax.nn.softmax(jnp.concatenate([s_sel, s_own], axis=-1), axis=-1).astype(v.dtype)
        p_sel = pr[..., :n_sel * MOBA_BLOCK].reshape(b, h, Q_CHUNK, n_sel, MOBA_BLOCK)
        p_own = pr[..., n_sel * MOBA_BLOCK:]
        return jnp.einsum('bhqnk,bhqnkd->bhqd', p_sel, v_sel) + jnp.einsum('bhqk,bhkd->bhqd', p_own, v_own)

    out = lax.map(chunk, jnp.arange(s // Q_CHUNK))
    return out.transpose(1, 2, 0, 3, 4).reshape(b, h, s, d)


def nsa_attention(q, k_c, v_c, k_s, v_s, k_w, v_w, gates, ck1, ck2, pe_k, cv1, cv2, pe_v):
    b, hq, s, d = q.shape
    hk = k_c.shape[1]
    g = hq // hk
    qg = q.reshape(b, hk, g, s, d)
    n_cmp = (s - CMP_LEN) // CMP_STRIDE + 1
    starts = np.arange(n_cmp) * CMP_STRIDE
    cidx = starts[:, None] + np.arange(CMP_LEN)[None, :]

    def compress(t, w1, w2, pe):
        blocks = (t[:, :, cidx] + pe).reshape(b, hk, n_cmp, CMP_LEN * d)
        return jax.nn.gelu(blocks @ w1) @ w2

    k_cmp = compress(k_c, ck1, ck2, pe_k)
    v_cmp = compress(v_c, cv1, cv2, pe_v)
    cmp_last = jnp.asarray(starts + CMP_LEN - 1)
    n_slc = s // SLC_BLOCK
    slc_starts = np.arange(n_slc) * SLC_BLOCK
    overlap = jnp.asarray(((starts[:, None] < slc_starts[None, :] + SLC_BLOCK)
                           & (starts[:, None] + CMP_LEN > slc_starts[None, :])).astype(np.float32))
    ks_blk = k_s.reshape(b, hk, n_slc, SLC_BLOCK, d)
    vs_blk = v_s.reshape(b, hk, n_slc, SLC_BLOCK, d)
    n_top = min(SLC_TOPK, n_slc)
    b_ix = jnp.arange(b)[:, None, None, None]
    k_ix = jnp.arange(hk)[None, :, None, None]
    blk_ids = jnp.arange(n_slc)

    def chunk(c):
        q0 = c * Q_CHUNK
        qc = lax.dynamic_slice_in_dim(qg, q0, Q_CHUNK, axis=3)
        qpos = q0 + jnp.arange(Q_CHUNK)
        ok = cmp_last[None, :] <= qpos[:, None]
        s_c = jnp.einsum('bkgqd,bknd->bkgqn', qc, k_cmp).astype(jnp.float32) * SCALE
        p_c = jnp.where(ok, jax.nn.softmax(jnp.where(ok, s_c, NEG), axis=-1), 0.0)
        o_cmp = jnp.einsum('bkgqn,bknd->bkgqd', p_c.astype(v_c.dtype), v_cmp)
        imp = jnp.einsum('bkgqn,nm->bkqm', p_c, overlap)
        cur = (qpos // SLC_BLOCK)[:, None]
        valid = blk_ids[None, :] <= cur
        forced = valid & ((blk_ids[None, :] == 0) | (blk_ids[None, :] > cur - SLC_LOCAL))
        imp = jnp.where(forced, FORCE, jnp.where(valid, imp, NEG))
        _, idx = lax.top_k(imp, n_top)
        k_sel = ks_blk[b_ix, k_ix, idx]
        v_sel = vs_blk[b_ix, k_ix, idx].reshape(b, hk, Q_CHUNK, n_top * SLC_BLOCK, d)
        kpos = idx[..., None] * SLC_BLOCK + jnp.arange(SLC_BLOCK)
        causal = (kpos <= qpos[:, None, None])[:, :, None]
        s_s = jnp.einsum('bkgqd,bkqnjd->bkgqnj', qc, k_sel).astype(jnp.float32) * SCALE
        s_s = jnp.where(causal, s_s, NEG).reshape(b, hk, g, Q_CHUNK, n_top * SLC_BLOCK)
        p_s = jax.nn.softmax(s_s, axis=-1).astype(v_s.dtype)
        o_slc = jnp.einsum('bkgqj,bkqjd->bkgqd', p_s, v_sel)
        return o_cmp, o_slc

    o_cmp, o_slc = lax.map(chunk, jnp.arange(s // Q_CHUNK))
    o_cmp = o_cmp.transpose(1, 2, 3, 0, 4, 5).reshape(b, hk, g, s, d)
    o_slc = o_slc.transpose(1, 2, 3, 0, 4, 5).reshape(b, hk, g, s, d)
    o_win, _ = banded_attention(qg, k_w, v_w, NSA_WINDOW - 1)
    gt = gates.reshape(b, hk, g, s, 3).astype(q.dtype)
    o = gt[..., 0:1] * o_cmp + gt[..., 1:2] * o_slc + gt[..., 2:3] * o_win
    return o.reshape(b, hq, s, d)


def dilated_attention(q, k, v):
    b, h, s, d = q.shape
    outs, lses = [], []
    for gi, (window, dil) in enumerate(DIL_CONFIGS):
        lo = gi * DIL_HEADS_PER_GROUP
        m = s // dil

        def to_sub(t):
            t = t[:, lo:lo + DIL_HEADS_PER_GROUP]
            t = t.reshape(b, DIL_HEADS_PER_GROUP, m, dil, d).transpose(0, 1, 3, 2, 4)
            return t.reshape(b, DIL_HEADS_PER_GROUP * dil, m, d)

        o, lse = banded_attention(to_sub(q)[:, :, None], to_sub(k), to_sub(v), window // dil)
        o = o[:, :, 0].reshape(b, DIL_HEADS_PER_GROUP, dil, m, d).transpose(0, 1, 3, 2, 4)
        lse = lse[:, :, 0].reshape(b, DIL_HEADS_PER_GROUP, dil, m).transpose(0, 1, 3, 2)
        outs.append(o.reshape(b, DIL_HEADS_PER_GROUP, s, d))
        lses.append(lse.reshape(b, DIL_HEADS_PER_GROUP, s))
    w = jax.nn.softmax(jnp.stack(lses), axis=0).astype(q.dtype)
    return jnp.sum(w[..., None] * jnp.stack(outs), axis=0)


def token_mixers(h, positions, w_in, ck1, ck2, pe_k, cv1, cv2, pe_v):
    b, s, _ = h.shape
    z = h @ w_in
    (qa, ka, va, qb, kbc, vbc, kbs, vbs, kbw, vbw, gb, qc, kc, vc) = jnp.split(z, SPLIT_POINTS, axis=-1)
    o_a = moba_attention(partial_rotary(split_heads(qa), positions),
                         partial_rotary(split_heads(ka), positions), split_heads(va))
    gates = jax.nn.sigmoid(gb.astype(jnp.float32)).reshape(b, s, NSA_HEADS, 3).transpose(0, 2, 1, 3)
    o_b = nsa_attention(partial_rotary(split_heads(qb), positions),
                        partial_rotary(split_heads(kbc), positions), split_heads(vbc),
                        partial_rotary(split_heads(kbs), positions), split_heads(vbs),
                        partial_rotary(split_heads(kbw), positions), split_heads(vbw),
                        gates, ck1, ck2, pe_k, cv1, cv2, pe_v)
    o_c = dilated_attention(partial_rotary(split_heads(qc), positions),
                            partial_rotary(split_heads(kc), positions), split_heads(vc))
    return jnp.concatenate([merge_heads(o_a), merge_heads(o_b), merge_heads(o_c)], axis=-1)


def grouped_moe(h, router_w, router_b, w_gate, w_up, w_down):
    b, s, dm = h.shape
    t = h.reshape(b * s, dm)
    logits = (t @ router_w).astype(jnp.float32) + router_b.astype(jnp.float32)
    probs = jax.nn.softmax(logits, axis=-1)
    grp_top = lax.top_k(probs.reshape(-1, N_GROUPS, EXPERTS_PER_GROUP), TOP_K)[0]
    g_sel = jnp.argmax(jnp.sum(grp_top, axis=-1), axis=-1)
    in_grp = (jnp.arange(N_EXPERTS) // EXPERTS_PER_GROUP)[None, :] == g_sel[:, None]
    w_top, idx = lax.top_k(jnp.where(in_grp, probs, -1.0), TOP_K)
    w_top = w_top / jnp.sum(w_top, axis=-1, keepdims=True)
    combine = jnp.sum(jax.nn.one_hot(idx, N_EXPERTS, dtype=jnp.float32) * w_top[..., None], axis=1)
    hid = jax.nn.silu(jnp.einsum('td,edf->etf', t, w_gate)) * jnp.einsum('td,edf->etf', t, w_up)
    hid = hid * combine.T[:, :, None].astype(hid.dtype)
    return jnp.einsum('etf,efd->td', hid, w_down).reshape(b, s, dm)


def setup_inputs(seed: int = 0) -> dict:
    key = jax.random.key(seed)
    ks = jax.random.split(key, 25)
    f32 = jnp.float32

    def nrm(k, shape, scale):
        return jax.random.normal(k, shape, f32) * scale

    return {
        'x': nrm(ks[0], (BATCH, SEQ, D_MODEL), 1.0),
        'p': nrm(ks[1], (DEPTH, BATCH, SEQ, PLE_DIM), 1.0),
        'positions': jnp.arange(SEQ, dtype=jnp.int32)[None, :] + jax.random.randint(ks[2], (BATCH, 1), 0, 1024, dtype=jnp.int32),
        'ln_in_g': 1.0 + nrm(ks[3], (D_MODEL,), 0.02),
        'ln_in_b': nrm(ks[4], (D_MODEL,), 0.02),
        'w_in': nrm(ks[5], (DEPTH, D_MODEL, IN_COLS), D_MODEL ** -0.5),
        'w_out': nrm(ks[6], (DEPTH, MIX_OUT, D_MODEL), MIX_OUT ** -0.5 * DEEPNORM_BETA),
        'nsa_ck1': nrm(ks[7], (DEPTH, CMP_LEN * HEAD_DIM, CMP_HIDDEN), (CMP_LEN * HEAD_DIM) ** -0.5),
        'nsa_ck2': nrm(ks[8], (DEPTH, CMP_HIDDEN, HEAD_DIM), CMP_HIDDEN ** -0.5),
        'nsa_pe_k': nrm(ks[9], (DEPTH, CMP_LEN, HEAD_DIM), 0.1),
        'nsa_cv1': nrm(ks[10], (DEPTH, CMP_LEN * HEAD_DIM, CMP_HIDDEN), (CMP_LEN * HEAD_DIM) ** -0.5),
        'nsa_cv2': nrm(ks[11], (DEPTH, CMP_HIDDEN, HEAD_DIM), CMP_HIDDEN ** -0.5),
        'nsa_pe_v': nrm(ks[12], (DEPTH, CMP_LEN, HEAD_DIM), 0.1),
        'ln1_g': 1.0 + nrm(ks[13], (DEPTH, D_MODEL), 0.02),
        'ln1_b': nrm(ks[14], (DEPTH, D_MODEL), 0.02),
        'router_w': nrm(ks[15], (D_MODEL, N_EXPERTS), D_MODEL ** -0.5),
        'router_b': nrm(ks[16], (N_EXPERTS,), 0.01),
        'w_gate': nrm(ks[17], (DEPTH, N_EXPERTS, D_MODEL, EXPERT_HIDDEN), D_MODEL ** -0.5),
        'w_up': nrm(ks[18], (DEPTH, N_EXPERTS, D_MODEL, EXPERT_HIDDEN), D_MODEL ** -0.5),
        'w_down': nrm(ks[19], (DEPTH, N_EXPERTS, EXPERT_HIDDEN, D_MODEL), EXPERT_HIDDEN ** -0.5 * DEEPNORM_BETA),
        'ple_proj': nrm(ks[20], (DEPTH, PLE_DIM, D_MODEL), PLE_DIM ** -0.5 * DEEPNORM_BETA),
        'ple_gate_w': nrm(ks[21], (DEPTH, D_MODEL, D_MODEL), D_MODEL ** -0.5),
        'ple_gate_b': nrm(ks[22], (DEPTH, D_MODEL), 0.02),
        'ln2_g': 1.0 + nrm(ks[23], (DEPTH, D_MODEL), 0.02),
        'ln2_b': nrm(ks[24], (DEPTH, D_MODEL), 0.02),
    }


def reference(x, p, positions, ln_in_g, ln_in_b, w_in, w_out, nsa_ck1, nsa_ck2, nsa_pe_k,
              nsa_cv1, nsa_cv2, nsa_pe_v, ln1_g, ln1_b, router_w, router_b, w_gate, w_up, w_down,
              ple_proj, ple_gate_w, ple_gate_b, ln2_g, ln2_b):
    h = layer_norm(x, ln_in_g, ln_in_b)
    for i in range(DEPTH):
        mixed = token_mixers(h, positions, w_in[i], nsa_ck1[i], nsa_ck2[i], nsa_pe_k[i],
                             nsa_cv1[i], nsa_cv2[i], nsa_pe_v[i])
        h = layer_norm(DEEPNORM_ALPHA * h + mixed @ w_out[i], ln1_g[i], ln1_b[i])
        ffn = grouped_moe(h, router_w, router_b, w_gate[i], w_up[i], w_down[i])
        ple = jax.nn.sigmoid(h @ ple_gate_w[i] + ple_gate_b[i]) * (p[i] @ ple_proj[i])
        h = layer_norm(DEEPNORM_ALPHA * h + ffn + ple, ln2_g[i], ln2_b[i])
    return h
```

```python
import functools
import math

import numpy as np
import jax
import jax.numpy as jnp
from jax import lax
from jax.experimental import pallas as pl
from jax.experimental.pallas import tpu as pltpu

F32 = jnp.float32
BF16 = jnp.bfloat16

D_MODEL = 2048
BATCH = 2
SEQ = 4096
DEPTH = 4
TOKENS = BATCH * SEQ
HEAD_DIM = 64
ROT_DIM = HEAD_DIM // 4
ROPE_THETA = 500000.0
NEG = -1e30
FORCE = 1e30
LN_EPS = 1e-5
SCALE = HEAD_DIM ** -0.5

MOBA_HEADS = 8
MOBA_BLOCK = 256
MOBA_TOPK = 3
MOBA_NB = SEQ // MOBA_BLOCK

NSA_HEADS = 12
NSA_KV_HEADS = 3
CMP_LEN = 32
CMP_STRIDE = 16
CMP_HIDDEN = 128
N_CMP = (SEQ - CMP_LEN) // CMP_STRIDE + 1
N_CMP_PAD = 256
SLC_BLOCK = 64
SLC_TOPK = 16
SLC_LOCAL = 2
N_SLC = SEQ // SLC_BLOCK
NSA_WINDOW = 512

DIL_CONFIGS = ((128, 1), (512, 4), (2048, 16))
DIL_HEADS_PER_GROUP = 4

N_EXPERTS = 16
N_GROUPS = 4
EXPERTS_PER_GROUP = 4
EXPERT_HIDDEN = D_MODEL // 4
PLE_DIM = 256

DEEPNORM_ALPHA = (2 * DEPTH) ** 0.25

LANES = 128
VMEM_LIMIT = 56 * 1024 * 1024

ROT_MQ, ROT_MK, ROT_NQ, ROT_NKC, ROT_NKS, ROT_NKW, ROT_DQ, ROT_DK = 0, 4, 8, 14, 17, 20, 23, 29
ROT_BLOCKS = 36
PL_MV, PL_NVC, PL_NVS, PL_NVW, PL_DV = 0, 4, 7, 10, 13
PL_BLOCKS = 20

NT_DIMS = (((1,), (1,)), ((), ()))


def _nt(a, b):
    return lax.dot_general(a, b, NT_DIMS, preferred_element_type=F32)


def _nn(a, b):
    return jnp.dot(a, b, preferred_element_type=F32)


def _params(*sem):
    return pltpu.CompilerParams(dimension_semantics=sem, vmem_limit_bytes=VMEM_LIMIT)


def _layer_norm(y, g, b):
    mu = jnp.mean(y, axis=-1, keepdims=True)
    yc = y - mu
    var = jnp.mean(yc * yc, axis=-1, keepdims=True)
    return yc * lax.rsqrt(var + LN_EPS) * g + b


def _ln_kernel(x_ref, g_ref, b_ref, h_ref, hb_ref):
    h = _layer_norm(x_ref[...], g_ref[...], b_ref[...])
    h_ref[...] = h
    hb_ref[...] = h.astype(BF16)


def _ln_in(x, g, b):
    tm = 512
    row = pl.BlockSpec((tm, D_MODEL), lambda i: (i, 0))
    vec = pl.BlockSpec((1, D_MODEL), lambda i: (0, 0))
    return pl.pallas_call(
        _ln_kernel,
        out_shape=(jax.ShapeDtypeStruct((TOKENS, D_MODEL), F32),
                   jax.ShapeDtypeStruct((TOKENS, D_MODEL), BF16)),
        grid=(TOKENS // tm,),
        in_specs=[row, vec, vec],
        out_specs=(row, row),
        compiler_params=_params("parallel"),
        name="ln_in",
    )(x, g.reshape(1, -1), b.reshape(1, -1))


def _proj_kernel(x_ref, w_ref, o_ref):
    o_ref[...] = _nn(x_ref[...], w_ref[...]).astype(o_ref.dtype)


def _proj_rot_kernel(x_ref, w_ref, c_ref, s1_ref, s2_ref, o_ref):
    z = _nn(x_ref[...], w_ref[...])
    c, s1, s2 = c_ref[...], s1_ref[...], s2_ref[...]
    half = ROT_DIM // 2
    for j in range(z.shape[1] // LANES):
        zc = z[:, j * LANES:(j + 1) * LANES]
        r = zc * c + pltpu.roll(zc, LANES - half, 1) * s1 + pltpu.roll(zc, half, 1) * s2
        o_ref[:, j * LANES:(j + 1) * LANES] = r.astype(o_ref.dtype)


def _project(hb, w, out_dtype, rope=None, tn=512):
    tm = 1024
    n = w.shape[1]
    x_spec = pl.BlockSpec((tm, D_MODEL), lambda i, j: (i, 0))
    w_spec = pl.BlockSpec((D_MODEL, tn), lambda i, j: (0, j))
    o_spec = pl.BlockSpec((tm, tn), lambda i, j: (i, j))
    if rope is None:
        kern, extra, extra_specs = _proj_kernel, (), []
    else:
        t_spec = pl.BlockSpec((tm, LANES), lambda i, j: (i, 0))
        kern, extra, extra_specs = _proj_rot_kernel, rope, [t_spec] * 3
    return pl.pallas_call(
        kern,
        out_shape=jax.ShapeDtypeStruct((TOKENS, n), out_dtype),
        grid=(TOKENS // tm, n // tn),
        in_specs=[x_spec, w_spec] + extra_specs,
        out_specs=o_spec,
        compiler_params=_params("parallel", "arbitrary"),
        name="in_proj_rot" if rope is not None else "in_proj",
    )(hb, w, *extra)


def _split_pair(q):
    lane = lax.broadcasted_iota(jnp.int32, q.shape, 1)
    zero = jnp.zeros_like(q)
    return jnp.where(lane < HEAD_DIM, q, zero), jnp.where(lane >= HEAD_DIM, q, zero)


def _merge_pair(lo, hi):
    lane = lax.broadcasted_iota(jnp.int32, lo.shape, 1)
    return jnp.where(lane < HEAD_DIM, lo, hi)


def _band_block(qs, kt, vt, q0, start, qc, n_back):
    rows, nk = qs.shape[0], kt.shape[0]
    s = _nt(qs, kt)
    row = lax.broadcasted_iota(jnp.int32, (rows, nk), 0)
    col = lax.broadcasted_iota(jnp.int32, (rows, nk), 1)
    diff = (q0 - start) + (row & (qc - 1)) - col
    s = jnp.where((diff >= 0) & (diff <= n_back), s, NEG)
    m = jnp.max(s, axis=-1, keepdims=True)
    p = jnp.exp(s - m)
    l = jnp.sum(p, axis=-1, keepdims=True)
    o = _nn(p.astype(BF16), vt) / l
    return o, m + jnp.log(l)


def _online_step(carry, s, vt):
    m, l, acc = carry
    m_new = jnp.maximum(m, jnp.max(s, axis=-1, keepdims=True))
    alpha = jnp.exp(m - m_new)
    p = jnp.exp(s - m_new)
    l = alpha * l + jnp.sum(p, axis=-1, keepdims=True)
    acc = alpha * acc + _nn(p.astype(BF16), vt)
    return m_new, l, acc


def _rank_rows(g, n_rows):
    blk = lax.broadcasted_iota(jnp.int32, g.shape, 0)
    rank = jnp.zeros(g.shape, F32)
    for m in range(n_rows):
        gm = g[m:m + 1, :]
        tie = jnp.where(blk > m, 1.0, 0.0)
        rank = rank + jnp.where(gm > g, 1.0, jnp.where(gm == g, tie, 0.0))
    return rank


def _rows_to_cols(sel_t):
    n, r = sel_t.shape
    padded = jnp.concatenate([sel_t, jnp.zeros((LANES - n, r), F32)], axis=0)
    return jnp.transpose(padded).astype(BF16)


MOBA_QC = 256


def _moba_kernel(q_ref, k_ref, v_ref, e_ref, o_ref, kmean_ref):
    c = pl.program_id(2)
    qc = MOBA_QC

    @pl.when(c == 0)
    def _():
        row = lax.broadcasted_iota(jnp.int32, (MOBA_NB, SEQ), 0)
        col = lax.broadcasted_iota(jnp.int32, (MOBA_NB, SEQ), 1)
        avg = jnp.where((col >> 8) == row, 1.0 / MOBA_BLOCK, 0.0).astype(BF16)
        kmean_ref[...] = _nn(avg, k_ref[0])

    q_lo, q_hi = _split_pair(q_ref[0])
    qs = jnp.concatenate([q_lo, q_hi], axis=0)
    rows = 2 * qc

    gate = _nt(kmean_ref[...].astype(BF16), qs)
    blk = lax.broadcasted_iota(jnp.int32, gate.shape, 0)
    past = blk < c
    gate = jnp.where(past, gate, NEG)
    rank = _rank_rows(gate, MOBA_NB)
    sel_t = jnp.where(past, jnp.where(rank < MOBA_TOPK, 1.0, 0.0), 0.0)
    sel = _rows_to_cols(sel_t)

    def body(t, carry):
        ks = pl.multiple_of(t * MOBA_BLOCK, MOBA_BLOCK)
        kt = k_ref[0, pl.ds(ks, MOBA_BLOCK), :]
        vt = v_ref[0, pl.ds(ks, MOBA_BLOCK), :]
        keep = _nn(sel, e_ref[t])
        s = jnp.where(keep > 0.5, _nt(qs, kt), NEG)
        return _online_step(carry, s, vt)

    init = (jnp.full((rows, 1), NEG, F32), jnp.zeros((rows, 1), F32), jnp.zeros((rows, LANES), F32))
    carry = lax.fori_loop(0, c, body, init)

    ks = pl.multiple_of(c * MOBA_BLOCK, MOBA_BLOCK)
    kt = k_ref[0, pl.ds(ks, MOBA_BLOCK), :]
    vt = v_ref[0, pl.ds(ks, MOBA_BLOCK), :]
    row = lax.broadcasted_iota(jnp.int32, (rows, MOBA_BLOCK), 0)
    col = lax.broadcasted_iota(jnp.int32, (rows, MOBA_BLOCK), 1)
    s = jnp.where(col <= (row & (qc - 1)), _nt(qs, kt), NEG)
    _, l, acc = _online_step(carry, s, vt)
    o = acc / l
    o_ref[0] = _merge_pair(o[:qc], o[qc:]).astype(o_ref.dtype)


def _moba(z_rot, z_pl, e3):
    qc = MOBA_QC
    grid = (BATCH, MOBA_HEADS // 2, SEQ // qc)
    return pl.pallas_call(
        _moba_kernel,
        out_shape=jax.ShapeDtypeStruct((BATCH, SEQ, MOBA_HEADS * HEAD_DIM), BF16),
        grid=grid,
        in_specs=[
            pl.BlockSpec((1, qc, LANES), lambda b, p, c: (b, c, ROT_MQ + p)),
            pl.BlockSpec((1, SEQ, LANES), lambda b, p, c: (b, 0, ROT_MK + p)),
            pl.BlockSpec((1, SEQ, LANES), lambda b, p, c: (b, 0, PL_MV + p)),
            pl.BlockSpec(e3.shape, lambda b, p, c: (0, 0, 0)),
        ],
        out_specs=pl.BlockSpec((1, qc, LANES), lambda b, p, c: (b, c, p)),
        scratch_shapes=[pltpu.VMEM((MOBA_NB, LANES), F32)],
        compiler_params=_params("parallel", "parallel", "arbitrary"),
        name="moba",
    )(z_rot, z_rot, z_pl, e3)


def _compress_one(x_ref, pe_ref, w1_ref, w2_ref, o_ref):
    x = x_ref[0, 0].astype(F32)
    top = (x + pe_ref[0:1, :]).astype(BF16)
    bot = (x + pe_ref[1:2, :]).astype(BF16)
    a = _nn(top, w1_ref[0])
    bm = _nn(bot, w1_ref[1])
    pre = a + pltpu.roll(bm, N_CMP_PAD - 1, 0)
    hid = jax.nn.gelu(pre)
    out = _nn(hid.astype(BF16), w2_ref[...])
    row = lax.broadcasted_iota(jnp.int32, out.shape, 0)
    o_ref[0, 0] = jnp.where(row < N_CMP, out, 0.0).astype(o_ref.dtype)


def _compress_kernel(xk_ref, xv_ref, pk_ref, pv_ref, k1_ref, k2_ref, v1_ref, v2_ref, ok_ref, ov_ref):
    _compress_one(xk_ref, pk_ref, k1_ref, k2_ref, ok_ref)
    _compress_one(xv_ref, pv_ref, v1_ref, v2_ref, ov_ref)


def _compress(xk, xv, pk, pv, k1, k2, v1, v2):
    chunk_w = CMP_STRIDE * HEAD_DIM
    x_spec = pl.BlockSpec((1, 1, N_CMP_PAD, chunk_w), lambda b, j: (b, j, 0, 0))
    pe_spec = pl.BlockSpec((2, chunk_w), lambda b, j: (0, 0))
    w1_spec = pl.BlockSpec((2, chunk_w, CMP_HIDDEN), lambda b, j: (0, 0, 0))
    w2_spec = pl.BlockSpec((CMP_HIDDEN, LANES), lambda b, j: (0, 0))
    o_spec = pl.BlockSpec((1, 1, N_CMP_PAD, LANES), lambda b, j: (b, j, 0, 0))
    o_shape = jax.ShapeDtypeStruct((BATCH, NSA_KV_HEADS, N_CMP_PAD, LANES), BF16)
    return pl.pallas_call(
        _compress_kernel,
        out_shape=(o_shape, o_shape),
        grid=(BATCH, NSA_KV_HEADS),
        in_specs=[x_spec, x_spec, pe_spec, pe_spec, w1_spec, w2_spec, w1_spec, w2_spec],
        out_specs=(o_spec, o_spec),
        compiler_params=_params("parallel", "parallel"),
        name="nsa_compress",
    )(xk, xv, pk, pv, k1, k2, v1, v2)


NSA_QC = 128
NSA_KT = 512
NSA_WIN_KEYS = NSA_WINDOW + NSA_QC


def _nsa_kernel(qa_ref, qb_ref, kc_ref, vc_ref, ks_ref, vs_ref, kw_ref, vw_ref, gl_ref, ovt_ref, e_ref,
                o_ref):
    c = pl.program_id(2)
    qc = NSA_QC
    q0 = c * qc
    qa_lo, qa_hi = _split_pair(qa_ref[0])
    qb_lo, qb_hi = _split_pair(qb_ref[0])
    qs = jnp.concatenate([qa_lo, qa_hi, qb_lo, qb_hi], axis=0)
    rows = 4 * qc

    s_c = _nt(qs, kc_ref[0, 0])
    row = lax.broadcasted_iota(jnp.int32, s_c.shape, 0)
    col = lax.broadcasted_iota(jnp.int32, s_c.shape, 1)
    ok = (col * CMP_STRIDE + (CMP_LEN - 1)) <= (q0 + (row & (qc - 1)))
    s_c = jnp.where(ok, s_c, NEG)
    m_c = jnp.max(s_c, axis=-1, keepdims=True)
    e_c = jnp.where(ok, jnp.exp(s_c - m_c), 0.0)
    l_c = jnp.sum(e_c, axis=-1, keepdims=True)
    p_c = e_c / jnp.where(l_c > 0.0, l_c, 1.0)
    p_cb = p_c.astype(BF16)
    o_cmp = _nn(p_cb, vc_ref[0, 0])

    imp4 = _nt(ovt_ref[...], p_cb)
    imp = imp4[:, 0:qc] + imp4[:, qc:2 * qc] + imp4[:, 2 * qc:3 * qc] + imp4[:, 3 * qc:4 * qc]
    blk = lax.broadcasted_iota(jnp.int32, imp.shape, 0)
    cur = (q0 + lax.broadcasted_iota(jnp.int32, imp.shape, 1)) >> 6
    valid = blk <= cur
    forced = valid & ((blk == 0) | (blk > cur - SLC_LOCAL))
    imp = jnp.where(forced, FORCE, jnp.where(valid, imp, NEG))
    rank = _rank_rows(imp, N_SLC)
    sel_t = jnp.where(valid, jnp.where(rank < SLC_TOPK, 1.0, 0.0), 0.0)
    sel = _rows_to_cols(sel_t)

    qpos = lax.broadcasted_iota(jnp.int32, (qc, NSA_KT), 0) + q0
    kcol = lax.broadcasted_iota(jnp.int32, (qc, NSA_KT), 1)

    def body(t, carry):
        ks0 = pl.multiple_of(t * NSA_KT, NSA_KT)
        kt = ks_ref[0, pl.ds(ks0, NSA_KT), :]
        vt = vs_ref[0, pl.ds(ks0, NSA_KT), :]
        keep = jnp.where(kcol + ks0 <= qpos, _nn(sel, e_ref[t]), 0.0)
        keep4 = jnp.concatenate([keep, keep, keep, keep], axis=0)
        s = jnp.where(keep4 > 0.5, _nt(qs, kt), NEG)
        return _online_step(carry, s, vt)

    init = (jnp.full((rows, 1), NEG, F32), jnp.zeros((rows, 1), F32), jnp.zeros((rows, LANES), F32))
    _, l_s, acc_s = lax.fori_loop(0, c // (NSA_KT // qc) + 1, body, init)
    o_slc = acc_s / l_s

    start = pl.multiple_of(jnp.maximum(q0 - NSA_WINDOW, 0), qc)
    o_win, _ = _band_block(qs, kw_ref[0, pl.ds(start, NSA_WIN_KEYS), :], vw_ref[0, pl.ds(start, NSA_WIN_KEYS), :],
                           q0, start, qc, NSA_WINDOW - 1)

    gate = jax.nn.sigmoid(gl_ref[...])
    outs = []
    for i in range(4):
        sl = slice(i * qc, (i + 1) * qc)
        outs.append(gate[:, 3 * i:3 * i + 1] * o_cmp[sl] + gate[:, 3 * i + 1:3 * i + 2] * o_slc[sl]
                    + gate[:, 3 * i + 2:3 * i + 3] * o_win[sl])
    o_ref[0, :, 0:LANES] = _merge_pair(outs[0], outs[1]).astype(o_ref.dtype)
    o_ref[0, :, LANES:2 * LANES] = _merge_pair(outs[2], outs[3]).astype(o_ref.dtype)


def _nsa(z_rot, z_pl, k_cmp, v_cmp, gate_logits, ovt, e3):
    qc = NSA_QC
    seq_spec = lambda base: pl.BlockSpec((1, SEQ, LANES), lambda b, j, c: (b, 0, base + j))
    cmp_spec = pl.BlockSpec((1, 1, N_CMP_PAD, LANES), lambda b, j, c: (b, j, 0, 0))
    return pl.pallas_call(
        _nsa_kernel,
        out_shape=jax.ShapeDtypeStruct((BATCH, SEQ, NSA_HEADS * HEAD_DIM), BF16),
        grid=(BATCH, NSA_KV_HEADS, SEQ // qc),
        in_specs=[
            pl.BlockSpec((1, qc, LANES), lambda b, j, c: (b, c, ROT_NQ + 2 * j)),
            pl.BlockSpec((1, qc, LANES), lambda b, j, c: (b, c, ROT_NQ + 2 * j + 1)),
            cmp_spec, cmp_spec,
            seq_spec(ROT_NKS), seq_spec(PL_NVS), seq_spec(ROT_NKW), seq_spec(PL_NVW),
            pl.BlockSpec((qc, LANES), lambda b, j, c: (b * (SEQ // qc) + c, j)),
            pl.BlockSpec(ovt.shape, lambda b, j, c: (0, 0)),
            pl.BlockSpec(e3.shape, lambda b, j, c: (0, 0, 0)),
        ],
        out_specs=pl.BlockSpec((1, qc, 2 * LANES), lambda b, j, c: (b, c, j)),
        compiler_params=_params("parallel", "parallel", "arbitrary"),
        name="nsa",
    )(z_rot, z_rot, k_cmp, v_cmp, z_rot, z_pl, z_rot, z_pl, gate_logits, ovt, e3)


DIL_QC = 128


def _dil_kernel(q_ref, k_ref, v_ref, o_ref, lse_ref, *, n_back, nk):
    c = pl.program_id(2)
    qc = DIL_QC
    q0 = c * qc
    q_lo, q_hi = _split_pair(q_ref[0])
    qs = jnp.concatenate([q_lo, q_hi], axis=0)
    start = pl.multiple_of(jnp.maximum(q0 - (nk - qc), 0), qc)
    o, lse = _band_block(qs, k_ref[0, pl.ds(start, nk), :], v_ref[0, pl.ds(start, nk), :], q0, start, qc, n_back)
    o_ref[0] = _merge_pair(o[:qc], o[qc:])
    lse_ref[0] = _merge_pair(jnp.broadcast_to(lse[:qc], (qc, LANES)), jnp.broadcast_to(lse[qc:], (qc, LANES)))


def _dilated_group(z_rot, z_pl, gi):
    window, dil = DIL_CONFIGS[gi]
    qc = DIL_QC
    m = SEQ // dil
    n_back = window // dil
    nk = min(m, qc + -(-n_back // qc) * qc)
    zr = z_rot.reshape(BATCH, m, dil * ROT_BLOCKS * LANES)
    zp = z_pl.reshape(BATCH, m, dil * PL_BLOCKS * LANES)
    col = lambda base, nblk: (lambda b, rp, c: (b, 0, (rp // 2) * nblk + base + 2 * gi + rp % 2))
    q_map = lambda b, rp, c: (b, c, (rp // 2) * ROT_BLOCKS + ROT_DQ + 2 * gi + rp % 2)
    out_spec = pl.BlockSpec((1, qc, LANES), lambda b, rp, c: (b, c, rp))
    shape = jax.ShapeDtypeStruct((BATCH, m, dil * 2 * LANES), F32)
    o, lse = pl.pallas_call(
        functools.partial(_dil_kernel, n_back=n_back, nk=nk),
        out_shape=(shape, shape),
        grid=(BATCH, dil * 2, m // qc),
        in_specs=[
            pl.BlockSpec((1, qc, LANES), q_map),
            pl.BlockSpec((1, m, LANES), col(ROT_DK, ROT_BLOCKS)),
            pl.BlockSpec((1, m, LANES), col(PL_DV, PL_BLOCKS)),
        ],
        out_specs=(out_spec, out_spec),
        compiler_params=_params("parallel", "parallel", "arbitrary"),
        name="dilated_%d" % dil,
    )(zr, zr, zp)
    width = DIL_HEADS_PER_GROUP * HEAD_DIM
    return o.reshape(TOKENS, width), lse.reshape(TOKENS, width)


def _dil_merge_kernel(o0_ref, o1_ref, o2_ref, l0_ref, l1_ref, l2_ref, out_ref):
    l0, l1, l2 = l0_ref[...], l1_ref[...], l2_ref[...]
    m = jnp.maximum(jnp.maximum(l0, l1), l2)
    e0, e1, e2 = jnp.exp(l0 - m), jnp.exp(l1 - m), jnp.exp(l2 - m)
    den = e0 + e1 + e2
    out = (e0 / den) * o0_ref[...] + (e1 / den) * o1_ref[...] + (e2 / den) * o2_ref[...]
    out_ref[...] = out.astype(out_ref.dtype)


def _dil_merge(outs, lses):
    tm = 1024
    width = DIL_HEADS_PER_GROUP * HEAD_DIM
    spec = pl.BlockSpec((tm, width), lambda i: (i, 0))
    return pl.pallas_call(
        _dil_merge_kernel,
        out_shape=jax.ShapeDtypeStruct((TOKENS, width), BF16),
        grid=(TOKENS // tm,),
        in_specs=[spec] * 6,
        out_specs=spec,
        compiler_params=_params("parallel"),
        name="dilated_merge",
    )(*outs, *lses)


def _out_proj_kernel(oa_ref, ob_ref, oc_ref, h_ref, wa_ref, wb_ref, wc_ref, g_ref, b_ref, h1_ref, h1b_ref):
    y = _nn(oa_ref[...], wa_ref[...]) + _nn(ob_ref[...], wb_ref[...]) + _nn(oc_ref[...], wc_ref[...])
    h1 = _layer_norm(DEEPNORM_ALPHA * h_ref[...] + y, g_ref[...], b_ref[...])
    h1_ref[...] = h1
    h1b_ref[...] = h1.astype(BF16)


def _out_proj(oa, ob, oc, h, wa, wb, wc, g, b):
    tm = 512
    rows = lambda w: pl.BlockSpec((tm, w), lambda i: (i, 0))
    full = lambda a: pl.BlockSpec(a.shape, lambda i: (0, 0))
    return pl.pallas_call(
        _out_proj_kernel,
        out_shape=(jax.ShapeDtypeStruct((TOKENS, D_MODEL), F32),
                   jax.ShapeDtypeStruct((TOKENS, D_MODEL), BF16)),
        grid=(TOKENS // tm,),
        in_specs=[rows(oa.shape[1]), rows(ob.shape[1]), rows(oc.shape[1]), rows(D_MODEL),
                  full(wa), full(wb), full(wc), full(g), full(b)],
        out_specs=(rows(D_MODEL), rows(D_MODEL)),
        compiler_params=_params("parallel"),
        name="out_proj_ln",
    )(oa, ob, oc, h, wa, wb, wc, g, b)


def _router_kernel(hb_ref, rw_ref, rb_ref, comb_ref):
    logits = _nt(rw_ref[...], hb_ref[...]) + rb_ref[...]
    mx = jnp.max(logits, axis=0, keepdims=True)
    ex = jnp.exp(logits - mx)
    probs = ex / jnp.sum(ex, axis=0, keepdims=True)
    p = [probs[e:e + 1, :] for e in range(N_EXPERTS)]
    best, g_sel = None, None
    for g in range(N_GROUPS):
        a, b, c, d = p[4 * g:4 * g + 4]
        hi1, lo1, hi2, lo2 = jnp.maximum(a, b), jnp.minimum(a, b), jnp.maximum(c, d), jnp.minimum(c, d)
        top2 = jnp.maximum(hi1, hi2) + jnp.maximum(jnp.minimum(hi1, hi2), jnp.maximum(lo1, lo2))
        if g == 0:
            best, g_sel = top2, jnp.zeros_like(top2)
        else:
            better = top2 > best
            best = jnp.where(better, top2, best)
            g_sel = jnp.where(better, float(g), g_sel)
    picked = []
    for e in range(N_EXPERTS):
        g = e // EXPERTS_PER_GROUP
        rank = jnp.zeros_like(best)
        for o in range(4 * g, 4 * g + 4):
            if o < e:
                rank = rank + jnp.where(p[o] >= p[e], 1.0, 0.0)
            elif o > e:
                rank = rank + jnp.where(p[o] > p[e], 1.0, 0.0)
        picked.append(jnp.where((g_sel == float(g)) & (rank < 2.0), p[e], 0.0))
    total = picked[0]
    for e in range(1, N_EXPERTS):
        total = total + picked[e]
    comb_ref[...] = jnp.concatenate(picked, axis=0) / total


def _router(hb, rw_t, rb):
    tm = 1024
    return pl.pallas_call(
        _router_kernel,
        out_shape=jax.ShapeDtypeStruct((N_EXPERTS, TOKENS), F32),
        grid=(TOKENS // tm,),
        in_specs=[pl.BlockSpec((tm, D_MODEL), lambda i: (i, 0)),
                  pl.BlockSpec((N_EXPERTS, D_MODEL), lambda i: (0, 0)),
                  pl.BlockSpec((N_EXPERTS, 1), lambda i: (0, 0))],
        out_specs=pl.BlockSpec((N_EXPERTS, tm), lambda i: (0, i)),
        compiler_params=_params("parallel"),
        name="router",
    )(hb, rw_t, rb)


def _moe_kernel(hb_ref, comb_ref, wg_ref, wu_ref, wd_ref, o_ref):
    e = pl.program_id(1)
    x = hb_ref[...]
    lane = lax.broadcasted_iota(jnp.int32, comb_ref.shape, 1)
    w_e = jnp.sum(jnp.where(lane == e, comb_ref[...], 0.0), axis=-1, keepdims=True)
    hid = jax.nn.silu(_nn(x, wg_ref[0])) * _nn(x, wu_ref[0]) * w_e
    y = _nn(hid.astype(BF16), wd_ref[0])

    @pl.when(e == 0)
    def _():
        o_ref[...] = y

    @pl.when(e > 0)
    def _():
        o_ref[...] += y


def _moe(hb, comb, wg, wu, wd):
    tm = 1024
    return pl.pallas_call(
        _moe_kernel,
        out_shape=jax.ShapeDtypeStruct((TOKENS, D_MODEL), F32),
        grid=(TOKENS // tm, N_EXPERTS),
        in_specs=[pl.BlockSpec((tm, D_MODEL), lambda i, e: (i, 0)),
                  pl.BlockSpec((tm, LANES), lambda i, e: (i, 0)),
                  pl.BlockSpec((1, D_MODEL, EXPERT_HIDDEN), lambda i, e: (e, 0, 0)),
                  pl.BlockSpec((1, D_MODEL, EXPERT_HIDDEN), lambda i, e: (e, 0, 0)),
                  pl.BlockSpec((1, EXPERT_HIDDEN, D_MODEL), lambda i, e: (e, 0, 0))],
        out_specs=pl.BlockSpec((tm, D_MODEL), lambda i, e: (i, 0)),
        compiler_params=_params("parallel", "arbitrary"),
        name="moe_experts",
    )(hb, comb, wg, wu, wd)


def _ple_ln_kernel(hb_ref, h_ref, ffn_ref, p_ref, gw_ref, gb_ref, pw_ref, g_ref, b_ref, h2_ref, h2b_ref):
    gate = jax.nn.sigmoid(_nn(hb_ref[...], gw_ref[...]) + gb_ref[...])
    ple = gate * _nn(p_ref[...].astype(BF16), pw_ref[...])
    h2 = _layer_norm(DEEPNORM_ALPHA * h_ref[...] + ffn_ref[...] + ple, g_ref[...], b_ref[...])
    h2_ref[...] = h2
    h2b_ref[...] = h2.astype(BF16)


def _ple_ln(hb, h, ffn, p, gw, gb, pw, g, b):
    tm = 256
    rows = lambda w: pl.BlockSpec((tm, w), lambda i: (i, 0))
    full = lambda a: pl.BlockSpec(a.shape, lambda i: (0, 0))
    return pl.pallas_call(
        _ple_ln_kernel,
        out_shape=(jax.ShapeDtypeStruct((TOKENS, D_MODEL), F32),
                   jax.ShapeDtypeStruct((TOKENS, D_MODEL), BF16)),
        grid=(TOKENS // tm,),
        in_specs=[rows(D_MODEL), rows(D_MODEL), rows(D_MODEL), rows(PLE_DIM),
                  full(gw), full(gb), full(pw), full(g), full(b)],
        out_specs=(rows(D_MODEL), rows(D_MODEL)),
        compiler_params=_params("parallel"),
        name="ple_ln",
    )(hb, h, ffn, p, gw, gb, pw, g, b)


def _rope_tables(positions):
    half = ROT_DIM // 2
    inv_freq = jnp.exp(jnp.arange(half, dtype=F32) * (-2.0 * math.log(ROPE_THETA) / ROT_DIM))
    ang = positions.astype(F32)[:, :, None] * inv_freq
    cos, sin = jnp.cos(ang), jnp.sin(ang)
    zeros = jnp.zeros_like(cos)
    rest = HEAD_DIM - ROT_DIM
    pad = lambda v: jnp.broadcast_to(jnp.asarray(v, F32), cos.shape[:2] + (rest,))
    c = jnp.concatenate([cos, cos, pad(1.0)], axis=-1)
    s1 = jnp.concatenate([-sin, zeros, pad(0.0)], axis=-1)
    s2 = jnp.concatenate([zeros, sin, pad(0.0)], axis=-1)
    tile = lambda t: jnp.concatenate([t, t], axis=-1).reshape(TOKENS, LANES)
    return tile(c), tile(s1), tile(s2)


def _split_w_in(w):
    mw, nq, nkv, dw = MOBA_HEADS * HEAD_DIM, NSA_HEADS * HEAD_DIM, NSA_KV_HEADS * HEAD_DIM, 12 * HEAD_DIM
    widths = (mw, mw, mw, nq) + (nkv,) * 6 + (NSA_HEADS * 3, dw, dw, dw)
    offs = np.concatenate([[0], np.cumsum(widths)])
    qa, ka, va, qb, kbc, vbc, kbs, vbs, kbw, vbw, gb, qc, kc, vc = (
        w[:, int(offs[i]):int(offs[i + 1])] for i in range(len(widths)))

    def dup(t):
        t = t.reshape(D_MODEL, NSA_KV_HEADS, 1, HEAD_DIM)
        return jnp.broadcast_to(t, (D_MODEL, NSA_KV_HEADS, 2, HEAD_DIM)).reshape(D_MODEL, NSA_KV_HEADS * LANES)

    zpad = jnp.zeros((D_MODEL, LANES), w.dtype)
    w_rot = jnp.concatenate([qa * SCALE, ka, qb * SCALE, dup(kbc), dup(kbs), dup(kbw), qc * SCALE, kc, zpad], axis=1)
    w_pl = jnp.concatenate([va, dup(vbc), dup(vbs), dup(vbw), vc, zpad], axis=1)
    gpad = jnp.zeros((D_MODEL, NSA_KV_HEADS, LANES - 12), w.dtype)
    w_gate = jnp.concatenate([gb.reshape(D_MODEL, NSA_KV_HEADS, 12), gpad], axis=-1).reshape(D_MODEL, -1)
    return w_rot.astype(BF16), w_pl.astype(BF16), w_gate.astype(BF16)


def _mask_tables():
    key = np.arange(SEQ)
    e_moba = np.zeros((MOBA_NB, LANES, MOBA_BLOCK), np.float32)
    for t in range(MOBA_NB):
        e_moba[t, t, :] = 1.0
    n_t = SEQ // NSA_KT
    e_nsa = np.zeros((n_t, LANES, NSA_KT), np.float32)
    for t in range(n_t):
        kk = key[t * NSA_KT:(t + 1) * NSA_KT]
        e_nsa[t, kk // SLC_BLOCK, np.arange(NSA_KT)] = 1.0
    starts = np.arange(N_CMP) * CMP_STRIDE
    slc = np.arange(N_SLC) * SLC_BLOCK
    ov = ((starts[:, None] < slc[None, :] + SLC_BLOCK) & (starts[:, None] + CMP_LEN > slc[None, :]))
    ovt = np.zeros((N_SLC, N_CMP_PAD), np.float32)
    ovt[:, :N_CMP] = ov.T
    return jnp.asarray(e_moba, BF16), jnp.asarray(e_nsa, BF16), jnp.asarray(ovt, BF16)


def _cmp_chunks(z, base):
    nblk = z.shape[-1] // LANES
    t = z.reshape(BATCH, SEQ // CMP_STRIDE, CMP_STRIDE, nblk, LANES)[:, :, :, base:base + NSA_KV_HEADS, :HEAD_DIM]
    return t.transpose(0, 3, 1, 2, 4).reshape(BATCH, NSA_KV_HEADS, SEQ // CMP_STRIDE, CMP_STRIDE * HEAD_DIM)


def kernel(x, p, positions, ln_in_g, ln_in_b, w_in, w_out, nsa_ck1, nsa_ck2, nsa_pe_k, nsa_cv1, nsa_cv2, nsa_pe_v, ln1_g, ln1_b, router_w, router_b, w_gate, w_up, w_down, ple_proj, ple_gate_w, ple_gate_b, ln2_g, ln2_b):
    rope = _rope_tables(positions)
    e_moba, e_nsa, ovt = _mask_tables()
    rw_t = router_w.T.astype(BF16)
    rb = router_b.reshape(N_EXPERTS, 1).astype(F32)
    chunk_w = CMP_STRIDE * HEAD_DIM
    vec = lambda v: v.reshape(1, -1)

    h, hb = _ln_in(x.reshape(TOKENS, D_MODEL), ln_in_g, ln_in_b)
    for i in range(DEPTH):
        w_rot, w_pl, w_gl = _split_w_in(w_in[i])
        z_rot = _project(hb, w_rot, BF16, rope=rope).reshape(BATCH, SEQ, ROT_BLOCKS * LANES)
        z_pl = _project(hb, w_pl, BF16).reshape(BATCH, SEQ, PL_BLOCKS * LANES)
        gate_logits = _project(hb, w_gl, F32, tn=NSA_KV_HEADS * LANES)

        o_a = _moba(z_rot, z_pl, e_moba)

        dup2 = lambda w2: jnp.concatenate([w2, w2], axis=1).astype(BF16)
        k_cmp, v_cmp = _compress(
            _cmp_chunks(z_rot, ROT_NKC), _cmp_chunks(z_pl, PL_NVC),
            nsa_pe_k[i].reshape(2, chunk_w), nsa_pe_v[i].reshape(2, chunk_w),
            nsa_ck1[i].reshape(2, chunk_w, CMP_HIDDEN).astype(BF16), dup2(nsa_ck2[i]),
            nsa_cv1[i].reshape(2, chunk_w, CMP_HIDDEN).astype(BF16), dup2(nsa_cv2[i]))
        o_b = _nsa(z_rot, z_pl, k_cmp, v_cmp, gate_logits, ovt, e_nsa)

        dil = [_dilated_group(z_rot, z_pl, gi) for gi in range(len(DIL_CONFIGS))]
        o_c = _dil_merge([d[0] for d in dil], [d[1] for d in dil])

        wo = w_out[i].astype(BF16)
        a_w, b_w = MOBA_HEADS * HEAD_DIM, NSA_HEADS * HEAD_DIM
        h, hb = _out_proj(o_a.reshape(TOKENS, a_w), o_b.reshape(TOKENS, b_w), o_c, h,
                          wo[:a_w], wo[a_w:a_w + b_w], wo[a_w + b_w:], vec(ln1_g[i]), vec(ln1_b[i]))

        comb_t = _router(hb, rw_t, rb)
        comb = jnp.pad(comb_t.T, ((0, 0), (0, LANES - N_EXPERTS)))
        ffn = _moe(hb, comb, w_gate[i].astype(BF16), w_up[i].astype(BF16), w_down[i].astype(BF16))
        h, hb = _ple_ln(hb, h, ffn, p[i].reshape(TOKENS, PLE_DIM), ple_gate_w[i].astype(BF16),
                        vec(ple_gate_b[i]), ple_proj[i].astype(BF16), vec(ln2_g[i]), vec(ln2_b[i]))
    return h.reshape(BATCH, SEQ, D_MODEL)
```

```python
import functools
import math

import numpy as np
import jax
import jax.numpy as jnp
from jax import lax
from jax.experimental import pallas as pl
from jax.experimental.pallas import tpu as pltpu

F32 = jnp.float32
BF16 = jnp.bfloat16

D_MODEL = 2048
BATCH = 2
SEQ = 4096
DEPTH = 4
TOKENS = BATCH * SEQ
HEAD_DIM = 64
ROT_DIM = HEAD_DIM // 4
ROPE_THETA = 500000.0
NEG = -1e30
FORCE = 1e30
LN_EPS = 1e-5
SCALE = HEAD_DIM ** -0.5

MOBA_HEADS = 8
MOBA_BLOCK = 256
MOBA_TOPK = 3
MOBA_NB = SEQ // MOBA_BLOCK

NSA_HEADS = 12
NSA_KV_HEADS = 3
CMP_LEN = 32
CMP_STRIDE = 16
CMP_HIDDEN = 128
N_CMP = (SEQ - CMP_LEN) // CMP_STRIDE + 1
N_CMP_PAD = 256
SLC_BLOCK = 64
SLC_TOPK = 16
SLC_LOCAL = 2
N_SLC = SEQ // SLC_BLOCK
NSA_WINDOW = 512

DIL_CONFIGS = ((128, 1), (512, 4), (2048, 16))
DIL_HEADS_PER_GROUP = 4
DIL_HEADS = DIL_HEADS_PER_GROUP * len(DIL_CONFIGS)

N_EXPERTS = 16
N_GROUPS = 4
EXPERTS_PER_GROUP = 4
EXPERT_HIDDEN = D_MODEL // 4
PLE_DIM = 256

DEEPNORM_ALPHA = (2 * DEPTH) ** 0.25

LANES = 128
VMEM_LIMIT = 56 * 1024 * 1024

ROT_MQ, ROT_MK, ROT_NQ, ROT_NKC, ROT_NKS, ROT_NKW = 0, 4, 8, 14, 17, 20
ROT_BLOCKS = 24
PL_MV, PL_NVC, PL_NVS, PL_NVW = 0, 4, 7, 10
PL_BLOCKS = 16
DIL_BLOCKS = DIL_HEADS // 2

NT_DIMS = (((1,), (1,)), ((), ()))


def _nt(a, b):
    return lax.dot_general(a, b, NT_DIMS, preferred_element_type=F32)


def _nn(a, b):
    return jnp.dot(a, b, preferred_element_type=F32)


def _params(*sem):
    return pltpu.CompilerParams(dimension_semantics=sem, vmem_limit_bytes=VMEM_LIMIT)


def _layer_norm(y, g, b):
    mu = jnp.mean(y, axis=-1, keepdims=True)
    yc = y - mu
    var = jnp.mean(yc * yc, axis=-1, keepdims=True)
    return yc * lax.rsqrt(var + LN_EPS) * g + b


def _ln_kernel(x_ref, g_ref, b_ref, h_ref, hb_ref):
    h = _layer_norm(x_ref[...], g_ref[...], b_ref[...])
    h_ref[...] = h
    hb_ref[...] = h.astype(BF16)


def _ln_in(x, g, b):
    tm = 512
    row = pl.BlockSpec((tm, D_MODEL), lambda i: (i, 0))
    vec = pl.BlockSpec((1, D_MODEL), lambda i: (0, 0))
    return pl.pallas_call(
        _ln_kernel,
        out_shape=(jax.ShapeDtypeStruct((TOKENS, D_MODEL), F32),
                   jax.ShapeDtypeStruct((TOKENS, D_MODEL), BF16)),
        grid=(TOKENS // tm,),
        in_specs=[row, vec, vec],
        out_specs=(row, row),
        compiler_params=_params("parallel"),
        name="ln_in",
    )(x, g.reshape(1, -1), b.reshape(1, -1))


def _proj_kernel(x_ref, w_ref, o_ref):
    o_ref[...] = _nn(x_ref[...], w_ref[...]).astype(o_ref.dtype)


def _proj_rot_kernel(x_ref, w_ref, c_ref, s1_ref, s2_ref, o_ref):
    z = _nn(x_ref[...], w_ref[...])
    c, s1, s2 = c_ref[...], s1_ref[...], s2_ref[...]
    half = ROT_DIM // 2
    for j in range(z.shape[1] // LANES):
        zc = z[:, j * LANES:(j + 1) * LANES]
        r = zc * c + pltpu.roll(zc, LANES - half, 1) * s1 + pltpu.roll(zc, half, 1) * s2
        o_ref[:, j * LANES:(j + 1) * LANES] = r.astype(o_ref.dtype)


def _project(hb, w, out_dtype, tn, rope=None):
    tm = 1024
    n = w.shape[1]
    x_spec = pl.BlockSpec((tm, D_MODEL), lambda i, j: (i, 0))
    w_spec = pl.BlockSpec((D_MODEL, tn), lambda i, j: (0, j))
    o_spec = pl.BlockSpec((tm, tn), lambda i, j: (i, j))
    if rope is None:
        kern, extra, extra_specs = _proj_kernel, (), []
    else:
        t_spec = pl.BlockSpec((tm, LANES), lambda i, j: (i, 0))
        kern, extra, extra_specs = _proj_rot_kernel, rope, [t_spec] * 3
    return pl.pallas_call(
        kern,
        out_shape=jax.ShapeDtypeStruct((TOKENS, n), out_dtype),
        grid=(TOKENS // tm, n // tn),
        in_specs=[x_spec, w_spec] + extra_specs,
        out_specs=o_spec,
        compiler_params=_params("parallel", "arbitrary"),
        name="in_proj_rot" if rope is not None else "in_proj",
    )(hb, w, *extra)


def _stack_heads(*q_blocks):
    parts = []
    for q in q_blocks:
        lane = lax.broadcasted_iota(jnp.int32, q.shape, 1)
        zero = jnp.zeros_like(q)
        parts += [jnp.where(lane < HEAD_DIM, q, zero), jnp.where(lane >= HEAD_DIM, q, zero)]
    return jnp.concatenate(parts, axis=0)


def _merge_pair_t(lo, hi):
    sub = lax.broadcasted_iota(jnp.int32, lo.shape, 0)
    return jnp.where(sub < HEAD_DIM, lo, hi)


def _band_bias_t(nk, qc, offset, n_back):
    key = lax.broadcasted_iota(jnp.int32, (nk, qc), 0)
    qry = lax.broadcasted_iota(jnp.int32, (nk, qc), 1)
    diff = offset + qry - key
    return jnp.where((diff >= 0) & (diff <= n_back), 0.0, NEG)


def _tile_lanes(x, n):
    return jnp.concatenate([x] * n, axis=1)


def _transpose_bf16(v):
    return jnp.transpose(v.astype(F32)).astype(BF16)


def _tree(x, op):
    n = x.shape[0]
    if n == 8:
        return x
    if n % 16 == 0:
        return op(_tree(x[:n // 2], op), _tree(x[n // 2:], op))
    acc = x[:8]
    for i in range(1, n // 8):
        acc = op(acc, x[8 * i:8 * i + 8])
    return acc


def _reduce_keys(x, op, final):
    return final(_tree(x, op), axis=0, keepdims=True)


def _softmax_block_t(s_t, pv):
    m = _reduce_keys(s_t, jnp.maximum, jnp.max)
    p = jnp.exp(s_t - m)
    l = _reduce_keys(p, jnp.add, jnp.sum)
    return pv(p.astype(BF16)) / l, m + jnp.log(l)


def _online_step_t(carry, s_t, m_t, pv):
    m, l, acc = carry
    m_new = jnp.maximum(m, m_t)
    alpha = jnp.exp(m - m_new)
    p = jnp.exp(s_t - m_new)
    l = alpha * l + _reduce_keys(p, jnp.add, jnp.sum)
    acc = alpha * acc + pv(p.astype(BF16))
    return m_new, l, acc


def _flash_tiles(n_tiles, last_tile, r, scores, pv_of):
    def produce(t):
        s_t = scores(t)
        return s_t, _reduce_keys(s_t, jnp.maximum, jnp.max)

    def body(t, carry):
        state, s_t, m_t = carry
        s_next, m_next = produce(jnp.minimum(t + 1, last_tile))
        return _online_step_t(state, s_t, m_t, pv_of(t)), s_next, m_next

    state, _, _ = lax.fori_loop(0, n_tiles, body, (_online_init_t(r),) + produce(0))
    return state


def _pv_tiles(vt_ref, first, n, rows):
    def pv(p):
        acc = _nn(vt_ref[first], p[:rows])
        for j in range(1, n):
            acc = acc + _nn(vt_ref[first + j], p[j * rows:(j + 1) * rows])
        return acc
    return pv


def _online_init_t(r):
    return (jnp.full((1, r), NEG, F32), jnp.zeros((1, r), F32), jnp.zeros((LANES, r), F32))


def _rank_rows(g, n_rows):
    blk = lax.broadcasted_iota(jnp.int32, g.shape, 0)
    rank = jnp.zeros(g.shape, F32)
    for m in range(n_rows):
        gm = g[m:m + 1, :]
        tie = jnp.where(blk > m, 1.0, 0.0)
        rank = rank + jnp.where(gm > g, 1.0, jnp.where(gm == g, tie, 0.0))
    return rank


MOBA_QC = 256
MOBA_KT = 2 * MOBA_BLOCK


def _moba_kernel(q_ref, k_ref, v_ref, o_ref, kmean_ref, vt_ref, bias_ref):
    c = pl.program_id(2)
    qc = MOBA_QC
    r = 2 * qc

    @pl.when(c == 0)
    def _():
        row = lax.broadcasted_iota(jnp.int32, (MOBA_NB, SEQ), 0)
        col = lax.broadcasted_iota(jnp.int32, (MOBA_NB, SEQ), 1)
        avg = jnp.where((col >> 8) == row, 1.0 / MOBA_BLOCK, 0.0).astype(BF16)
        kmean_ref[...] = _nn(avg, k_ref[0])
        for t in range(MOBA_NB):
            vt_ref[t] = _transpose_bf16(v_ref[0, t * MOBA_BLOCK:(t + 1) * MOBA_BLOCK, :])

    qs = _stack_heads(q_ref[0])

    gate = _nt(kmean_ref[...].astype(BF16), qs)
    blk = lax.broadcasted_iota(jnp.int32, gate.shape, 0)
    past = blk < c
    rank = _rank_rows(jnp.where(past, gate, NEG), MOBA_NB)
    bias_ref[...] = jnp.where(past & (rank < MOBA_TOPK), 0.0, NEG)

    per_tile = MOBA_KT // MOBA_BLOCK

    def scores(t):
        ks = pl.multiple_of(t * MOBA_KT, MOBA_KT)
        blocks = [jnp.broadcast_to(bias_ref[pl.ds(t * per_tile + j, 1), :], (MOBA_BLOCK, r))
                  for j in range(per_tile)]
        return _nt(k_ref[0, pl.ds(ks, MOBA_KT), :], qs) + jnp.concatenate(blocks, axis=0)

    state = _flash_tiles((c + per_tile - 1) // per_tile, SEQ // MOBA_KT - 1, r, scores,
                         lambda t: _pv_tiles(vt_ref, t * per_tile, per_tile, MOBA_BLOCK))

    ks = pl.multiple_of(c * MOBA_BLOCK, MOBA_BLOCK)
    causal = _band_bias_t(MOBA_BLOCK, qc, 0, MOBA_BLOCK)
    s_t = _nt(k_ref[0, pl.ds(ks, MOBA_BLOCK), :], qs) + _tile_lanes(causal, 2)
    _, l, acc = _online_step_t(state, s_t, _reduce_keys(s_t, jnp.maximum, jnp.max),
                               _pv_tiles(vt_ref, c, 1, MOBA_BLOCK))
    o_t = acc / l
    o_ref[0] = jnp.transpose(_merge_pair_t(o_t[:, :qc], o_t[:, qc:])).astype(o_ref.dtype)


def _moba(z_rot, z_pl):
    qc = MOBA_QC
    grid = (BATCH, MOBA_HEADS // 2, SEQ // qc)
    return pl.pallas_call(
        _moba_kernel,
        out_shape=jax.ShapeDtypeStruct((BATCH, SEQ, MOBA_HEADS * HEAD_DIM), BF16),
        grid=grid,
        in_specs=[
            pl.BlockSpec((1, qc, LANES), lambda b, p, c: (b, c, ROT_MQ + p)),
            pl.BlockSpec((1, SEQ, LANES), lambda b, p, c: (b, 0, ROT_MK + p)),
            pl.BlockSpec((1, SEQ, LANES), lambda b, p, c: (b, 0, PL_MV + p)),
        ],
        out_specs=pl.BlockSpec((1, qc, LANES), lambda b, p, c: (b, c, p)),
        scratch_shapes=[pltpu.VMEM((MOBA_NB, LANES), F32),
                        pltpu.VMEM((MOBA_NB, LANES, MOBA_BLOCK), BF16),
                        pltpu.VMEM((MOBA_NB, 2 * qc), F32)],
        compiler_params=_params("parallel", "parallel", "arbitrary"),
        name="moba",
    )(z_rot, z_rot, z_pl)


def _compress_one(x_ref, pe_ref, w1_ref, w2_ref, o_ref):
    x = x_ref[0, 0].astype(F32)
    top = (x + pe_ref[0:1, :]).astype(BF16)
    bot = (x + pe_ref[1:2, :]).astype(BF16)
    a = _nn(top, w1_ref[0])
    bm = _nn(bot, w1_ref[1])
    pre = a + pltpu.roll(bm, N_CMP_PAD - 1, 0)
    hid = jax.nn.gelu(pre)
    out = _nn(hid.astype(BF16), w2_ref[...])
    row = lax.broadcasted_iota(jnp.int32, out.shape, 0)
    o_ref[0, 0] = jnp.where(row < N_CMP, out, 0.0).astype(o_ref.dtype)


def _compress_kernel(xk_ref, xv_ref, pk_ref, pv_ref, k1_ref, k2_ref, v1_ref, v2_ref, ok_ref, ov_ref):
    _compress_one(xk_ref, pk_ref, k1_ref, k2_ref, ok_ref)
    _compress_one(xv_ref, pv_ref, v1_ref, v2_ref, ov_ref)


def _compress(xk, xv, pk, pv, k1, k2, v1, v2):
    chunk_w = CMP_STRIDE * HEAD_DIM
    x_spec = pl.BlockSpec((1, 1, N_CMP_PAD, chunk_w), lambda b, j: (b, j, 0, 0))
    pe_spec = pl.BlockSpec((2, chunk_w), lambda b, j: (0, 0))
    w1_spec = pl.BlockSpec((2, chunk_w, CMP_HIDDEN), lambda b, j: (0, 0, 0))
    w2_spec = pl.BlockSpec((CMP_HIDDEN, LANES), lambda b, j: (0, 0))
    o_spec = pl.BlockSpec((1, 1, N_CMP_PAD, LANES), lambda b, j: (b, j, 0, 0))
    o_shape = jax.ShapeDtypeStruct((BATCH, NSA_KV_HEADS, N_CMP_PAD, LANES), BF16)
    return pl.pallas_call(
        _compress_kernel,
        out_shape=(o_shape, o_shape),
        grid=(BATCH, NSA_KV_HEADS),
        in_specs=[x_spec, x_spec, pe_spec, pe_spec, w1_spec, w2_spec, w1_spec, w2_spec],
        out_specs=(o_spec, o_spec),
        compiler_params=_params("parallel", "parallel"),
        name="nsa_compress",
    )(xk, xv, pk, pv, k1, k2, v1, v2)


NSA_QC = 128
NSA_KT = 512
NSA_G = NSA_HEADS // NSA_KV_HEADS
NSA_WIN_TILES = NSA_WINDOW // NSA_QC + 1


def _nsa_kernel(qa_ref, qb_ref, kc_ref, vc_ref, ks_ref, vs_ref, kw_ref, vw_ref, gl_ref, ovt_ref,
                o_ref, vct_ref, vst_ref, vwt_ref, bias_ref):
    c = pl.program_id(2)
    qc = NSA_QC
    q0 = c * qc
    lanes_of = lambda t, i: t[:, i * qc:(i + 1) * qc]

    @pl.when(c == 0)
    def _():
        vct_ref[...] = _transpose_bf16(vc_ref[0, 0])
        for t in range(SEQ // NSA_KT):
            vst_ref[t] = _transpose_bf16(vs_ref[0, t * NSA_KT:(t + 1) * NSA_KT, :])
        for t in range(SEQ // qc):
            vwt_ref[t] = _transpose_bf16(vw_ref[0, t * qc:(t + 1) * qc, :])

    qs = _stack_heads(qa_ref[0], qb_ref[0])

    sc_t = _nt(kc_ref[0, 0], qs)
    n_idx = lax.broadcasted_iota(jnp.int32, (N_CMP_PAD, qc), 0)
    q_idx = lax.broadcasted_iota(jnp.int32, (N_CMP_PAD, qc), 1)
    ok = (n_idx * CMP_STRIDE + (CMP_LEN - 1)) <= (q0 + q_idx)
    p_heads = []
    for i in range(NSA_G):
        s_i = jnp.where(ok, lanes_of(sc_t, i), NEG)
        e_i = jnp.where(ok, jnp.exp(s_i - _reduce_keys(s_i, jnp.maximum, jnp.max)), 0.0)
        l_i = _reduce_keys(e_i, jnp.add, jnp.sum)
        p_heads.append((e_i / jnp.where(l_i > 0.0, l_i, 1.0)).astype(BF16))
    p_ct = jnp.concatenate(p_heads, axis=1)
    ocmp_t = _nn(vct_ref[...], p_ct)

    imp4 = _nn(ovt_ref[...], p_ct)
    imp = lanes_of(imp4, 0) + lanes_of(imp4, 1) + lanes_of(imp4, 2) + lanes_of(imp4, 3)
    blk = lax.broadcasted_iota(jnp.int32, imp.shape, 0)
    cur = (q0 + lax.broadcasted_iota(jnp.int32, imp.shape, 1)) >> 6
    valid = blk <= cur
    forced = valid & ((blk == 0) | (blk > cur - SLC_LOCAL))
    rank = _rank_rows(jnp.where(forced, FORCE, jnp.where(valid, imp, NEG)), N_SLC)
    bias_ref[...] = jnp.where(valid & (rank < SLC_TOPK), 0.0, NEG)

    key_row = lax.broadcasted_iota(jnp.int32, (NSA_KT, qc), 0)
    qpos = lax.broadcasted_iota(jnp.int32, (NSA_KT, qc), 1) + q0
    per_tile = NSA_KT // SLC_BLOCK

    def scores(t):
        ks0 = pl.multiple_of(t * NSA_KT, NSA_KT)
        blocks = [jnp.broadcast_to(bias_ref[pl.ds(t * per_tile + j, 1), :], (SLC_BLOCK, qc))
                  for j in range(per_tile)]
        bias = jnp.where(key_row + ks0 <= qpos, jnp.concatenate(blocks, axis=0), NEG)
        return _nt(ks_ref[0, pl.ds(ks0, NSA_KT), :], qs) + _tile_lanes(bias, NSA_G)

    _, l_s, acc_s = _flash_tiles(c // (NSA_KT // qc) + 1, SEQ // NSA_KT - 1, NSA_G * qc, scores,
                                 lambda t: _pv_tiles(vst_ref, t, 1, NSA_KT))
    oslc_t = acc_s / l_s

    t0 = jnp.maximum(c - NSA_WINDOW // qc, 0)
    start = pl.multiple_of(t0 * qc, qc)
    band = _band_bias_t(NSA_WIN_TILES * qc, qc, q0 - start, NSA_WINDOW - 1)
    sw_t = _nt(kw_ref[0, pl.ds(start, NSA_WIN_TILES * qc), :], qs) + _tile_lanes(band, NSA_G)

    owin_t, _ = _softmax_block_t(sw_t, _pv_tiles(vwt_ref, t0, NSA_WIN_TILES, qc))

    gate_t = jnp.transpose(jax.nn.sigmoid(gl_ref[...]))
    outs = []
    for i in range(NSA_G):
        outs.append(gate_t[3 * i:3 * i + 1, :] * lanes_of(ocmp_t, i)
                    + gate_t[3 * i + 1:3 * i + 2, :] * lanes_of(oslc_t, i)
                    + gate_t[3 * i + 2:3 * i + 3, :] * lanes_of(owin_t, i))
    o_ref[0, :, 0:LANES] = jnp.transpose(_merge_pair_t(outs[0], outs[1])).astype(o_ref.dtype)
    o_ref[0, :, LANES:2 * LANES] = jnp.transpose(_merge_pair_t(outs[2], outs[3])).astype(o_ref.dtype)


def _nsa(z_rot, z_pl, k_cmp, v_cmp, gate_logits, ovt):
    qc = NSA_QC
    seq_spec = lambda base: pl.BlockSpec((1, SEQ, LANES), lambda b, j, c: (b, 0, base + j))
    cmp_spec = pl.BlockSpec((1, 1, N_CMP_PAD, LANES), lambda b, j, c: (b, j, 0, 0))
    return pl.pallas_call(
        _nsa_kernel,
        out_shape=jax.ShapeDtypeStruct((BATCH, SEQ, NSA_HEADS * HEAD_DIM), BF16),
        grid=(BATCH, NSA_KV_HEADS, SEQ // qc),
        in_specs=[
            pl.BlockSpec((1, qc, LANES), lambda b, j, c: (b, c, ROT_NQ + 2 * j)),
            pl.BlockSpec((1, qc, LANES), lambda b, j, c: (b, c, ROT_NQ + 2 * j + 1)),
            cmp_spec, cmp_spec,
            seq_spec(ROT_NKS), seq_spec(PL_NVS), seq_spec(ROT_NKW), seq_spec(PL_NVW),
            pl.BlockSpec((qc, LANES), lambda b, j, c: (b * (SEQ // qc) + c, j)),
            pl.BlockSpec(ovt.shape, lambda b, j, c: (0, 0)),
        ],
        out_specs=pl.BlockSpec((1, qc, 2 * LANES), lambda b, j, c: (b, c, j)),
        scratch_shapes=[pltpu.VMEM((LANES, N_CMP_PAD), BF16),
                        pltpu.VMEM((SEQ // NSA_KT, LANES, NSA_KT), BF16),
                        pltpu.VMEM((SEQ // qc, LANES, qc), BF16),
                        pltpu.VMEM((N_SLC, qc), F32)],
        compiler_params=_params("parallel", "parallel", "arbitrary"),
        name="nsa",
    )(z_rot, z_rot, k_cmp, v_cmp, z_rot, z_pl, z_rot, z_pl, gate_logits, ovt)


DIL_QC = 128
DIL_STEPS = SEQ // DIL_QC


def _dil_group(q_ref, k_ref, v_ref, og_ref, lg_ref, gi):
    window, dil = DIL_CONFIGS[gi]
    qc = DIL_QC
    m = SEQ // dil
    n_back = window // dil
    nk = min(m, qc + -(-n_back // qc) * qc)
    chunks = m // qc

    def rows(first, n):
        return pl.ds(first, n) if dil == 1 else pl.ds(first, n, stride=dil)

    def body(idx, _):
        r = idx // chunks
        q0 = (idx % chunks) * qc
        start = jnp.maximum(q0 - (nk - qc), 0)
        q_rows = rows(r + dil * q0, qc)
        k_rows = rows(r + dil * start, nk)
        qs = _stack_heads(q_ref[0, q_rows, :].astype(BF16))
        v_t = _transpose_bf16(v_ref[0, k_rows, :])
        band = _band_bias_t(nk, qc, q0 - start, n_back)
        s_t = _nt(k_ref[0, k_rows, :].astype(BF16), qs) + _tile_lanes(band, 2)
        o_t, lse = _softmax_block_t(s_t, lambda p: _nn(v_t, p))
        lse_b = jnp.broadcast_to(lse, (LANES, 2 * qc))
        og_ref[gi, q_rows, :] = jnp.transpose(_merge_pair_t(o_t[:, :qc], o_t[:, qc:]))
        lg_ref[gi, q_rows, :] = jnp.transpose(_merge_pair_t(lse_b[:, :qc], lse_b[:, qc:]))
        return 0

    lax.fori_loop(0, DIL_STEPS, body, 0)


def _dil_kernel(q_ref, k_ref, v_ref, o_ref, og_ref, lg_ref):
    g = pl.program_id(2)
    n_groups = len(DIL_CONFIGS)
    for gi in range(n_groups):
        pl.when(g == gi)(functools.partial(_dil_group, q_ref, k_ref, v_ref, og_ref, lg_ref, gi))

    @pl.when(g == n_groups - 1)
    def _():
        rows = 512

        def body(i, _):
            sl = pl.ds(pl.multiple_of(i * rows, rows), rows)
            l0, l1, l2 = lg_ref[0, sl, :], lg_ref[1, sl, :], lg_ref[2, sl, :]
            mx = jnp.maximum(jnp.maximum(l0, l1), l2)
            e0, e1, e2 = jnp.exp(l0 - mx), jnp.exp(l1 - mx), jnp.exp(l2 - mx)
            den = e0 + e1 + e2
            out = (e0 / den) * og_ref[0, sl, :] + (e1 / den) * og_ref[1, sl, :] + (e2 / den) * og_ref[2, sl, :]
            o_ref[0, sl, :] = out.astype(o_ref.dtype)
            return 0

        lax.fori_loop(0, SEQ // rows, body, 0)


def _dilated(zd_rot, zd_pl):
    n_groups = len(DIL_CONFIGS)
    width = DIL_HEADS_PER_GROUP * HEAD_DIM
    col = lambda base: (lambda b, p, g: (b, 0, base + 2 * g + p))
    blk = lambda base: pl.BlockSpec((1, SEQ, LANES), col(base))
    return pl.pallas_call(
        _dil_kernel,
        out_shape=jax.ShapeDtypeStruct((BATCH, SEQ, width), BF16),
        grid=(BATCH, 2, n_groups),
        in_specs=[blk(0), blk(DIL_BLOCKS), blk(0)],
        out_specs=pl.BlockSpec((1, SEQ, LANES), lambda b, p, g: (b, 0, p)),
        scratch_shapes=[pltpu.VMEM((n_groups, SEQ, LANES), F32), pltpu.VMEM((n_groups, SEQ, LANES), F32)],
        compiler_params=_params("parallel", "parallel", "arbitrary"),
        name="dilated",
    )(zd_rot, zd_rot, zd_pl)


def _out_proj_kernel(oa_ref, ob_ref, oc_ref, h_ref, wa_ref, wb_ref, wc_ref, g_ref, b_ref, h1_ref, h1b_ref):
    y = _nn(oa_ref[...], wa_ref[...]) + _nn(ob_ref[...], wb_ref[...]) + _nn(oc_ref[...], wc_ref[...])
    h1 = _layer_norm(DEEPNORM_ALPHA * h_ref[...] + y, g_ref[...], b_ref[...])
    h1_ref[...] = h1
    h1b_ref[...] = h1.astype(BF16)


def _out_proj(oa, ob, oc, h, wa, wb, wc, g, b):
    tm = 512
    rows = lambda w: pl.BlockSpec((tm, w), lambda i: (i, 0))
    full = lambda a: pl.BlockSpec(a.shape, lambda i: (0, 0))
    return pl.pallas_call(
        _out_proj_kernel,
        out_shape=(jax.ShapeDtypeStruct((TOKENS, D_MODEL), F32),
                   jax.ShapeDtypeStruct((TOKENS, D_MODEL), BF16)),
        grid=(TOKENS // tm,),
        in_specs=[rows(oa.shape[1]), rows(ob.shape[1]), rows(oc.shape[1]), rows(D_MODEL),
                  full(wa), full(wb), full(wc), full(g), full(b)],
        out_specs=(rows(D_MODEL), rows(D_MODEL)),
        compiler_params=_params("parallel"),
        name="out_proj_ln",
    )(oa, ob, oc, h, wa, wb, wc, g, b)


def _router_kernel(hb_ref, rw_ref, rb_ref, comb_ref):
    logits = _nt(rw_ref[...], hb_ref[...]) + rb_ref[...]
    mx = jnp.max(logits, axis=0, keepdims=True)
    ex = jnp.exp(logits - mx)
    probs = ex / jnp.sum(ex, axis=0, keepdims=True)
    p = [probs[e:e + 1, :] for e in range(N_EXPERTS)]
    best, g_sel = None, None
    for g in range(N_GROUPS):
        a, b, c, d = p[4 * g:4 * g + 4]
        hi1, lo1, hi2, lo2 = jnp.maximum(a, b), jnp.minimum(a, b), jnp.maximum(c, d), jnp.minimum(c, d)
        top2 = jnp.maximum(hi1, hi2) + jnp.maximum(jnp.minimum(hi1, hi2), jnp.maximum(lo1, lo2))
        if g == 0:
            best, g_sel = top2, jnp.zeros_like(top2)
        else:
            better = top2 > best
            best = jnp.where(better, top2, best)
            g_sel = jnp.where(better, float(g), g_sel)
    picked = []
    for e in range(N_EXPERTS):
        g = e // EXPERTS_PER_GROUP
        rank = jnp.zeros_like(best)
        for o in range(4 * g, 4 * g + 4):
            if o < e:
                rank = rank + jnp.where(p[o] >= p[e], 1.0, 0.0)
            elif o > e:
                rank = rank + jnp.where(p[o] > p[e], 1.0, 0.0)
        picked.append(jnp.where((g_sel == float(g)) & (rank < 2.0), p[e], 0.0))
    total = picked[0]
    for e in range(1, N_EXPERTS):
        total = total + picked[e]
    comb_ref[...] = jnp.concatenate(picked, axis=0) / total


def _router(hb, rw_t, rb):
    tm = 1024
    return pl.pallas_call(
        _router_kernel,
        out_shape=jax.ShapeDtypeStruct((N_EXPERTS, TOKENS), F32),
        grid=(TOKENS // tm,),
        in_specs=[pl.BlockSpec((tm, D_MODEL), lambda i: (i, 0)),
                  pl.BlockSpec((N_EXPERTS, D_MODEL), lambda i: (0, 0)),
                  pl.BlockSpec((N_EXPERTS, 1), lambda i: (0, 0))],
        out_specs=pl.BlockSpec((N_EXPERTS, tm), lambda i: (0, i)),
        compiler_params=_params("parallel"),
        name="router",
    )(hb, rw_t, rb)


def _moe_kernel(hb_ref, comb_ref, wg_ref, wu_ref, wd_ref, o_ref):
    e = pl.program_id(1)
    x = hb_ref[...]
    lane = lax.broadcasted_iota(jnp.int32, comb_ref.shape, 1)
    w_e = jnp.sum(jnp.where(lane == e, comb_ref[...], 0.0), axis=-1, keepdims=True)
    hid = jax.nn.silu(_nn(x, wg_ref[0])) * _nn(x, wu_ref[0]) * w_e
    y = _nn(hid.astype(BF16), wd_ref[0])

    @pl.when(e == 0)
    def _():
        o_ref[...] = y

    @pl.when(e > 0)
    def _():
        o_ref[...] += y


def _moe(hb, comb, wg, wu, wd):
    tm = 1024
    return pl.pallas_call(
        _moe_kernel,
        out_shape=jax.ShapeDtypeStruct((TOKENS, D_MODEL), F32),
        grid=(TOKENS // tm, N_EXPERTS),
        in_specs=[pl.BlockSpec((tm, D_MODEL), lambda i, e: (i, 0)),
                  pl.BlockSpec((tm, LANES), lambda i, e: (i, 0)),
                  pl.BlockSpec((1, D_MODEL, EXPERT_HIDDEN), lambda i, e: (e, 0, 0)),
                  pl.BlockSpec((1, D_MODEL, EXPERT_HIDDEN), lambda i, e: (e, 0, 0)),
                  pl.BlockSpec((1, EXPERT_HIDDEN, D_MODEL), lambda i, e: (e, 0, 0))],
        out_specs=pl.BlockSpec((tm, D_MODEL), lambda i, e: (i, 0)),
        compiler_params=_params("parallel", "arbitrary"),
        name="moe_experts",
    )(hb, comb, wg, wu, wd)


def _ple_ln_kernel(hb_ref, h_ref, ffn_ref, p_ref, gw_ref, gb_ref, pw_ref, g_ref, b_ref, h2_ref, h2b_ref):
    gate = jax.nn.sigmoid(_nn(hb_ref[...], gw_ref[...]) + gb_ref[...])
    ple = gate * _nn(p_ref[...].astype(BF16), pw_ref[...])
    h2 = _layer_norm(DEEPNORM_ALPHA * h_ref[...] + ffn_ref[...] + ple, g_ref[...], b_ref[...])
    h2_ref[...] = h2
    h2b_ref[...] = h2.astype(BF16)


def _ple_ln(hb, h, ffn, p, gw, gb, pw, g, b):
    tm = 256
    rows = lambda w: pl.BlockSpec((tm, w), lambda i: (i, 0))
    full = lambda a: pl.BlockSpec(a.shape, lambda i: (0, 0))
    return pl.pallas_call(
        _ple_ln_kernel,
        out_shape=(jax.ShapeDtypeStruct((TOKENS, D_MODEL), F32),
                   jax.ShapeDtypeStruct((TOKENS, D_MODEL), BF16)),
        grid=(TOKENS // tm,),
        in_specs=[rows(D_MODEL), rows(D_MODEL), rows(D_MODEL), rows(PLE_DIM),
                  full(gw), full(gb), full(pw), full(g), full(b)],
        out_specs=(rows(D_MODEL), rows(D_MODEL)),
        compiler_params=_params("parallel"),
        name="ple_ln",
    )(hb, h, ffn, p, gw, gb, pw, g, b)


def _rope_tables(positions):
    half = ROT_DIM // 2
    inv_freq = jnp.exp(jnp.arange(half, dtype=F32) * (-2.0 * math.log(ROPE_THETA) / ROT_DIM))
    ang = positions.astype(F32)[:, :, None] * inv_freq
    cos, sin = jnp.cos(ang), jnp.sin(ang)
    zeros = jnp.zeros_like(cos)
    rest = HEAD_DIM - ROT_DIM
    pad = lambda v: jnp.broadcast_to(jnp.asarray(v, F32), cos.shape[:2] + (rest,))
    c = jnp.concatenate([cos, cos, pad(1.0)], axis=-1)
    s1 = jnp.concatenate([-sin, zeros, pad(0.0)], axis=-1)
    s2 = jnp.concatenate([zeros, sin, pad(0.0)], axis=-1)
    tile = lambda t: jnp.concatenate([t, t], axis=-1).reshape(TOKENS, LANES)
    return tile(c), tile(s1), tile(s2)


def _split_w_in(w):
    mw, nq, nkv, dw = MOBA_HEADS * HEAD_DIM, NSA_HEADS * HEAD_DIM, NSA_KV_HEADS * HEAD_DIM, DIL_HEADS * HEAD_DIM
    widths = (mw, mw, mw, nq) + (nkv,) * 6 + (NSA_HEADS * 3, dw, dw, dw)
    offs = np.concatenate([[0], np.cumsum(widths)])
    qa, ka, va, qb, kbc, vbc, kbs, vbs, kbw, vbw, gb, qc, kc, vc = (
        w[:, int(offs[i]):int(offs[i + 1])] for i in range(len(widths)))

    def dup(t):
        t = t.reshape(D_MODEL, NSA_KV_HEADS, 1, HEAD_DIM)
        return jnp.broadcast_to(t, (D_MODEL, NSA_KV_HEADS, 2, HEAD_DIM)).reshape(D_MODEL, NSA_KV_HEADS * LANES)

    zpad = lambda n: jnp.zeros((D_MODEL, n * LANES), w.dtype)
    w_rot = jnp.concatenate([qa * SCALE, ka, qb * SCALE, dup(kbc), dup(kbs), dup(kbw), zpad(1)], axis=1)
    w_pl = jnp.concatenate([va, dup(vbc), dup(vbs), dup(vbw), zpad(3)], axis=1)
    gpad = jnp.zeros((D_MODEL, NSA_KV_HEADS, LANES - 12), w.dtype)
    w_gl = jnp.concatenate([gb.reshape(D_MODEL, NSA_KV_HEADS, 12), gpad], axis=-1).reshape(D_MODEL, -1)
    w_dil_rot = jnp.concatenate([qc * SCALE, kc], axis=1)
    return tuple(t.astype(BF16) for t in (w_rot, w_pl, w_gl, w_dil_rot, vc))


def _overlap_table():
    starts = np.arange(N_CMP) * CMP_STRIDE
    slc = np.arange(N_SLC) * SLC_BLOCK
    ov = ((starts[:, None] < slc[None, :] + SLC_BLOCK) & (starts[:, None] + CMP_LEN > slc[None, :]))
    ovt = np.zeros((N_SLC, N_CMP_PAD), np.float32)
    ovt[:, :N_CMP] = ov.T
    return jnp.asarray(ovt, BF16)


def _cmp_chunks(z, base):
    nblk = z.shape[-1] // LANES
    t = z.reshape(BATCH, SEQ // CMP_STRIDE, CMP_STRIDE, nblk, LANES)[:, :, :, base:base + NSA_KV_HEADS, :HEAD_DIM]
    return t.transpose(0, 3, 1, 2, 4).reshape(BATCH, NSA_KV_HEADS, SEQ // CMP_STRIDE, CMP_STRIDE * HEAD_DIM)


def kernel(x, p, positions, ln_in_g, ln_in_b, w_in, w_out, nsa_ck1, nsa_ck2, nsa_pe_k, nsa_cv1, nsa_cv2, nsa_pe_v, ln1_g, ln1_b, router_w, router_b, w_gate, w_up, w_down, ple_proj, ple_gate_w, ple_gate_b, ln2_g, ln2_b):
    rope = _rope_tables(positions)
    ovt = _overlap_table()
    rw_t = router_w.T.astype(BF16)
    rb = router_b.reshape(N_EXPERTS, 1).astype(F32)
    chunk_w = CMP_STRIDE * HEAD_DIM
    vec = lambda v: v.reshape(1, -1)
    seq3 = lambda t: t.reshape(BATCH, SEQ, t.shape[-1])
    flat = lambda t: t.reshape(TOKENS, t.shape[-1])

    h, hb = _ln_in(x.reshape(TOKENS, D_MODEL), ln_in_g, ln_in_b)
    for i in range(DEPTH):
        w_rot, w_pl, w_gl, w_dil_rot, w_dil_pl = _split_w_in(w_in[i])
        z_rot = seq3(_project(hb, w_rot, BF16, 768, rope=rope))
        z_pl = seq3(_project(hb, w_pl, BF16, 1024))
        gate_logits = _project(hb, w_gl, F32, NSA_KV_HEADS * LANES)
        zd_rot = seq3(_project(hb, w_dil_rot, F32, 768, rope=rope))
        zd_pl = seq3(_project(hb, w_dil_pl, F32, 768))

        o_a = _moba(z_rot, z_pl)

        dup2 = lambda w2: jnp.concatenate([w2, w2], axis=1).astype(BF16)
        k_cmp, v_cmp = _compress(
            _cmp_chunks(z_rot, ROT_NKC), _cmp_chunks(z_pl, PL_NVC),
            nsa_pe_k[i].reshape(2, chunk_w), nsa_pe_v[i].reshape(2, chunk_w),
            nsa_ck1[i].reshape(2, chunk_w, CMP_HIDDEN).astype(BF16), dup2(nsa_ck2[i]),
            nsa_cv1[i].reshape(2, chunk_w, CMP_HIDDEN).astype(BF16), dup2(nsa_cv2[i]))
        o_b = _nsa(z_rot, z_pl, k_cmp, v_cmp, gate_logits, ovt)

        o_c = _dilated(zd_rot, zd_pl)

        wo = w_out[i].astype(BF16)
        a_w, b_w = MOBA_HEADS * HEAD_DIM, NSA_HEADS * HEAD_DIM
        h, hb = _out_proj(flat(o_a), flat(o_b), flat(o_c), h,
                          wo[:a_w], wo[a_w:a_w + b_w], wo[a_w + b_w:], vec(ln1_g[i]), vec(ln1_b[i]))

        comb_t = _router(hb, rw_t, rb)
        comb = jnp.pad(comb_t.T, ((0, 0), (0, LANES - N_EXPERTS)))
        ffn = _moe(hb, comb, w_gate[i].astype(BF16), w_up[i].astype(BF16), w_down[i].astype(BF16))
        h, hb = _ple_ln(hb, h, ffn, p[i].reshape(TOKENS, PLE_DIM), ple_gate_w[i].astype(BF16),
                        vec(ple_gate_b[i]), ple_proj[i].astype(BF16), vec(ln2_g[i]), vec(ln2_b[i]))
    return h.reshape(BATCH, SEQ, D_MODEL)
```

```python
import functools
import math

import numpy as np
import jax
import jax.numpy as jnp
from jax import lax
from jax.experimental import pallas as pl
from jax.experimental.pallas import tpu as pltpu

F32 = jnp.float32
BF16 = jnp.bfloat16

D_MODEL = 2048
BATCH = 2
SEQ = 4096
DEPTH = 4
TOKENS = BATCH * SEQ
HEAD_DIM = 64
ROT_DIM = HEAD_DIM // 4
ROPE_THETA = 500000.0
NEG = -1e30
FORCE = 1e30
LN_EPS = 1e-5
SCALE = HEAD_DIM ** -0.5

MOBA_HEADS = 8
MOBA_BLOCK = 256
MOBA_TOPK = 3
MOBA_NB = SEQ // MOBA_BLOCK

NSA_HEADS = 12
NSA_KV_HEADS = 3
CMP_LEN = 32
CMP_STRIDE = 16
CMP_HIDDEN = 128
N_CMP = (SEQ - CMP_LEN) // CMP_STRIDE + 1
N_CMP_PAD = 256
SLC_BLOCK = 64
SLC_TOPK = 16
SLC_LOCAL = 2
N_SLC = SEQ // SLC_BLOCK
NSA_WINDOW = 512

DIL_CONFIGS = ((128, 1), (512, 4), (2048, 16))
DIL_HEADS_PER_GROUP = 4
DIL_HEADS = DIL_HEADS_PER_GROUP * len(DIL_CONFIGS)

N_EXPERTS = 16
N_GROUPS = 4
EXPERTS_PER_GROUP = 4
EXPERT_HIDDEN = D_MODEL // 4
PLE_DIM = 256

DEEPNORM_ALPHA = (2 * DEPTH) ** 0.25

LANES = 128
VMEM_LIMIT = 56 * 1024 * 1024

ROT_MQ, ROT_MK, ROT_NQ, ROT_NKC, ROT_NKS, ROT_NKW = 0, 4, 8, 14, 17, 20
ROT_BLOCKS = 24
PL_MV, PL_NVC, PL_NVS, PL_NVW = 0, 4, 7, 10
PL_BLOCKS = 16
DIL_BLOCKS = DIL_HEADS // 2

NT_DIMS = (((1,), (1,)), ((), ()))


def _nt(a, b):
    return lax.dot_general(a, b, NT_DIMS, preferred_element_type=F32)


def _nn(a, b):
    return jnp.dot(a, b, preferred_element_type=F32)


def _params(*sem):
    return pltpu.CompilerParams(dimension_semantics=sem, vmem_limit_bytes=VMEM_LIMIT)


def _layer_norm(y, g, b):
    mu = jnp.mean(y, axis=-1, keepdims=True)
    yc = y - mu
    var = jnp.mean(yc * yc, axis=-1, keepdims=True)
    return yc * lax.rsqrt(var + LN_EPS) * g + b


def _ln_kernel(x_ref, g_ref, b_ref, h_ref, hb_ref):
    h = _layer_norm(x_ref[...], g_ref[...], b_ref[...])
    h_ref[...] = h
    hb_ref[...] = h.astype(BF16)


def _ln_in(x, g, b):
    tm = 512
    row = pl.BlockSpec((tm, D_MODEL), lambda i: (i, 0))
    vec = pl.BlockSpec((1, D_MODEL), lambda i: (0, 0))
    return pl.pallas_call(
        _ln_kernel,
        out_shape=(jax.ShapeDtypeStruct((TOKENS, D_MODEL), F32),
                   jax.ShapeDtypeStruct((TOKENS, D_MODEL), BF16)),
        grid=(TOKENS // tm,),
        in_specs=[row, vec, vec],
        out_specs=(row, row),
        compiler_params=_params("parallel"),
        name="ln_in",
    )(x, g.reshape(1, -1), b.reshape(1, -1))


def _proj_kernel(x_ref, w_ref, o_ref):
    o_ref[...] = _nn(x_ref[...], w_ref[...]).astype(o_ref.dtype)


def _proj_rot_kernel(x_ref, w_ref, c_ref, s1_ref, s2_ref, o_ref):
    z = _nn(x_ref[...], w_ref[...])
    c, s1, s2 = c_ref[...], s1_ref[...], s2_ref[...]
    half = ROT_DIM // 2
    for j in range(z.shape[1] // LANES):
        zc = z[:, j * LANES:(j + 1) * LANES]
        r = zc * c + pltpu.roll(zc, LANES - half, 1) * s1 + pltpu.roll(zc, half, 1) * s2
        o_ref[:, j * LANES:(j + 1) * LANES] = r.astype(o_ref.dtype)


def _project(hb, w, out_dtype, tn, rope=None):
    tm = 1024
    n = w.shape[1]
    x_spec = pl.BlockSpec((tm, D_MODEL), lambda i, j: (i, 0))
    w_spec = pl.BlockSpec((D_MODEL, tn), lambda i, j: (0, j))
    o_spec = pl.BlockSpec((tm, tn), lambda i, j: (i, j))
    if rope is None:
        kern, extra, extra_specs = _proj_kernel, (), []
    else:
        t_spec = pl.BlockSpec((tm, LANES), lambda i, j: (i, 0))
        kern, extra, extra_specs = _proj_rot_kernel, rope, [t_spec] * 3
    return pl.pallas_call(
        kern,
        out_shape=jax.ShapeDtypeStruct((TOKENS, n), out_dtype),
        grid=(TOKENS // tm, n // tn),
        in_specs=[x_spec, w_spec] + extra_specs,
        out_specs=o_spec,
        compiler_params=_params("parallel", "arbitrary"),
        name="in_proj_rot" if rope is not None else "in_proj",
    )(hb, w, *extra)


def _stack_heads(*q_blocks):
    parts = []
    for q in q_blocks:
        lane = lax.broadcasted_iota(jnp.int32, q.shape, 1)
        zero = jnp.zeros_like(q)
        parts += [jnp.where(lane < HEAD_DIM, q, zero), jnp.where(lane >= HEAD_DIM, q, zero)]
    return jnp.concatenate(parts, axis=0)


def _merge_pair_t(lo, hi):
    sub = lax.broadcasted_iota(jnp.int32, lo.shape, 0)
    return jnp.where(sub < HEAD_DIM, lo, hi)


def _band_bias_t(nk, qc, offset, n_back):
    key = lax.broadcasted_iota(jnp.int32, (nk, qc), 0)
    qry = lax.broadcasted_iota(jnp.int32, (nk, qc), 1)
    diff = offset + qry - key
    return jnp.where((diff >= 0) & (diff <= n_back), 0.0, NEG)


def _tile_lanes(x, n):
    return jnp.concatenate([x] * n, axis=1)


def _transpose_bf16(v):
    return jnp.transpose(v.astype(F32)).astype(BF16)


def _tree(x, op):
    n = x.shape[0]
    if n == 8:
        return x
    if n % 16 == 0:
        return op(_tree(x[:n // 2], op), _tree(x[n // 2:], op))
    acc = x[:8]
    for i in range(1, n // 8):
        acc = op(acc, x[8 * i:8 * i + 8])
    return acc


def _reduce_keys(x, op, final):
    return final(_tree(x, op), axis=0, keepdims=True)


def _softmax_block_t(s_t, pv):
    m = _reduce_keys(s_t, jnp.maximum, jnp.max)
    p = jnp.exp(s_t - m)
    l = _reduce_keys(p, jnp.add, jnp.sum)
    return pv(p.astype(BF16)) / l, m + jnp.log(l)


def _online_step_t(carry, s_t, m_t, pv):
    m, l, acc = carry
    m_new = jnp.maximum(m, m_t)
    alpha = jnp.exp(m - m_new)
    p = jnp.exp(s_t - m_new)
    l = alpha * l + _reduce_keys(p, jnp.add, jnp.sum)
    acc = alpha * acc + pv(p.astype(BF16))
    return m_new, l, acc


def _flash_tiles(n_tiles, last_tile, r, scores, pv_of):
    def produce(t):
        s_t = scores(t)
        return s_t, _reduce_keys(s_t, jnp.maximum, jnp.max)

    def body(t, carry):
        state, s_t, m_t = carry
        s_next, m_next = produce(jnp.minimum(t + 1, last_tile))
        return _online_step_t(state, s_t, m_t, pv_of(t)), s_next, m_next

    state, _, _ = lax.fori_loop(0, n_tiles, body, (_online_init_t(r),) + produce(0))
    return state


def _pv_tiles(vt_ref, first, n, rows):
    def pv(p):
        acc = _nn(vt_ref[first], p[:rows])
        for j in range(1, n):
            acc = acc + _nn(vt_ref[first + j], p[j * rows:(j + 1) * rows])
        return acc
    return pv


def _online_init_t(r):
    return (jnp.full((1, r), NEG, F32), jnp.zeros((1, r), F32), jnp.zeros((LANES, r), F32))


def _rank_rows(g, n_rows):
    sub = lax.broadcasted_iota(jnp.int32, (8, g.shape[1]), 0)
    rank = jnp.zeros(g.shape, F32)
    for m in range(n_rows):
        gm = g[m:m + 1, :]
        b = m // 8 * 8
        mid = g[b:b + 8]
        parts = [jnp.where(gm > mid, 1.0, jnp.where((gm == mid) & (sub > m - b), 1.0, 0.0))]
        if b > 0:
            parts.insert(0, jnp.where(gm > g[:b], 1.0, 0.0))
        if b + 8 < n_rows:
            parts.append(jnp.where(gm >= g[b + 8:], 1.0, 0.0))
        rank = rank + jnp.concatenate(parts, axis=0)
    return rank


MOBA_QC = 256
MOBA_KT = 2 * MOBA_BLOCK


def _moba_kernel(q_ref, k_ref, v_ref, o_ref, kmean_ref, vt_ref, bias_ref):
    c = pl.program_id(2)
    qc = MOBA_QC
    r = 2 * qc

    @pl.when(c == 0)
    def _():
        row = lax.broadcasted_iota(jnp.int32, (MOBA_NB, SEQ), 0)
        col = lax.broadcasted_iota(jnp.int32, (MOBA_NB, SEQ), 1)
        avg = jnp.where((col >> 8) == row, 1.0 / MOBA_BLOCK, 0.0).astype(BF16)
        kmean_ref[...] = _nn(avg, k_ref[0])
        for t in range(MOBA_NB):
            vt_ref[t] = _transpose_bf16(v_ref[0, t * MOBA_BLOCK:(t + 1) * MOBA_BLOCK, :])

    qs = _stack_heads(q_ref[0])

    gate = _nt(kmean_ref[...].astype(BF16), qs)
    blk = lax.broadcasted_iota(jnp.int32, gate.shape, 0)
    past = blk < c
    rank = _rank_rows(jnp.where(past, gate, NEG), MOBA_NB)
    bias_ref[...] = jnp.where(past & (rank < MOBA_TOPK), 0.0, NEG)

    per_tile = MOBA_KT // MOBA_BLOCK

    def scores(t):
        ks = pl.multiple_of(t * MOBA_KT, MOBA_KT)
        blocks = [jnp.broadcast_to(bias_ref[pl.ds(t * per_tile + j, 1), :], (MOBA_BLOCK, r))
                  for j in range(per_tile)]
        return _nt(k_ref[0, pl.ds(ks, MOBA_KT), :], qs) + jnp.concatenate(blocks, axis=0)

    state = _flash_tiles((c + per_tile - 1) // per_tile, SEQ // MOBA_KT - 1, r, scores,
                         lambda t: _pv_tiles(vt_ref, t * per_tile, per_tile, MOBA_BLOCK))

    ks = pl.multiple_of(c * MOBA_BLOCK, MOBA_BLOCK)
    causal = _band_bias_t(MOBA_BLOCK, qc, 0, MOBA_BLOCK)
    s_t = _nt(k_ref[0, pl.ds(ks, MOBA_BLOCK), :], qs) + _tile_lanes(causal, 2)
    _, l, acc = _online_step_t(state, s_t, _reduce_keys(s_t, jnp.maximum, jnp.max),
                               _pv_tiles(vt_ref, c, 1, MOBA_BLOCK))
    o_t = acc / l
    o_ref[0] = jnp.transpose(_merge_pair_t(o_t[:, :qc], o_t[:, qc:])).astype(o_ref.dtype)


def _moba(z_rot, z_pl):
    qc = MOBA_QC
    grid = (BATCH, MOBA_HEADS // 2, SEQ // qc)
    return pl.pallas_call(
        _moba_kernel,
        out_shape=jax.ShapeDtypeStruct((BATCH, SEQ, MOBA_HEADS * HEAD_DIM), BF16),
        grid=grid,
        in_specs=[
            pl.BlockSpec((1, qc, LANES), lambda b, p, c: (b, c, ROT_MQ + p)),
            pl.BlockSpec((1, SEQ, LANES), lambda b, p, c: (b, 0, ROT_MK + p)),
            pl.BlockSpec((1, SEQ, LANES), lambda b, p, c: (b, 0, PL_MV + p)),
        ],
        out_specs=pl.BlockSpec((1, qc, LANES), lambda b, p, c: (b, c, p)),
        scratch_shapes=[pltpu.VMEM((MOBA_NB, LANES), F32),
                        pltpu.VMEM((MOBA_NB, LANES, MOBA_BLOCK), BF16),
                        pltpu.VMEM((MOBA_NB, 2 * qc), F32)],
        compiler_params=_params("parallel", "parallel", "arbitrary"),
        name="moba",
    )(z_rot, z_rot, z_pl)


def _compress_one(x_ref, pe_ref, w1_ref, w2_ref, o_ref):
    x = x_ref[0, 0].astype(F32)
    top = (x + pe_ref[0:1, :]).astype(BF16)
    bot = (x + pe_ref[1:2, :]).astype(BF16)
    a = _nn(top, w1_ref[0])
    bm = _nn(bot, w1_ref[1])
    pre = a + pltpu.roll(bm, N_CMP_PAD - 1, 0)
    hid = jax.nn.gelu(pre)
    out = _nn(hid.astype(BF16), w2_ref[...])
    row = lax.broadcasted_iota(jnp.int32, out.shape, 0)
    o_ref[0, 0] = jnp.where(row < N_CMP, out, 0.0).astype(o_ref.dtype)


def _compress_kernel(xk_ref, xv_ref, pk_ref, pv_ref, k1_ref, k2_ref, v1_ref, v2_ref, ok_ref, ov_ref):
    _compress_one(xk_ref, pk_ref, k1_ref, k2_ref, ok_ref)
    _compress_one(xv_ref, pv_ref, v1_ref, v2_ref, ov_ref)


def _compress(xk, xv, pk, pv, k1, k2, v1, v2):
    chunk_w = CMP_STRIDE * HEAD_DIM
    x_spec = pl.BlockSpec((1, 1, N_CMP_PAD, chunk_w), lambda b, j: (b, j, 0, 0))
    pe_spec = pl.BlockSpec((2, chunk_w), lambda b, j: (0, 0))
    w1_spec = pl.BlockSpec((2, chunk_w, CMP_HIDDEN), lambda b, j: (0, 0, 0))
    w2_spec = pl.BlockSpec((CMP_HIDDEN, LANES), lambda b, j: (0, 0))
    o_spec = pl.BlockSpec((1, 1, N_CMP_PAD, LANES), lambda b, j: (b, j, 0, 0))
    o_shape = jax.ShapeDtypeStruct((BATCH, NSA_KV_HEADS, N_CMP_PAD, LANES), BF16)
    return pl.pallas_call(
        _compress_kernel,
        out_shape=(o_shape, o_shape),
        grid=(BATCH, NSA_KV_HEADS),
        in_specs=[x_spec, x_spec, pe_spec, pe_spec, w1_spec, w2_spec, w1_spec, w2_spec],
        out_specs=(o_spec, o_spec),
        compiler_params=_params("parallel", "parallel"),
        name="nsa_compress",
    )(xk, xv, pk, pv, k1, k2, v1, v2)


NSA_QC = 128
NSA_KT = 512
NSA_G = NSA_HEADS // NSA_KV_HEADS
NSA_WIN_TILES = NSA_WINDOW // NSA_QC + 1


def _nsa_kernel(qa_ref, qb_ref, kc_ref, vc_ref, ks_ref, vs_ref, kw_ref, vw_ref, gl_ref, ovt_ref,
                o_ref, vct_ref, vst_ref, vwt_ref, bias_ref):
    c = pl.program_id(2)
    qc = NSA_QC
    q0 = c * qc
    lanes_of = lambda t, i: t[:, i * qc:(i + 1) * qc]

    @pl.when(c == 0)
    def _():
        vct_ref[...] = _transpose_bf16(vc_ref[0, 0])
        for t in range(SEQ // NSA_KT):
            vst_ref[t] = _transpose_bf16(vs_ref[0, t * NSA_KT:(t + 1) * NSA_KT, :])
        for t in range(SEQ // qc):
            vwt_ref[t] = _transpose_bf16(vw_ref[0, t * qc:(t + 1) * qc, :])

    qs = _stack_heads(qa_ref[0], qb_ref[0])

    sc_t = _nt(kc_ref[0, 0], qs)
    n_idx = lax.broadcasted_iota(jnp.int32, (N_CMP_PAD, qc), 0)
    q_idx = lax.broadcasted_iota(jnp.int32, (N_CMP_PAD, qc), 1)
    ok = (n_idx * CMP_STRIDE + (CMP_LEN - 1)) <= (q0 + q_idx)
    p_heads = []
    for i in range(NSA_G):
        s_i = jnp.where(ok, lanes_of(sc_t, i), NEG)
        e_i = jnp.where(ok, jnp.exp(s_i - _reduce_keys(s_i, jnp.maximum, jnp.max)), 0.0)
        l_i = _reduce_keys(e_i, jnp.add, jnp.sum)
        p_heads.append((e_i / jnp.where(l_i > 0.0, l_i, 1.0)).astype(BF16))
    p_ct = jnp.concatenate(p_heads, axis=1)
    ocmp_t = _nn(vct_ref[...], p_ct)

    imp4 = _nn(ovt_ref[...], p_ct)
    imp = lanes_of(imp4, 0) + lanes_of(imp4, 1) + lanes_of(imp4, 2) + lanes_of(imp4, 3)
    blk = lax.broadcasted_iota(jnp.int32, imp.shape, 0)
    cur = (q0 + lax.broadcasted_iota(jnp.int32, imp.shape, 1)) >> 6
    valid = blk <= cur
    forced = valid & ((blk == 0) | (blk > cur - SLC_LOCAL))
    rank = _rank_rows(jnp.where(forced, FORCE, jnp.where(valid, imp, NEG)), N_SLC)
    bias_ref[...] = jnp.where(valid & (rank < SLC_TOPK), 0.0, NEG)

    key_row = lax.broadcasted_iota(jnp.int32, (NSA_KT, qc), 0)
    qpos = lax.broadcasted_iota(jnp.int32, (NSA_KT, qc), 1) + q0
    per_tile = NSA_KT // SLC_BLOCK

    def scores(t):
        ks0 = pl.multiple_of(t * NSA_KT, NSA_KT)
        blocks = [jnp.broadcast_to(bias_ref[pl.ds(t * per_tile + j, 1), :], (SLC_BLOCK, qc))
                  for j in range(per_tile)]
        bias = jnp.where(key_row + ks0 <= qpos, jnp.concatenate(blocks, axis=0), NEG)
        return _nt(ks_ref[0, pl.ds(ks0, NSA_KT), :], qs) + _tile_lanes(bias, NSA_G)

    _, l_s, acc_s = _flash_tiles(c // (NSA_KT // qc) + 1, SEQ // NSA_KT - 1, NSA_G * qc, scores,
                                 lambda t: _pv_tiles(vst_ref, t, 1, NSA_KT))
    oslc_t = acc_s / l_s

    t0 = jnp.maximum(c - NSA_WINDOW // qc, 0)
    start = pl.multiple_of(t0 * qc, qc)
    band = _band_bias_t(NSA_WIN_TILES * qc, qc, q0 - start, NSA_WINDOW - 1)
    sw_t = _nt(kw_ref[0, pl.ds(start, NSA_WIN_TILES * qc), :], qs) + _tile_lanes(band, NSA_G)

    owin_t, _ = _softmax_block_t(sw_t, _pv_tiles(vwt_ref, t0, NSA_WIN_TILES, qc))

    gate_t = jnp.transpose(jax.nn.sigmoid(gl_ref[...]))
    outs = []
    for i in range(NSA_G):
        outs.append(gate_t[3 * i:3 * i + 1, :] * lanes_of(ocmp_t, i)
                    + gate_t[3 * i + 1:3 * i + 2, :] * lanes_of(oslc_t, i)
                    + gate_t[3 * i + 2:3 * i + 3, :] * lanes_of(owin_t, i))
    o_ref[0, :, 0:LANES] = jnp.transpose(_merge_pair_t(outs[0], outs[1])).astype(o_ref.dtype)
    o_ref[0, :, LANES:2 * LANES] = jnp.transpose(_merge_pair_t(outs[2], outs[3])).astype(o_ref.dtype)


def _nsa(z_rot, z_pl, k_cmp, v_cmp, gate_logits, ovt):
    qc = NSA_QC
    seq_spec = lambda base: pl.BlockSpec((1, SEQ, LANES), lambda b, j, c: (b, 0, base + j))
    cmp_spec = pl.BlockSpec((1, 1, N_CMP_PAD, LANES), lambda b, j, c: (b, j, 0, 0))
    return pl.pallas_call(
        _nsa_kernel,
        out_shape=jax.ShapeDtypeStruct((BATCH, SEQ, NSA_HEADS * HEAD_DIM), BF16),
        grid=(BATCH, NSA_KV_HEADS, SEQ // qc),
        in_specs=[
            pl.BlockSpec((1, qc, LANES), lambda b, j, c: (b, c, ROT_NQ + 2 * j)),
            pl.BlockSpec((1, qc, LANES), lambda b, j, c: (b, c, ROT_NQ + 2 * j + 1)),
            cmp_spec, cmp_spec,
            seq_spec(ROT_NKS), seq_spec(PL_NVS), seq_spec(ROT_NKW), seq_spec(PL_NVW),
            pl.BlockSpec((qc, LANES), lambda b, j, c: (b * (SEQ // qc) + c, j)),
            pl.BlockSpec(ovt.shape, lambda b, j, c: (0, 0)),
        ],
        out_specs=pl.BlockSpec((1, qc, 2 * LANES), lambda b, j, c: (b, c, j)),
        scratch_shapes=[pltpu.VMEM((LANES, N_CMP_PAD), BF16),
                        pltpu.VMEM((SEQ // NSA_KT, LANES, NSA_KT), BF16),
                        pltpu.VMEM((SEQ // qc, LANES, qc), BF16),
                        pltpu.VMEM((N_SLC, qc), F32)],
        compiler_params=_params("parallel", "parallel", "arbitrary"),
        name="nsa",
    )(z_rot, z_rot, k_cmp, v_cmp, z_rot, z_pl, z_rot, z_pl, gate_logits, ovt)


DIL_QC = 128
DIL_STEPS = SEQ // DIL_QC


def _dil_group(q_ref, k_ref, v_ref, og_ref, lg_ref, gi):
    window, dil = DIL_CONFIGS[gi]
    qc = DIL_QC
    m = SEQ // dil
    n_back = window // dil
    nk = min(m, qc + -(-n_back // qc) * qc)
    chunks = m // qc

    def rows(first, n):
        return pl.ds(first, n) if dil == 1 else pl.ds(first, n, stride=dil)

    def body(idx, _):
        r = idx // chunks
        q0 = (idx % chunks) * qc
        start = jnp.maximum(q0 - (nk - qc), 0)
        q_rows = rows(r + dil * q0, qc)
        k_rows = rows(r + dil * start, nk)
        qs = _stack_heads(q_ref[0, q_rows, :].astype(BF16))
        v_t = _transpose_bf16(v_ref[0, k_rows, :])
        band = _band_bias_t(nk, qc, q0 - start, n_back)
        s_t = _nt(k_ref[0, k_rows, :].astype(BF16), qs) + _tile_lanes(band, 2)
        o_t, lse = _softmax_block_t(s_t, lambda p: _nn(v_t, p))
        lse_b = jnp.broadcast_to(lse, (LANES, 2 * qc))
        og_ref[gi, q_rows, :] = jnp.transpose(_merge_pair_t(o_t[:, :qc], o_t[:, qc:]))
        lg_ref[gi, q_rows, :] = jnp.transpose(_merge_pair_t(lse_b[:, :qc], lse_b[:, qc:]))
        return 0

    lax.fori_loop(0, DIL_STEPS, body, 0)


def _dil_kernel(q_ref, k_ref, v_ref, o_ref, og_ref, lg_ref):
    g = pl.program_id(2)
    n_groups = len(DIL_CONFIGS)
    for gi in range(n_groups):
        pl.when(g == gi)(functools.partial(_dil_group, q_ref, k_ref, v_ref, og_ref, lg_ref, gi))

    @pl.when(g == n_groups - 1)
    def _():
        rows = 512

        def body(i, _):
            sl = pl.ds(pl.multiple_of(i * rows, rows), rows)
            l0, l1, l2 = lg_ref[0, sl, :], lg_ref[1, sl, :], lg_ref[2, sl, :]
            mx = jnp.maximum(jnp.maximum(l0, l1), l2)
            e0, e1, e2 = jnp.exp(l0 - mx), jnp.exp(l1 - mx), jnp.exp(l2 - mx)
            den = e0 + e1 + e2
            out = (e0 / den) * og_ref[0, sl, :] + (e1 / den) * og_ref[1, sl, :] + (e2 / den) * og_ref[2, sl, :]
            o_ref[0, sl, :] = out.astype(o_ref.dtype)
            return 0

        lax.fori_loop(0, SEQ // rows, body, 0)


def _dilated(zd_rot, zd_pl):
    n_groups = len(DIL_CONFIGS)
    width = DIL_HEADS_PER_GROUP * HEAD_DIM
    col = lambda base: (lambda b, p, g: (b, 0, base + 2 * g + p))
    blk = lambda base: pl.BlockSpec((1, SEQ, LANES), col(base))
    return pl.pallas_call(
        _dil_kernel,
        out_shape=jax.ShapeDtypeStruct((BATCH, SEQ, width), BF16),
        grid=(BATCH, 2, n_groups),
        in_specs=[blk(0), blk(DIL_BLOCKS), blk(0)],
        out_specs=pl.BlockSpec((1, SEQ, LANES), lambda b, p, g: (b, 0, p)),
        scratch_shapes=[pltpu.VMEM((n_groups, SEQ, LANES), F32), pltpu.VMEM((n_groups, SEQ, LANES), F32)],
        compiler_params=_params("parallel", "parallel", "arbitrary"),
        name="dilated",
    )(zd_rot, zd_rot, zd_pl)


def _out_proj_kernel(oa_ref, ob_ref, oc_ref, h_ref, wa_ref, wb_ref, wc_ref, g_ref, b_ref, h1_ref, h1b_ref):
    y = _nn(oa_ref[...], wa_ref[...]) + _nn(ob_ref[...], wb_ref[...]) + _nn(oc_ref[...], wc_ref[...])
    h1 = _layer_norm(DEEPNORM_ALPHA * h_ref[...] + y, g_ref[...], b_ref[...])
    h1_ref[...] = h1
    h1b_ref[...] = h1.astype(BF16)


def _out_proj(oa, ob, oc, h, wa, wb, wc, g, b):
    tm = 512
    rows = lambda w: pl.BlockSpec((tm, w), lambda i: (i, 0))
    full = lambda a: pl.BlockSpec(a.shape, lambda i: (0, 0))
    return pl.pallas_call(
        _out_proj_kernel,
        out_shape=(jax.ShapeDtypeStruct((TOKENS, D_MODEL), F32),
                   jax.ShapeDtypeStruct((TOKENS, D_MODEL), BF16)),
        grid=(TOKENS // tm,),
        in_specs=[rows(oa.shape[1]), rows(ob.shape[1]), rows(oc.shape[1]), rows(D_MODEL),
                  full(wa), full(wb), full(wc), full(g), full(b)],
        out_specs=(rows(D_MODEL), rows(D_MODEL)),
        compiler_params=_params("parallel"),
        name="out_proj_ln",
    )(oa, ob, oc, h, wa, wb, wc, g, b)


def _router_kernel(hb_ref, rw_ref, rb_ref, comb_ref, sel_ref):
    logits = _nt(rw_ref[...], hb_ref[...]) + rb_ref[...]
    mx = jnp.max(logits, axis=0, keepdims=True)
    ex = jnp.exp(logits - mx)
    probs = ex / jnp.sum(ex, axis=0, keepdims=True)
    p = [probs[e:e + 1, :] for e in range(N_EXPERTS)]
    best, g_sel = None, None
    for g in range(N_GROUPS):
        a, b, c, d = p[4 * g:4 * g + 4]
        hi1, lo1, hi2, lo2 = jnp.maximum(a, b), jnp.minimum(a, b), jnp.maximum(c, d), jnp.minimum(c, d)
        top2 = jnp.maximum(hi1, hi2) + jnp.maximum(jnp.minimum(hi1, hi2), jnp.maximum(lo1, lo2))
        if g == 0:
            best, g_sel = top2, jnp.zeros_like(top2)
        else:
            better = top2 > best
            best = jnp.where(better, top2, best)
            g_sel = jnp.where(better, float(g), g_sel)
    chosen, picked = [], []
    for e in range(N_EXPERTS):
        g = e // EXPERTS_PER_GROUP
        rank = jnp.zeros_like(best)
        for o in range(4 * g, 4 * g + 4):
            if o < e:
                rank = rank + jnp.where(p[o] >= p[e], 1.0, 0.0)
            elif o > e:
                rank = rank + jnp.where(p[o] > p[e], 1.0, 0.0)
        chosen.append(jnp.where((g_sel == float(g)) & (rank < 2.0), 1.0, 0.0))
        picked.append(chosen[e] * p[e])
    total = picked[0]
    for e in range(1, N_EXPERTS):
        total = total + picked[e]
    comb_ref[...] = jnp.concatenate(picked, axis=0) / total
    sel_ref[...] = jnp.concatenate(chosen, axis=0)


def _router(hb, rw_t, rb):
    tm = 1024
    out = jax.ShapeDtypeStruct((N_EXPERTS, TOKENS), F32)
    o_spec = pl.BlockSpec((N_EXPERTS, tm), lambda i: (0, i))
    return pl.pallas_call(
        _router_kernel,
        out_shape=(out, out),
        grid=(TOKENS // tm,),
        in_specs=[pl.BlockSpec((tm, D_MODEL), lambda i: (i, 0)),
                  pl.BlockSpec((N_EXPERTS, D_MODEL), lambda i: (0, 0)),
                  pl.BlockSpec((N_EXPERTS, 1), lambda i: (0, 0))],
        out_specs=(o_spec, o_spec),
        compiler_params=_params("parallel"),
        name="router",
    )(hb, rw_t, rb)


def _routing_tables(comb_t, sel_t):
    sel = sel_t > 0.5
    cnt = jnp.sum(sel, axis=1, dtype=jnp.int32)
    cnt_pad = (cnt + (MOE_TILE - 1)) // MOE_TILE * MOE_TILE
    ends = jnp.cumsum(cnt_pad)
    rank = jnp.cumsum(sel.astype(jnp.int32), axis=1) - 1
    pos = (ends - cnt_pad)[:, None] + rank
    pos_lo = jnp.min(jnp.where(sel, pos, MOE_ROWS), axis=0)
    pos_hi = jnp.max(jnp.where(sel, pos, -1), axis=0)
    w_lo = jnp.sum(jnp.where(sel & (pos == pos_lo), comb_t, 0.0), axis=0)
    w_hi = jnp.sum(jnp.where(sel & (pos == pos_hi), comb_t, 0.0), axis=0)
    w = jnp.zeros((TOKENS, LANES), F32).at[:, 0].set(w_lo).at[:, 1].set(w_hi)
    n_tiles = ends[-1] // MOE_TILE
    tile_start = jnp.arange(MOE_TILES, dtype=jnp.int32) * MOE_TILE
    tile_start = jnp.minimum(tile_start, ends[-1] - MOE_TILE)
    tile_expert = jnp.sum((ends[None, :] <= tile_start[:, None]).astype(jnp.int32), axis=1)
    return jnp.stack([pos_lo, pos_hi]).astype(jnp.int32), w, tile_expert, n_tiles.reshape(1).astype(jnp.int32)


MOE_TILE = 256
MOE_TILES = 2 * TOKENS // MOE_TILE + N_EXPERTS
MOE_ROWS = MOE_TILES * MOE_TILE
SLAB = D_MODEL // LANES


def _to_slabs(ref, x, rows):
    for j in range(SLAB):
        ref[pl.ds(j, rows, stride=SLAB), :] = x[:, j * LANES:(j + 1) * LANES]


def _from_slabs(ref, rows):
    return jnp.concatenate([ref[pl.ds(j, rows, stride=SLAB), :] for j in range(SLAB)], axis=1)


def _slab_rows(row):
    return pl.ds(pl.multiple_of(row * SLAB, SLAB), SLAB)


def _dispatch_kernel(pos_ref, h_ref, init_ref, xs_ref, slab_ref, sem):
    del init_ref
    tm = h_ref.shape[0]
    base = pl.program_id(0) * tm
    _to_slabs(slab_ref, h_ref[...], tm)

    def copy(t, which):
        return pltpu.make_async_copy(slab_ref.at[_slab_rows(t), :],
                                     xs_ref.at[_slab_rows(pos_ref[which, base + t]), :], sem)

    def start(t, _):
        copy(t, 0).start()
        copy(t, 1).start()
        return 0

    def wait(t, _):
        copy(t, 0).wait()
        copy(t, 1).wait()
        return 0

    lax.fori_loop(0, tm, start, 0)
    lax.fori_loop(0, tm, wait, 0)


def _dispatch(pos, h):
    tm = 256
    grid_spec = pltpu.PrefetchScalarGridSpec(
        num_scalar_prefetch=1,
        grid=(TOKENS // tm,),
        in_specs=[pl.BlockSpec((tm, D_MODEL), lambda i, pos: (i, 0)),
                  pl.BlockSpec(memory_space=pl.ANY)],
        out_specs=pl.BlockSpec(memory_space=pl.ANY),
        scratch_shapes=[pltpu.VMEM((tm * SLAB, LANES), F32), pltpu.SemaphoreType.DMA],
    )
    return pl.pallas_call(
        _dispatch_kernel,
        out_shape=jax.ShapeDtypeStruct((MOE_ROWS * SLAB, LANES), F32),
        grid_spec=grid_spec,
        input_output_aliases={2: 0},
        compiler_params=_params("arbitrary"),
        name="moe_dispatch",
    )(pos, h, jnp.zeros((MOE_ROWS * SLAB, LANES), F32))


def _experts_kernel(te_ref, nt_ref, xs_ref, wg_ref, wu_ref, wd_ref, ys_ref, wgb_ref, wub_ref, wdb_ref):
    k = pl.program_id(0)
    e = te_ref[k]
    e_prev = te_ref[jnp.maximum(k - 1, 0)]

    @pl.when((k == 0) | (e != e_prev))
    def _():
        wgb_ref[...] = wg_ref[0].astype(BF16)
        wub_ref[...] = wu_ref[0].astype(BF16)
        wdb_ref[...] = wd_ref[0].astype(BF16)

    @pl.when(k < nt_ref[0])
    def _():
        x = _from_slabs(xs_ref, MOE_TILE).astype(BF16)
        hid = jax.nn.silu(_nn(x, wgb_ref[...])) * _nn(x, wub_ref[...])
        _to_slabs(ys_ref, _nn(hid.astype(BF16), wdb_ref[...]), MOE_TILE)

    @pl.when(k >= nt_ref[0])
    def _():
        ys_ref[...] = jnp.zeros(ys_ref.shape, F32)


def _experts(tile_expert, n_tiles, xs, wg, wu, wd):
    w_in_spec = pl.BlockSpec((1, D_MODEL, EXPERT_HIDDEN), lambda k, te, nt: (te[k], 0, 0))
    grid_spec = pltpu.PrefetchScalarGridSpec(
        num_scalar_prefetch=2,
        grid=(MOE_TILES,),
        in_specs=[pl.BlockSpec((MOE_TILE * SLAB, LANES), lambda k, te, nt: (jnp.minimum(k, nt[0] - 1), 0)),
                  w_in_spec, w_in_spec,
                  pl.BlockSpec((1, EXPERT_HIDDEN, D_MODEL), lambda k, te, nt: (te[k], 0, 0))],
        out_specs=pl.BlockSpec((MOE_TILE * SLAB, LANES), lambda k, te, nt: (k, 0)),
        scratch_shapes=[pltpu.VMEM((D_MODEL, EXPERT_HIDDEN), BF16), pltpu.VMEM((D_MODEL, EXPERT_HIDDEN), BF16),
                        pltpu.VMEM((EXPERT_HIDDEN, D_MODEL), BF16)],
    )
    return pl.pallas_call(
        _experts_kernel,
        out_shape=jax.ShapeDtypeStruct((MOE_ROWS * SLAB, LANES), F32),
        grid_spec=grid_spec,
        compiler_params=_params("arbitrary"),
        name="moe_experts",
    )(tile_expert, n_tiles, xs, wg, wu, wd)


def _ple_ln_kernel(pos_ref, hb_ref, h_ref, ys_ref, w_ref, p_ref, gw_ref, gb_ref, pw_ref, g_ref, b_ref,
                   h2_ref, h2b_ref, lo_ref, hi_ref, sem):
    tm = h_ref.shape[0]
    base = pl.program_id(0) * tm
    bufs = (lo_ref, hi_ref)

    def copy(t, which):
        return pltpu.make_async_copy(ys_ref.at[_slab_rows(pos_ref[which, base + t]), :],
                                     bufs[which].at[_slab_rows(t), :], sem)

    def start(t, _):
        copy(t, 0).start()
        copy(t, 1).start()
        return 0

    def wait(t, _):
        copy(t, 0).wait()
        copy(t, 1).wait()
        return 0

    lax.fori_loop(0, tm, start, 0)
    gate = jax.nn.sigmoid(_nn(hb_ref[...], gw_ref[...]) + gb_ref[...])
    ple = gate * _nn(p_ref[...].astype(BF16), pw_ref[...])
    lax.fori_loop(0, tm, wait, 0)
    w = w_ref[...]
    ffn = w[:, 0:1] * _from_slabs(lo_ref, tm) + w[:, 1:2] * _from_slabs(hi_ref, tm)
    h2 = _layer_norm(DEEPNORM_ALPHA * h_ref[...] + ffn + ple, g_ref[...], b_ref[...])
    h2_ref[...] = h2
    h2b_ref[...] = h2.astype(BF16)


def _ple_ln(pos, hb, h, ys, w, p, gw, gb, pw, g, b):
    tm = 256
    rows = lambda width: pl.BlockSpec((tm, width), lambda i, pos: (i, 0))
    full = lambda a: pl.BlockSpec(a.shape, lambda i, pos: (0, 0))
    grid_spec = pltpu.PrefetchScalarGridSpec(
        num_scalar_prefetch=1,
        grid=(TOKENS // tm,),
        in_specs=[rows(D_MODEL), rows(D_MODEL), pl.BlockSpec(memory_space=pl.ANY), rows(LANES), rows(PLE_DIM),
                  full(gw), full(gb), full(pw), full(g), full(b)],
        out_specs=(rows(D_MODEL), rows(D_MODEL)),
        scratch_shapes=[pltpu.VMEM((tm * SLAB, LANES), F32), pltpu.VMEM((tm * SLAB, LANES), F32),
                        pltpu.SemaphoreType.DMA],
    )
    return pl.pallas_call(
        _ple_ln_kernel,
        out_shape=(jax.ShapeDtypeStruct((TOKENS, D_MODEL), F32),
                   jax.ShapeDtypeStruct((TOKENS, D_MODEL), BF16)),
        grid_spec=grid_spec,
        compiler_params=_params("arbitrary"),
        name="ple_ln",
    )(pos, hb, h, ys, w, p, gw, gb, pw, g, b)


def _rope_tables(positions):
    half = ROT_DIM // 2
    inv_freq = jnp.exp(jnp.arange(half, dtype=F32) * (-2.0 * math.log(ROPE_THETA) / ROT_DIM))
    ang = positions.astype(F32)[:, :, None] * inv_freq
    cos, sin = jnp.cos(ang), jnp.sin(ang)
    zeros = jnp.zeros_like(cos)
    rest = HEAD_DIM - ROT_DIM
    pad = lambda v: jnp.broadcast_to(jnp.asarray(v, F32), cos.shape[:2] + (rest,))
    c = jnp.concatenate([cos, cos, pad(1.0)], axis=-1)
    s1 = jnp.concatenate([-sin, zeros, pad(0.0)], axis=-1)
    s2 = jnp.concatenate([zeros, sin, pad(0.0)], axis=-1)
    tile = lambda t: jnp.concatenate([t, t], axis=-1).reshape(TOKENS, LANES)
    return tile(c), tile(s1), tile(s2)


def _split_w_in(w):
    mw, nq, nkv, dw = MOBA_HEADS * HEAD_DIM, NSA_HEADS * HEAD_DIM, NSA_KV_HEADS * HEAD_DIM, DIL_HEADS * HEAD_DIM
    widths = (mw, mw, mw, nq) + (nkv,) * 6 + (NSA_HEADS * 3, dw, dw, dw)
    offs = np.concatenate([[0], np.cumsum(widths)])
    qa, ka, va, qb, kbc, vbc, kbs, vbs, kbw, vbw, gb, qc, kc, vc = (
        w[:, int(offs[i]):int(offs[i + 1])] for i in range(len(widths)))

    def dup(t):
        t = t.reshape(D_MODEL, NSA_KV_HEADS, 1, HEAD_DIM)
        return jnp.broadcast_to(t, (D_MODEL, NSA_KV_HEADS, 2, HEAD_DIM)).reshape(D_MODEL, NSA_KV_HEADS * LANES)

    zpad = lambda n: jnp.zeros((D_MODEL, n * LANES), w.dtype)
    w_rot = jnp.concatenate([qa * SCALE, ka, qb * SCALE, dup(kbc), dup(kbs), dup(kbw), zpad(1)], axis=1)
    w_pl = jnp.concatenate([va, dup(vbc), dup(vbs), dup(vbw), zpad(3)], axis=1)
    gpad = jnp.zeros((D_MODEL, NSA_KV_HEADS, LANES - 12), w.dtype)
    w_gl = jnp.concatenate([gb.reshape(D_MODEL, NSA_KV_HEADS, 12), gpad], axis=-1).reshape(D_MODEL, -1)
    w_dil_rot = jnp.concatenate([qc * SCALE, kc], axis=1)
    return tuple(t.astype(BF16) for t in (w_rot, w_pl, w_gl, w_dil_rot, vc))


def _overlap_table():
    starts = np.arange(N_CMP) * CMP_STRIDE
    slc = np.arange(N_SLC) * SLC_BLOCK
    ov = ((starts[:, None] < slc[None, :] + SLC_BLOCK) & (starts[:, None] + CMP_LEN > slc[None, :]))
    ovt = np.zeros((N_SLC, N_CMP_PAD), np.float32)
    ovt[:, :N_CMP] = ov.T
    return jnp.asarray(ovt, BF16)


def _cmp_chunks(z, base):
    nblk = z.shape[-1] // LANES
    t = z.reshape(BATCH, SEQ // CMP_STRIDE, CMP_STRIDE, nblk, LANES)[:, :, :, base:base + NSA_KV_HEADS, :HEAD_DIM]
    return t.transpose(0, 3, 1, 2, 4).reshape(BATCH, NSA_KV_HEADS, SEQ // CMP_STRIDE, CMP_STRIDE * HEAD_DIM)


def kernel(x, p, positions, ln_in_g, ln_in_b, w_in, w_out, nsa_ck1, nsa_ck2, nsa_pe_k, nsa_cv1, nsa_cv2, nsa_pe_v, ln1_g, ln1_b, router_w, router_b, w_gate, w_up, w_down, ple_proj, ple_gate_w, ple_gate_b, ln2_g, ln2_b):
    rope = _rope_tables(positions)
    ovt = _overlap_table()
    rw_t = router_w.T.astype(BF16)
    rb = router_b.reshape(N_EXPERTS, 1).astype(F32)
    chunk_w = CMP_STRIDE * HEAD_DIM
    vec = lambda v: v.reshape(1, -1)
    seq3 = lambda t: t.reshape(BATCH, SEQ, t.shape[-1])
    flat = lambda t: t.reshape(TOKENS, t.shape[-1])

    h, hb = _ln_in(x.reshape(TOKENS, D_MODEL), ln_in_g, ln_in_b)
    for i in range(DEPTH):
        w_rot, w_pl, w_gl, w_dil_rot, w_dil_pl = _split_w_in(w_in[i])
        z_rot = seq3(_project(hb, w_rot, BF16, 768, rope=rope))
        z_pl = seq3(_project(hb, w_pl, BF16, 1024))
        gate_logits = _project(hb, w_gl, F32, NSA_KV_HEADS * LANES)
        zd_rot = seq3(_project(hb, w_dil_rot, F32, 768, rope=rope))
        zd_pl = seq3(_project(hb, w_dil_pl, F32, 768))

        o_a = _moba(z_rot, z_pl)

        dup2 = lambda w2: jnp.concatenate([w2, w2], axis=1).astype(BF16)
        k_cmp, v_cmp = _compress(
            _cmp_chunks(z_rot, ROT_NKC), _cmp_chunks(z_pl, PL_NVC),
            nsa_pe_k[i].reshape(2, chunk_w), nsa_pe_v[i].reshape(2, chunk_w),
            nsa_ck1[i].reshape(2, chunk_w, CMP_HIDDEN).astype(BF16), dup2(nsa_ck2[i]),
            nsa_cv1[i].reshape(2, chunk_w, CMP_HIDDEN).astype(BF16), dup2(nsa_cv2[i]))
        o_b = _nsa(z_rot, z_pl, k_cmp, v_cmp, gate_logits, ovt)

        o_c = _dilated(zd_rot, zd_pl)

        wo = w_out[i].astype(BF16)
        a_w, b_w = MOBA_HEADS * HEAD_DIM, NSA_HEADS * HEAD_DIM
        h, hb = _out_proj(flat(o_a), flat(o_b), flat(o_c), h,
                          wo[:a_w], wo[a_w:a_w + b_w], wo[a_w + b_w:], vec(ln1_g[i]), vec(ln1_b[i]))

        pos, w_tok, tile_expert, n_tiles = _routing_tables(*_router(hb, rw_t, rb))
        xs = _dispatch(pos, h)
        ys = _experts(tile_expert, n_tiles, xs, w_gate[i], w_up[i], w_down[i])
        h, hb = _ple_ln(pos, hb, h, ys, w_tok, p[i].reshape(TOKENS, PLE_DIM), ple_gate_w[i].astype(BF16),
                        vec(ple_gate_b[i]), ple_proj[i].astype(BF16), vec(ln2_g[i]), vec(ln2_b[i]))
    return h.reshape(BATCH, SEQ, D_MODEL)
```

```python
import functools
import math

import numpy as np
import jax
import jax.numpy as jnp
from jax import lax
from jax.experimental import pallas as pl
from jax.experimental.pallas import tpu as pltpu

F32 = jnp.float32
BF16 = jnp.bfloat16

D_MODEL = 2048
BATCH = 2
SEQ = 4096
DEPTH = 4
TOKENS = BATCH * SEQ
HEAD_DIM = 64
ROT_DIM = HEAD_DIM // 4
ROPE_THETA = 500000.0
NEG = -1e30
FORCE = 1e30
LN_EPS = 1e-5
SCALE = HEAD_DIM ** -0.5

MOBA_HEADS = 8
MOBA_BLOCK = 256
MOBA_TOPK = 3
MOBA_NB = SEQ // MOBA_BLOCK

NSA_HEADS = 12
NSA_KV_HEADS = 3
CMP_LEN = 32
CMP_STRIDE = 16
CMP_HIDDEN = 128
N_CMP = (SEQ - CMP_LEN) // CMP_STRIDE + 1
N_CMP_PAD = 256
SLC_BLOCK = 64
SLC_TOPK = 16
SLC_LOCAL = 2
N_SLC = SEQ // SLC_BLOCK
NSA_WINDOW = 512

DIL_CONFIGS = ((128, 1), (512, 4), (2048, 16))
DIL_HEADS_PER_GROUP = 4
DIL_HEADS = DIL_HEADS_PER_GROUP * len(DIL_CONFIGS)

N_EXPERTS = 16
N_GROUPS = 4
EXPERTS_PER_GROUP = 4
EXPERT_HIDDEN = D_MODEL // 4
PLE_DIM = 256

DEEPNORM_ALPHA = (2 * DEPTH) ** 0.25

LANES = 128
VMEM_LIMIT = 56 * 1024 * 1024

ROT_MQ, ROT_MK, ROT_NQ, ROT_NKC, ROT_NKS, ROT_NKW = 0, 4, 8, 14, 17, 20
ROT_BLOCKS = 24
PL_MV, PL_NVC, PL_NVS, PL_NVW = 0, 4, 7, 10
PL_BLOCKS = 16
DIL_BLOCKS = DIL_HEADS // 2

NT_DIMS = (((1,), (1,)), ((), ()))


def _nt(a, b):
    return lax.dot_general(a, b, NT_DIMS, preferred_element_type=F32)


def _nn(a, b):
    return jnp.dot(a, b, preferred_element_type=F32)


def _params(*sem):
    return pltpu.CompilerParams(dimension_semantics=sem, vmem_limit_bytes=VMEM_LIMIT)


def _layer_norm(y, g, b):
    mu = jnp.mean(y, axis=-1, keepdims=True)
    yc = y - mu
    var = jnp.mean(yc * yc, axis=-1, keepdims=True)
    return yc * lax.rsqrt(var + LN_EPS) * g + b


def _ln_kernel(x_ref, g_ref, b_ref, h_ref, hb_ref):
    h = _layer_norm(x_ref[...], g_ref[...], b_ref[...])
    h_ref[...] = h
    hb_ref[...] = h.astype(BF16)


def _ln_in(x, g, b):
    tm = 512
    row = pl.BlockSpec((tm, D_MODEL), lambda i: (i, 0))
    vec = pl.BlockSpec((1, D_MODEL), lambda i: (0, 0))
    return pl.pallas_call(
        _ln_kernel,
        out_shape=(jax.ShapeDtypeStruct((TOKENS, D_MODEL), F32),
                   jax.ShapeDtypeStruct((TOKENS, D_MODEL), BF16)),
        grid=(TOKENS // tm,),
        in_specs=[row, vec, vec],
        out_specs=(row, row),
        compiler_params=_params("parallel"),
        name="ln_in",
    )(x, g.reshape(1, -1), b.reshape(1, -1))


def _proj_kernel(x_ref, w_ref, o_ref):
    o_ref[...] = _nn(x_ref[...], w_ref[...]).astype(o_ref.dtype)


def _proj_rot_kernel(x_ref, w_ref, c_ref, s1_ref, s2_ref, o_ref):
    z = _nn(x_ref[...], w_ref[...])
    c, s1, s2 = c_ref[...], s1_ref[...], s2_ref[...]
    half = ROT_DIM // 2
    for j in range(z.shape[1] // LANES):
        zc = z[:, j * LANES:(j + 1) * LANES]
        r = zc * c + pltpu.roll(zc, LANES - half, 1) * s1 + pltpu.roll(zc, half, 1) * s2
        o_ref[:, j * LANES:(j + 1) * LANES] = r.astype(o_ref.dtype)


def _project(hb, w, out_dtype, tn, rope=None):
    tm = 1024
    n = w.shape[1]
    x_spec = pl.BlockSpec((tm, D_MODEL), lambda i, j: (i, 0))
    w_spec = pl.BlockSpec((D_MODEL, tn), lambda i, j: (0, j))
    o_spec = pl.BlockSpec((tm, tn), lambda i, j: (i, j))
    if rope is None:
        kern, extra, extra_specs = _proj_kernel, (), []
    else:
        t_spec = pl.BlockSpec((tm, LANES), lambda i, j: (i, 0))
        kern, extra, extra_specs = _proj_rot_kernel, rope, [t_spec] * 3
    return pl.pallas_call(
        kern,
        out_shape=jax.ShapeDtypeStruct((TOKENS, n), out_dtype),
        grid=(TOKENS // tm, n // tn),
        in_specs=[x_spec, w_spec] + extra_specs,
        out_specs=o_spec,
        compiler_params=_params("parallel", "arbitrary"),
        name="in_proj_rot" if rope is not None else "in_proj",
    )(hb, w, *extra)


def _stack_heads(*q_blocks):
    parts = []
    for q in q_blocks:
        lane = lax.broadcasted_iota(jnp.int32, q.shape, 1)
        zero = jnp.zeros_like(q)
        parts += [jnp.where(lane < HEAD_DIM, q, zero), jnp.where(lane >= HEAD_DIM, q, zero)]
    return jnp.concatenate(parts, axis=0)


def _merge_pair_t(lo, hi):
    sub = lax.broadcasted_iota(jnp.int32, lo.shape, 0)
    return jnp.where(sub < HEAD_DIM, lo, hi)


def _band_bias_t(nk, qc, offset, n_back):
    key = lax.broadcasted_iota(jnp.int32, (nk, qc), 0)
    qry = lax.broadcasted_iota(jnp.int32, (nk, qc), 1)
    diff = offset + qry - key
    return jnp.where((diff >= 0) & (diff <= n_back), 0.0, NEG)


def _tile_lanes(x, n):
    return jnp.concatenate([x] * n, axis=1)


def _transpose_bf16(v):
    return jnp.transpose(v.astype(F32)).astype(BF16)


def _tree(x, op):
    n = x.shape[0]
    if n == 8:
        return x
    if n % 16 == 0:
        return op(_tree(x[:n // 2], op), _tree(x[n // 2:], op))
    acc = x[:8]
    for i in range(1, n // 8):
        acc = op(acc, x[8 * i:8 * i + 8])
    return acc


def _reduce_keys(x, op, final):
    return final(_tree(x, op), axis=0, keepdims=True)


VT_ROWS = LANES + 16


def _transpose_aug(v):
    vt = jnp.transpose(v.astype(F32))
    sub = lax.broadcasted_iota(jnp.int32, (VT_ROWS - LANES, v.shape[0]), 0)
    return jnp.concatenate([vt, jnp.where(sub == 0, 1.0, 0.0)], axis=0).astype(BF16)


def _probs(s_t, m):
    return jnp.exp((s_t - m).astype(BF16))


def _normalise(acc):
    l = acc[LANES:LANES + 1]
    return acc[:LANES] / l, l


def _softmax_block_t(s_t, pv):
    m = _reduce_keys(s_t, jnp.maximum, jnp.max)
    out, l = _normalise(pv(_probs(s_t, m)))
    return out, m + jnp.log(l)


def _online_step_t(carry, s_t, m_t, pv):
    m, acc = carry
    m_new = jnp.maximum(m, m_t)
    acc = jnp.exp(m - m_new) * acc + pv(_probs(s_t, m_new))
    return m_new, acc


def _flash_tiles(n_tiles, last_tile, r, scores, pv_of):
    def produce(t):
        s_t = scores(t)
        return s_t, _reduce_keys(s_t, jnp.maximum, jnp.max)

    def body(t, carry):
        state, s_t, m_t = carry
        s_next, m_next = produce(jnp.minimum(t + 1, last_tile))
        return _online_step_t(state, s_t, m_t, pv_of(t)), s_next, m_next

    state, _, _ = lax.fori_loop(0, n_tiles, body, (_online_init_t(r),) + produce(0))
    return state


def _pv_tiles(vt_ref, first, n, rows):
    def pv(p):
        acc = _nn(vt_ref[first], p[:rows])
        for j in range(1, n):
            acc = acc + _nn(vt_ref[first + j], p[j * rows:(j + 1) * rows])
        return acc
    return pv


def _online_init_t(r):
    return (jnp.full((1, r), NEG, F32), jnp.zeros((VT_ROWS, r), F32))


def _rank_rows(g, n_rows):
    sub = lax.broadcasted_iota(jnp.int32, (8, g.shape[1]), 0)
    rank = jnp.zeros(g.shape, F32)
    for m in range(n_rows):
        gm = g[m:m + 1, :]
        b = m // 8 * 8
        mid = g[b:b + 8]
        parts = [jnp.where(gm > mid, 1.0, jnp.where((gm == mid) & (sub > m - b), 1.0, 0.0))]
        if b > 0:
            parts.insert(0, jnp.where(gm > g[:b], 1.0, 0.0))
        if b + 8 < n_rows:
            parts.append(jnp.where(gm >= g[b + 8:], 1.0, 0.0))
        rank = rank + jnp.concatenate(parts, axis=0)
    return rank


MOBA_QC = 256
MOBA_KT = 2 * MOBA_BLOCK


def _moba_kernel(q_ref, k_ref, v_ref, o_ref, kmean_ref, vt_ref, bias_ref):
    c = pl.program_id(2)
    qc = MOBA_QC
    r = 2 * qc

    @pl.when(c == 0)
    def _():
        row = lax.broadcasted_iota(jnp.int32, (MOBA_NB, SEQ), 0)
        col = lax.broadcasted_iota(jnp.int32, (MOBA_NB, SEQ), 1)
        avg = jnp.where((col >> 8) == row, 1.0 / MOBA_BLOCK, 0.0).astype(BF16)
        kmean_ref[...] = _nn(avg, k_ref[0])
        for t in range(MOBA_NB):
            vt_ref[t] = _transpose_aug(v_ref[0, t * MOBA_BLOCK:(t + 1) * MOBA_BLOCK, :])

    qs = _stack_heads(q_ref[0])

    gate = _nt(kmean_ref[...].astype(BF16), qs)
    blk = lax.broadcasted_iota(jnp.int32, gate.shape, 0)
    past = blk < c
    rank = _rank_rows(jnp.where(past, gate, NEG), MOBA_NB)
    bias_ref[...] = jnp.where(past & (rank < MOBA_TOPK), 0.0, NEG)

    per_tile = MOBA_KT // MOBA_BLOCK

    def scores(t):
        ks = pl.multiple_of(t * MOBA_KT, MOBA_KT)
        blocks = [jnp.broadcast_to(bias_ref[pl.ds(t * per_tile + j, 1), :], (MOBA_BLOCK, r))
                  for j in range(per_tile)]
        return _nt(k_ref[0, pl.ds(ks, MOBA_KT), :], qs) + jnp.concatenate(blocks, axis=0)

    state = _flash_tiles((c + per_tile - 1) // per_tile, SEQ // MOBA_KT - 1, r, scores,
                         lambda t: _pv_tiles(vt_ref, t * per_tile, per_tile, MOBA_BLOCK))

    ks = pl.multiple_of(c * MOBA_BLOCK, MOBA_BLOCK)
    causal = _band_bias_t(MOBA_BLOCK, qc, 0, MOBA_BLOCK)
    s_t = _nt(k_ref[0, pl.ds(ks, MOBA_BLOCK), :], qs) + _tile_lanes(causal, 2)
    _, acc = _online_step_t(state, s_t, _reduce_keys(s_t, jnp.maximum, jnp.max),
                            _pv_tiles(vt_ref, c, 1, MOBA_BLOCK))
    o_t, _ = _normalise(acc)
    o_ref[0] = jnp.transpose(_merge_pair_t(o_t[:, :qc], o_t[:, qc:])).astype(o_ref.dtype)


def _moba(z_rot, z_pl):
    qc = MOBA_QC
    grid = (BATCH, MOBA_HEADS // 2, SEQ // qc)
    return pl.pallas_call(
        _moba_kernel,
        out_shape=jax.ShapeDtypeStruct((BATCH, SEQ, MOBA_HEADS * HEAD_DIM), BF16),
        grid=grid,
        in_specs=[
            pl.BlockSpec((1, qc, LANES), lambda b, p, c: (b, c, ROT_MQ + p)),
            pl.BlockSpec((1, SEQ, LANES), lambda b, p, c: (b, 0, ROT_MK + p)),
            pl.BlockSpec((1, SEQ, LANES), lambda b, p, c: (b, 0, PL_MV + p)),
        ],
        out_specs=pl.BlockSpec((1, qc, LANES), lambda b, p, c: (b, c, p)),
        scratch_shapes=[pltpu.VMEM((MOBA_NB, LANES), F32),
                        pltpu.VMEM((MOBA_NB, VT_ROWS, MOBA_BLOCK), BF16),
                        pltpu.VMEM((MOBA_NB, 2 * qc), F32)],
        compiler_params=_params("parallel", "parallel", "arbitrary"),
        name="moba",
    )(z_rot, z_rot, z_pl)


def _compress_one(x_ref, pe_ref, w1_ref, w2_ref, o_ref):
    x = x_ref[0, 0].astype(F32)
    top = (x + pe_ref[0:1, :]).astype(BF16)
    bot = (x + pe_ref[1:2, :]).astype(BF16)
    a = _nn(top, w1_ref[0])
    bm = _nn(bot, w1_ref[1])
    pre = a + pltpu.roll(bm, N_CMP_PAD - 1, 0)
    hid = jax.nn.gelu(pre)
    out = _nn(hid.astype(BF16), w2_ref[...])
    row = lax.broadcasted_iota(jnp.int32, out.shape, 0)
    o_ref[0, 0] = jnp.where(row < N_CMP, out, 0.0).astype(o_ref.dtype)


def _compress_kernel(xk_ref, xv_ref, pk_ref, pv_ref, k1_ref, k2_ref, v1_ref, v2_ref, ok_ref, ov_ref):
    _compress_one(xk_ref, pk_ref, k1_ref, k2_ref, ok_ref)
    _compress_one(xv_ref, pv_ref, v1_ref, v2_ref, ov_ref)


def _compress(xk, xv, pk, pv, k1, k2, v1, v2):
    chunk_w = CMP_STRIDE * HEAD_DIM
    x_spec = pl.BlockSpec((1, 1, N_CMP_PAD, chunk_w), lambda b, j: (b, j, 0, 0))
    pe_spec = pl.BlockSpec((2, chunk_w), lambda b, j: (0, 0))
    w1_spec = pl.BlockSpec((2, chunk_w, CMP_HIDDEN), lambda b, j: (0, 0, 0))
    w2_spec = pl.BlockSpec((CMP_HIDDEN, LANES), lambda b, j: (0, 0))
    o_spec = pl.BlockSpec((1, 1, N_CMP_PAD, LANES), lambda b, j: (b, j, 0, 0))
    o_shape = jax.ShapeDtypeStruct((BATCH, NSA_KV_HEADS, N_CMP_PAD, LANES), BF16)
    return pl.pallas_call(
        _compress_kernel,
        out_shape=(o_shape, o_shape),
        grid=(BATCH, NSA_KV_HEADS),
        in_specs=[x_spec, x_spec, pe_spec, pe_spec, w1_spec, w2_spec, w1_spec, w2_spec],
        out_specs=(o_spec, o_spec),
        compiler_params=_params("parallel", "parallel"),
        name="nsa_compress",
    )(xk, xv, pk, pv, k1, k2, v1, v2)


NSA_QC = 128
NSA_KT = 512
NSA_G = NSA_HEADS // NSA_KV_HEADS
NSA_WIN_TILES = NSA_WINDOW // NSA_QC + 1


def _nsa_kernel(qa_ref, qb_ref, kc_ref, vc_ref, ks_ref, vs_ref, kw_ref, vw_ref, gl_ref, ovt_ref,
                o_ref, vct_ref, vst_ref, vwt_ref, bias_ref):
    c = pl.program_id(2)
    qc = NSA_QC
    q0 = c * qc
    lanes_of = lambda t, i: t[:, i * qc:(i + 1) * qc]

    @pl.when(c == 0)
    def _():
        vct_ref[...] = _transpose_bf16(vc_ref[0, 0])
        for t in range(SEQ // NSA_KT):
            vst_ref[t] = _transpose_aug(vs_ref[0, t * NSA_KT:(t + 1) * NSA_KT, :])
        for t in range(SEQ // qc):
            vwt_ref[t] = _transpose_aug(vw_ref[0, t * qc:(t + 1) * qc, :])

    qs = _stack_heads(qa_ref[0], qb_ref[0])

    sc_t = _nt(kc_ref[0, 0], qs)
    n_idx = lax.broadcasted_iota(jnp.int32, (N_CMP_PAD, qc), 0)
    q_idx = lax.broadcasted_iota(jnp.int32, (N_CMP_PAD, qc), 1)
    ok = (n_idx * CMP_STRIDE + (CMP_LEN - 1)) <= (q0 + q_idx)
    p_heads = []
    for i in range(NSA_G):
        s_i = jnp.where(ok, lanes_of(sc_t, i), NEG)
        e_i = jnp.where(ok, jnp.exp(s_i - _reduce_keys(s_i, jnp.maximum, jnp.max)), 0.0)
        l_i = _reduce_keys(e_i, jnp.add, jnp.sum)
        p_heads.append((e_i / jnp.where(l_i > 0.0, l_i, 1.0)).astype(BF16))
    p_ct = jnp.concatenate(p_heads, axis=1)
    ocmp_t = _nn(vct_ref[...], p_ct)

    imp4 = _nn(ovt_ref[...], p_ct)
    imp = lanes_of(imp4, 0) + lanes_of(imp4, 1) + lanes_of(imp4, 2) + lanes_of(imp4, 3)
    blk = lax.broadcasted_iota(jnp.int32, imp.shape, 0)
    cur = (q0 + lax.broadcasted_iota(jnp.int32, imp.shape, 1)) >> 6
    valid = blk <= cur
    forced = valid & ((blk == 0) | (blk > cur - SLC_LOCAL))
    rank = _rank_rows(jnp.where(forced, FORCE, jnp.where(valid, imp, NEG)), N_SLC)
    bias_ref[...] = jnp.where(valid & (rank < SLC_TOPK), 0.0, NEG)

    key_row = lax.broadcasted_iota(jnp.int32, (NSA_KT, qc), 0)
    qpos = lax.broadcasted_iota(jnp.int32, (NSA_KT, qc), 1) + q0
    per_tile = NSA_KT // SLC_BLOCK

    def scores(t):
        ks0 = pl.multiple_of(t * NSA_KT, NSA_KT)
        blocks = [jnp.broadcast_to(bias_ref[pl.ds(t * per_tile + j, 1), :], (SLC_BLOCK, qc))
                  for j in range(per_tile)]
        bias = jnp.where(key_row + ks0 <= qpos, jnp.concatenate(blocks, axis=0), NEG)
        return _nt(ks_ref[0, pl.ds(ks0, NSA_KT), :], qs) + _tile_lanes(bias, NSA_G)

    _, acc_s = _flash_tiles(c // (NSA_KT // qc) + 1, SEQ // NSA_KT - 1, NSA_G * qc, scores,
                            lambda t: _pv_tiles(vst_ref, t, 1, NSA_KT))
    oslc_t, _ = _normalise(acc_s)

    t0 = jnp.maximum(c - NSA_WINDOW // qc, 0)
    start = pl.multiple_of(t0 * qc, qc)
    band = _band_bias_t(NSA_WIN_TILES * qc, qc, q0 - start, NSA_WINDOW - 1)
    sw_t = _nt(kw_ref[0, pl.ds(start, NSA_WIN_TILES * qc), :], qs) + _tile_lanes(band, NSA_G)

    owin_t, _ = _softmax_block_t(sw_t, _pv_tiles(vwt_ref, t0, NSA_WIN_TILES, qc))

    gate_t = jnp.transpose(jax.nn.sigmoid(gl_ref[...]))
    outs = []
    for i in range(NSA_G):
        outs.append(gate_t[3 * i:3 * i + 1, :] * lanes_of(ocmp_t, i)
                    + gate_t[3 * i + 1:3 * i + 2, :] * lanes_of(oslc_t, i)
                    + gate_t[3 * i + 2:3 * i + 3, :] * lanes_of(owin_t, i))
    o_ref[0, :, 0:LANES] = jnp.transpose(_merge_pair_t(outs[0], outs[1])).astype(o_ref.dtype)
    o_ref[0, :, LANES:2 * LANES] = jnp.transpose(_merge_pair_t(outs[2], outs[3])).astype(o_ref.dtype)


def _nsa(z_rot, z_pl, k_cmp, v_cmp, gate_logits, ovt):
    qc = NSA_QC
    seq_spec = lambda base: pl.BlockSpec((1, SEQ, LANES), lambda b, j, c: (b, 0, base + j))
    cmp_spec = pl.BlockSpec((1, 1, N_CMP_PAD, LANES), lambda b, j, c: (b, j, 0, 0))
    return pl.pallas_call(
        _nsa_kernel,
        out_shape=jax.ShapeDtypeStruct((BATCH, SEQ, NSA_HEADS * HEAD_DIM), BF16),
        grid=(BATCH, NSA_KV_HEADS, SEQ // qc),
        in_specs=[
            pl.BlockSpec((1, qc, LANES), lambda b, j, c: (b, c, ROT_NQ + 2 * j)),
            pl.BlockSpec((1, qc, LANES), lambda b, j, c: (b, c, ROT_NQ + 2 * j + 1)),
            cmp_spec, cmp_spec,
            seq_spec(ROT_NKS), seq_spec(PL_NVS), seq_spec(ROT_NKW), seq_spec(PL_NVW),
            pl.BlockSpec((qc, LANES), lambda b, j, c: (b * (SEQ // qc) + c, j)),
            pl.BlockSpec(ovt.shape, lambda b, j, c: (0, 0)),
        ],
        out_specs=pl.BlockSpec((1, qc, 2 * LANES), lambda b, j, c: (b, c, j)),
        scratch_shapes=[pltpu.VMEM((LANES, N_CMP_PAD), BF16),
                        pltpu.VMEM((SEQ // NSA_KT, VT_ROWS, NSA_KT), BF16),
                        pltpu.VMEM((SEQ // qc, VT_ROWS, qc), BF16),
                        pltpu.VMEM((N_SLC, qc), F32)],
        compiler_params=_params("parallel", "parallel", "arbitrary"),
        name="nsa",
    )(z_rot, z_rot, k_cmp, v_cmp, z_rot, z_pl, z_rot, z_pl, gate_logits, ovt)


DIL_QC = 128
DIL_STEPS = SEQ // DIL_QC
DIL_UNROLL = 4


def _dil_group(q_ref, k_ref, v_ref, og_ref, lg_ref, gi):
    window, dil = DIL_CONFIGS[gi]
    qc = DIL_QC
    m = SEQ // dil
    n_back = window // dil
    nk = min(m, qc + -(-n_back // qc) * qc)
    chunks = m // qc

    def rows(first, n):
        return pl.ds(first, n) if dil == 1 else pl.ds(first, n, stride=dil)

    def block(idx):
        r = idx // chunks
        q0 = (idx % chunks) * qc
        start = jnp.maximum(q0 - (nk - qc), 0)
        q_rows = rows(r + dil * q0, qc)
        k_rows = rows(r + dil * start, nk)
        qs = _stack_heads(q_ref[0, q_rows, :].astype(BF16))
        v_t = _transpose_aug(v_ref[0, k_rows, :])
        band = _band_bias_t(nk, qc, q0 - start, n_back)
        s_t = _nt(k_ref[0, k_rows, :].astype(BF16), qs) + _tile_lanes(band, 2)
        o_t, lse = _softmax_block_t(s_t, lambda p: _nn(v_t, p))
        lse_b = jnp.broadcast_to(lse, (LANES, 2 * qc))
        og_ref[gi, q_rows, :] = jnp.transpose(_merge_pair_t(o_t[:, :qc], o_t[:, qc:]))
        lg_ref[gi, q_rows, :] = jnp.transpose(_merge_pair_t(lse_b[:, :qc], lse_b[:, qc:]))

    def body(i, _):
        for u in range(DIL_UNROLL):
            block(i * DIL_UNROLL + u)
        return 0

    lax.fori_loop(0, DIL_STEPS // DIL_UNROLL, body, 0)


def _dil_kernel(q_ref, k_ref, v_ref, o_ref, og_ref, lg_ref):
    g = pl.program_id(2)
    n_groups = len(DIL_CONFIGS)
    for gi in range(n_groups):
        pl.when(g == gi)(functools.partial(_dil_group, q_ref, k_ref, v_ref, og_ref, lg_ref, gi))

    @pl.when(g == n_groups - 1)
    def _():
        rows = 512

        def body(i, _):
            sl = pl.ds(pl.multiple_of(i * rows, rows), rows)
            l0, l1, l2 = lg_ref[0, sl, :], lg_ref[1, sl, :], lg_ref[2, sl, :]
            mx = jnp.maximum(jnp.maximum(l0, l1), l2)
            e0, e1, e2 = jnp.exp(l0 - mx), jnp.exp(l1 - mx), jnp.exp(l2 - mx)
            den = e0 + e1 + e2
            out = (e0 / den) * og_ref[0, sl, :] + (e1 / den) * og_ref[1, sl, :] + (e2 / den) * og_ref[2, sl, :]
            o_ref[0, sl, :] = out.astype(o_ref.dtype)
            return 0

        lax.fori_loop(0, SEQ // rows, body, 0)


def _dilated(zd_rot, zd_pl):
    n_groups = len(DIL_CONFIGS)
    width = DIL_HEADS_PER_GROUP * HEAD_DIM
    col = lambda base: (lambda b, p, g: (b, 0, base + 2 * g + p))
    blk = lambda base: pl.BlockSpec((1, SEQ, LANES), col(base))
    return pl.pallas_call(
        _dil_kernel,
        out_shape=jax.ShapeDtypeStruct((BATCH, SEQ, width), BF16),
        grid=(BATCH, 2, n_groups),
        in_specs=[blk(0), blk(DIL_BLOCKS), blk(0)],
        out_specs=pl.BlockSpec((1, SEQ, LANES), lambda b, p, g: (b, 0, p)),
        scratch_shapes=[pltpu.VMEM((n_groups, SEQ, LANES), F32), pltpu.VMEM((n_groups, SEQ, LANES), F32)],
        compiler_params=_params("parallel", "parallel", "arbitrary"),
        name="dilated",
    )(zd_rot, zd_rot, zd_pl)


def _out_proj_kernel(oa_ref, ob_ref, oc_ref, h_ref, wa_ref, wb_ref, wc_ref, g_ref, b_ref, h1_ref, h1b_ref):
    y = _nn(oa_ref[...], wa_ref[...]) + _nn(ob_ref[...], wb_ref[...]) + _nn(oc_ref[...], wc_ref[...])
    h1 = _layer_norm(DEEPNORM_ALPHA * h_ref[...] + y, g_ref[...], b_ref[...])
    h1_ref[...] = h1
    h1b_ref[...] = h1.astype(BF16)


def _out_proj(oa, ob, oc, h, wa, wb, wc, g, b):
    tm = 512
    rows = lambda w: pl.BlockSpec((tm, w), lambda i: (i, 0))
    full = lambda a: pl.BlockSpec(a.shape, lambda i: (0, 0))
    return pl.pallas_call(
        _out_proj_kernel,
        out_shape=(jax.ShapeDtypeStruct((TOKENS, D_MODEL), F32),
                   jax.ShapeDtypeStruct((TOKENS, D_MODEL), BF16)),
        grid=(TOKENS // tm,),
        in_specs=[rows(oa.shape[1]), rows(ob.shape[1]), rows(oc.shape[1]), rows(D_MODEL),
                  full(wa), full(wb), full(wc), full(g), full(b)],
        out_specs=(rows(D_MODEL), rows(D_MODEL)),
        compiler_params=_params("parallel"),
        name="out_proj_ln",
    )(oa, ob, oc, h, wa, wb, wc, g, b)


def _router_kernel(hb_ref, rw_ref, rb_ref, comb_ref, sel_ref):
    logits = _nt(rw_ref[...], hb_ref[...]) + rb_ref[...]
    mx = jnp.max(logits, axis=0, keepdims=True)
    ex = jnp.exp(logits - mx)
    probs = ex / jnp.sum(ex, axis=0, keepdims=True)
    p = [probs[e:e + 1, :] for e in range(N_EXPERTS)]
    best, g_sel = None, None
    for g in range(N_GROUPS):
        a, b, c, d = p[4 * g:4 * g + 4]
        hi1, lo1, hi2, lo2 = jnp.maximum(a, b), jnp.minimum(a, b), jnp.maximum(c, d), jnp.minimum(c, d)
        top2 = jnp.maximum(hi1, hi2) + jnp.maximum(jnp.minimum(hi1, hi2), jnp.maximum(lo1, lo2))
        if g == 0:
            best, g_sel = top2, jnp.zeros_like(top2)
        else:
            better = top2 > best
            best = jnp.where(better, top2, best)
            g_sel = jnp.where(better, float(g), g_sel)
    chosen, picked = [], []
    for e in range(N_EXPERTS):
        g = e // EXPERTS_PER_GROUP
        rank = jnp.zeros_like(best)
        for o in range(4 * g, 4 * g + 4):
            if o < e:
                rank = rank + jnp.where(p[o] >= p[e], 1.0, 0.0)
            elif o > e:
                rank = rank + jnp.where(p[o] > p[e], 1.0, 0.0)
        chosen.append(jnp.where((g_sel == float(g)) & (rank < 2.0), 1.0, 0.0))
        picked.append(chosen[e] * p[e])
    total = picked[0]
    for e in range(1, N_EXPERTS):
        total = total + picked[e]
    comb_ref[...] = jnp.concatenate(picked, axis=0) / total
    sel_ref[...] = jnp.concatenate(chosen, axis=0)


def _router(hb, rw_t, rb):
    tm = 1024
    out = jax.ShapeDtypeStruct((N_EXPERTS, TOKENS), F32)
    o_spec = pl.BlockSpec((N_EXPERTS, tm), lambda i: (0, i))
    return pl.pallas_call(
        _router_kernel,
        out_shape=(out, out),
        grid=(TOKENS // tm,),
        in_specs=[pl.BlockSpec((tm, D_MODEL), lambda i: (i, 0)),
                  pl.BlockSpec((N_EXPERTS, D_MODEL), lambda i: (0, 0)),
                  pl.BlockSpec((N_EXPERTS, 1), lambda i: (0, 0))],
        out_specs=(o_spec, o_spec),
        compiler_params=_params("parallel"),
        name="router",
    )(hb, rw_t, rb)


def _routing_tables(comb_t, sel_t):
    sel = sel_t > 0.5
    cnt = jnp.sum(sel, axis=1, dtype=jnp.int32)
    cnt_pad = (cnt + (MOE_TILE - 1)) // MOE_TILE * MOE_TILE
    ends = jnp.cumsum(cnt_pad)
    rank = jnp.cumsum(sel.astype(jnp.int32), axis=1) - 1
    pos = (ends - cnt_pad)[:, None] + rank
    pos_lo = jnp.min(jnp.where(sel, pos, MOE_ROWS), axis=0)
    pos_hi = jnp.max(jnp.where(sel, pos, -1), axis=0)
    w_lo = jnp.sum(jnp.where(sel & (pos == pos_lo), comb_t, 0.0), axis=0)
    w_hi = jnp.sum(jnp.where(sel & (pos == pos_hi), comb_t, 0.0), axis=0)
    w = jnp.zeros((TOKENS, LANES), F32).at[:, 0].set(w_lo).at[:, 1].set(w_hi)
    n_tiles = ends[-1] // MOE_TILE
    tile_start = jnp.arange(MOE_TILES, dtype=jnp.int32) * MOE_TILE
    tile_start = jnp.minimum(tile_start, ends[-1] - MOE_TILE)
    tile_expert = jnp.sum((ends[None, :] <= tile_start[:, None]).astype(jnp.int32), axis=1)
    return jnp.stack([pos_lo, pos_hi]).astype(jnp.int32), w, tile_expert, n_tiles.reshape(1).astype(jnp.int32)


MOE_TILE = 256
MOE_TILES = 2 * TOKENS // MOE_TILE + N_EXPERTS
MOE_ROWS = MOE_TILES * MOE_TILE
SLAB = D_MODEL // LANES


def _to_slabs(ref, x, rows):
    for j in range(SLAB):
        ref[pl.ds(j, rows, stride=SLAB), :] = x[:, j * LANES:(j + 1) * LANES]


def _from_slabs(ref, rows):
    return jnp.concatenate([ref[pl.ds(j, rows, stride=SLAB), :] for j in range(SLAB)], axis=1)


def _slab_rows(row):
    return pl.ds(pl.multiple_of(row * SLAB, SLAB), SLAB)


def _dispatch_kernel(pos_ref, h_ref, init_ref, xs_ref, slab_ref, sem):
    del init_ref
    tm = h_ref.shape[0]
    base = pl.program_id(0) * tm
    _to_slabs(slab_ref, h_ref[...], tm)

    def copy(t, which):
        return pltpu.make_async_copy(slab_ref.at[_slab_rows(t), :],
                                     xs_ref.at[_slab_rows(pos_ref[which, base + t]), :], sem)

    def start(t, _):
        copy(t, 0).start()
        copy(t, 1).start()
        return 0

    lax.fori_loop(0, tm, start, 0, unroll=8)
    whole = pltpu.make_async_copy(slab_ref, xs_ref.at[pl.ds(0, tm * SLAB), :], sem)
    whole.wait()
    whole.wait()


def _dispatch(pos, h):
    tm = 256
    grid_spec = pltpu.PrefetchScalarGridSpec(
        num_scalar_prefetch=1,
        grid=(TOKENS // tm,),
        in_specs=[pl.BlockSpec((tm, D_MODEL), lambda i, pos: (i, 0)),
                  pl.BlockSpec(memory_space=pl.ANY)],
        out_specs=pl.BlockSpec(memory_space=pl.ANY),
        scratch_shapes=[pltpu.VMEM((tm * SLAB, LANES), F32), pltpu.SemaphoreType.DMA],
    )
    return pl.pallas_call(
        _dispatch_kernel,
        out_shape=jax.ShapeDtypeStruct((MOE_ROWS * SLAB, LANES), F32),
        grid_spec=grid_spec,
        input_output_aliases={2: 0},
        compiler_params=_params("arbitrary"),
        name="moe_dispatch",
    )(pos, h, jnp.zeros((MOE_ROWS * SLAB, LANES), F32))


def _experts_kernel(te_ref, nt_ref, xs_ref, wg_ref, wu_ref, wd_ref, ys_ref, wgb_ref, wub_ref, wdb_ref):
    k = pl.program_id(0)
    e = te_ref[k]
    e_prev = te_ref[jnp.maximum(k - 1, 0)]

    @pl.when((k == 0) | (e != e_prev))
    def _():
        wgb_ref[...] = wg_ref[0].astype(BF16)
        wub_ref[...] = wu_ref[0].astype(BF16)
        wdb_ref[...] = wd_ref[0].astype(BF16)

    @pl.when(k < nt_ref[0])
    def _():
        x = _from_slabs(xs_ref, MOE_TILE).astype(BF16)
        hid = jax.nn.silu(_nn(x, wgb_ref[...])) * _nn(x, wub_ref[...])
        _to_slabs(ys_ref, _nn(hid.astype(BF16), wdb_ref[...]), MOE_TILE)

    @pl.when(k >= nt_ref[0])
    def _():
        ys_ref[...] = jnp.zeros(ys_ref.shape, F32)


def _experts(tile_expert, n_tiles, xs, wg, wu, wd):
    w_in_spec = pl.BlockSpec((1, D_MODEL, EXPERT_HIDDEN), lambda k, te, nt: (te[k], 0, 0))
    grid_spec = pltpu.PrefetchScalarGridSpec(
        num_scalar_prefetch=2,
        grid=(MOE_TILES,),
        in_specs=[pl.BlockSpec((MOE_TILE * SLAB, LANES), lambda k, te, nt: (jnp.minimum(k, nt[0] - 1), 0)),
                  w_in_spec, w_in_spec,
                  pl.BlockSpec((1, EXPERT_HIDDEN, D_MODEL), lambda k, te, nt: (te[k], 0, 0))],
        out_specs=pl.BlockSpec((MOE_TILE * SLAB, LANES), lambda k, te, nt: (k, 0)),
        scratch_shapes=[pltpu.VMEM((D_MODEL, EXPERT_HIDDEN), BF16), pltpu.VMEM((D_MODEL, EXPERT_HIDDEN), BF16),
                        pltpu.VMEM((EXPERT_HIDDEN, D_MODEL), BF16)],
    )
    return pl.pallas_call(
        _experts_kernel,
        out_shape=jax.ShapeDtypeStruct((MOE_ROWS * SLAB, LANES), F32),
        grid_spec=grid_spec,
        compiler_params=_params("arbitrary"),
        name="moe_experts",
    )(tile_expert, n_tiles, xs, wg, wu, wd)


def _ple_ln_kernel(pos_ref, hb_ref, h_ref, ys_ref, w_ref, p_ref, gw_ref, gb_ref, pw_ref, g_ref, b_ref,
                   h2_ref, h2b_ref, lo_ref, hi_ref, sem):
    tm = h_ref.shape[0]
    i = pl.program_id(0)
    slot = i & 1
    bufs = (lo_ref, hi_ref)

    def fetch(tile, into):
        def start(t, _):
            for which in range(2):
                pltpu.make_async_copy(ys_ref.at[_slab_rows(pos_ref[which, tile * tm + t]), :],
                                      bufs[which].at[into, _slab_rows(t), :], sem.at[into]).start()
            return 0
        lax.fori_loop(0, tm, start, 0, unroll=8)

    @pl.when(i == 0)
    def _():
        fetch(0, 0)

    @pl.when(i + 1 < pl.num_programs(0))
    def _():
        fetch(i + 1, 1 - slot)

    gate = jax.nn.sigmoid(_nn(hb_ref[...], gw_ref[...]) + gb_ref[...])
    ple = gate * _nn(p_ref[...].astype(BF16), pw_ref[...])
    for which in range(2):
        pltpu.make_async_copy(ys_ref.at[pl.ds(0, tm * SLAB), :], bufs[which].at[slot], sem.at[slot]).wait()
    w = w_ref[...]
    ffn = w[:, 0:1] * _from_slabs(lo_ref.at[slot], tm) + w[:, 1:2] * _from_slabs(hi_ref.at[slot], tm)
    h2 = _layer_norm(DEEPNORM_ALPHA * h_ref[...] + ffn + ple, g_ref[...], b_ref[...])
    h2_ref[...] = h2
    h2b_ref[...] = h2.astype(BF16)


def _ple_ln(pos, hb, h, ys, w, p, gw, gb, pw, g, b):
    tm = 256
    rows = lambda width: pl.BlockSpec((tm, width), lambda i, pos: (i, 0))
    full = lambda a: pl.BlockSpec(a.shape, lambda i, pos: (0, 0))
    grid_spec = pltpu.PrefetchScalarGridSpec(
        num_scalar_prefetch=1,
        grid=(TOKENS // tm,),
        in_specs=[rows(D_MODEL), rows(D_MODEL), pl.BlockSpec(memory_space=pl.ANY), rows(LANES), rows(PLE_DIM),
                  full(gw), full(gb), full(pw), full(g), full(b)],
        out_specs=(rows(D_MODEL), rows(D_MODEL)),
        scratch_shapes=[pltpu.VMEM((2, tm * SLAB, LANES), F32), pltpu.VMEM((2, tm * SLAB, LANES), F32),
                        pltpu.SemaphoreType.DMA((2,))],
    )
    return pl.pallas_call(
        _ple_ln_kernel,
        out_shape=(jax.ShapeDtypeStruct((TOKENS, D_MODEL), F32),
                   jax.ShapeDtypeStruct((TOKENS, D_MODEL), BF16)),
        grid_spec=grid_spec,
        compiler_params=_params("arbitrary"),
        name="ple_ln",
    )(pos, hb, h, ys, w, p, gw, gb, pw, g, b)


def _rope_tables(positions):
    half = ROT_DIM // 2
    inv_freq = jnp.exp(jnp.arange(half, dtype=F32) * (-2.0 * math.log(ROPE_THETA) / ROT_DIM))
    ang = positions.astype(F32)[:, :, None] * inv_freq
    cos, sin = jnp.cos(ang), jnp.sin(ang)
    zeros = jnp.zeros_like(cos)
    rest = HEAD_DIM - ROT_DIM
    pad = lambda v: jnp.broadcast_to(jnp.asarray(v, F32), cos.shape[:2] + (rest,))
    c = jnp.concatenate([cos, cos, pad(1.0)], axis=-1)
    s1 = jnp.concatenate([-sin, zeros, pad(0.0)], axis=-1)
    s2 = jnp.concatenate([zeros, sin, pad(0.0)], axis=-1)
    tile = lambda t: jnp.concatenate([t, t], axis=-1).reshape(TOKENS, LANES)
    return tile(c), tile(s1), tile(s2)


def _split_w_in(w):
    mw, nq, nkv, dw = MOBA_HEADS * HEAD_DIM, NSA_HEADS * HEAD_DIM, NSA_KV_HEADS * HEAD_DIM, DIL_HEADS * HEAD_DIM
    widths = (mw, mw, mw, nq) + (nkv,) * 6 + (NSA_HEADS * 3, dw, dw, dw)
    offs = np.concatenate([[0], np.cumsum(widths)])
    qa, ka, va, qb, kbc, vbc, kbs, vbs, kbw, vbw, gb, qc, kc, vc = (
        w[:, int(offs[i]):int(offs[i + 1])] for i in range(len(widths)))

    def dup(t):
        t = t.reshape(D_MODEL, NSA_KV_HEADS, 1, HEAD_DIM)
        return jnp.broadcast_to(t, (D_MODEL, NSA_KV_HEADS, 2, HEAD_DIM)).reshape(D_MODEL, NSA_KV_HEADS * LANES)

    zpad = lambda n: jnp.zeros((D_MODEL, n * LANES), w.dtype)
    w_rot = jnp.concatenate([qa * SCALE, ka, qb * SCALE, dup(kbc), dup(kbs), dup(kbw), zpad(1)], axis=1)
    w_pl = jnp.concatenate([va, dup(vbc), dup(vbs), dup(vbw), zpad(3)], axis=1)
    gpad = jnp.zeros((D_MODEL, NSA_KV_HEADS, LANES - 12), w.dtype)
    w_gl = jnp.concatenate([gb.reshape(D_MODEL, NSA_KV_HEADS, 12), gpad], axis=-1).reshape(D_MODEL, -1)
    w_dil_rot = jnp.concatenate([qc * SCALE, kc], axis=1)
    return tuple(t.astype(BF16) for t in (w_rot, w_pl, w_gl, w_dil_rot, vc))


def _overlap_table():
    starts = np.arange(N_CMP) * CMP_STRIDE
    slc = np.arange(N_SLC) * SLC_BLOCK
    ov = ((starts[:, None] < slc[None, :] + SLC_BLOCK) & (starts[:, None] + CMP_LEN > slc[None, :]))
    ovt = np.zeros((N_SLC, N_CMP_PAD), np.float32)
    ovt[:, :N_CMP] = ov.T
    return jnp.asarray(ovt, BF16)


def _cmp_chunks(z, base):
    nblk = z.shape[-1] // LANES
    t = z.reshape(BATCH, SEQ // CMP_STRIDE, CMP_STRIDE, nblk, LANES)[:, :, :, base:base + NSA_KV_HEADS, :HEAD_DIM]
    return t.transpose(0, 3, 1, 2, 4).reshape(BATCH, NSA_KV_HEADS, SEQ // CMP_STRIDE, CMP_STRIDE * HEAD_DIM)


def kernel(x, p, positions, ln_in_g, ln_in_b, w_in, w_out, nsa_ck1, nsa_ck2, nsa_pe_k, nsa_cv1, nsa_cv2, nsa_pe_v, ln1_g, ln1_b, router_w, router_b, w_gate, w_up, w_down, ple_proj, ple_gate_w, ple_gate_b, ln2_g, ln2_b):
    rope = _rope_tables(positions)
    ovt = _overlap_table()
    rw_t = router_w.T.astype(BF16)
    rb = router_b.reshape(N_EXPERTS, 1).astype(F32)
    chunk_w = CMP_STRIDE * HEAD_DIM
    vec = lambda v: v.reshape(1, -1)
    seq3 = lambda t: t.reshape(BATCH, SEQ, t.shape[-1])
    flat = lambda t: t.reshape(TOKENS, t.shape[-1])

    h, hb = _ln_in(x.reshape(TOKENS, D_MODEL), ln_in_g, ln_in_b)
    for i in range(DEPTH):
        w_rot, w_pl, w_gl, w_dil_rot, w_dil_pl = _split_w_in(w_in[i])
        z_rot = seq3(_project(hb, w_rot, BF16, 768, rope=rope))
        z_pl = seq3(_project(hb, w_pl, BF16, 1024))
        gate_logits = _project(hb, w_gl, F32, NSA_KV_HEADS * LANES)
        zd_rot = seq3(_project(hb, w_dil_rot, F32, 768, rope=rope))
        zd_pl = seq3(_project(hb, w_dil_pl, F32, 768))

        o_a = _moba(z_rot, z_pl)

        dup2 = lambda w2: jnp.concatenate([w2, w2], axis=1).astype(BF16)
        k_cmp, v_cmp = _compress(
            _cmp_chunks(z_rot, ROT_NKC), _cmp_chunks(z_pl, PL_NVC),
            nsa_pe_k[i].reshape(2, chunk_w), nsa_pe_v[i].reshape(2, chunk_w),
            nsa_ck1[i].reshape(2, chunk_w, CMP_HIDDEN).astype(BF16), dup2(nsa_ck2[i]),
            nsa_cv1[i].reshape(2, chunk_w, CMP_HIDDEN).astype(BF16), dup2(nsa_cv2[i]))
        o_b = _nsa(z_rot, z_pl, k_cmp, v_cmp, gate_logits, ovt)

        o_c = _dilated(zd_rot, zd_pl)

        wo = w_out[i].astype(BF16)
        a_w, b_w = MOBA_HEADS * HEAD_DIM, NSA_HEADS * HEAD_DIM
        h, hb = _out_proj(flat(o_a), flat(o_b), flat(o_c), h,
                          wo[:a_w], wo[a_w:a_w + b_w], wo[a_w + b_w:], vec(ln1_g[i]), vec(ln1_b[i]))

        pos, w_tok, tile_expert, n_tiles = _routing_tables(*_router(hb, rw_t, rb))
        xs = _dispatch(pos, h)
        ys = _experts(tile_expert, n_tiles, xs, w_gate[i], w_up[i], w_down[i])
        h, hb = _ple_ln(pos, hb, h, ys, w_tok, p[i].reshape(TOKENS, PLE_DIM), ple_gate_w[i].astype(BF16),
                        vec(ple_gate_b[i]), ple_proj[i].astype(BF16), vec(ln2_g[i]), vec(ln2_b[i]))
    return h.reshape(BATCH, SEQ, D_MODEL)
```

```python
import functools
import math

import numpy as np
import jax
import jax.numpy as jnp
from jax import lax
from jax.experimental import pallas as pl
from jax.experimental.pallas import tpu as pltpu

F32 = jnp.float32
BF16 = jnp.bfloat16

D_MODEL = 2048
BATCH = 2
SEQ = 4096
DEPTH = 4
TOKENS = BATCH * SEQ
HEAD_DIM = 64
ROT_DIM = HEAD_DIM // 4
ROPE_THETA = 500000.0
NEG = -1e30
FORCE = 1e30
LN_EPS = 1e-5
SCALE = HEAD_DIM ** -0.5

MOBA_HEADS = 8
MOBA_BLOCK = 256
MOBA_TOPK = 3
MOBA_NB = SEQ // MOBA_BLOCK

NSA_HEADS = 12
NSA_KV_HEADS = 3
CMP_LEN = 32
CMP_STRIDE = 16
CMP_HIDDEN = 128
N_CMP = (SEQ - CMP_LEN) // CMP_STRIDE + 1
N_CMP_PAD = 256
SLC_BLOCK = 64
SLC_TOPK = 16
SLC_LOCAL = 2
N_SLC = SEQ // SLC_BLOCK
NSA_WINDOW = 512

DIL_CONFIGS = ((128, 1), (512, 4), (2048, 16))
DIL_HEADS_PER_GROUP = 4
DIL_HEADS = DIL_HEADS_PER_GROUP * len(DIL_CONFIGS)

N_EXPERTS = 16
N_GROUPS = 4
EXPERTS_PER_GROUP = 4
EXPERT_HIDDEN = D_MODEL // 4
PLE_DIM = 256

DEEPNORM_ALPHA = (2 * DEPTH) ** 0.25

LANES = 128
VMEM_LIMIT = 56 * 1024 * 1024

ROT_MQ, ROT_MK, ROT_NQ, ROT_NKC, ROT_NKS, ROT_NKW = 0, 4, 8, 14, 17, 20
ROT_BLOCKS = 24
PL_MV, PL_NVC, PL_NVS, PL_NVW = 0, 4, 7, 10
PL_BLOCKS = 16
DIL_BLOCKS = DIL_HEADS // 2

NT_DIMS = (((1,), (1,)), ((), ()))


def _nt(a, b):
    return lax.dot_general(a, b, NT_DIMS, preferred_element_type=F32)


def _nn(a, b):
    return jnp.dot(a, b, preferred_element_type=F32)


def _params(*sem):
    return pltpu.CompilerParams(dimension_semantics=sem, vmem_limit_bytes=VMEM_LIMIT)


def _layer_norm(y, g, b):
    mu = jnp.mean(y, axis=-1, keepdims=True)
    yc = y - mu
    var = jnp.mean(yc * yc, axis=-1, keepdims=True)
    return yc * lax.rsqrt(var + LN_EPS) * g + b


def _ln_kernel(x_ref, g_ref, b_ref, h_ref, hb_ref):
    h = _layer_norm(x_ref[...], g_ref[...], b_ref[...])
    h_ref[...] = h
    hb_ref[...] = h.astype(BF16)


def _ln_in(x, g, b):
    tm = 512
    row = pl.BlockSpec((tm, D_MODEL), lambda i: (i, 0))
    vec = pl.BlockSpec((1, D_MODEL), lambda i: (0, 0))
    return pl.pallas_call(
        _ln_kernel,
        out_shape=(jax.ShapeDtypeStruct((TOKENS, D_MODEL), F32),
                   jax.ShapeDtypeStruct((TOKENS, D_MODEL), BF16)),
        grid=(TOKENS // tm,),
        in_specs=[row, vec, vec],
        out_specs=(row, row),
        compiler_params=_params("parallel"),
        name="ln_in",
    )(x, g.reshape(1, -1), b.reshape(1, -1))


def _proj_kernel(x_ref, w_ref, o_ref):
    o_ref[...] = _nn(x_ref[...], w_ref[...]).astype(o_ref.dtype)


def _proj_rot_kernel(x_ref, w_ref, c_ref, s1_ref, s2_ref, o_ref):
    z = _nn(x_ref[...], w_ref[...])
    c, s1, s2 = c_ref[...], s1_ref[...], s2_ref[...]
    half = ROT_DIM // 2
    for j in range(z.shape[1] // LANES):
        zc = z[:, j * LANES:(j + 1) * LANES]
        r = zc * c + pltpu.roll(zc, LANES - half, 1) * s1 + pltpu.roll(zc, half, 1) * s2
        o_ref[:, j * LANES:(j + 1) * LANES] = r.astype(o_ref.dtype)


def _project(hb, w, out_dtype, tn, rope=None):
    tm = 1024
    n = w.shape[1]
    x_spec = pl.BlockSpec((tm, D_MODEL), lambda i, j: (i, 0))
    w_spec = pl.BlockSpec((D_MODEL, tn), lambda i, j: (0, j))
    o_spec = pl.BlockSpec((tm, tn), lambda i, j: (i, j))
    if rope is None:
        kern, extra, extra_specs = _proj_kernel, (), []
    else:
        t_spec = pl.BlockSpec((tm, LANES), lambda i, j: (i, 0))
        kern, extra, extra_specs = _proj_rot_kernel, rope, [t_spec] * 3
    return pl.pallas_call(
        kern,
        out_shape=jax.ShapeDtypeStruct((TOKENS, n), out_dtype),
        grid=(TOKENS // tm, n // tn),
        in_specs=[x_spec, w_spec] + extra_specs,
        out_specs=o_spec,
        compiler_params=_params("parallel", "arbitrary"),
        name="in_proj_rot" if rope is not None else "in_proj",
    )(hb, w, *extra)


def _stack_heads(*q_blocks):
    parts = []
    for q in q_blocks:
        lane = lax.broadcasted_iota(jnp.int32, q.shape, 1)
        zero = jnp.zeros_like(q)
        parts += [jnp.where(lane < HEAD_DIM, q, zero), jnp.where(lane >= HEAD_DIM, q, zero)]
    return jnp.concatenate(parts, axis=0)


def _merge_pair_t(lo, hi):
    sub = lax.broadcasted_iota(jnp.int32, lo.shape, 0)
    return jnp.where(sub < HEAD_DIM, lo, hi)


def _band_bias_t(nk, qc, offset, n_back):
    key = lax.broadcasted_iota(jnp.int32, (nk, qc), 0)
    qry = lax.broadcasted_iota(jnp.int32, (nk, qc), 1)
    diff = offset + qry - key
    return jnp.where((diff >= 0) & (diff <= n_back), 0.0, NEG)


def _tile_lanes(x, n):
    return jnp.concatenate([x] * n, axis=1)


def _transpose_bf16(v):
    return jnp.transpose(v.astype(F32)).astype(BF16)


def _tree(x, op):
    n = x.shape[0]
    if n == 8:
        return x
    if n % 16 == 0:
        return op(_tree(x[:n // 2], op), _tree(x[n // 2:], op))
    acc = x[:8]
    for i in range(1, n // 8):
        acc = op(acc, x[8 * i:8 * i + 8])
    return acc


def _reduce_keys(x, op, final):
    return final(_tree(x, op), axis=0, keepdims=True)


VT_ROWS = LANES + 16


def _transpose_aug(v):
    vt = jnp.transpose(v.astype(F32))
    sub = lax.broadcasted_iota(jnp.int32, (VT_ROWS - LANES, v.shape[0]), 0)
    return jnp.concatenate([vt, jnp.where(sub == 0, 1.0, 0.0)], axis=0).astype(BF16)


def _probs(s_t, m):
    return jnp.exp((s_t - m).astype(BF16))


def _normalise(acc):
    l = acc[LANES:LANES + 1]
    return acc[:LANES] / l, l


def _softmax_block_t(s_t, pv):
    m = _reduce_keys(s_t, jnp.maximum, jnp.max)
    out, l = _normalise(pv(_probs(s_t, m)))
    return out, m + jnp.log(l)


def _online_step_t(carry, s_t, m_t, pv):
    m, acc = carry
    m_new = jnp.maximum(m, m_t)
    acc = jnp.exp(m - m_new) * acc + pv(_probs(s_t, m_new))
    return m_new, acc


def _flash_tiles(n_tiles, last_tile, r, scores, pv_of):
    def produce(t):
        s_t = scores(t)
        return s_t, _reduce_keys(s_t, jnp.maximum, jnp.max)

    def body(t, carry):
        state, s_t, m_t = carry
        s_next, m_next = produce(jnp.minimum(t + 1, last_tile))
        return _online_step_t(state, s_t, m_t, pv_of(t)), s_next, m_next

    state, _, _ = lax.fori_loop(0, n_tiles, body, (_online_init_t(r),) + produce(0))
    return state


def _pv_tiles(vt_ref, first, n, rows):
    def pv(p):
        acc = _nn(vt_ref[first], p[:rows])
        for j in range(1, n):
            acc = acc + _nn(vt_ref[first + j], p[j * rows:(j + 1) * rows])
        return acc
    return pv


def _online_init_t(r):
    return (jnp.full((1, r), NEG, F32), jnp.zeros((VT_ROWS, r), F32))


def _rank_rows(g, n_rows):
    sub = lax.broadcasted_iota(jnp.int32, (8, g.shape[1]), 0)
    rank = jnp.zeros(g.shape, F32)
    for m in range(n_rows):
        gm = g[m:m + 1, :]
        b = m // 8 * 8
        mid = g[b:b + 8]
        parts = [jnp.where(gm > mid, 1.0, jnp.where((gm == mid) & (sub > m - b), 1.0, 0.0))]
        if b > 0:
            parts.insert(0, jnp.where(gm > g[:b], 1.0, 0.0))
        if b + 8 < n_rows:
            parts.append(jnp.where(gm >= g[b + 8:], 1.0, 0.0))
        rank = rank + jnp.concatenate(parts, axis=0)
    return rank


MOBA_QC = 256
MOBA_KT = 2 * MOBA_BLOCK


def _moba_kernel(q_ref, k_ref, v_ref, o_ref, kmean_ref, vt_ref, bias_ref):
    c = pl.program_id(2)
    qc = MOBA_QC
    r = 2 * qc

    @pl.when(c == 0)
    def _():
        row = lax.broadcasted_iota(jnp.int32, (MOBA_NB, SEQ), 0)
        col = lax.broadcasted_iota(jnp.int32, (MOBA_NB, SEQ), 1)
        avg = jnp.where((col >> 8) == row, 1.0 / MOBA_BLOCK, 0.0).astype(BF16)
        kmean_ref[...] = _nn(avg, k_ref[0])
        for t in range(MOBA_NB):
            vt_ref[t] = _transpose_aug(v_ref[0, t * MOBA_BLOCK:(t + 1) * MOBA_BLOCK, :])

    qs = _stack_heads(q_ref[0])

    gate = _nt(kmean_ref[...].astype(BF16), qs)
    blk = lax.broadcasted_iota(jnp.int32, gate.shape, 0)
    past = blk < c
    rank = _rank_rows(jnp.where(past, gate, NEG), MOBA_NB)
    bias_ref[...] = jnp.where(past & (rank < MOBA_TOPK), 0.0, NEG)

    per_tile = MOBA_KT // MOBA_BLOCK

    def scores(t):
        ks = pl.multiple_of(t * MOBA_KT, MOBA_KT)
        blocks = [jnp.broadcast_to(bias_ref[pl.ds(t * per_tile + j, 1), :], (MOBA_BLOCK, r))
                  for j in range(per_tile)]
        return _nt(k_ref[0, pl.ds(ks, MOBA_KT), :], qs) + jnp.concatenate(blocks, axis=0)

    state = _flash_tiles((c + per_tile - 1) // per_tile, SEQ // MOBA_KT - 1, r, scores,
                         lambda t: _pv_tiles(vt_ref, t * per_tile, per_tile, MOBA_BLOCK))

    ks = pl.multiple_of(c * MOBA_BLOCK, MOBA_BLOCK)
    causal = _band_bias_t(MOBA_BLOCK, qc, 0, MOBA_BLOCK)
    s_t = _nt(k_ref[0, pl.ds(ks, MOBA_BLOCK), :], qs) + _tile_lanes(causal, 2)
    _, acc = _online_step_t(state, s_t, _reduce_keys(s_t, jnp.maximum, jnp.max),
                            _pv_tiles(vt_ref, c, 1, MOBA_BLOCK))
    o_t, _ = _normalise(acc)
    o_ref[0] = jnp.transpose(_merge_pair_t(o_t[:, :qc], o_t[:, qc:])).astype(o_ref.dtype)


def _moba(z_rot, z_pl):
    qc = MOBA_QC
    grid = (BATCH, MOBA_HEADS // 2, SEQ // qc)
    return pl.pallas_call(
        _moba_kernel,
        out_shape=jax.ShapeDtypeStruct((BATCH, SEQ, MOBA_HEADS * HEAD_DIM), BF16),
        grid=grid,
        in_specs=[
            pl.BlockSpec((1, qc, LANES), lambda b, p, c: (b, c, ROT_MQ + p)),
            pl.BlockSpec((1, SEQ, LANES), lambda b, p, c: (b, 0, ROT_MK + p)),
            pl.BlockSpec((1, SEQ, LANES), lambda b, p, c: (b, 0, PL_MV + p)),
        ],
        out_specs=pl.BlockSpec((1, qc, LANES), lambda b, p, c: (b, c, p)),
        scratch_shapes=[pltpu.VMEM((MOBA_NB, LANES), F32),
                        pltpu.VMEM((MOBA_NB, VT_ROWS, MOBA_BLOCK), BF16),
                        pltpu.VMEM((MOBA_NB, 2 * qc), F32)],
        compiler_params=_params("parallel", "parallel", "arbitrary"),
        name="moba",
    )(z_rot, z_rot, z_pl)


def _compress_one(x_ref, pe_ref, w1_ref, w2_ref, o_ref):
    x = x_ref[0, 0].astype(F32)
    top = (x + pe_ref[0:1, :]).astype(BF16)
    bot = (x + pe_ref[1:2, :]).astype(BF16)
    a = _nn(top, w1_ref[0])
    bm = _nn(bot, w1_ref[1])
    pre = a + pltpu.roll(bm, N_CMP_PAD - 1, 0)
    hid = jax.nn.gelu(pre)
    out = _nn(hid.astype(BF16), w2_ref[...])
    row = lax.broadcasted_iota(jnp.int32, out.shape, 0)
    o_ref[0, 0] = jnp.where(row < N_CMP, out, 0.0).astype(o_ref.dtype)


def _compress_kernel(xk_ref, xv_ref, pk_ref, pv_ref, k1_ref, k2_ref, v1_ref, v2_ref, ok_ref, ov_ref):
    _compress_one(xk_ref, pk_ref, k1_ref, k2_ref, ok_ref)
    _compress_one(xv_ref, pv_ref, v1_ref, v2_ref, ov_ref)


def _compress(xk, xv, pk, pv, k1, k2, v1, v2):
    chunk_w = CMP_STRIDE * HEAD_DIM
    x_spec = pl.BlockSpec((1, 1, N_CMP_PAD, chunk_w), lambda b, j: (b, j, 0, 0))
    pe_spec = pl.BlockSpec((2, chunk_w), lambda b, j: (0, 0))
    w1_spec = pl.BlockSpec((2, chunk_w, CMP_HIDDEN), lambda b, j: (0, 0, 0))
    w2_spec = pl.BlockSpec((CMP_HIDDEN, LANES), lambda b, j: (0, 0))
    o_spec = pl.BlockSpec((1, 1, N_CMP_PAD, LANES), lambda b, j: (b, j, 0, 0))
    o_shape = jax.ShapeDtypeStruct((BATCH, NSA_KV_HEADS, N_CMP_PAD, LANES), BF16)
    return pl.pallas_call(
        _compress_kernel,
        out_shape=(o_shape, o_shape),
        grid=(BATCH, NSA_KV_HEADS),
        in_specs=[x_spec, x_spec, pe_spec, pe_spec, w1_spec, w2_spec, w1_spec, w2_spec],
        out_specs=(o_spec, o_spec),
        compiler_params=_params("parallel", "parallel"),
        name="nsa_compress",
    )(xk, xv, pk, pv, k1, k2, v1, v2)


NSA_QC = 256
NSA_KT = 512
NSA_G = NSA_HEADS // NSA_KV_HEADS
NSA_WIN_TILES = NSA_WINDOW // NSA_QC + 1


def _nsa_kernel(qa_ref, qb_ref, kc_ref, vc_ref, ks_ref, vs_ref, kw_ref, vw_ref, gl_ref, ovt_ref,
                o_ref, vct_ref, vst_ref, vwt_ref, bias_ref):
    c = pl.program_id(2)
    qc = NSA_QC
    q0 = c * qc
    lanes_of = lambda t, i: t[:, i * qc:(i + 1) * qc]

    @pl.when(c == 0)
    def _():
        vct_ref[...] = _transpose_bf16(vc_ref[0, 0])
        for t in range(SEQ // NSA_KT):
            vst_ref[t] = _transpose_aug(vs_ref[0, t * NSA_KT:(t + 1) * NSA_KT, :])
        for t in range(SEQ // qc):
            vwt_ref[t] = _transpose_aug(vw_ref[0, t * qc:(t + 1) * qc, :])

    qs = _stack_heads(qa_ref[0], qb_ref[0])

    sc_t = _nt(kc_ref[0, 0], qs)
    n_idx = lax.broadcasted_iota(jnp.int32, (N_CMP_PAD, qc), 0)
    q_idx = lax.broadcasted_iota(jnp.int32, (N_CMP_PAD, qc), 1)
    ok = (n_idx * CMP_STRIDE + (CMP_LEN - 1)) <= (q0 + q_idx)
    p_heads = []
    for i in range(NSA_G):
        s_i = jnp.where(ok, lanes_of(sc_t, i), NEG)
        e_i = jnp.where(ok, jnp.exp(s_i - _reduce_keys(s_i, jnp.maximum, jnp.max)), 0.0)
        l_i = _reduce_keys(e_i, jnp.add, jnp.sum)
        p_heads.append((e_i / jnp.where(l_i > 0.0, l_i, 1.0)).astype(BF16))
    p_ct = jnp.concatenate(p_heads, axis=1)
    ocmp_t = _nn(vct_ref[...], p_ct)

    imp4 = _nn(ovt_ref[...], p_ct)
    imp = lanes_of(imp4, 0) + lanes_of(imp4, 1) + lanes_of(imp4, 2) + lanes_of(imp4, 3)
    blk = lax.broadcasted_iota(jnp.int32, imp.shape, 0)
    cur = (q0 + lax.broadcasted_iota(jnp.int32, imp.shape, 1)) >> 6
    valid = blk <= cur
    forced = valid & ((blk == 0) | (blk > cur - SLC_LOCAL))
    rank = _rank_rows(jnp.where(forced, FORCE, jnp.where(valid, imp, NEG)), N_SLC)
    bias_ref[...] = jnp.where(valid & (rank < SLC_TOPK), 0.0, NEG)

    key_row = lax.broadcasted_iota(jnp.int32, (NSA_KT, qc), 0)
    qpos = lax.broadcasted_iota(jnp.int32, (NSA_KT, qc), 1) + q0
    per_tile = NSA_KT // SLC_BLOCK

    def scores(t):
        ks0 = pl.multiple_of(t * NSA_KT, NSA_KT)
        blocks = [jnp.broadcast_to(bias_ref[pl.ds(t * per_tile + j, 1), :], (SLC_BLOCK, qc))
                  for j in range(per_tile)]
        bias = jnp.where(key_row + ks0 <= qpos, jnp.concatenate(blocks, axis=0), NEG)
        return _nt(ks_ref[0, pl.ds(ks0, NSA_KT), :], qs) + _tile_lanes(bias, NSA_G)

    _, acc_s = _flash_tiles(c // (NSA_KT // qc) + 1, SEQ // NSA_KT - 1, NSA_G * qc, scores,
                            lambda t: _pv_tiles(vst_ref, t, 1, NSA_KT))
    oslc_t, _ = _normalise(acc_s)

    t0 = jnp.maximum(c - NSA_WINDOW // qc, 0)
    start = pl.multiple_of(t0 * qc, qc)
    band = _band_bias_t(NSA_WIN_TILES * qc, qc, q0 - start, NSA_WINDOW - 1)
    sw_t = _nt(kw_ref[0, pl.ds(start, NSA_WIN_TILES * qc), :], qs) + _tile_lanes(band, NSA_G)

    owin_t, _ = _softmax_block_t(sw_t, _pv_tiles(vwt_ref, t0, NSA_WIN_TILES, qc))

    gate_t = jnp.transpose(jax.nn.sigmoid(gl_ref[...]))
    outs = []
    for i in range(NSA_G):
        outs.append(gate_t[3 * i:3 * i + 1, :] * lanes_of(ocmp_t, i)
                    + gate_t[3 * i + 1:3 * i + 2, :] * lanes_of(oslc_t, i)
                    + gate_t[3 * i + 2:3 * i + 3, :] * lanes_of(owin_t, i))
    o_ref[0, :, 0:LANES] = jnp.transpose(_merge_pair_t(outs[0], outs[1])).astype(o_ref.dtype)
    o_ref[0, :, LANES:2 * LANES] = jnp.transpose(_merge_pair_t(outs[2], outs[3])).astype(o_ref.dtype)


def _nsa(z_rot, z_pl, k_cmp, v_cmp, gate_logits, ovt):
    qc = NSA_QC
    seq_spec = lambda base: pl.BlockSpec((1, SEQ, LANES), lambda b, j, c: (b, 0, base + j))
    cmp_spec = pl.BlockSpec((1, 1, N_CMP_PAD, LANES), lambda b, j, c: (b, j, 0, 0))
    return pl.pallas_call(
        _nsa_kernel,
        out_shape=jax.ShapeDtypeStruct((BATCH, SEQ, NSA_HEADS * HEAD_DIM), BF16),
        grid=(BATCH, NSA_KV_HEADS, SEQ // qc),
        in_specs=[
            pl.BlockSpec((1, qc, LANES), lambda b, j, c: (b, c, ROT_NQ + 2 * j)),
            pl.BlockSpec((1, qc, LANES), lambda b, j, c: (b, c, ROT_NQ + 2 * j + 1)),
            cmp_spec, cmp_spec,
            seq_spec(ROT_NKS), seq_spec(PL_NVS), seq_spec(ROT_NKW), seq_spec(PL_NVW),
            pl.BlockSpec((qc, LANES), lambda b, j, c: (b * (SEQ // qc) + c, j)),
            pl.BlockSpec(ovt.shape, lambda b, j, c: (0, 0)),
        ],
        out_specs=pl.BlockSpec((1, qc, 2 * LANES), lambda b, j, c: (b, c, j)),
        scratch_shapes=[pltpu.VMEM((LANES, N_CMP_PAD), BF16),
                        pltpu.VMEM((SEQ // NSA_KT, VT_ROWS, NSA_KT), BF16),
                        pltpu.VMEM((SEQ // qc, VT_ROWS, qc), BF16),
                        pltpu.VMEM((N_SLC, qc), F32)],
        compiler_params=_params("parallel", "parallel", "arbitrary"),
        name="nsa",
    )(z_rot, z_rot, k_cmp, v_cmp, z_rot, z_pl, z_rot, z_pl, gate_logits, ovt)


DIL_QC = 128
DIL_STEPS = SEQ // DIL_QC
DIL_UNROLL = 4


def _dil_group(q_ref, k_ref, v_ref, og_ref, lg_ref, gi):
    window, dil = DIL_CONFIGS[gi]
    qc = DIL_QC
    m = SEQ // dil
    n_back = window // dil
    nk = min(m, qc + -(-n_back // qc) * qc)
    chunks = m // qc

    def rows(first, n):
        return pl.ds(first, n) if dil == 1 else pl.ds(first, n, stride=dil)

    def block(idx):
        r = idx // chunks
        q0 = (idx % chunks) * qc
        start = jnp.maximum(q0 - (nk - qc), 0)
        q_rows = rows(r + dil * q0, qc)
        k_rows = rows(r + dil * start, nk)
        qs = _stack_heads(q_ref[0, q_rows, :].astype(BF16))
        v_t = _transpose_aug(v_ref[0, k_rows, :])
        band = _band_bias_t(nk, qc, q0 - start, n_back)
        s_t = _nt(k_ref[0, k_rows, :].astype(BF16), qs) + _tile_lanes(band, 2)
        o_t, lse = _softmax_block_t(s_t, lambda p: _nn(v_t, p))
        lse_b = jnp.broadcast_to(lse, (LANES, 2 * qc))
        og_ref[gi, q_rows, :] = jnp.transpose(_merge_pair_t(o_t[:, :qc], o_t[:, qc:]))
        lg_ref[gi, q_rows, :] = jnp.transpose(_merge_pair_t(lse_b[:, :qc], lse_b[:, qc:]))

    def body(i, _):
        for u in range(DIL_UNROLL):
            block(i * DIL_UNROLL + u)
        return 0

    lax.fori_loop(0, DIL_STEPS // DIL_UNROLL, body, 0)


def _dil_kernel(q_ref, k_ref, v_ref, o_ref, og_ref, lg_ref):
    g = pl.program_id(2)
    n_groups = len(DIL_CONFIGS)
    for gi in range(n_groups):
        pl.when(g == gi)(functools.partial(_dil_group, q_ref, k_ref, v_ref, og_ref, lg_ref, gi))

    @pl.when(g == n_groups - 1)
    def _():
        rows = 512

        def body(i, _):
            sl = pl.ds(pl.multiple_of(i * rows, rows), rows)
            l0, l1, l2 = lg_ref[0, sl, :], lg_ref[1, sl, :], lg_ref[2, sl, :]
            mx = jnp.maximum(jnp.maximum(l0, l1), l2)
            e0, e1, e2 = jnp.exp(l0 - mx), jnp.exp(l1 - mx), jnp.exp(l2 - mx)
            den = e0 + e1 + e2
            out = (e0 / den) * og_ref[0, sl, :] + (e1 / den) * og_ref[1, sl, :] + (e2 / den) * og_ref[2, sl, :]
            o_ref[0, sl, :] = out.astype(o_ref.dtype)
            return 0

        lax.fori_loop(0, SEQ // rows, body, 0)


def _dilated(zd_rot, zd_pl):
    n_groups = len(DIL_CONFIGS)
    width = DIL_HEADS_PER_GROUP * HEAD_DIM
    col = lambda base: (lambda b, p, g: (b, 0, base + 2 * g + p))
    blk = lambda base: pl.BlockSpec((1, SEQ, LANES), col(base))
    return pl.pallas_call(
        _dil_kernel,
        out_shape=jax.ShapeDtypeStruct((BATCH, SEQ, width), BF16),
        grid=(BATCH, 2, n_groups),
        in_specs=[blk(0), blk(DIL_BLOCKS), blk(0)],
        out_specs=pl.BlockSpec((1, SEQ, LANES), lambda b, p, g: (b, 0, p)),
        scratch_shapes=[pltpu.VMEM((n_groups, SEQ, LANES), F32), pltpu.VMEM((n_groups, SEQ, LANES), F32)],
        compiler_params=_params("parallel", "parallel", "arbitrary"),
        name="dilated",
    )(zd_rot, zd_rot, zd_pl)


def _out_proj_kernel(oa_ref, ob_ref, oc_ref, h_ref, wa_ref, wb_ref, wc_ref, g_ref, b_ref, h1_ref, h1b_ref):
    y = _nn(oa_ref[...], wa_ref[...]) + _nn(ob_ref[...], wb_ref[...]) + _nn(oc_ref[...], wc_ref[...])
    h1 = _layer_norm(DEEPNORM_ALPHA * h_ref[...] + y, g_ref[...], b_ref[...])
    h1_ref[...] = h1
    h1b_ref[...] = h1.astype(BF16)


def _out_proj(oa, ob, oc, h, wa, wb, wc, g, b):
    tm = 512
    rows = lambda w: pl.BlockSpec((tm, w), lambda i: (i, 0))
    full = lambda a: pl.BlockSpec(a.shape, lambda i: (0, 0))
    return pl.pallas_call(
        _out_proj_kernel,
        out_shape=(jax.ShapeDtypeStruct((TOKENS, D_MODEL), F32),
                   jax.ShapeDtypeStruct((TOKENS, D_MODEL), BF16)),
        grid=(TOKENS // tm,),
        in_specs=[rows(oa.shape[1]), rows(ob.shape[1]), rows(oc.shape[1]), rows(D_MODEL),
                  full(wa), full(wb), full(wc), full(g), full(b)],
        out_specs=(rows(D_MODEL), rows(D_MODEL)),
        compiler_params=_params("parallel"),
        name="out_proj_ln",
    )(oa, ob, oc, h, wa, wb, wc, g, b)


def _router_kernel(hb_ref, rw_ref, rb_ref, comb_ref, sel_ref):
    logits = _nt(rw_ref[...], hb_ref[...]) + rb_ref[...]
    mx = jnp.max(logits, axis=0, keepdims=True)
    ex = jnp.exp(logits - mx)
    probs = ex / jnp.sum(ex, axis=0, keepdims=True)
    p = [probs[e:e + 1, :] for e in range(N_EXPERTS)]
    best, g_sel = None, None
    for g in range(N_GROUPS):
        a, b, c, d = p[4 * g:4 * g + 4]
        hi1, lo1, hi2, lo2 = jnp.maximum(a, b), jnp.minimum(a, b), jnp.maximum(c, d), jnp.minimum(c, d)
        top2 = jnp.maximum(hi1, hi2) + jnp.maximum(jnp.minimum(hi1, hi2), jnp.maximum(lo1, lo2))
        if g == 0:
            best, g_sel = top2, jnp.zeros_like(top2)
        else:
            better = top2 > best
            best = jnp.where(better, top2, best)
            g_sel = jnp.where(better, float(g), g_sel)
    chosen, picked = [], []
    for e in range(N_EXPERTS):
        g = e // EXPERTS_PER_GROUP
        rank = jnp.zeros_like(best)
        for o in range(4 * g, 4 * g + 4):
            if o < e:
                rank = rank + jnp.where(p[o] >= p[e], 1.0, 0.0)
            elif o > e:
                rank = rank + jnp.where(p[o] > p[e], 1.0, 0.0)
        chosen.append(jnp.where((g_sel == float(g)) & (rank < 2.0), 1.0, 0.0))
        picked.append(chosen[e] * p[e])
    total = picked[0]
    for e in range(1, N_EXPERTS):
        total = total + picked[e]
    comb_ref[...] = jnp.concatenate(picked, axis=0) / total
    sel_ref[...] = jnp.concatenate(chosen, axis=0)


def _router(hb, rw_t, rb):
    tm = 1024
    out = jax.ShapeDtypeStruct((N_EXPERTS, TOKENS), F32)
    o_spec = pl.BlockSpec((N_EXPERTS, tm), lambda i: (0, i))
    return pl.pallas_call(
        _router_kernel,
        out_shape=(out, out),
        grid=(TOKENS // tm,),
        in_specs=[pl.BlockSpec((tm, D_MODEL), lambda i: (i, 0)),
                  pl.BlockSpec((N_EXPERTS, D_MODEL), lambda i: (0, 0)),
                  pl.BlockSpec((N_EXPERTS, 1), lambda i: (0, 0))],
        out_specs=(o_spec, o_spec),
        compiler_params=_params("parallel"),
        name="router",
    )(hb, rw_t, rb)


def _routing_tables(comb_t, sel_t):
    sel = sel_t > 0.5
    cnt = jnp.sum(sel, axis=1, dtype=jnp.int32)
    cnt_pad = (cnt + (MOE_TILE - 1)) // MOE_TILE * MOE_TILE
    ends = jnp.cumsum(cnt_pad)
    rank = jnp.cumsum(sel.astype(jnp.int32), axis=1) - 1
    pos = (ends - cnt_pad)[:, None] + rank
    pos_lo = jnp.min(jnp.where(sel, pos, MOE_ROWS), axis=0)
    pos_hi = jnp.max(jnp.where(sel, pos, -1), axis=0)
    w_lo = jnp.sum(jnp.where(sel & (pos == pos_lo), comb_t, 0.0), axis=0)
    w_hi = jnp.sum(jnp.where(sel & (pos == pos_hi), comb_t, 0.0), axis=0)
    w = jnp.zeros((TOKENS, LANES), F32).at[:, 0].set(w_lo).at[:, 1].set(w_hi)
    n_tiles = ends[-1] // MOE_TILE
    tile_start = jnp.arange(MOE_TILES, dtype=jnp.int32) * MOE_TILE
    tile_start = jnp.minimum(tile_start, ends[-1] - MOE_TILE)
    tile_expert = jnp.sum((ends[None, :] <= tile_start[:, None]).astype(jnp.int32), axis=1)
    return jnp.stack([pos_lo, pos_hi]).astype(jnp.int32), w, tile_expert, n_tiles.reshape(1).astype(jnp.int32)


MOE_TILE = 256
MOE_TILES = 2 * TOKENS // MOE_TILE + N_EXPERTS
MOE_ROWS = MOE_TILES * MOE_TILE
SLAB = D_MODEL // LANES


def _to_slabs(ref, x, rows):
    for j in range(SLAB):
        ref[pl.ds(j, rows, stride=SLAB), :] = x[:, j * LANES:(j + 1) * LANES]


def _from_slabs(ref, rows):
    return jnp.concatenate([ref[pl.ds(j, rows, stride=SLAB), :] for j in range(SLAB)], axis=1)


def _slab_rows(row):
    return pl.ds(pl.multiple_of(row * SLAB, SLAB), SLAB)


def _dispatch_kernel(pos_ref, h_ref, init_ref, xs_ref, slab_ref, sem):
    del init_ref
    tm = h_ref.shape[0]
    base = pl.program_id(0) * tm
    _to_slabs(slab_ref, h_ref[...], tm)

    def copy(t, which):
        return pltpu.make_async_copy(slab_ref.at[_slab_rows(t), :],
                                     xs_ref.at[_slab_rows(pos_ref[which, base + t]), :], sem)

    def start(t, _):
        copy(t, 0).start()
        copy(t, 1).start()
        return 0

    lax.fori_loop(0, tm, start, 0, unroll=8)
    whole = pltpu.make_async_copy(slab_ref, xs_ref.at[pl.ds(0, tm * SLAB), :], sem)
    whole.wait()
    whole.wait()


def _dispatch(pos, h):
    tm = 256
    grid_spec = pltpu.PrefetchScalarGridSpec(
        num_scalar_prefetch=1,
        grid=(TOKENS // tm,),
        in_specs=[pl.BlockSpec((tm, D_MODEL), lambda i, pos: (i, 0)),
                  pl.BlockSpec(memory_space=pl.ANY)],
        out_specs=pl.BlockSpec(memory_space=pl.ANY),
        scratch_shapes=[pltpu.VMEM((tm * SLAB, LANES), F32), pltpu.SemaphoreType.DMA],
    )
    return pl.pallas_call(
        _dispatch_kernel,
        out_shape=jax.ShapeDtypeStruct((MOE_ROWS * SLAB, LANES), F32),
        grid_spec=grid_spec,
        input_output_aliases={2: 0},
        compiler_params=_params("arbitrary"),
        name="moe_dispatch",
    )(pos, h, jnp.zeros((MOE_ROWS * SLAB, LANES), F32))


def _experts_kernel(te_ref, nt_ref, xs_ref, wg_ref, wu_ref, wd_ref, ys_ref, wgb_ref, wub_ref, wdb_ref):
    k = pl.program_id(0)
    e = te_ref[k]
    e_prev = te_ref[jnp.maximum(k - 1, 0)]

    @pl.when((k == 0) | (e != e_prev))
    def _():
        wgb_ref[...] = wg_ref[0, 0].astype(BF16)
        wub_ref[...] = wu_ref[0, 0].astype(BF16)
        wdb_ref[...] = wd_ref[0, 0].astype(BF16)

    @pl.when(k < nt_ref[0])
    def _():
        x = _from_slabs(xs_ref, MOE_TILE).astype(BF16)
        hid = jax.nn.silu(_nn(x, wgb_ref[...])) * _nn(x, wub_ref[...])
        _to_slabs(ys_ref, _nn(hid.astype(BF16), wdb_ref[...]), MOE_TILE)

    @pl.when(k >= nt_ref[0])
    def _():
        ys_ref[...] = jnp.zeros(ys_ref.shape, F32)


def _experts(tile_expert, n_tiles, xs, wg, wu, wd, layer):
    w_in_spec = pl.BlockSpec((1, 1, D_MODEL, EXPERT_HIDDEN), lambda k, te, nt: (layer, te[k], 0, 0))
    grid_spec = pltpu.PrefetchScalarGridSpec(
        num_scalar_prefetch=2,
        grid=(MOE_TILES,),
        in_specs=[pl.BlockSpec((MOE_TILE * SLAB, LANES), lambda k, te, nt: (jnp.minimum(k, nt[0] - 1), 0)),
                  w_in_spec, w_in_spec,
                  pl.BlockSpec((1, 1, EXPERT_HIDDEN, D_MODEL), lambda k, te, nt: (layer, te[k], 0, 0))],
        out_specs=pl.BlockSpec((MOE_TILE * SLAB, LANES), lambda k, te, nt: (k, 0)),
        scratch_shapes=[pltpu.VMEM((D_MODEL, EXPERT_HIDDEN), BF16), pltpu.VMEM((D_MODEL, EXPERT_HIDDEN), BF16),
                        pltpu.VMEM((EXPERT_HIDDEN, D_MODEL), BF16)],
    )
    return pl.pallas_call(
        _experts_kernel,
        out_shape=jax.ShapeDtypeStruct((MOE_ROWS * SLAB, LANES), F32),
        grid_spec=grid_spec,
        compiler_params=_params("arbitrary"),
        name="moe_experts",
    )(tile_expert, n_tiles, xs, wg, wu, wd)


def _ple_ln_kernel(pos_ref, hb_ref, h_ref, ys_ref, w_ref, p_ref, gw_ref, gb_ref, pw_ref, g_ref, b_ref,
                   h2_ref, h2b_ref, lo_ref, hi_ref, sem):
    tm = h_ref.shape[0]
    i = pl.program_id(0)
    slot = i & 1
    bufs = (lo_ref, hi_ref)

    def fetch(tile, into):
        def start(t, _):
            for which in range(2):
                pltpu.make_async_copy(ys_ref.at[_slab_rows(pos_ref[which, tile * tm + t]), :],
                                      bufs[which].at[into, _slab_rows(t), :], sem.at[into]).start()
            return 0
        lax.fori_loop(0, tm, start, 0, unroll=8)

    @pl.when(i == 0)
    def _():
        fetch(0, 0)

    @pl.when(i + 1 < pl.num_programs(0))
    def _():
        fetch(i + 1, 1 - slot)

    gate = jax.nn.sigmoid(_nn(hb_ref[...], gw_ref[...]) + gb_ref[...])
    ple = gate * _nn(p_ref[...].astype(BF16), pw_ref[...])
    for which in range(2):
        pltpu.make_async_copy(ys_ref.at[pl.ds(0, tm * SLAB), :], bufs[which].at[slot], sem.at[slot]).wait()
    w = w_ref[...]
    ffn = w[:, 0:1] * _from_slabs(lo_ref.at[slot], tm) + w[:, 1:2] * _from_slabs(hi_ref.at[slot], tm)
    h2 = _layer_norm(DEEPNORM_ALPHA * h_ref[...] + ffn + ple, g_ref[...], b_ref[...])
    h2_ref[...] = h2
    h2b_ref[...] = h2.astype(BF16)


def _ple_ln(pos, hb, h, ys, w, p, layer, gw, gb, pw, g, b):
    tm = 256
    p_spec = pl.BlockSpec((tm, PLE_DIM), lambda i, pos: (layer * (TOKENS // tm) + i, 0))
    rows = lambda width: pl.BlockSpec((tm, width), lambda i, pos: (i, 0))
    full = lambda a: pl.BlockSpec(a.shape, lambda i, pos: (0, 0))
    grid_spec = pltpu.PrefetchScalarGridSpec(
        num_scalar_prefetch=1,
        grid=(TOKENS // tm,),
        in_specs=[rows(D_MODEL), rows(D_MODEL), pl.BlockSpec(memory_space=pl.ANY), rows(LANES), p_spec,
                  full(gw), full(gb), full(pw), full(g), full(b)],
        out_specs=(rows(D_MODEL), rows(D_MODEL)),
        scratch_shapes=[pltpu.VMEM((2, tm * SLAB, LANES), F32), pltpu.VMEM((2, tm * SLAB, LANES), F32),
                        pltpu.SemaphoreType.DMA((2,))],
    )
    return pl.pallas_call(
        _ple_ln_kernel,
        out_shape=(jax.ShapeDtypeStruct((TOKENS, D_MODEL), F32),
                   jax.ShapeDtypeStruct((TOKENS, D_MODEL), BF16)),
        grid_spec=grid_spec,
        compiler_params=_params("arbitrary"),
        name="ple_ln",
    )(pos, hb, h, ys, w, p, gw, gb, pw, g, b)


def _rope_tables(positions):
    half = ROT_DIM // 2
    inv_freq = jnp.exp(jnp.arange(half, dtype=F32) * (-2.0 * math.log(ROPE_THETA) / ROT_DIM))
    ang = positions.astype(F32)[:, :, None] * inv_freq
    cos, sin = jnp.cos(ang), jnp.sin(ang)
    zeros = jnp.zeros_like(cos)
    rest = HEAD_DIM - ROT_DIM
    pad = lambda v: jnp.broadcast_to(jnp.asarray(v, F32), cos.shape[:2] + (rest,))
    c = jnp.concatenate([cos, cos, pad(1.0)], axis=-1)
    s1 = jnp.concatenate([-sin, zeros, pad(0.0)], axis=-1)
    s2 = jnp.concatenate([zeros, sin, pad(0.0)], axis=-1)
    tile = lambda t: jnp.concatenate([t, t], axis=-1).reshape(TOKENS, LANES)
    return tile(c), tile(s1), tile(s2)


def _split_w_in(w):
    mw, nq, nkv, dw = MOBA_HEADS * HEAD_DIM, NSA_HEADS * HEAD_DIM, NSA_KV_HEADS * HEAD_DIM, DIL_HEADS * HEAD_DIM
    widths = (mw, mw, mw, nq) + (nkv,) * 6 + (NSA_HEADS * 3, dw, dw, dw)
    offs = np.concatenate([[0], np.cumsum(widths)])
    qa, ka, va, qb, kbc, vbc, kbs, vbs, kbw, vbw, gb, qc, kc, vc = (
        w[:, int(offs[i]):int(offs[i + 1])] for i in range(len(widths)))

    def dup(t):
        t = t.reshape(D_MODEL, NSA_KV_HEADS, 1, HEAD_DIM)
        return jnp.broadcast_to(t, (D_MODEL, NSA_KV_HEADS, 2, HEAD_DIM)).reshape(D_MODEL, NSA_KV_HEADS * LANES)

    zpad = lambda n: jnp.zeros((D_MODEL, n * LANES), w.dtype)
    w_rot = jnp.concatenate([qa * SCALE, ka, qb * SCALE, dup(kbc), dup(kbs), dup(kbw), zpad(1)], axis=1)
    w_pl = jnp.concatenate([va, dup(vbc), dup(vbs), dup(vbw), zpad(3)], axis=1)
    gpad = jnp.zeros((D_MODEL, NSA_KV_HEADS, LANES - 12), w.dtype)
    w_gl = jnp.concatenate([gb.reshape(D_MODEL, NSA_KV_HEADS, 12), gpad], axis=-1).reshape(D_MODEL, -1)
    w_dil_rot = jnp.concatenate([qc * SCALE, kc], axis=1)
    return tuple(t.astype(BF16) for t in (w_rot, w_pl, w_gl, w_dil_rot, vc))


def _overlap_table():
    starts = np.arange(N_CMP) * CMP_STRIDE
    slc = np.arange(N_SLC) * SLC_BLOCK
    ov = ((starts[:, None] < slc[None, :] + SLC_BLOCK) & (starts[:, None] + CMP_LEN > slc[None, :]))
    ovt = np.zeros((N_SLC, N_CMP_PAD), np.float32)
    ovt[:, :N_CMP] = ov.T
    return jnp.asarray(ovt, BF16)


def _cmp_chunks(z, base):
    nblk = z.shape[-1] // LANES
    t = z.reshape(BATCH, SEQ // CMP_STRIDE, CMP_STRIDE, nblk, LANES)[:, :, :, base:base + NSA_KV_HEADS, :HEAD_DIM]
    return t.transpose(0, 3, 1, 2, 4).reshape(BATCH, NSA_KV_HEADS, SEQ // CMP_STRIDE, CMP_STRIDE * HEAD_DIM)


def kernel(x, p, positions, ln_in_g, ln_in_b, w_in, w_out, nsa_ck1, nsa_ck2, nsa_pe_k, nsa_cv1, nsa_cv2, nsa_pe_v, ln1_g, ln1_b, router_w, router_b, w_gate, w_up, w_down, ple_proj, ple_gate_w, ple_gate_b, ln2_g, ln2_b):
    rope = _rope_tables(positions)
    ovt = _overlap_table()
    rw_t = router_w.T.astype(BF16)
    rb = router_b.reshape(N_EXPERTS, 1).astype(F32)
    chunk_w = CMP_STRIDE * HEAD_DIM
    vec = lambda v: v.reshape(1, -1)
    seq3 = lambda t: t.reshape(BATCH, SEQ, t.shape[-1])
    flat = lambda t: t.reshape(TOKENS, t.shape[-1])

    h, hb = _ln_in(x.reshape(TOKENS, D_MODEL), ln_in_g, ln_in_b)
    for i in range(DEPTH):
        w_rot, w_pl, w_gl, w_dil_rot, w_dil_pl = _split_w_in(w_in[i])
        z_rot = seq3(_project(hb, w_rot, BF16, 768, rope=rope))
        z_pl = seq3(_project(hb, w_pl, BF16, 1024))
        gate_logits = _project(hb, w_gl, F32, NSA_KV_HEADS * LANES)
        zd_rot = seq3(_project(hb, w_dil_rot, F32, 768, rope=rope))
        zd_pl = seq3(_project(hb, w_dil_pl, F32, 768))

        o_a = _moba(z_rot, z_pl)

        dup2 = lambda w2: jnp.concatenate([w2, w2], axis=1).astype(BF16)
        k_cmp, v_cmp = _compress(
            _cmp_chunks(z_rot, ROT_NKC), _cmp_chunks(z_pl, PL_NVC),
            nsa_pe_k[i].reshape(2, chunk_w), nsa_pe_v[i].reshape(2, chunk_w),
            nsa_ck1[i].reshape(2, chunk_w, CMP_HIDDEN).astype(BF16), dup2(nsa_ck2[i]),
            nsa_cv1[i].reshape(2, chunk_w, CMP_HIDDEN).astype(BF16), dup2(nsa_cv2[i]))
        o_b = _nsa(z_rot, z_pl, k_cmp, v_cmp, gate_logits, ovt)

        o_c = _dilated(zd_rot, zd_pl)

        wo = w_out[i].astype(BF16)
        a_w, b_w = MOBA_HEADS * HEAD_DIM, NSA_HEADS * HEAD_DIM
        h, hb = _out_proj(flat(o_a), flat(o_b), flat(o_c), h,
                          wo[:a_w], wo[a_w:a_w + b_w], wo[a_w + b_w:], vec(ln1_g[i]), vec(ln1_b[i]))

        pos, w_tok, tile_expert, n_tiles = _routing_tables(*_router(hb, rw_t, rb))
        xs = _dispatch(pos, h)
        ys = _experts(tile_expert, n_tiles, xs, w_gate, w_up, w_down, i)
        h, hb = _ple_ln(pos, hb, h, ys, w_tok, p.reshape(DEPTH * TOKENS, PLE_DIM), i, ple_gate_w[i].astype(BF16),
                        vec(ple_gate_b[i]), ple_proj[i].astype(BF16), vec(ln2_g[i]), vec(ln2_b[i]))
    return h.reshape(BATCH, SEQ, D_MODEL)
```

```python
import functools
import math

import numpy as np
import jax
import jax.numpy as jnp
from jax import lax
from jax.experimental import pallas as pl
from jax.experimental.pallas import tpu as pltpu

F32 = jnp.float32
BF16 = jnp.bfloat16

D_MODEL = 2048
BATCH = 2
SEQ = 4096
DEPTH = 4
TOKENS = BATCH * SEQ
HEAD_DIM = 64
ROT_DIM = HEAD_DIM // 4
ROPE_THETA = 500000.0
NEG = -1e30
FORCE = 1e30
LN_EPS = 1e-5
SCALE = HEAD_DIM ** -0.5
LOG2_E = math.log2(math.e)
Q_SCALE = SCALE * LOG2_E

MOBA_HEADS = 8
MOBA_BLOCK = 256
MOBA_TOPK = 3
MOBA_NB = SEQ // MOBA_BLOCK

NSA_HEADS = 12
NSA_KV_HEADS = 3
CMP_LEN = 32
CMP_STRIDE = 16
CMP_HIDDEN = 128
N_CMP = (SEQ - CMP_LEN) // CMP_STRIDE + 1
N_CMP_PAD = 256
SLC_BLOCK = 64
SLC_TOPK = 16
SLC_LOCAL = 2
N_SLC = SEQ // SLC_BLOCK
NSA_WINDOW = 512

DIL_CONFIGS = ((128, 1), (512, 4), (2048, 16))
DIL_HEADS_PER_GROUP = 4
DIL_HEADS = DIL_HEADS_PER_GROUP * len(DIL_CONFIGS)

N_EXPERTS = 16
N_GROUPS = 4
EXPERTS_PER_GROUP = 4
EXPERT_HIDDEN = D_MODEL // 4
PLE_DIM = 256

DEEPNORM_ALPHA = (2 * DEPTH) ** 0.25

LANES = 128
VMEM_LIMIT = 56 * 1024 * 1024

ROT_MQ, ROT_MK, ROT_NQ, ROT_NKC, ROT_NKS, ROT_NKW = 0, 4, 8, 14, 17, 20
ROT_BLOCKS = 24
PL_MV, PL_NVC, PL_NVS, PL_NVW = 0, 4, 7, 10
PL_BLOCKS = 16
DIL_BLOCKS = DIL_HEADS // 2

NT_DIMS = (((1,), (1,)), ((), ()))


def _nt(a, b):
    return lax.dot_general(a, b, NT_DIMS, preferred_element_type=F32)


def _nn(a, b):
    return jnp.dot(a, b, preferred_element_type=F32)


def _params(*sem):
    return pltpu.CompilerParams(dimension_semantics=sem, vmem_limit_bytes=VMEM_LIMIT)


def _layer_norm(y, g, b):
    mu = jnp.mean(y, axis=-1, keepdims=True)
    yc = y - mu
    var = jnp.mean(yc * yc, axis=-1, keepdims=True)
    return yc * lax.rsqrt(var + LN_EPS) * g + b


def _ln_kernel(x_ref, g_ref, b_ref, h_ref, hb_ref):
    h = _layer_norm(x_ref[...], g_ref[...], b_ref[...])
    h_ref[...] = h
    hb_ref[...] = h.astype(BF16)


def _ln_in(x, g, b):
    tm = 512
    row = pl.BlockSpec((tm, D_MODEL), lambda i: (i, 0))
    vec = pl.BlockSpec((1, D_MODEL), lambda i: (0, 0))
    return pl.pallas_call(
        _ln_kernel,
        out_shape=(jax.ShapeDtypeStruct((TOKENS, D_MODEL), F32),
                   jax.ShapeDtypeStruct((TOKENS, D_MODEL), BF16)),
        grid=(TOKENS // tm,),
        in_specs=[row, vec, vec],
        out_specs=(row, row),
        compiler_params=_params("parallel"),
        name="ln_in",
    )(x, g.reshape(1, -1), b.reshape(1, -1))


def _proj_kernel(x_ref, w_ref, o_ref):
    o_ref[...] = _nn(x_ref[...], w_ref[...]).astype(o_ref.dtype)


def _proj_rot_kernel(x_ref, w_ref, c_ref, s1_ref, s2_ref, o_ref):
    z = _nn(x_ref[...], w_ref[...])
    c, s1, s2 = c_ref[...], s1_ref[...], s2_ref[...]
    half = ROT_DIM // 2
    for j in range(z.shape[1] // LANES):
        zc = z[:, j * LANES:(j + 1) * LANES]
        r = zc * c + pltpu.roll(zc, LANES - half, 1) * s1 + pltpu.roll(zc, half, 1) * s2
        o_ref[:, j * LANES:(j + 1) * LANES] = r.astype(o_ref.dtype)


def _project(hb, w, out_dtype, tn, rope=None):
    tm = 1024
    n = w.shape[1]
    x_spec = pl.BlockSpec((tm, D_MODEL), lambda i, j: (i, 0))
    w_spec = pl.BlockSpec((D_MODEL, tn), lambda i, j: (0, j))
    o_spec = pl.BlockSpec((tm, tn), lambda i, j: (i, j))
    if rope is None:
        kern, extra, extra_specs = _proj_kernel, (), []
    else:
        t_spec = pl.BlockSpec((tm, LANES), lambda i, j: (i, 0))
        kern, extra, extra_specs = _proj_rot_kernel, rope, [t_spec] * 3
    return pl.pallas_call(
        kern,
        out_shape=jax.ShapeDtypeStruct((TOKENS, n), out_dtype),
        grid=(TOKENS // tm, n // tn),
        in_specs=[x_spec, w_spec] + extra_specs,
        out_specs=o_spec,
        compiler_params=_params("parallel", "arbitrary"),
        name="in_proj_rot" if rope is not None else "in_proj",
    )(hb, w, *extra)


def _stack_heads(*q_blocks):
    parts = []
    for q in q_blocks:
        lane = lax.broadcasted_iota(jnp.int32, q.shape, 1)
        zero = jnp.zeros_like(q)
        parts += [jnp.where(lane < HEAD_DIM, q, zero), jnp.where(lane >= HEAD_DIM, q, zero)]
    return jnp.concatenate(parts, axis=0)


def _merge_pair_t(lo, hi):
    sub = lax.broadcasted_iota(jnp.int32, lo.shape, 0)
    return jnp.where(sub < HEAD_DIM, lo, hi)


def _band_bias_t(nk, qc, offset, n_back):
    key = lax.broadcasted_iota(jnp.int32, (nk, qc), 0)
    qry = lax.broadcasted_iota(jnp.int32, (nk, qc), 1)
    diff = offset + qry - key
    return jnp.where((diff >= 0) & (diff <= n_back), 0.0, NEG)


def _tile_lanes(x, n):
    return jnp.concatenate([x] * n, axis=1)


def _transpose_bf16(v):
    return jnp.transpose(v.astype(F32)).astype(BF16)


def _tree(x, op):
    n = x.shape[0]
    if n == 8:
        return x
    if n % 16 == 0:
        return op(_tree(x[:n // 2], op), _tree(x[n // 2:], op))
    acc = x[:8]
    for i in range(1, n // 8):
        acc = op(acc, x[8 * i:8 * i + 8])
    return acc


def _reduce_keys(x, op, final):
    return final(_tree(x, op), axis=0, keepdims=True)


VT_ROWS = LANES + 16


def _transpose_aug(v):
    vt = jnp.transpose(v.astype(F32))
    sub = lax.broadcasted_iota(jnp.int32, (VT_ROWS - LANES, v.shape[0]), 0)
    return jnp.concatenate([vt, jnp.where(sub == 0, 1.0, 0.0)], axis=0).astype(BF16)


def _probs(s_t, m):
    return jnp.exp2((s_t - m).astype(BF16))


def _normalise(acc):
    l = acc[LANES:LANES + 1]
    return acc[:LANES] / l, l


def _softmax_block_t(s_t, pv):
    m = _reduce_keys(s_t, jnp.maximum, jnp.max)
    out, l = _normalise(pv(_probs(s_t, m)))
    return out, m + jnp.log(l) * LOG2_E


def _online_step_t(carry, s_t, m_t, pv):
    m, acc = carry
    m_new = jnp.maximum(m, m_t)
    acc = jnp.exp2(m - m_new) * acc + pv(_probs(s_t, m_new))
    return m_new, acc


def _flash_tiles(n_tiles, last_tile, init, scores, pv_of):
    def produce(t):
        s_t = scores(t)
        return s_t, _reduce_keys(s_t, jnp.maximum, jnp.max)

    def body(t, carry):
        state, s_t, m_t = carry
        s_next, m_next = produce(jnp.minimum(t + 1, last_tile))
        return _online_step_t(state, s_t, m_t, pv_of(t)), s_next, m_next

    state, _, _ = lax.fori_loop(0, n_tiles, body, (init,) + produce(0))
    return state


def _pv_tiles(vt_ref, first, n, rows):
    def pv(p):
        acc = _nn(vt_ref[first], p[:rows])
        for j in range(1, n):
            acc = acc + _nn(vt_ref[first + j], p[j * rows:(j + 1) * rows])
        return acc
    return pv


def _online_init_t(r):
    return (jnp.full((1, r), NEG, F32), jnp.zeros((VT_ROWS, r), F32))


def _rank_rows(g, n_rows):
    sub = lax.broadcasted_iota(jnp.int32, (8, g.shape[1]), 0)
    rank = jnp.zeros(g.shape, F32)
    for m in range(n_rows):
        gm = g[m:m + 1, :]
        b = m // 8 * 8
        mid = g[b:b + 8]
        parts = [jnp.where(gm > mid, 1.0, jnp.where((gm == mid) & (sub > m - b), 1.0, 0.0))]
        if b > 0:
            parts.insert(0, jnp.where(gm > g[:b], 1.0, 0.0))
        if b + 8 < n_rows:
            parts.append(jnp.where(gm >= g[b + 8:], 1.0, 0.0))
        rank = rank + jnp.concatenate(parts, axis=0)
    return rank


MOBA_QC = 256
MOBA_KT = 2 * MOBA_BLOCK


def _moba_kernel(q_ref, k_ref, v_ref, o_ref, kmean_ref, vt_ref, bias_ref):
    c = pl.program_id(2)
    qc = MOBA_QC
    r = 2 * qc

    @pl.when(c == 0)
    def _():
        row = lax.broadcasted_iota(jnp.int32, (MOBA_NB, SEQ), 0)
        col = lax.broadcasted_iota(jnp.int32, (MOBA_NB, SEQ), 1)
        avg = jnp.where((col >> 8) == row, 1.0 / MOBA_BLOCK, 0.0).astype(BF16)
        kmean_ref[...] = _nn(avg, k_ref[0])
        for t in range(MOBA_NB):
            vt_ref[t] = _transpose_aug(v_ref[0, t * MOBA_BLOCK:(t + 1) * MOBA_BLOCK, :])

    qs = _stack_heads(q_ref[0])

    ks = pl.multiple_of(c * MOBA_BLOCK, MOBA_BLOCK)
    s_own = _nt(k_ref[0, pl.ds(ks, MOBA_BLOCK), :], qs) + _tile_lanes(_band_bias_t(MOBA_BLOCK, qc, 0, MOBA_BLOCK), 2)
    state = _online_step_t(_online_init_t(r), s_own, _reduce_keys(s_own, jnp.maximum, jnp.max),
                           _pv_tiles(vt_ref, c, 1, MOBA_BLOCK))

    gate = _nt(kmean_ref[...].astype(BF16), qs)
    blk = lax.broadcasted_iota(jnp.int32, gate.shape, 0)
    past = blk < c
    rank = _rank_rows(jnp.where(past, gate, NEG), MOBA_NB)
    bias_ref[...] = jnp.where(past & (rank < MOBA_TOPK), 0.0, NEG)

    per_tile = MOBA_KT // MOBA_BLOCK

    def scores(t):
        ks = pl.multiple_of(t * MOBA_KT, MOBA_KT)
        blocks = [jnp.broadcast_to(bias_ref[pl.ds(t * per_tile + j, 1), :], (MOBA_BLOCK, r))
                  for j in range(per_tile)]
        return _nt(k_ref[0, pl.ds(ks, MOBA_KT), :], qs) + jnp.concatenate(blocks, axis=0)

    _, acc = _flash_tiles((c + per_tile - 1) // per_tile, SEQ // MOBA_KT - 1, state, scores,
                          lambda t: _pv_tiles(vt_ref, t * per_tile, per_tile, MOBA_BLOCK))
    o_t, _ = _normalise(acc)
    o_ref[0] = jnp.transpose(_merge_pair_t(o_t[:, :qc], o_t[:, qc:])).astype(o_ref.dtype)


def _moba(z_rot, z_pl):
    qc = MOBA_QC
    grid = (BATCH, MOBA_HEADS // 2, SEQ // qc)
    return pl.pallas_call(
        _moba_kernel,
        out_shape=jax.ShapeDtypeStruct((BATCH, SEQ, MOBA_HEADS * HEAD_DIM), BF16),
        grid=grid,
        in_specs=[
            pl.BlockSpec((1, qc, LANES), lambda b, p, c: (b, c, ROT_MQ + p)),
            pl.BlockSpec((1, SEQ, LANES), lambda b, p, c: (b, 0, ROT_MK + p)),
            pl.BlockSpec((1, SEQ, LANES), lambda b, p, c: (b, 0, PL_MV + p)),
        ],
        out_specs=pl.BlockSpec((1, qc, LANES), lambda b, p, c: (b, c, p)),
        scratch_shapes=[pltpu.VMEM((MOBA_NB, LANES), F32),
                        pltpu.VMEM((MOBA_NB, VT_ROWS, MOBA_BLOCK), BF16),
                        pltpu.VMEM((MOBA_NB, 2 * qc), F32)],
        compiler_params=_params("parallel", "parallel", "arbitrary"),
        name="moba",
    )(z_rot, z_rot, z_pl)


def _compress_one(x_ref, pe_ref, w1_ref, w2_ref, o_ref):
    x = x_ref[0, 0].astype(F32)
    top = (x + pe_ref[0:1, :]).astype(BF16)
    bot = (x + pe_ref[1:2, :]).astype(BF16)
    a = _nn(top, w1_ref[0])
    bm = _nn(bot, w1_ref[1])
    pre = a + pltpu.roll(bm, N_CMP_PAD - 1, 0)
    hid = jax.nn.gelu(pre)
    out = _nn(hid.astype(BF16), w2_ref[...])
    row = lax.broadcasted_iota(jnp.int32, out.shape, 0)
    o_ref[0, 0] = jnp.where(row < N_CMP, out, 0.0).astype(o_ref.dtype)


def _compress_kernel(xk_ref, xv_ref, pk_ref, pv_ref, k1_ref, k2_ref, v1_ref, v2_ref, ok_ref, ov_ref):
    _compress_one(xk_ref, pk_ref, k1_ref, k2_ref, ok_ref)
    _compress_one(xv_ref, pv_ref, v1_ref, v2_ref, ov_ref)


def _compress(xk, xv, pk, pv, k1, k2, v1, v2):
    chunk_w = CMP_STRIDE * HEAD_DIM
    x_spec = pl.BlockSpec((1, 1, N_CMP_PAD, chunk_w), lambda b, j: (b, j, 0, 0))
    pe_spec = pl.BlockSpec((2, chunk_w), lambda b, j: (0, 0))
    w1_spec = pl.BlockSpec((2, chunk_w, CMP_HIDDEN), lambda b, j: (0, 0, 0))
    w2_spec = pl.BlockSpec((CMP_HIDDEN, LANES), lambda b, j: (0, 0))
    o_spec = pl.BlockSpec((1, 1, N_CMP_PAD, LANES), lambda b, j: (b, j, 0, 0))
    o_shape = jax.ShapeDtypeStruct((BATCH, NSA_KV_HEADS, N_CMP_PAD, LANES), BF16)
    return pl.pallas_call(
        _compress_kernel,
        out_shape=(o_shape, o_shape),
        grid=(BATCH, NSA_KV_HEADS),
        in_specs=[x_spec, x_spec, pe_spec, pe_spec, w1_spec, w2_spec, w1_spec, w2_spec],
        out_specs=(o_spec, o_spec),
        compiler_params=_params("parallel", "parallel"),
        name="nsa_compress",
    )(xk, xv, pk, pv, k1, k2, v1, v2)


NSA_QC = 256
NSA_KT = 512
NSA_G = NSA_HEADS // NSA_KV_HEADS
NSA_WIN_TILES = NSA_WINDOW // NSA_QC + 1


def _nsa_kernel(qa_ref, qb_ref, kc_ref, vc_ref, ks_ref, vs_ref, kw_ref, vw_ref, gl_ref, ovt_ref,
                o_ref, vct_ref, vst_ref, vwt_ref, bias_ref):
    c = pl.program_id(2)
    qc = NSA_QC
    q0 = c * qc
    lanes_of = lambda t, i: t[:, i * qc:(i + 1) * qc]

    @pl.when(c == 0)
    def _():
        vct_ref[...] = _transpose_bf16(vc_ref[0, 0])
        for t in range(SEQ // NSA_KT):
            vst_ref[t] = _transpose_aug(vs_ref[0, t * NSA_KT:(t + 1) * NSA_KT, :])
        for t in range(SEQ // qc):
            vwt_ref[t] = _transpose_aug(vw_ref[0, t * qc:(t + 1) * qc, :])

    qs = _stack_heads(qa_ref[0], qb_ref[0])

    sc_t = _nt(kc_ref[0, 0], qs)
    t0 = jnp.maximum(c - NSA_WINDOW // qc, 0)
    start = pl.multiple_of(t0 * qc, qc)
    sw_t = _nt(kw_ref[0, pl.ds(start, NSA_WIN_TILES * qc), :], qs)

    n_idx = lax.broadcasted_iota(jnp.int32, (N_CMP_PAD, qc), 0)
    q_idx = lax.broadcasted_iota(jnp.int32, (N_CMP_PAD, qc), 1)
    ok = (n_idx * CMP_STRIDE + (CMP_LEN - 1)) <= (q0 + q_idx)
    p_heads = []
    for i in range(NSA_G):
        s_i = jnp.where(ok, lanes_of(sc_t, i), NEG)
        e_i = jnp.where(ok, jnp.exp2(s_i - _reduce_keys(s_i, jnp.maximum, jnp.max)), 0.0)
        l_i = _reduce_keys(e_i, jnp.add, jnp.sum)
        p_heads.append((e_i / jnp.where(l_i > 0.0, l_i, 1.0)).astype(BF16))
    p_ct = jnp.concatenate(p_heads, axis=1)
    ocmp_t = _nn(vct_ref[...], p_ct)

    band = _band_bias_t(NSA_WIN_TILES * qc, qc, q0 - start, NSA_WINDOW - 1)
    owin_t, _ = _softmax_block_t(sw_t + _tile_lanes(band, NSA_G), _pv_tiles(vwt_ref, t0, NSA_WIN_TILES, qc))
    gate_t = jnp.transpose(jax.nn.sigmoid(gl_ref[...]))
    gate = lambda i, r: gate_t[3 * i + r:3 * i + r + 1, :]
    partial_out = [gate(i, 0) * lanes_of(ocmp_t, i) + gate(i, 2) * lanes_of(owin_t, i) for i in range(NSA_G)]

    imp4 = _nn(ovt_ref[...], p_ct)
    imp = lanes_of(imp4, 0) + lanes_of(imp4, 1) + lanes_of(imp4, 2) + lanes_of(imp4, 3)
    blk = lax.broadcasted_iota(jnp.int32, imp.shape, 0)
    cur = (q0 + lax.broadcasted_iota(jnp.int32, imp.shape, 1)) >> 6
    valid = blk <= cur
    forced = valid & ((blk == 0) | (blk > cur - SLC_LOCAL))
    rank = _rank_rows(jnp.where(forced, FORCE, jnp.where(valid, imp, NEG)), N_SLC)
    bias_ref[...] = jnp.where(valid & (rank < SLC_TOPK), 0.0, NEG)

    key_row = lax.broadcasted_iota(jnp.int32, (NSA_KT, qc), 0)
    qpos = lax.broadcasted_iota(jnp.int32, (NSA_KT, qc), 1) + q0
    per_tile = NSA_KT // SLC_BLOCK

    def scores(t):
        ks0 = pl.multiple_of(t * NSA_KT, NSA_KT)
        blocks = [jnp.broadcast_to(bias_ref[pl.ds(t * per_tile + j, 1), :], (SLC_BLOCK, qc))
                  for j in range(per_tile)]
        bias = jnp.where(key_row + ks0 <= qpos, jnp.concatenate(blocks, axis=0), NEG)
        return _nt(ks_ref[0, pl.ds(ks0, NSA_KT), :], qs) + _tile_lanes(bias, NSA_G)

    _, acc_s = _flash_tiles(c // (NSA_KT // qc) + 1, SEQ // NSA_KT - 1, _online_init_t(NSA_G * qc), scores,
                            lambda t: _pv_tiles(vst_ref, t, 1, NSA_KT))
    oslc_t, _ = _normalise(acc_s)
    outs = [partial_out[i] + gate(i, 1) * lanes_of(oslc_t, i) for i in range(NSA_G)]
    o_ref[0, :, 0:LANES] = jnp.transpose(_merge_pair_t(outs[0], outs[1])).astype(o_ref.dtype)
    o_ref[0, :, LANES:2 * LANES] = jnp.transpose(_merge_pair_t(outs[2], outs[3])).astype(o_ref.dtype)


def _nsa(z_rot, z_pl, k_cmp, v_cmp, gate_logits, ovt):
    qc = NSA_QC
    seq_spec = lambda base: pl.BlockSpec((1, SEQ, LANES), lambda b, j, c: (b, 0, base + j))
    cmp_spec = pl.BlockSpec((1, 1, N_CMP_PAD, LANES), lambda b, j, c: (b, j, 0, 0))
    return pl.pallas_call(
        _nsa_kernel,
        out_shape=jax.ShapeDtypeStruct((BATCH, SEQ, NSA_HEADS * HEAD_DIM), BF16),
        grid=(BATCH, NSA_KV_HEADS, SEQ // qc),
        in_specs=[
            pl.BlockSpec((1, qc, LANES), lambda b, j, c: (b, c, ROT_NQ + 2 * j)),
            pl.BlockSpec((1, qc, LANES), lambda b, j, c: (b, c, ROT_NQ + 2 * j + 1)),
            cmp_spec, cmp_spec,
            seq_spec(ROT_NKS), seq_spec(PL_NVS), seq_spec(ROT_NKW), seq_spec(PL_NVW),
            pl.BlockSpec((qc, LANES), lambda b, j, c: (b * (SEQ // qc) + c, j)),
            pl.BlockSpec(ovt.shape, lambda b, j, c: (0, 0)),
        ],
        out_specs=pl.BlockSpec((1, qc, 2 * LANES), lambda b, j, c: (b, c, j)),
        scratch_shapes=[pltpu.VMEM((LANES, N_CMP_PAD), BF16),
                        pltpu.VMEM((SEQ // NSA_KT, VT_ROWS, NSA_KT), BF16),
                        pltpu.VMEM((SEQ // qc, VT_ROWS, qc), BF16),
                        pltpu.VMEM((N_SLC, qc), F32)],
        compiler_params=_params("parallel", "parallel", "arbitrary"),
        name="nsa",
    )(z_rot, z_rot, k_cmp, v_cmp, z_rot, z_pl, z_rot, z_pl, gate_logits, ovt)


DIL_QC = 128
DIL_STEPS = SEQ // DIL_QC
DIL_UNROLL = 8


def _dil_group(q_ref, k_ref, v_ref, og_ref, lg_ref, gi):
    window, dil = DIL_CONFIGS[gi]
    qc = DIL_QC
    m = SEQ // dil
    n_back = window // dil
    nk = min(m, qc + -(-n_back // qc) * qc)
    chunks = m // qc

    def rows(first, n):
        return pl.ds(first, n) if dil == 1 else pl.ds(first, n, stride=dil)

    def block(idx):
        r = idx // chunks
        q0 = (idx % chunks) * qc
        start = jnp.maximum(q0 - (nk - qc), 0)
        q_rows = rows(r + dil * q0, qc)
        k_rows = rows(r + dil * start, nk)
        qs = _stack_heads(q_ref[0, q_rows, :].astype(BF16))
        v_t = _transpose_aug(v_ref[0, k_rows, :])
        band = _band_bias_t(nk, qc, q0 - start, n_back)
        s_t = _nt(k_ref[0, k_rows, :].astype(BF16), qs) + _tile_lanes(band, 2)
        o_t, lse = _softmax_block_t(s_t, lambda p: _nn(v_t, p))
        lse_b = jnp.broadcast_to(lse, (LANES, 2 * qc))
        og_ref[gi, q_rows, :] = jnp.transpose(_merge_pair_t(o_t[:, :qc], o_t[:, qc:]))
        lg_ref[gi, q_rows, :] = jnp.transpose(_merge_pair_t(lse_b[:, :qc], lse_b[:, qc:]))

    def body(i, _):
        for u in range(DIL_UNROLL):
            block(i * DIL_UNROLL + u)
        return 0

    lax.fori_loop(0, DIL_STEPS // DIL_UNROLL, body, 0)


def _dil_kernel(q_ref, k_ref, v_ref, o_ref, og_ref, lg_ref):
    g = pl.program_id(2)
    n_groups = len(DIL_CONFIGS)
    for gi in range(n_groups):
        pl.when(g == gi)(functools.partial(_dil_group, q_ref, k_ref, v_ref, og_ref, lg_ref, gi))

    @pl.when(g == n_groups - 1)
    def _():
        rows = 512

        def body(i, _):
            sl = pl.ds(pl.multiple_of(i * rows, rows), rows)
            l0, l1, l2 = lg_ref[0, sl, :], lg_ref[1, sl, :], lg_ref[2, sl, :]
            mx = jnp.maximum(jnp.maximum(l0, l1), l2)
            e0, e1, e2 = jnp.exp2(l0 - mx), jnp.exp2(l1 - mx), jnp.exp2(l2 - mx)
            den = e0 + e1 + e2
            out = (e0 / den) * og_ref[0, sl, :] + (e1 / den) * og_ref[1, sl, :] + (e2 / den) * og_ref[2, sl, :]
            o_ref[0, sl, :] = out.astype(o_ref.dtype)
            return 0

        lax.fori_loop(0, SEQ // rows, body, 0)


def _dilated(zd_rot, zd_pl):
    n_groups = len(DIL_CONFIGS)
    width = DIL_HEADS_PER_GROUP * HEAD_DIM
    col = lambda base: (lambda b, p, g: (b, 0, base + 2 * g + p))
    blk = lambda base: pl.BlockSpec((1, SEQ, LANES), col(base))
    return pl.pallas_call(
        _dil_kernel,
        out_shape=jax.ShapeDtypeStruct((BATCH, SEQ, width), BF16),
        grid=(BATCH, 2, n_groups),
        in_specs=[blk(0), blk(DIL_BLOCKS), blk(0)],
        out_specs=pl.BlockSpec((1, SEQ, LANES), lambda b, p, g: (b, 0, p)),
        scratch_shapes=[pltpu.VMEM((n_groups, SEQ, LANES), F32), pltpu.VMEM((n_groups, SEQ, LANES), F32)],
        compiler_params=_params("parallel", "parallel", "arbitrary"),
        name="dilated",
    )(zd_rot, zd_rot, zd_pl)


def _out_proj_kernel(oa_ref, ob_ref, oc_ref, h_ref, wa_ref, wb_ref, wc_ref, g_ref, b_ref, h1_ref, h1b_ref):
    y = _nn(oa_ref[...], wa_ref[...]) + _nn(ob_ref[...], wb_ref[...]) + _nn(oc_ref[...], wc_ref[...])
    h1 = _layer_norm(DEEPNORM_ALPHA * h_ref[...] + y, g_ref[...], b_ref[...])
    h1_ref[...] = h1
    h1b_ref[...] = h1.astype(BF16)


def _out_proj(oa, ob, oc, h, wa, wb, wc, g, b):
    tm = 512
    rows = lambda w: pl.BlockSpec((tm, w), lambda i: (i, 0))
    full = lambda a: pl.BlockSpec(a.shape, lambda i: (0, 0))
    return pl.pallas_call(
        _out_proj_kernel,
        out_shape=(jax.ShapeDtypeStruct((TOKENS, D_MODEL), F32),
                   jax.ShapeDtypeStruct((TOKENS, D_MODEL), BF16)),
        grid=(TOKENS // tm,),
        in_specs=[rows(oa.shape[1]), rows(ob.shape[1]), rows(oc.shape[1]), rows(D_MODEL),
                  full(wa), full(wb), full(wc), full(g), full(b)],
        out_specs=(rows(D_MODEL), rows(D_MODEL)),
        compiler_params=_params("parallel"),
        name="out_proj_ln",
    )(oa, ob, oc, h, wa, wb, wc, g, b)


def _router_kernel(hb_ref, rw_ref, rb_ref, comb_ref, sel_ref):
    logits = _nt(rw_ref[...], hb_ref[...]) + rb_ref[...]
    mx = jnp.max(logits, axis=0, keepdims=True)
    ex = jnp.exp(logits - mx)
    probs = ex / jnp.sum(ex, axis=0, keepdims=True)
    p = [probs[e:e + 1, :] for e in range(N_EXPERTS)]
    best, g_sel = None, None
    for g in range(N_GROUPS):
        a, b, c, d = p[4 * g:4 * g + 4]
        hi1, lo1, hi2, lo2 = jnp.maximum(a, b), jnp.minimum(a, b), jnp.maximum(c, d), jnp.minimum(c, d)
        top2 = jnp.maximum(hi1, hi2) + jnp.maximum(jnp.minimum(hi1, hi2), jnp.maximum(lo1, lo2))
        if g == 0:
            best, g_sel = top2, jnp.zeros_like(top2)
        else:
            better = top2 > best
            best = jnp.where(better, top2, best)
            g_sel = jnp.where(better, float(g), g_sel)
    chosen, picked = [], []
    for e in range(N_EXPERTS):
        g = e // EXPERTS_PER_GROUP
        rank = jnp.zeros_like(best)
        for o in range(4 * g, 4 * g + 4):
            if o < e:
                rank = rank + jnp.where(p[o] >= p[e], 1.0, 0.0)
            elif o > e:
                rank = rank + jnp.where(p[o] > p[e], 1.0, 0.0)
        chosen.append(jnp.where((g_sel == float(g)) & (rank < 2.0), 1.0, 0.0))
        picked.append(chosen[e] * p[e])
    total = picked[0]
    for e in range(1, N_EXPERTS):
        total = total + picked[e]
    comb_ref[...] = jnp.concatenate(picked, axis=0) / total
    sel_ref[...] = jnp.concatenate(chosen, axis=0)


def _router(hb, rw_t, rb):
    tm = 1024
    out = jax.ShapeDtypeStruct((N_EXPERTS, TOKENS), F32)
    o_spec = pl.BlockSpec((N_EXPERTS, tm), lambda i: (0, i))
    return pl.pallas_call(
        _router_kernel,
        out_shape=(out, out),
        grid=(TOKENS // tm,),
        in_specs=[pl.BlockSpec((tm, D_MODEL), lambda i: (i, 0)),
                  pl.BlockSpec((N_EXPERTS, D_MODEL), lambda i: (0, 0)),
                  pl.BlockSpec((N_EXPERTS, 1), lambda i: (0, 0))],
        out_specs=(o_spec, o_spec),
        compiler_params=_params("parallel"),
        name="router",
    )(hb, rw_t, rb)


def _routing_tables(comb_t, sel_t):
    sel = sel_t > 0.5
    cnt = jnp.sum(sel, axis=1, dtype=jnp.int32)
    cnt_pad = (cnt + (MOE_TILE - 1)) // MOE_TILE * MOE_TILE
    ends = jnp.cumsum(cnt_pad)
    rank = jnp.cumsum(sel.astype(jnp.int32), axis=1) - 1
    pos = (ends - cnt_pad)[:, None] + rank
    pos_lo = jnp.min(jnp.where(sel, pos, MOE_ROWS), axis=0)
    pos_hi = jnp.max(jnp.where(sel, pos, -1), axis=0)
    w_lo = jnp.sum(jnp.where(sel & (pos == pos_lo), comb_t, 0.0), axis=0)
    w_hi = jnp.sum(jnp.where(sel & (pos == pos_hi), comb_t, 0.0), axis=0)
    w = jnp.zeros((TOKENS, LANES), F32).at[:, 0].set(w_lo).at[:, 1].set(w_hi)
    n_tiles = ends[-1] // MOE_TILE
    tile_start = jnp.arange(MOE_TILES, dtype=jnp.int32) * MOE_TILE
    tile_start = jnp.minimum(tile_start, ends[-1] - MOE_TILE)
    tile_expert = jnp.sum((ends[None, :] <= tile_start[:, None]).astype(jnp.int32), axis=1)
    return jnp.stack([pos_lo, pos_hi]).astype(jnp.int32), w, tile_expert, n_tiles.reshape(1).astype(jnp.int32)


MOE_TILE = 256
MOE_TILES = 2 * TOKENS // MOE_TILE + N_EXPERTS
MOE_ROWS = MOE_TILES * MOE_TILE
SLAB = D_MODEL // LANES


def _to_slabs(ref, x, rows):
    for j in range(SLAB):
        ref[pl.ds(j, rows, stride=SLAB), :] = x[:, j * LANES:(j + 1) * LANES]


def _from_slabs(ref, rows):
    return jnp.concatenate([ref[pl.ds(j, rows, stride=SLAB), :] for j in range(SLAB)], axis=1)


def _slab_rows(row):
    return pl.ds(pl.multiple_of(row * SLAB, SLAB), SLAB)


def _dispatch_kernel(pos_ref, h_ref, init_ref, xs_ref, slab_ref, sem):
    del init_ref
    tm = h_ref.shape[0]
    base = pl.program_id(0) * tm
    _to_slabs(slab_ref, h_ref[...], tm)

    def copy(t, which):
        return pltpu.make_async_copy(slab_ref.at[_slab_rows(t), :],
                                     xs_ref.at[_slab_rows(pos_ref[which, base + t]), :], sem)

    def start(t, _):
        copy(t, 0).start()
        copy(t, 1).start()
        return 0

    lax.fori_loop(0, tm, start, 0, unroll=8)
    whole = pltpu.make_async_copy(slab_ref, xs_ref.at[pl.ds(0, tm * SLAB), :], sem)
    whole.wait()
    whole.wait()


def _dispatch(pos, h):
    tm = 256
    grid_spec = pltpu.PrefetchScalarGridSpec(
        num_scalar_prefetch=1,
        grid=(TOKENS // tm,),
        in_specs=[pl.BlockSpec((tm, D_MODEL), lambda i, pos: (i, 0)),
                  pl.BlockSpec(memory_space=pl.ANY)],
        out_specs=pl.BlockSpec(memory_space=pl.ANY),
        scratch_shapes=[pltpu.VMEM((tm * SLAB, LANES), F32), pltpu.SemaphoreType.DMA],
    )
    return pl.pallas_call(
        _dispatch_kernel,
        out_shape=jax.ShapeDtypeStruct((MOE_ROWS * SLAB, LANES), F32),
        grid_spec=grid_spec,
        input_output_aliases={2: 0},
        compiler_params=_params("arbitrary"),
        name="moe_dispatch",
    )(pos, h, jnp.zeros((MOE_ROWS * SLAB, LANES), F32))


def _experts_kernel(te_ref, nt_ref, xs_ref, wg_ref, wu_ref, wd_ref, ys_ref, wgb_ref, wub_ref, wdb_ref):
    k = pl.program_id(0)
    e = te_ref[k]
    e_prev = te_ref[jnp.maximum(k - 1, 0)]

    @pl.when((k == 0) | (e != e_prev))
    def _():
        wgb_ref[...] = wg_ref[0, 0].astype(BF16)
        wub_ref[...] = wu_ref[0, 0].astype(BF16)
        wdb_ref[...] = wd_ref[0, 0].astype(BF16)

    @pl.when(k < nt_ref[0])
    def _():
        x = _from_slabs(xs_ref, MOE_TILE).astype(BF16)
        hid = jax.nn.silu(_nn(x, wgb_ref[...])) * _nn(x, wub_ref[...])
        _to_slabs(ys_ref, _nn(hid.astype(BF16), wdb_ref[...]), MOE_TILE)

    @pl.when(k >= nt_ref[0])
    def _():
        ys_ref[...] = jnp.zeros(ys_ref.shape, F32)


def _experts(tile_expert, n_tiles, xs, wg, wu, wd, layer):
    w_in_spec = pl.BlockSpec((1, 1, D_MODEL, EXPERT_HIDDEN), lambda k, te, nt: (layer, te[k], 0, 0))
    grid_spec = pltpu.PrefetchScalarGridSpec(
        num_scalar_prefetch=2,
        grid=(MOE_TILES,),
        in_specs=[pl.BlockSpec((MOE_TILE * SLAB, LANES), lambda k, te, nt: (jnp.minimum(k, nt[0] - 1), 0)),
                  w_in_spec, w_in_spec,
                  pl.BlockSpec((1, 1, EXPERT_HIDDEN, D_MODEL), lambda k, te, nt: (layer, te[k], 0, 0))],
        out_specs=pl.BlockSpec((MOE_TILE * SLAB, LANES), lambda k, te, nt: (k, 0)),
        scratch_shapes=[pltpu.VMEM((D_MODEL, EXPERT_HIDDEN), BF16), pltpu.VMEM((D_MODEL, EXPERT_HIDDEN), BF16),
                        pltpu.VMEM((EXPERT_HIDDEN, D_MODEL), BF16)],
    )
    return pl.pallas_call(
        _experts_kernel,
        out_shape=jax.ShapeDtypeStruct((MOE_ROWS * SLAB, LANES), F32),
        grid_spec=grid_spec,
        compiler_params=_params("arbitrary"),
        name="moe_experts",
    )(tile_expert, n_tiles, xs, wg, wu, wd)


def _ple_ln_kernel(pos_ref, hb_ref, h_ref, ys_ref, w_ref, p_ref, gw_ref, gb_ref, pw_ref, g_ref, b_ref,
                   h2_ref, h2b_ref, lo_ref, hi_ref, sem):
    tm = h_ref.shape[0]
    i = pl.program_id(0)
    slot = i & 1
    bufs = (lo_ref, hi_ref)

    def fetch(tile, into):
        def start(t, _):
            for which in range(2):
                pltpu.make_async_copy(ys_ref.at[_slab_rows(pos_ref[which, tile * tm + t]), :],
                                      bufs[which].at[into, _slab_rows(t), :], sem.at[into]).start()
            return 0
        lax.fori_loop(0, tm, start, 0, unroll=8)

    @pl.when(i == 0)
    def _():
        fetch(0, 0)

    @pl.when(i + 1 < pl.num_programs(0))
    def _():
        fetch(i + 1, 1 - slot)

    gate = jax.nn.sigmoid(_nn(hb_ref[...], gw_ref[...]) + gb_ref[...])
    ple = gate * _nn(p_ref[...].astype(BF16), pw_ref[...])
    for which in range(2):
        pltpu.make_async_copy(ys_ref.at[pl.ds(0, tm * SLAB), :], bufs[which].at[slot], sem.at[slot]).wait()
    w = w_ref[...]
    ffn = w[:, 0:1] * _from_slabs(lo_ref.at[slot], tm) + w[:, 1:2] * _from_slabs(hi_ref.at[slot], tm)
    h2 = _layer_norm(DEEPNORM_ALPHA * h_ref[...] + ffn + ple, g_ref[...], b_ref[...])
    h2_ref[...] = h2
    h2b_ref[...] = h2.astype(BF16)


def _ple_ln(pos, hb, h, ys, w, p, layer, gw, gb, pw, g, b):
    tm = 256
    p_spec = pl.BlockSpec((tm, PLE_DIM), lambda i, pos: (layer * (TOKENS // tm) + i, 0))
    rows = lambda width: pl.BlockSpec((tm, width), lambda i, pos: (i, 0))
    full = lambda a: pl.BlockSpec(a.shape, lambda i, pos: (0, 0))
    grid_spec = pltpu.PrefetchScalarGridSpec(
        num_scalar_prefetch=1,
        grid=(TOKENS // tm,),
        in_specs=[rows(D_MODEL), rows(D_MODEL), pl.BlockSpec(memory_space=pl.ANY), rows(LANES), p_spec,
                  full(gw), full(gb), full(pw), full(g), full(b)],
        out_specs=(rows(D_MODEL), rows(D_MODEL)),
        scratch_shapes=[pltpu.VMEM((2, tm * SLAB, LANES), F32), pltpu.VMEM((2, tm * SLAB, LANES), F32),
                        pltpu.SemaphoreType.DMA((2,))],
    )
    return pl.pallas_call(
        _ple_ln_kernel,
        out_shape=(jax.ShapeDtypeStruct((TOKENS, D_MODEL), F32),
                   jax.ShapeDtypeStruct((TOKENS, D_MODEL), BF16)),
        grid_spec=grid_spec,
        compiler_params=_params("arbitrary"),
        name="ple_ln",
    )(pos, hb, h, ys, w, p, gw, gb, pw, g, b)


def _rope_tables(positions):
    half = ROT_DIM // 2
    inv_freq = jnp.exp(jnp.arange(half, dtype=F32) * (-2.0 * math.log(ROPE_THETA) / ROT_DIM))
    ang = positions.astype(F32)[:, :, None] * inv_freq
    cos, sin = jnp.cos(ang), jnp.sin(ang)
    zeros = jnp.zeros_like(cos)
    rest = HEAD_DIM - ROT_DIM
    pad = lambda v: jnp.broadcast_to(jnp.asarray(v, F32), cos.shape[:2] + (rest,))
    c = jnp.concatenate([cos, cos, pad(1.0)], axis=-1)
    s1 = jnp.concatenate([-sin, zeros, pad(0.0)], axis=-1)
    s2 = jnp.concatenate([zeros, sin, pad(0.0)], axis=-1)
    tile = lambda t: jnp.concatenate([t, t], axis=-1).reshape(TOKENS, LANES)
    return tile(c), tile(s1), tile(s2)


def _split_w_in(w):
    mw, nq, nkv, dw = MOBA_HEADS * HEAD_DIM, NSA_HEADS * HEAD_DIM, NSA_KV_HEADS * HEAD_DIM, DIL_HEADS * HEAD_DIM
    widths = (mw, mw, mw, nq) + (nkv,) * 6 + (NSA_HEADS * 3, dw, dw, dw)
    offs = np.concatenate([[0], np.cumsum(widths)])
    qa, ka, va, qb, kbc, vbc, kbs, vbs, kbw, vbw, gb, qc, kc, vc = (
        w[:, int(offs[i]):int(offs[i + 1])] for i in range(len(widths)))

    def dup(t):
        t = t.reshape(D_MODEL, NSA_KV_HEADS, 1, HEAD_DIM)
        return jnp.broadcast_to(t, (D_MODEL, NSA_KV_HEADS, 2, HEAD_DIM)).reshape(D_MODEL, NSA_KV_HEADS * LANES)

    zpad = lambda n: jnp.zeros((D_MODEL, n * LANES), w.dtype)
    w_rot = jnp.concatenate([qa * Q_SCALE, ka, qb * Q_SCALE, dup(kbc), dup(kbs), dup(kbw), zpad(1)], axis=1)
    w_pl = jnp.concatenate([va, dup(vbc), dup(vbs), dup(vbw), zpad(3)], axis=1)
    gpad = jnp.zeros((D_MODEL, NSA_KV_HEADS, LANES - 12), w.dtype)
    w_gl = jnp.concatenate([gb.reshape(D_MODEL, NSA_KV_HEADS, 12), gpad], axis=-1).reshape(D_MODEL, -1)
    w_dil_rot = jnp.concatenate([qc * Q_SCALE, kc], axis=1)
    return tuple(t.astype(BF16) for t in (w_rot, w_pl, w_gl, w_dil_rot, vc))


def _overlap_table():
    starts = np.arange(N_CMP) * CMP_STRIDE
    slc = np.arange(N_SLC) * SLC_BLOCK
    ov = ((starts[:, None] < slc[None, :] + SLC_BLOCK) & (starts[:, None] + CMP_LEN > slc[None, :]))
    ovt = np.zeros((N_SLC, N_CMP_PAD), np.float32)
    ovt[:, :N_CMP] = ov.T
    return jnp.asarray(ovt, BF16)


def _cmp_chunks(z, base):
    nblk = z.shape[-1] // LANES
    t = z.reshape(BATCH, SEQ // CMP_STRIDE, CMP_STRIDE, nblk, LANES)[:, :, :, base:base + NSA_KV_HEADS, :HEAD_DIM]
    return t.transpose(0, 3, 1, 2, 4).reshape(BATCH, NSA_KV_HEADS, SEQ // CMP_STRIDE, CMP_STRIDE * HEAD_DIM)


def kernel(x, p, positions, ln_in_g, ln_in_b, w_in, w_out, nsa_ck1, nsa_ck2, nsa_pe_k, nsa_cv1, nsa_cv2, nsa_pe_v, ln1_g, ln1_b, router_w, router_b, w_gate, w_up, w_down, ple_proj, ple_gate_w, ple_gate_b, ln2_g, ln2_b):
    rope = _rope_tables(positions)
    ovt = _overlap_table()
    rw_t = router_w.T.astype(BF16)
    rb = router_b.reshape(N_EXPERTS, 1).astype(F32)
    chunk_w = CMP_STRIDE * HEAD_DIM
    vec = lambda v: v.reshape(1, -1)
    seq3 = lambda t: t.reshape(BATCH, SEQ, t.shape[-1])
    flat = lambda t: t.reshape(TOKENS, t.shape[-1])

    h, hb = _ln_in(x.reshape(TOKENS, D_MODEL), ln_in_g, ln_in_b)
    for i in range(DEPTH):
        w_rot, w_pl, w_gl, w_dil_rot, w_dil_pl = _split_w_in(w_in[i])
        z_rot = seq3(_project(hb, w_rot, BF16, 768, rope=rope))
        z_pl = seq3(_project(hb, w_pl, BF16, 1024))
        gate_logits = _project(hb, w_gl, F32, NSA_KV_HEADS * LANES)
        zd_rot = seq3(_project(hb, w_dil_rot, F32, 768, rope=rope))
        zd_pl = seq3(_project(hb, w_dil_pl, F32, 768))

        o_a = _moba(z_rot, z_pl)

        dup2 = lambda w2: jnp.concatenate([w2, w2], axis=1).astype(BF16)
        k_cmp, v_cmp = _compress(
            _cmp_chunks(z_rot, ROT_NKC), _cmp_chunks(z_pl, PL_NVC),
            nsa_pe_k[i].reshape(2, chunk_w), nsa_pe_v[i].reshape(2, chunk_w),
            nsa_ck1[i].reshape(2, chunk_w, CMP_HIDDEN).astype(BF16), dup2(nsa_ck2[i]),
            nsa_cv1[i].reshape(2, chunk_w, CMP_HIDDEN).astype(BF16), dup2(nsa_cv2[i]))
        o_b = _nsa(z_rot, z_pl, k_cmp, v_cmp, gate_logits, ovt)

        o_c = _dilated(zd_rot, zd_pl)

        wo = w_out[i].astype(BF16)
        a_w, b_w = MOBA_HEADS * HEAD_DIM, NSA_HEADS * HEAD_DIM
        h, hb = _out_proj(flat(o_a), flat(o_b), flat(o_c), h,
                          wo[:a_w], wo[a_w:a_w + b_w], wo[a_w + b_w:], vec(ln1_g[i]), vec(ln1_b[i]))

        pos, w_tok, tile_expert, n_tiles = _routing_tables(*_router(hb, rw_t, rb))
        xs = _dispatch(pos, h)
        ys = _experts(tile_expert, n_tiles, xs, w_gate, w_up, w_down, i)
        h, hb = _ple_ln(pos, hb, h, ys, w_tok, p.reshape(DEPTH * TOKENS, PLE_DIM), i, ple_gate_w[i].astype(BF16),
                        vec(ple_gate_b[i]), ple_proj[i].astype(BF16), vec(ln2_g[i]), vec(ln2_b[i]))
    return h.reshape(BATCH, SEQ, D_MODEL)
```

```python
import functools
import math

import numpy as np
import jax
import jax.numpy as jnp
from jax import lax
from jax.experimental import pallas as pl
from jax.experimental.pallas import tpu as pltpu

F32 = jnp.float32
BF16 = jnp.bfloat16

D_MODEL = 2048
BATCH = 2
SEQ = 4096
DEPTH = 4
TOKENS = BATCH * SEQ
HEAD_DIM = 64
ROT_DIM = HEAD_DIM // 4
ROPE_THETA = 500000.0
NEG = -1e30
FORCE = 1e30
LN_EPS = 1e-5
SCALE = HEAD_DIM ** -0.5
LOG2_E = math.log2(math.e)
Q_SCALE = SCALE * LOG2_E

MOBA_HEADS = 8
MOBA_BLOCK = 256
MOBA_TOPK = 3
MOBA_NB = SEQ // MOBA_BLOCK

NSA_HEADS = 12
NSA_KV_HEADS = 3
CMP_LEN = 32
CMP_STRIDE = 16
CMP_HIDDEN = 128
N_CMP = (SEQ - CMP_LEN) // CMP_STRIDE + 1
N_CMP_PAD = 256
SLC_BLOCK = 64
SLC_TOPK = 16
SLC_LOCAL = 2
N_SLC = SEQ // SLC_BLOCK
NSA_WINDOW = 512

DIL_CONFIGS = ((128, 1), (512, 4), (2048, 16))
DIL_HEADS_PER_GROUP = 4
DIL_HEADS = DIL_HEADS_PER_GROUP * len(DIL_CONFIGS)

N_EXPERTS = 16
N_GROUPS = 4
EXPERTS_PER_GROUP = 4
EXPERT_HIDDEN = D_MODEL // 4
PLE_DIM = 256

DEEPNORM_ALPHA = (2 * DEPTH) ** 0.25

LANES = 128
VMEM_LIMIT = 56 * 1024 * 1024

ROT_MQ, ROT_MK, ROT_NQ, ROT_NKC, ROT_NKS, ROT_NKW = 0, 4, 8, 14, 17, 20
ROT_BLOCKS = 24
PL_MV, PL_NVC, PL_NVS, PL_NVW = 0, 4, 7, 10
PL_BLOCKS = 16
DIL_BLOCKS = DIL_HEADS // 2

NT_DIMS = (((1,), (1,)), ((), ()))


def _nt(a, b):
    return lax.dot_general(a, b, NT_DIMS, preferred_element_type=F32)


def _nn(a, b):
    return jnp.dot(a, b, preferred_element_type=F32)


def _params(*sem):
    return pltpu.CompilerParams(dimension_semantics=sem, vmem_limit_bytes=VMEM_LIMIT)


def _layer_norm(y, g, b):
    mu = jnp.mean(y, axis=-1, keepdims=True)
    yc = y - mu
    var = jnp.mean(yc * yc, axis=-1, keepdims=True)
    return yc * lax.rsqrt(var + LN_EPS) * g + b


def _ln_kernel(x_ref, g_ref, b_ref, h_ref, hb_ref):
    h = _layer_norm(x_ref[...], g_ref[...], b_ref[...])
    h_ref[...] = h
    hb_ref[...] = h.astype(BF16)


def _ln_in(x, g, b):
    tm = 512
    row = pl.BlockSpec((tm, D_MODEL), lambda i: (i, 0))
    vec = pl.BlockSpec((1, D_MODEL), lambda i: (0, 0))
    return pl.pallas_call(
        _ln_kernel,
        out_shape=(jax.ShapeDtypeStruct((TOKENS, D_MODEL), F32),
                   jax.ShapeDtypeStruct((TOKENS, D_MODEL), BF16)),
        grid=(TOKENS // tm,),
        in_specs=[row, vec, vec],
        out_specs=(row, row),
        compiler_params=_params("parallel"),
        name="ln_in",
    )(x, g.reshape(1, -1), b.reshape(1, -1))


def _proj_kernel(x_ref, w_ref, o_ref):
    o_ref[...] = _nn(x_ref[...], w_ref[...]).astype(o_ref.dtype)


def _proj_rot_kernel(x_ref, w_ref, c_ref, s1_ref, s2_ref, o_ref):
    x = x_ref[...]
    c, s1, s2 = c_ref[...], s1_ref[...], s2_ref[...]
    half = ROT_DIM // 2
    for j0 in range(0, o_ref.shape[1], 2 * LANES):
        z = _nn(x, w_ref[:, j0:j0 + 2 * LANES])
        for j in range(j0, j0 + 2 * LANES, LANES):
            zc = z[:, j - j0:j - j0 + LANES]
            r = zc * c + pltpu.roll(zc, LANES - half, 1) * s1 + pltpu.roll(zc, half, 1) * s2
            o_ref[:, j:j + LANES] = r.astype(o_ref.dtype)


def _project(hb, w, out_dtype, tn, rope=None):
    tm = 1024
    n = w.shape[1]
    x_spec = pl.BlockSpec((tm, D_MODEL), lambda i, j: (i, 0))
    w_spec = pl.BlockSpec((D_MODEL, tn), lambda i, j: (0, j))
    o_spec = pl.BlockSpec((tm, tn), lambda i, j: (i, j))
    if rope is None:
        kern, extra, extra_specs = _proj_kernel, (), []
    else:
        t_spec = pl.BlockSpec((tm, LANES), lambda i, j: (i, 0))
        kern, extra, extra_specs = _proj_rot_kernel, rope, [t_spec] * 3
    return pl.pallas_call(
        kern,
        out_shape=jax.ShapeDtypeStruct((TOKENS, n), out_dtype),
        grid=(TOKENS // tm, n // tn),
        in_specs=[x_spec, w_spec] + extra_specs,
        out_specs=o_spec,
        compiler_params=_params("parallel", "arbitrary"),
        name="in_proj_rot" if rope is not None else "in_proj",
    )(hb, w, *extra)


def _stack_heads(*q_blocks):
    parts = []
    for q in q_blocks:
        lane = lax.broadcasted_iota(jnp.int32, q.shape, 1)
        zero = jnp.zeros_like(q)
        parts += [jnp.where(lane < HEAD_DIM, q, zero), jnp.where(lane >= HEAD_DIM, q, zero)]
    return jnp.concatenate(parts, axis=0)


def _merge_pair_t(lo, hi):
    sub = lax.broadcasted_iota(jnp.int32, lo.shape, 0)
    return jnp.where(sub < HEAD_DIM, lo, hi)


def _band_bias_t(nk, qc, offset, n_back):
    key = lax.broadcasted_iota(jnp.int32, (nk, qc), 0)
    qry = lax.broadcasted_iota(jnp.int32, (nk, qc), 1)
    diff = offset + qry - key
    return jnp.where((diff >= 0) & (diff <= n_back), 0.0, NEG)


def _tile_lanes(x, n):
    return jnp.concatenate([x] * n, axis=1)


def _transpose_bf16(v):
    return jnp.transpose(v.astype(F32)).astype(BF16)


def _tree(x, op):
    n = x.shape[0]
    if n == 8:
        return x
    if n % 16 == 0:
        return op(_tree(x[:n // 2], op), _tree(x[n // 2:], op))
    acc = x[:8]
    for i in range(1, n // 8):
        acc = op(acc, x[8 * i:8 * i + 8])
    return acc


def _reduce_keys(x, op, final):
    return final(_tree(x, op), axis=0, keepdims=True)


VT_ROWS = LANES + 16


def _transpose_aug(v):
    vt = jnp.transpose(v.astype(F32))
    sub = lax.broadcasted_iota(jnp.int32, (VT_ROWS - LANES, v.shape[0]), 0)
    return jnp.concatenate([vt, jnp.where(sub == 0, 1.0, 0.0)], axis=0).astype(BF16)


def _probs(s_t, m):
    return jnp.exp2((s_t - m).astype(BF16))


def _normalise(acc):
    l = acc[LANES:LANES + 1]
    return acc[:LANES] / l, l


def _softmax_block_t(s_t, pv):
    m = _reduce_keys(s_t, jnp.maximum, jnp.max)
    out, l = _normalise(pv(_probs(s_t, m)))
    return out, m + jnp.log(l) * LOG2_E


def _online_step_t(carry, s_t, m_t, pv):
    m, acc = carry
    m_new = jnp.maximum(m, m_t)
    acc = jnp.exp2(m - m_new) * acc + pv(_probs(s_t, m_new))
    return m_new, acc


def _flash_tiles(n_tiles, last_tile, init, scores, pv_of):
    def produce(t):
        s_t = scores(t)
        return s_t, _reduce_keys(s_t, jnp.maximum, jnp.max)

    def body(t, carry):
        state, s_t, m_t = carry
        s_next, m_next = produce(jnp.minimum(t + 1, last_tile))
        return _online_step_t(state, s_t, m_t, pv_of(t)), s_next, m_next

    state, _, _ = lax.fori_loop(0, n_tiles, body, (init,) + produce(0))
    return state


def _pv_tiles(vt_ref, first, n, rows):
    def pv(p):
        acc = _nn(vt_ref[first], p[:rows])
        for j in range(1, n):
            acc = acc + _nn(vt_ref[first + j], p[j * rows:(j + 1) * rows])
        return acc
    return pv


def _online_init_t(r):
    return (jnp.full((1, r), NEG, F32), jnp.zeros((VT_ROWS, r), F32))


def _rank_rows(g, n_rows):
    sub = lax.broadcasted_iota(jnp.int32, (8, g.shape[1]), 0)
    rank = jnp.zeros(g.shape, F32)
    for m in range(n_rows):
        gm = g[m:m + 1, :]
        b = m // 8 * 8
        mid = g[b:b + 8]
        parts = [jnp.where(gm > mid, 1.0, jnp.where((gm == mid) & (sub > m - b), 1.0, 0.0))]
        if b > 0:
            parts.insert(0, jnp.where(gm > g[:b], 1.0, 0.0))
        if b + 8 < n_rows:
            parts.append(jnp.where(gm >= g[b + 8:], 1.0, 0.0))
        rank = rank + jnp.concatenate(parts, axis=0)
    return rank


MOBA_QC = 256
MOBA_KT = 2 * MOBA_BLOCK


def _moba_kernel(q_ref, k_ref, v_ref, o_ref, kmean_ref, vt_ref, bias_ref):
    c = pl.program_id(2)
    qc = MOBA_QC
    r = 2 * qc

    @pl.when(c == 0)
    def _():
        row = lax.broadcasted_iota(jnp.int32, (MOBA_NB, SEQ), 0)
        col = lax.broadcasted_iota(jnp.int32, (MOBA_NB, SEQ), 1)
        avg = jnp.where((col >> 8) == row, 1.0 / MOBA_BLOCK, 0.0).astype(BF16)
        kmean_ref[...] = _nn(avg, k_ref[0])
        for t in range(MOBA_NB):
            vt_ref[t] = _transpose_aug(v_ref[0, t * MOBA_BLOCK:(t + 1) * MOBA_BLOCK, :])

    qs = _stack_heads(q_ref[0])

    ks = pl.multiple_of(c * MOBA_BLOCK, MOBA_BLOCK)
    s_own = _nt(k_ref[0, pl.ds(ks, MOBA_BLOCK), :], qs) + _tile_lanes(_band_bias_t(MOBA_BLOCK, qc, 0, MOBA_BLOCK), 2)
    state = _online_step_t(_online_init_t(r), s_own, _reduce_keys(s_own, jnp.maximum, jnp.max),
                           _pv_tiles(vt_ref, c, 1, MOBA_BLOCK))

    gate = _nt(kmean_ref[...].astype(BF16), qs)
    blk = lax.broadcasted_iota(jnp.int32, gate.shape, 0)
    past = blk < c
    rank = _rank_rows(jnp.where(past, gate, NEG), MOBA_NB)
    bias_ref[...] = jnp.where(past & (rank < MOBA_TOPK), 0.0, NEG)

    per_tile = MOBA_KT // MOBA_BLOCK

    def scores(t):
        ks = pl.multiple_of(t * MOBA_KT, MOBA_KT)
        blocks = [jnp.broadcast_to(bias_ref[pl.ds(t * per_tile + j, 1), :], (MOBA_BLOCK, r))
                  for j in range(per_tile)]
        return _nt(k_ref[0, pl.ds(ks, MOBA_KT), :], qs) + jnp.concatenate(blocks, axis=0)

    _, acc = _flash_tiles((c + per_tile - 1) // per_tile, SEQ // MOBA_KT - 1, state, scores,
                          lambda t: _pv_tiles(vt_ref, t * per_tile, per_tile, MOBA_BLOCK))
    o_t, _ = _normalise(acc)
    o_ref[0] = jnp.transpose(_merge_pair_t(o_t[:, :qc], o_t[:, qc:])).astype(o_ref.dtype)


def _moba(z_rot, z_pl):
    qc = MOBA_QC
    grid = (BATCH, MOBA_HEADS // 2, SEQ // qc)
    return pl.pallas_call(
        _moba_kernel,
        out_shape=jax.ShapeDtypeStruct((BATCH, SEQ, MOBA_HEADS * HEAD_DIM), BF16),
        grid=grid,
        in_specs=[
            pl.BlockSpec((1, qc, LANES), lambda b, p, c: (b, c, ROT_MQ + p)),
            pl.BlockSpec((1, SEQ, LANES), lambda b, p, c: (b, 0, ROT_MK + p)),
            pl.BlockSpec((1, SEQ, LANES), lambda b, p, c: (b, 0, PL_MV + p)),
        ],
        out_specs=pl.BlockSpec((1, qc, LANES), lambda b, p, c: (b, c, p)),
        scratch_shapes=[pltpu.VMEM((MOBA_NB, LANES), F32),
                        pltpu.VMEM((MOBA_NB, VT_ROWS, MOBA_BLOCK), BF16),
                        pltpu.VMEM((MOBA_NB, 2 * qc), F32)],
        compiler_params=_params("parallel", "parallel", "arbitrary"),
        name="moba",
    )(z_rot, z_rot, z_pl)


def _compress_one(x_ref, pe_ref, w1_ref, w2_ref, o_ref):
    x = x_ref[0, 0].astype(F32)
    top = (x + pe_ref[0:1, :]).astype(BF16)
    bot = (x + pe_ref[1:2, :]).astype(BF16)
    a = _nn(top, w1_ref[0])
    bm = _nn(bot, w1_ref[1])
    pre = a + pltpu.roll(bm, N_CMP_PAD - 1, 0)
    hid = jax.nn.gelu(pre)
    out = _nn(hid.astype(BF16), w2_ref[...])
    row = lax.broadcasted_iota(jnp.int32, out.shape, 0)
    o_ref[0, 0] = jnp.where(row < N_CMP, out, 0.0).astype(o_ref.dtype)


def _compress_kernel(xk_ref, xv_ref, pk_ref, pv_ref, k1_ref, k2_ref, v1_ref, v2_ref, ok_ref, ov_ref):
    _compress_one(xk_ref, pk_ref, k1_ref, k2_ref, ok_ref)
    _compress_one(xv_ref, pv_ref, v1_ref, v2_ref, ov_ref)


def _compress(xk, xv, pk, pv, k1, k2, v1, v2):
    chunk_w = CMP_STRIDE * HEAD_DIM
    x_spec = pl.BlockSpec((1, 1, N_CMP_PAD, chunk_w), lambda b, j: (b, j, 0, 0))
    pe_spec = pl.BlockSpec((2, chunk_w), lambda b, j: (0, 0))
    w1_spec = pl.BlockSpec((2, chunk_w, CMP_HIDDEN), lambda b, j: (0, 0, 0))
    w2_spec = pl.BlockSpec((CMP_HIDDEN, LANES), lambda b, j: (0, 0))
    o_spec = pl.BlockSpec((1, 1, N_CMP_PAD, LANES), lambda b, j: (b, j, 0, 0))
    o_shape = jax.ShapeDtypeStruct((BATCH, NSA_KV_HEADS, N_CMP_PAD, LANES), BF16)
    return pl.pallas_call(
        _compress_kernel,
        out_shape=(o_shape, o_shape),
        grid=(BATCH, NSA_KV_HEADS),
        in_specs=[x_spec, x_spec, pe_spec, pe_spec, w1_spec, w2_spec, w1_spec, w2_spec],
        out_specs=(o_spec, o_spec),
        compiler_params=_params("parallel", "parallel"),
        name="nsa_compress",
    )(xk, xv, pk, pv, k1, k2, v1, v2)


NSA_QC = 256
NSA_KT = 512
NSA_G = NSA_HEADS // NSA_KV_HEADS
NSA_WIN_TILES = NSA_WINDOW // NSA_QC + 1


def _nsa_kernel(qa_ref, qb_ref, kc_ref, vc_ref, ks_ref, vs_ref, kw_ref, vw_ref, gl_ref, ovt_ref,
                o_ref, vct_ref, vst_ref, vwt_ref, bias_ref):
    c = pl.program_id(2)
    qc = NSA_QC
    q0 = c * qc
    lanes_of = lambda t, i: t[:, i * qc:(i + 1) * qc]

    @pl.when(c == 0)
    def _():
        vct_ref[...] = _transpose_bf16(vc_ref[0, 0])
        for t in range(SEQ // NSA_KT):
            vst_ref[t] = _transpose_aug(vs_ref[0, t * NSA_KT:(t + 1) * NSA_KT, :])
        for t in range(SEQ // qc):
            vwt_ref[t] = _transpose_aug(vw_ref[0, t * qc:(t + 1) * qc, :])

    qs = _stack_heads(qa_ref[0], qb_ref[0])

    sc_t = _nt(kc_ref[0, 0], qs)
    t0 = jnp.maximum(c - NSA_WINDOW // qc, 0)
    start = pl.multiple_of(t0 * qc, qc)
    sw_t = _nt(kw_ref[0, pl.ds(start, NSA_WIN_TILES * qc), :], qs)

    n_idx = lax.broadcasted_iota(jnp.int32, (N_CMP_PAD, qc), 0)
    q_idx = lax.broadcasted_iota(jnp.int32, (N_CMP_PAD, qc), 1)
    ok = (n_idx * CMP_STRIDE + (CMP_LEN - 1)) <= (q0 + q_idx)
    p_heads = []
    for i in range(NSA_G):
        s_i = jnp.where(ok, lanes_of(sc_t, i), NEG)
        e_i = jnp.where(ok, jnp.exp2(s_i - _reduce_keys(s_i, jnp.maximum, jnp.max)), 0.0)
        l_i = _reduce_keys(e_i, jnp.add, jnp.sum)
        p_heads.append((e_i / jnp.where(l_i > 0.0, l_i, 1.0)).astype(BF16))
    p_ct = jnp.concatenate(p_heads, axis=1)
    ocmp_t = _nn(vct_ref[...], p_ct)

    band = _band_bias_t(NSA_WIN_TILES * qc, qc, q0 - start, NSA_WINDOW - 1)
    owin_t, _ = _softmax_block_t(sw_t + _tile_lanes(band, NSA_G), _pv_tiles(vwt_ref, t0, NSA_WIN_TILES, qc))
    gate_t = jnp.transpose(jax.nn.sigmoid(gl_ref[...]))
    gate = lambda i, r: gate_t[3 * i + r:3 * i + r + 1, :]
    partial_out = [gate(i, 0) * lanes_of(ocmp_t, i) + gate(i, 2) * lanes_of(owin_t, i) for i in range(NSA_G)]

    imp4 = _nn(ovt_ref[...], p_ct)
    imp = lanes_of(imp4, 0) + lanes_of(imp4, 1) + lanes_of(imp4, 2) + lanes_of(imp4, 3)
    blk = lax.broadcasted_iota(jnp.int32, imp.shape, 0)
    cur = (q0 + lax.broadcasted_iota(jnp.int32, imp.shape, 1)) >> 6
    valid = blk <= cur
    forced = valid & ((blk == 0) | (blk > cur - SLC_LOCAL))
    rank = _rank_rows(jnp.where(forced, FORCE, jnp.where(valid, imp, NEG)), N_SLC)
    bias_ref[...] = jnp.where(valid & (rank < SLC_TOPK), 0.0, NEG)

    key_row = lax.broadcasted_iota(jnp.int32, (NSA_KT, qc), 0)
    qpos = lax.broadcasted_iota(jnp.int32, (NSA_KT, qc), 1) + q0
    per_tile = NSA_KT // SLC_BLOCK

    def scores(t):
        ks0 = pl.multiple_of(t * NSA_KT, NSA_KT)
        blocks = [jnp.broadcast_to(bias_ref[pl.ds(t * per_tile + j, 1), :], (SLC_BLOCK, qc))
                  for j in range(per_tile)]
        bias = jnp.where(key_row + ks0 <= qpos, jnp.concatenate(blocks, axis=0), NEG)
        return _nt(ks_ref[0, pl.ds(ks0, NSA_KT), :], qs) + _tile_lanes(bias, NSA_G)

    _, acc_s = _flash_tiles(c // (NSA_KT // qc) + 1, SEQ // NSA_KT - 1, _online_init_t(NSA_G * qc), scores,
                            lambda t: _pv_tiles(vst_ref, t, 1, NSA_KT))
    oslc_t, _ = _normalise(acc_s)
    outs = [partial_out[i] + gate(i, 1) * lanes_of(oslc_t, i) for i in range(NSA_G)]
    o_ref[0, :, 0:LANES] = jnp.transpose(_merge_pair_t(outs[0], outs[1])).astype(o_ref.dtype)
    o_ref[0, :, LANES:2 * LANES] = jnp.transpose(_merge_pair_t(outs[2], outs[3])).astype(o_ref.dtype)


def _nsa(z_rot, z_pl, k_cmp, v_cmp, gate_logits, ovt):
    qc = NSA_QC
    seq_spec = lambda base: pl.BlockSpec((1, SEQ, LANES), lambda b, j, c: (b, 0, base + j))
    cmp_spec = pl.BlockSpec((1, 1, N_CMP_PAD, LANES), lambda b, j, c: (b, j, 0, 0))
    return pl.pallas_call(
        _nsa_kernel,
        out_shape=jax.ShapeDtypeStruct((BATCH, SEQ, NSA_HEADS * HEAD_DIM), BF16),
        grid=(BATCH, NSA_KV_HEADS, SEQ // qc),
        in_specs=[
            pl.BlockSpec((1, qc, LANES), lambda b, j, c: (b, c, ROT_NQ + 2 * j)),
            pl.BlockSpec((1, qc, LANES), lambda b, j, c: (b, c, ROT_NQ + 2 * j + 1)),
            cmp_spec, cmp_spec,
            seq_spec(ROT_NKS), seq_spec(PL_NVS), seq_spec(ROT_NKW), seq_spec(PL_NVW),
            pl.BlockSpec((qc, LANES), lambda b, j, c: (b * (SEQ // qc) + c, j)),
            pl.BlockSpec(ovt.shape, lambda b, j, c: (0, 0)),
        ],
        out_specs=pl.BlockSpec((1, qc, 2 * LANES), lambda b, j, c: (b, c, j)),
        scratch_shapes=[pltpu.VMEM((LANES, N_CMP_PAD), BF16),
                        pltpu.VMEM((SEQ // NSA_KT, VT_ROWS, NSA_KT), BF16),
                        pltpu.VMEM((SEQ // qc, VT_ROWS, qc), BF16),
                        pltpu.VMEM((N_SLC, qc), F32)],
        compiler_params=_params("parallel", "parallel", "arbitrary"),
        name="nsa",
    )(z_rot, z_rot, k_cmp, v_cmp, z_rot, z_pl, z_rot, z_pl, gate_logits, ovt)


DIL_QC = 128
DIL_STEPS = SEQ // DIL_QC
DIL_UNROLL = 8


def _dil_group(q_ref, k_ref, v_ref, og_ref, lg_ref, gi):
    window, dil = DIL_CONFIGS[gi]
    qc = DIL_QC
    m = SEQ // dil
    n_back = window // dil
    nk = min(m, qc + -(-n_back // qc) * qc)
    chunks = m // qc

    def rows(first, n):
        return pl.ds(first, n) if dil == 1 else pl.ds(first, n, stride=dil)

    def block(idx):
        r = idx // chunks
        q0 = (idx % chunks) * qc
        start = jnp.maximum(q0 - (nk - qc), 0)
        q_rows = rows(r + dil * q0, qc)
        k_rows = rows(r + dil * start, nk)
        qs = _stack_heads(q_ref[0, q_rows, :].astype(BF16))
        v_t = _transpose_aug(v_ref[0, k_rows, :])
        band = _band_bias_t(nk, qc, q0 - start, n_back)
        s_t = _nt(k_ref[0, k_rows, :].astype(BF16), qs) + _tile_lanes(band, 2)
        o_t, lse = _softmax_block_t(s_t, lambda p: _nn(v_t, p))
        lse_b = jnp.broadcast_to(lse, (LANES, 2 * qc))
        og_ref[gi, q_rows, :] = jnp.transpose(_merge_pair_t(o_t[:, :qc], o_t[:, qc:]))
        lg_ref[gi, q_rows, :] = jnp.transpose(_merge_pair_t(lse_b[:, :qc], lse_b[:, qc:]))

    def body(i, _):
        for u in range(DIL_UNROLL):
            block(i * DIL_UNROLL + u)
        return 0

    lax.fori_loop(0, DIL_STEPS // DIL_UNROLL, body, 0)


def _dil_kernel(q_ref, k_ref, v_ref, o_ref, og_ref, lg_ref):
    g = pl.program_id(2)
    n_groups = len(DIL_CONFIGS)
    for gi in range(n_groups):
        pl.when(g == gi)(functools.partial(_dil_group, q_ref, k_ref, v_ref, og_ref, lg_ref, gi))

    @pl.when(g == n_groups - 1)
    def _():
        rows = 512

        def body(i, _):
            sl = pl.ds(pl.multiple_of(i * rows, rows), rows)
            l0, l1, l2 = lg_ref[0, sl, :], lg_ref[1, sl, :], lg_ref[2, sl, :]
            mx = jnp.maximum(jnp.maximum(l0, l1), l2)
            e0, e1, e2 = jnp.exp2(l0 - mx), jnp.exp2(l1 - mx), jnp.exp2(l2 - mx)
            den = e0 + e1 + e2
            out = (e0 / den) * og_ref[0, sl, :] + (e1 / den) * og_ref[1, sl, :] + (e2 / den) * og_ref[2, sl, :]
            o_ref[0, sl, :] = out.astype(o_ref.dtype)
            return 0

        lax.fori_loop(0, SEQ // rows, body, 0)


def _dilated(zd_rot, zd_pl):
    n_groups = len(DIL_CONFIGS)
    width = DIL_HEADS_PER_GROUP * HEAD_DIM
    col = lambda base: (lambda b, p, g: (b, 0, base + 2 * g + p))
    blk = lambda base: pl.BlockSpec((1, SEQ, LANES), col(base))
    return pl.pallas_call(
        _dil_kernel,
        out_shape=jax.ShapeDtypeStruct((BATCH, SEQ, width), BF16),
        grid=(BATCH, 2, n_groups),
        in_specs=[blk(0), blk(DIL_BLOCKS), blk(0)],
        out_specs=pl.BlockSpec((1, SEQ, LANES), lambda b, p, g: (b, 0, p)),
        scratch_shapes=[pltpu.VMEM((n_groups, SEQ, LANES), F32), pltpu.VMEM((n_groups, SEQ, LANES), F32)],
        compiler_params=_params("parallel", "parallel", "arbitrary"),
        name="dilated",
    )(zd_rot, zd_rot, zd_pl)


def _out_proj_kernel(oa_ref, ob_ref, oc_ref, h_ref, wa_ref, wb_ref, wc_ref, g_ref, b_ref, h1_ref, h1b_ref):
    y = _nn(oa_ref[...], wa_ref[...]) + _nn(ob_ref[...], wb_ref[...]) + _nn(oc_ref[...], wc_ref[...])
    h1 = _layer_norm(DEEPNORM_ALPHA * h_ref[...] + y, g_ref[...], b_ref[...])
    h1_ref[...] = h1
    h1b_ref[...] = h1.astype(BF16)


def _out_proj(oa, ob, oc, h, wa, wb, wc, g, b):
    tm = 512
    rows = lambda w: pl.BlockSpec((tm, w), lambda i: (i, 0))
    full = lambda a: pl.BlockSpec(a.shape, lambda i: (0, 0))
    return pl.pallas_call(
        _out_proj_kernel,
        out_shape=(jax.ShapeDtypeStruct((TOKENS, D_MODEL), F32),
                   jax.ShapeDtypeStruct((TOKENS, D_MODEL), BF16)),
        grid=(TOKENS // tm,),
        in_specs=[rows(oa.shape[1]), rows(ob.shape[1]), rows(oc.shape[1]), rows(D_MODEL),
                  full(wa), full(wb), full(wc), full(g), full(b)],
        out_specs=(rows(D_MODEL), rows(D_MODEL)),
        compiler_params=_params("parallel"),
        name="out_proj_ln",
    )(oa, ob, oc, h, wa, wb, wc, g, b)


def _router_kernel(hb_ref, rw_ref, rb_ref, comb_ref, sel_ref):
    logits = _nt(rw_ref[...], hb_ref[...]) + rb_ref[...]
    mx = jnp.max(logits, axis=0, keepdims=True)
    ex = jnp.exp(logits - mx)
    probs = ex / jnp.sum(ex, axis=0, keepdims=True)
    p = [probs[e:e + 1, :] for e in range(N_EXPERTS)]
    best, g_sel = None, None
    for g in range(N_GROUPS):
        a, b, c, d = p[4 * g:4 * g + 4]
        hi1, lo1, hi2, lo2 = jnp.maximum(a, b), jnp.minimum(a, b), jnp.maximum(c, d), jnp.minimum(c, d)
        top2 = jnp.maximum(hi1, hi2) + jnp.maximum(jnp.minimum(hi1, hi2), jnp.maximum(lo1, lo2))
        if g == 0:
            best, g_sel = top2, jnp.zeros_like(top2)
        else:
            better = top2 > best
            best = jnp.where(better, top2, best)
            g_sel = jnp.where(better, float(g), g_sel)
    chosen, picked = [], []
    for e in range(N_EXPERTS):
        g = e // EXPERTS_PER_GROUP
        rank = jnp.zeros_like(best)
        for o in range(4 * g, 4 * g + 4):
            if o < e:
                rank = rank + jnp.where(p[o] >= p[e], 1.0, 0.0)
            elif o > e:
                rank = rank + jnp.where(p[o] > p[e], 1.0, 0.0)
        chosen.append(jnp.where((g_sel == float(g)) & (rank < 2.0), 1.0, 0.0))
        picked.append(chosen[e] * p[e])
    total = picked[0]
    for e in range(1, N_EXPERTS):
        total = total + picked[e]
    comb_ref[...] = jnp.concatenate(picked, axis=0) / total
    sel_ref[...] = jnp.concatenate(chosen, axis=0)


def _router(hb, rw_t, rb):
    tm = 1024
    out = jax.ShapeDtypeStruct((N_EXPERTS, TOKENS), F32)
    o_spec = pl.BlockSpec((N_EXPERTS, tm), lambda i: (0, i))
    return pl.pallas_call(
        _router_kernel,
        out_shape=(out, out),
        grid=(TOKENS // tm,),
        in_specs=[pl.BlockSpec((tm, D_MODEL), lambda i: (i, 0)),
                  pl.BlockSpec((N_EXPERTS, D_MODEL), lambda i: (0, 0)),
                  pl.BlockSpec((N_EXPERTS, 1), lambda i: (0, 0))],
        out_specs=(o_spec, o_spec),
        compiler_params=_params("parallel"),
        name="router",
    )(hb, rw_t, rb)


def _routing_tables(comb_t, sel_t):
    sel = sel_t > 0.5
    cnt = jnp.sum(sel, axis=1, dtype=jnp.int32)
    cnt_pad = (cnt + (MOE_TILE - 1)) // MOE_TILE * MOE_TILE
    ends = jnp.cumsum(cnt_pad)
    rank = jnp.cumsum(sel.astype(jnp.int32), axis=1) - 1
    pos = (ends - cnt_pad)[:, None] + rank
    pos_lo = jnp.min(jnp.where(sel, pos, MOE_ROWS), axis=0)
    pos_hi = jnp.max(jnp.where(sel, pos, -1), axis=0)
    w_lo = jnp.sum(jnp.where(sel & (pos == pos_lo), comb_t, 0.0), axis=0)
    w_hi = jnp.sum(jnp.where(sel & (pos == pos_hi), comb_t, 0.0), axis=0)
    w = jnp.zeros((TOKENS, LANES), F32).at[:, 0].set(w_lo).at[:, 1].set(w_hi)
    n_tiles = ends[-1] // MOE_TILE
    tile_start = jnp.arange(MOE_TILES, dtype=jnp.int32) * MOE_TILE
    tile_start = jnp.minimum(tile_start, ends[-1] - MOE_TILE)
    tile_expert = jnp.sum((ends[None, :] <= tile_start[:, None]).astype(jnp.int32), axis=1)
    return jnp.stack([pos_lo, pos_hi]).astype(jnp.int32), w, tile_expert, n_tiles.reshape(1).astype(jnp.int32)


MOE_TILE = 256
MOE_TILES = 2 * TOKENS // MOE_TILE + N_EXPERTS
MOE_ROWS = MOE_TILES * MOE_TILE
SLAB = D_MODEL // LANES


def _to_slabs(ref, x, rows):
    for j in range(SLAB):
        ref[pl.ds(j, rows, stride=SLAB), :] = x[:, j * LANES:(j + 1) * LANES]


def _from_slabs(ref, rows):
    return jnp.concatenate([ref[pl.ds(j, rows, stride=SLAB), :] for j in range(SLAB)], axis=1)


def _slab_rows(row, n=SLAB):
    return pl.ds(pl.multiple_of(row * n, n), n)


XSLAB = SLAB // 2
U32 = jnp.uint32


def _to_packed_slabs(ref, x, rows):
    bits = lambda t: lax.bitcast_convert_type(t.astype(BF16).astype(F32), U32)
    for j in range(XSLAB):
        hi = bits(x[:, 2 * j * LANES:(2 * j + 1) * LANES])
        lo = bits(x[:, (2 * j + 1) * LANES:(2 * j + 2) * LANES])
        ref[pl.ds(j, rows, stride=XSLAB), :] = hi | (lo >> 16)


def _from_packed_slabs(ref, rows):
    parts = []
    for j in range(XSLAB):
        u = ref[pl.ds(j, rows, stride=XSLAB), :]
        parts.append(lax.bitcast_convert_type(u & jnp.uint32(0xFFFF0000), F32).astype(BF16))
        parts.append(lax.bitcast_convert_type(u << 16, F32).astype(BF16))
    return jnp.concatenate(parts, axis=1)


def _dispatch_kernel(pos_ref, h_ref, init_ref, xs_ref, slab_ref, sem):
    del init_ref
    tm = h_ref.shape[0]
    base = pl.program_id(0) * tm
    _to_packed_slabs(slab_ref, h_ref[...], tm)

    def copy(t, which):
        return pltpu.make_async_copy(slab_ref.at[_slab_rows(t, XSLAB), :],
                                     xs_ref.at[_slab_rows(pos_ref[which, base + t], XSLAB), :], sem)

    def start(t, _):
        copy(t, 0).start()
        copy(t, 1).start()
        return 0

    lax.fori_loop(0, tm, start, 0, unroll=8)
    whole = pltpu.make_async_copy(slab_ref, xs_ref.at[pl.ds(0, tm * XSLAB), :], sem)
    whole.wait()
    whole.wait()


def _dispatch(pos, h):
    tm = 256
    grid_spec = pltpu.PrefetchScalarGridSpec(
        num_scalar_prefetch=1,
        grid=(TOKENS // tm,),
        in_specs=[pl.BlockSpec((tm, D_MODEL), lambda i, pos: (i, 0)),
                  pl.BlockSpec(memory_space=pl.ANY)],
        out_specs=pl.BlockSpec(memory_space=pl.ANY),
        scratch_shapes=[pltpu.VMEM((tm * XSLAB, LANES), U32), pltpu.SemaphoreType.DMA],
    )
    return pl.pallas_call(
        _dispatch_kernel,
        out_shape=jax.ShapeDtypeStruct((MOE_ROWS * XSLAB, LANES), U32),
        grid_spec=grid_spec,
        input_output_aliases={2: 0},
        compiler_params=_params("arbitrary"),
        name="moe_dispatch",
    )(pos, h, jnp.zeros((MOE_ROWS * XSLAB, LANES), U32))


def _experts_kernel(te_ref, nt_ref, xs_ref, wg_ref, wu_ref, wd_ref, ys_ref, wgb_ref, wub_ref, wdb_ref):
    k = pl.program_id(0)
    e = te_ref[k]
    e_prev = te_ref[jnp.maximum(k - 1, 0)]

    @pl.when((k == 0) | (e != e_prev))
    def _():
        wgb_ref[...] = wg_ref[0, 0].astype(BF16)
        wub_ref[...] = wu_ref[0, 0].astype(BF16)
        wdb_ref[...] = wd_ref[0, 0].astype(BF16)

    @pl.when(k < nt_ref[0])
    def _():
        x = _from_packed_slabs(xs_ref, MOE_TILE)
        hid = jax.nn.silu(_nn(x, wgb_ref[...])) * _nn(x, wub_ref[...])
        _to_slabs(ys_ref, _nn(hid.astype(BF16), wdb_ref[...]), MOE_TILE)

    @pl.when(k >= nt_ref[0])
    def _():
        ys_ref[...] = jnp.zeros(ys_ref.shape, F32)


def _experts(tile_expert, n_tiles, xs, wg, wu, wd, layer):
    w_in_spec = pl.BlockSpec((1, 1, D_MODEL, EXPERT_HIDDEN), lambda k, te, nt: (layer, te[k], 0, 0))
    grid_spec = pltpu.PrefetchScalarGridSpec(
        num_scalar_prefetch=2,
        grid=(MOE_TILES,),
        in_specs=[pl.BlockSpec((MOE_TILE * XSLAB, LANES), lambda k, te, nt: (jnp.minimum(k, nt[0] - 1), 0)),
                  w_in_spec, w_in_spec,
                  pl.BlockSpec((1, 1, EXPERT_HIDDEN, D_MODEL), lambda k, te, nt: (layer, te[k], 0, 0))],
        out_specs=pl.BlockSpec((MOE_TILE * SLAB, LANES), lambda k, te, nt: (k, 0)),
        scratch_shapes=[pltpu.VMEM((D_MODEL, EXPERT_HIDDEN), BF16), pltpu.VMEM((D_MODEL, EXPERT_HIDDEN), BF16),
                        pltpu.VMEM((EXPERT_HIDDEN, D_MODEL), BF16)],
    )
    return pl.pallas_call(
        _experts_kernel,
        out_shape=jax.ShapeDtypeStruct((MOE_ROWS * SLAB, LANES), F32),
        grid_spec=grid_spec,
        compiler_params=_params("arbitrary"),
        name="moe_experts",
    )(tile_expert, n_tiles, xs, wg, wu, wd)


def _ple_ln_kernel(pos_ref, hb_ref, h_ref, ys_ref, w_ref, p_ref, gw_ref, gb_ref, pw_ref, g_ref, b_ref,
                   h2_ref, h2b_ref, lo_ref, hi_ref, sem):
    tm = h_ref.shape[0]
    i = pl.program_id(0)
    slot = i & 1
    bufs = (lo_ref, hi_ref)

    def fetch(tile, into):
        def start(t, _):
            for which in range(2):
                pltpu.make_async_copy(ys_ref.at[_slab_rows(pos_ref[which, tile * tm + t]), :],
                                      bufs[which].at[into, _slab_rows(t), :], sem.at[into]).start()
            return 0
        lax.fori_loop(0, tm, start, 0, unroll=8)

    @pl.when(i == 0)
    def _():
        fetch(0, 0)

    @pl.when(i + 1 < pl.num_programs(0))
    def _():
        fetch(i + 1, 1 - slot)

    gate = jax.nn.sigmoid(_nn(hb_ref[...], gw_ref[...]) + gb_ref[...])
    ple = gate * _nn(p_ref[...].astype(BF16), pw_ref[...])
    for which in range(2):
        pltpu.make_async_copy(ys_ref.at[pl.ds(0, tm * SLAB), :], bufs[which].at[slot], sem.at[slot]).wait()
    w = w_ref[...]
    ffn = w[:, 0:1] * _from_slabs(lo_ref.at[slot], tm) + w[:, 1:2] * _from_slabs(hi_ref.at[slot], tm)
    h2 = _layer_norm(DEEPNORM_ALPHA * h_ref[...] + ffn + ple, g_ref[...], b_ref[...])
    h2_ref[...] = h2
    h2b_ref[...] = h2.astype(BF16)


def _ple_ln(pos, hb, h, ys, w, p, layer, gw, gb, pw, g, b):
    tm = 256
    p_spec = pl.BlockSpec((tm, PLE_DIM), lambda i, pos: (layer * (TOKENS // tm) + i, 0))
    rows = lambda width: pl.BlockSpec((tm, width), lambda i, pos: (i, 0))
    full = lambda a: pl.BlockSpec(a.shape, lambda i, pos: (0, 0))
    grid_spec = pltpu.PrefetchScalarGridSpec(
        num_scalar_prefetch=1,
        grid=(TOKENS // tm,),
        in_specs=[rows(D_MODEL), rows(D_MODEL), pl.BlockSpec(memory_space=pl.ANY), rows(LANES), p_spec,
                  full(gw), full(gb), full(pw), full(g), full(b)],
        out_specs=(rows(D_MODEL), rows(D_MODEL)),
        scratch_shapes=[pltpu.VMEM((2, tm * SLAB, LANES), F32), pltpu.VMEM((2, tm * SLAB, LANES), F32),
                        pltpu.SemaphoreType.DMA((2,))],
    )
    return pl.pallas_call(
        _ple_ln_kernel,
        out_shape=(jax.ShapeDtypeStruct((TOKENS, D_MODEL), F32),
                   jax.ShapeDtypeStruct((TOKENS, D_MODEL), BF16)),
        grid_spec=grid_spec,
        compiler_params=_params("arbitrary"),
        name="ple_ln",
    )(pos, hb, h, ys, w, p, gw, gb, pw, g, b)


def _rope_tables(positions):
    half = ROT_DIM // 2
    inv_freq = jnp.exp(jnp.arange(half, dtype=F32) * (-2.0 * math.log(ROPE_THETA) / ROT_DIM))
    ang = positions.astype(F32)[:, :, None] * inv_freq
    cos, sin = jnp.cos(ang), jnp.sin(ang)
    zeros = jnp.zeros_like(cos)
    rest = HEAD_DIM - ROT_DIM
    pad = lambda v: jnp.broadcast_to(jnp.asarray(v, F32), cos.shape[:2] + (rest,))
    c = jnp.concatenate([cos, cos, pad(1.0)], axis=-1)
    s1 = jnp.concatenate([-sin, zeros, pad(0.0)], axis=-1)
    s2 = jnp.concatenate([zeros, sin, pad(0.0)], axis=-1)
    tile = lambda t: jnp.concatenate([t, t], axis=-1).reshape(TOKENS, LANES)
    return tile(c), tile(s1), tile(s2)


def _split_w_in(w):
    mw, nq, nkv, dw = MOBA_HEADS * HEAD_DIM, NSA_HEADS * HEAD_DIM, NSA_KV_HEADS * HEAD_DIM, DIL_HEADS * HEAD_DIM
    widths = (mw, mw, mw, nq) + (nkv,) * 6 + (NSA_HEADS * 3, dw, dw, dw)
    offs = np.concatenate([[0], np.cumsum(widths)])
    qa, ka, va, qb, kbc, vbc, kbs, vbs, kbw, vbw, gb, qc, kc, vc = (
        w[:, int(offs[i]):int(offs[i + 1])] for i in range(len(widths)))

    def dup(t):
        t = t.reshape(D_MODEL, NSA_KV_HEADS, 1, HEAD_DIM)
        return jnp.broadcast_to(t, (D_MODEL, NSA_KV_HEADS, 2, HEAD_DIM)).reshape(D_MODEL, NSA_KV_HEADS * LANES)

    zpad = lambda n: jnp.zeros((D_MODEL, n * LANES), w.dtype)
    w_rot = jnp.concatenate([qa * Q_SCALE, ka, qb * Q_SCALE, dup(kbc), dup(kbs), dup(kbw), zpad(1)], axis=1)
    w_pl = jnp.concatenate([va, dup(vbc), dup(vbs), dup(vbw), zpad(3)], axis=1)
    gpad = jnp.zeros((D_MODEL, NSA_KV_HEADS, LANES - 12), w.dtype)
    w_gl = jnp.concatenate([gb.reshape(D_MODEL, NSA_KV_HEADS, 12), gpad], axis=-1).reshape(D_MODEL, -1)
    w_dil_rot = jnp.concatenate([qc * Q_SCALE, kc], axis=1)
    return tuple(t.astype(BF16) for t in (w_rot, w_pl, w_gl, w_dil_rot, vc))


def _overlap_table():
    starts = np.arange(N_CMP) * CMP_STRIDE
    slc = np.arange(N_SLC) * SLC_BLOCK
    ov = ((starts[:, None] < slc[None, :] + SLC_BLOCK) & (starts[:, None] + CMP_LEN > slc[None, :]))
    ovt = np.zeros((N_SLC, N_CMP_PAD), np.float32)
    ovt[:, :N_CMP] = ov.T
    return jnp.asarray(ovt, BF16)


def _cmp_chunks(z, base):
    nblk = z.shape[-1] // LANES
    t = z.reshape(BATCH, SEQ // CMP_STRIDE, CMP_STRIDE, nblk, LANES)[:, :, :, base:base + NSA_KV_HEADS, :HEAD_DIM]
    return t.transpose(0, 3, 1, 2, 4).reshape(BATCH, NSA_KV_HEADS, SEQ // CMP_STRIDE, CMP_STRIDE * HEAD_DIM)


def kernel(x, p, positions, ln_in_g, ln_in_b, w_in, w_out, nsa_ck1, nsa_ck2, nsa_pe_k, nsa_cv1, nsa_cv2, nsa_pe_v, ln1_g, ln1_b, router_w, router_b, w_gate, w_up, w_down, ple_proj, ple_gate_w, ple_gate_b, ln2_g, ln2_b):
    rope = _rope_tables(positions)
    ovt = _overlap_table()
    rw_t = router_w.T.astype(BF16)
    rb = router_b.reshape(N_EXPERTS, 1).astype(F32)
    chunk_w = CMP_STRIDE * HEAD_DIM
    vec = lambda v: v.reshape(1, -1)
    seq3 = lambda t: t.reshape(BATCH, SEQ, t.shape[-1])
    flat = lambda t: t.reshape(TOKENS, t.shape[-1])

    h, hb = _ln_in(x.reshape(TOKENS, D_MODEL), ln_in_g, ln_in_b)
    for i in range(DEPTH):
        w_rot, w_pl, w_gl, w_dil_rot, w_dil_pl = _split_w_in(w_in[i])
        z_rot = seq3(_project(hb, w_rot, BF16, 768, rope=rope))
        z_pl = seq3(_project(hb, w_pl, BF16, 1024))
        gate_logits = _project(hb, w_gl, F32, NSA_KV_HEADS * LANES)
        zd_rot = seq3(_project(hb, w_dil_rot, F32, 768, rope=rope))
        zd_pl = seq3(_project(hb, w_dil_pl, F32, 768))

        o_a = _moba(z_rot, z_pl)

        dup2 = lambda w2: jnp.concatenate([w2, w2], axis=1).astype(BF16)
        k_cmp, v_cmp = _compress(
            _cmp_chunks(z_rot, ROT_NKC), _cmp_chunks(z_pl, PL_NVC),
            nsa_pe_k[i].reshape(2, chunk_w), nsa_pe_v[i].reshape(2, chunk_w),
            nsa_ck1[i].reshape(2, chunk_w, CMP_HIDDEN).astype(BF16), dup2(nsa_ck2[i]),
            nsa_cv1[i].reshape(2, chunk_w, CMP_HIDDEN).astype(BF16), dup2(nsa_cv2[i]))
        o_b = _nsa(z_rot, z_pl, k_cmp, v_cmp, gate_logits, ovt)

        o_c = _dilated(zd_rot, zd_pl)

        wo = w_out[i].astype(BF16)
        a_w, b_w = MOBA_HEADS * HEAD_DIM, NSA_HEADS * HEAD_DIM
        h, hb = _out_proj(flat(o_a), flat(o_b), flat(o_c), h,
                          wo[:a_w], wo[a_w:a_w + b_w], wo[a_w + b_w:], vec(ln1_g[i]), vec(ln1_b[i]))

        pos, w_tok, tile_expert, n_tiles = _routing_tables(*_router(hb, rw_t, rb))
        xs = _dispatch(pos, h)
        ys = _experts(tile_expert, n_tiles, xs, w_gate, w_up, w_down, i)
        h, hb = _ple_ln(pos, hb, h, ys, w_tok, p.reshape(DEPTH * TOKENS, PLE_DIM), i, ple_gate_w[i].astype(BF16),
                        vec(ple_gate_b[i]), ple_proj[i].astype(BF16), vec(ln2_g[i]), vec(ln2_b[i]))
    return h.reshape(BATCH, SEQ, D_MODEL)
```

```python
import functools
import math

import numpy as np
import jax
import jax.numpy as jnp
from jax import lax
from jax.experimental import pallas as pl
from jax.experimental.pallas import tpu as pltpu

F32 = jnp.float32
BF16 = jnp.bfloat16

D_MODEL = 2048
BATCH = 2
SEQ = 4096
DEPTH = 4
TOKENS = BATCH * SEQ
HEAD_DIM = 64
ROT_DIM = HEAD_DIM // 4
ROPE_THETA = 500000.0
NEG = -1e30
FORCE = 1e30
LN_EPS = 1e-5
SCALE = HEAD_DIM ** -0.5
LOG2_E = math.log2(math.e)
Q_SCALE = SCALE * LOG2_E

MOBA_HEADS = 8
MOBA_BLOCK = 256
MOBA_TOPK = 3
MOBA_NB = SEQ // MOBA_BLOCK

NSA_HEADS = 12
NSA_KV_HEADS = 3
CMP_LEN = 32
CMP_STRIDE = 16
CMP_HIDDEN = 128
N_CMP = (SEQ - CMP_LEN) // CMP_STRIDE + 1
N_CMP_PAD = 256
SLC_BLOCK = 64
SLC_TOPK = 16
SLC_LOCAL = 2
N_SLC = SEQ // SLC_BLOCK
NSA_WINDOW = 512

DIL_CONFIGS = ((128, 1), (512, 4), (2048, 16))
DIL_HEADS_PER_GROUP = 4
DIL_HEADS = DIL_HEADS_PER_GROUP * len(DIL_CONFIGS)

N_EXPERTS = 16
N_GROUPS = 4
EXPERTS_PER_GROUP = 4
EXPERT_HIDDEN = D_MODEL // 4
PLE_DIM = 256

DEEPNORM_ALPHA = (2 * DEPTH) ** 0.25

LANES = 128
VMEM_LIMIT = 56 * 1024 * 1024

ROT_MQ, ROT_MK, ROT_NQ, ROT_NKC, ROT_NKS, ROT_NKW = 0, 4, 8, 14, 17, 20
ROT_BLOCKS = 24
PL_MV, PL_NVC, PL_NVS, PL_NVW = 0, 4, 7, 10
PL_BLOCKS = 16
DIL_BLOCKS = DIL_HEADS // 2

NT_DIMS = (((1,), (1,)), ((), ()))


def _nt(a, b):
    return lax.dot_general(a, b, NT_DIMS, preferred_element_type=F32)


def _nn(a, b):
    return jnp.dot(a, b, preferred_element_type=F32)


def _params(*sem):
    return pltpu.CompilerParams(dimension_semantics=sem, vmem_limit_bytes=VMEM_LIMIT)


def _layer_norm(y, g, b):
    mu = jnp.mean(y, axis=-1, keepdims=True)
    yc = y - mu
    var = jnp.mean(yc * yc, axis=-1, keepdims=True)
    return yc * lax.rsqrt(var + LN_EPS) * g + b


def _ln_kernel(x_ref, g_ref, b_ref, h_ref, hb_ref):
    h = _layer_norm(x_ref[...], g_ref[...], b_ref[...])
    h_ref[...] = h
    hb_ref[...] = h.astype(BF16)


def _ln_in(x, g, b):
    tm = 512
    row = pl.BlockSpec((tm, D_MODEL), lambda i: (i, 0))
    vec = pl.BlockSpec((1, D_MODEL), lambda i: (0, 0))
    return pl.pallas_call(
        _ln_kernel,
        out_shape=(jax.ShapeDtypeStruct((TOKENS, D_MODEL), F32),
                   jax.ShapeDtypeStruct((TOKENS, D_MODEL), BF16)),
        grid=(TOKENS // tm,),
        in_specs=[row, vec, vec],
        out_specs=(row, row),
        compiler_params=_params("parallel"),
        name="ln_in",
    )(x, g.reshape(1, -1), b.reshape(1, -1))


def _proj_kernel(x_ref, w_ref, o_ref):
    o_ref[...] = _nn(x_ref[...], w_ref[...]).astype(o_ref.dtype)


def _proj_rot_kernel(x_ref, w_ref, c_ref, s1_ref, s2_ref, o_ref):
    x = x_ref[...]
    c, s1, s2 = c_ref[...], s1_ref[...], s2_ref[...]
    half = ROT_DIM // 2
    for j0 in range(0, o_ref.shape[1], 2 * LANES):
        z = _nn(x, w_ref[:, j0:j0 + 2 * LANES])
        for j in range(j0, j0 + 2 * LANES, LANES):
            zc = z[:, j - j0:j - j0 + LANES]
            r = zc * c + pltpu.roll(zc, LANES - half, 1) * s1 + pltpu.roll(zc, half, 1) * s2
            o_ref[:, j:j + LANES] = r.astype(o_ref.dtype)


def _project(hb, w, out_dtype, tn, rope=None):
    tm = 1024
    n = w.shape[1]
    x_spec = pl.BlockSpec((tm, D_MODEL), lambda i, j: (i, 0))
    w_spec = pl.BlockSpec((D_MODEL, tn), lambda i, j: (0, j))
    o_spec = pl.BlockSpec((tm, tn), lambda i, j: (i, j))
    if rope is None:
        kern, extra, extra_specs = _proj_kernel, (), []
    else:
        t_spec = pl.BlockSpec((tm, LANES), lambda i, j: (i, 0))
        kern, extra, extra_specs = _proj_rot_kernel, rope, [t_spec] * 3
    return pl.pallas_call(
        kern,
        out_shape=jax.ShapeDtypeStruct((TOKENS, n), out_dtype),
        grid=(TOKENS // tm, n // tn),
        in_specs=[x_spec, w_spec] + extra_specs,
        out_specs=o_spec,
        compiler_params=_params("parallel", "arbitrary"),
        name="in_proj_rot" if rope is not None else "in_proj",
    )(hb, w, *extra)


def _stack_heads(*q_blocks):
    parts = []
    for q in q_blocks:
        lane = lax.broadcasted_iota(jnp.int32, q.shape, 1)
        zero = jnp.zeros_like(q)
        parts += [jnp.where(lane < HEAD_DIM, q, zero), jnp.where(lane >= HEAD_DIM, q, zero)]
    return jnp.concatenate(parts, axis=0)


def _merge_pair_t(lo, hi):
    sub = lax.broadcasted_iota(jnp.int32, lo.shape, 0)
    return jnp.where(sub < HEAD_DIM, lo, hi)


def _band_bias_t(nk, qc, offset, n_back):
    key = lax.broadcasted_iota(jnp.int32, (nk, qc), 0)
    qry = lax.broadcasted_iota(jnp.int32, (nk, qc), 1)
    diff = offset + qry - key
    return jnp.where((diff >= 0) & (diff <= n_back), 0.0, NEG)


def _tile_lanes(x, n):
    return jnp.concatenate([x] * n, axis=1)


def _transpose_bf16(v):
    return jnp.transpose(v.astype(F32)).astype(BF16)


def _tree(x, op):
    n = x.shape[0]
    if n == 8:
        return x
    if n % 16 == 0:
        return op(_tree(x[:n // 2], op), _tree(x[n // 2:], op))
    acc = x[:8]
    for i in range(1, n // 8):
        acc = op(acc, x[8 * i:8 * i + 8])
    return acc


def _reduce_keys(x, op, final):
    return final(_tree(x, op), axis=0, keepdims=True)


VT_ROWS = LANES + 16


def _transpose_aug(v):
    vt = jnp.transpose(v.astype(F32))
    sub = lax.broadcasted_iota(jnp.int32, (VT_ROWS - LANES, v.shape[0]), 0)
    return jnp.concatenate([vt, jnp.where(sub == 0, 1.0, 0.0)], axis=0).astype(BF16)


def _probs(s_t, m):
    return jnp.exp2((s_t - m).astype(BF16))


def _normalise(acc):
    l = acc[LANES:LANES + 1]
    return acc[:LANES] / l, l


def _softmax_block_t(s_t, pv):
    m = _reduce_keys(s_t, jnp.maximum, jnp.max)
    out, l = _normalise(pv(_probs(s_t, m)))
    return out, m + jnp.log(l) * LOG2_E


def _online_step_t(carry, s_t, m_t, pv):
    m, acc = carry
    m_new = jnp.maximum(m, m_t)
    acc = jnp.exp2(m - m_new) * acc + pv(_probs(s_t, m_new))
    return m_new, acc


def _flash_tiles(n_tiles, last_tile, init, scores, pv_of):
    strips = range(len(init))

    def produce(t):
        s = tuple(scores(t))
        return s, tuple(_reduce_keys(s_i, jnp.maximum, jnp.max) for s_i in s)

    def body(t, carry):
        state, s_t, m_t = carry
        s_next, m_next = produce(jnp.minimum(t + 1, last_tile))
        pv = pv_of(t)
        return tuple(_online_step_t(state[i], s_t[i], m_t[i], pv) for i in strips), s_next, m_next

    state, _, _ = lax.fori_loop(0, n_tiles, body, (tuple(init),) + produce(0))
    return state


def _pv_tiles(vt_ref, first, n, rows):
    def pv(p):
        acc = _nn(vt_ref[first], p[:rows])
        for j in range(1, n):
            acc = acc + _nn(vt_ref[first + j], p[j * rows:(j + 1) * rows])
        return acc
    return pv


def _online_init_t(r):
    return (jnp.full((1, r), NEG, F32), jnp.zeros((VT_ROWS, r), F32))


def _rank_rows(g, n_rows):
    sub = lax.broadcasted_iota(jnp.int32, (8, g.shape[1]), 0)
    rank = jnp.zeros(g.shape, F32)
    for m in range(n_rows):
        gm = g[m:m + 1, :]
        b = m // 8 * 8
        mid = g[b:b + 8]
        parts = [jnp.where(gm > mid, 1.0, jnp.where((gm == mid) & (sub > m - b), 1.0, 0.0))]
        if b > 0:
            parts.insert(0, jnp.where(gm > g[:b], 1.0, 0.0))
        if b + 8 < n_rows:
            parts.append(jnp.where(gm >= g[b + 8:], 1.0, 0.0))
        rank = rank + jnp.concatenate(parts, axis=0)
    return rank


MOBA_QC = 256
MOBA_KT = 2 * MOBA_BLOCK


def _moba_kernel(q_ref, k_ref, v_ref, o_ref, kmean_ref, vt_ref, bias_ref):
    c = pl.program_id(2)
    qc = MOBA_QC
    r = 2 * qc

    @pl.when(c == 0)
    def _():
        row = lax.broadcasted_iota(jnp.int32, (MOBA_NB, SEQ), 0)
        col = lax.broadcasted_iota(jnp.int32, (MOBA_NB, SEQ), 1)
        avg = jnp.where((col >> 8) == row, 1.0 / MOBA_BLOCK, 0.0).astype(BF16)
        kmean_ref[...] = _nn(avg, k_ref[0])
        for t in range(MOBA_NB):
            vt_ref[t] = _transpose_aug(v_ref[0, t * MOBA_BLOCK:(t + 1) * MOBA_BLOCK, :])

    qs = _stack_heads(q_ref[0])

    ks = pl.multiple_of(c * MOBA_BLOCK, MOBA_BLOCK)
    heads = [qs[:qc], qs[qc:]]
    causal = _band_bias_t(MOBA_BLOCK, qc, 0, MOBA_BLOCK)
    s_own = [_nt(k_ref[0, pl.ds(ks, MOBA_BLOCK), :], q_h) + causal for q_h in heads]
    state = [_online_step_t(_online_init_t(qc), s_h, _reduce_keys(s_h, jnp.maximum, jnp.max),
                            _pv_tiles(vt_ref, c, 1, MOBA_BLOCK)) for s_h in s_own]

    gate = _nt(kmean_ref[...].astype(BF16), qs)
    blk = lax.broadcasted_iota(jnp.int32, gate.shape, 0)
    past = blk < c
    rank = _rank_rows(jnp.where(past, gate, NEG), MOBA_NB)
    bias_ref[...] = jnp.where(past & (rank < MOBA_TOPK), 0.0, NEG)

    per_tile = MOBA_KT // MOBA_BLOCK

    def scores(t):
        ks = pl.multiple_of(t * MOBA_KT, MOBA_KT)
        kt = k_ref[0, pl.ds(ks, MOBA_KT), :]
        rows = [bias_ref[pl.ds(t * per_tile + j, 1), :] for j in range(per_tile)]
        out = []
        for h in range(2):
            blocks = [jnp.broadcast_to(row[:, h * qc:(h + 1) * qc], (MOBA_BLOCK, qc)) for row in rows]
            out.append(_nt(kt, heads[h]) + jnp.concatenate(blocks, axis=0))
        return out

    state = _flash_tiles((c + per_tile - 1) // per_tile, SEQ // MOBA_KT - 1, state, scores,
                         lambda t: _pv_tiles(vt_ref, t * per_tile, per_tile, MOBA_BLOCK))
    o_lo, o_hi = (_normalise(acc)[0] for _, acc in state)
    o_ref[0] = jnp.transpose(_merge_pair_t(o_lo, o_hi)).astype(o_ref.dtype)


def _moba(z_rot, z_pl):
    qc = MOBA_QC
    grid = (BATCH, MOBA_HEADS // 2, SEQ // qc)
    return pl.pallas_call(
        _moba_kernel,
        out_shape=jax.ShapeDtypeStruct((BATCH, SEQ, MOBA_HEADS * HEAD_DIM), BF16),
        grid=grid,
        in_specs=[
            pl.BlockSpec((1, qc, LANES), lambda b, p, c: (b, c, ROT_MQ + p)),
            pl.BlockSpec((1, SEQ, LANES), lambda b, p, c: (b, 0, ROT_MK + p)),
            pl.BlockSpec((1, SEQ, LANES), lambda b, p, c: (b, 0, PL_MV + p)),
        ],
        out_specs=pl.BlockSpec((1, qc, LANES), lambda b, p, c: (b, c, p)),
        scratch_shapes=[pltpu.VMEM((MOBA_NB, LANES), F32),
                        pltpu.VMEM((MOBA_NB, VT_ROWS, MOBA_BLOCK), BF16),
                        pltpu.VMEM((MOBA_NB, 2 * qc), F32)],
        compiler_params=_params("parallel", "parallel", "arbitrary"),
        name="moba",
    )(z_rot, z_rot, z_pl)


def _compress_one(x_ref, pe_ref, w1_ref, w2_ref, o_ref):
    x = x_ref[0, 0].astype(F32)
    top = (x + pe_ref[0:1, :]).astype(BF16)
    bot = (x + pe_ref[1:2, :]).astype(BF16)
    a = _nn(top, w1_ref[0])
    bm = _nn(bot, w1_ref[1])
    pre = a + pltpu.roll(bm, N_CMP_PAD - 1, 0)
    hid = jax.nn.gelu(pre)
    out = _nn(hid.astype(BF16), w2_ref[...])
    row = lax.broadcasted_iota(jnp.int32, out.shape, 0)
    o_ref[0, 0] = jnp.where(row < N_CMP, out, 0.0).astype(o_ref.dtype)


def _compress_kernel(xk_ref, xv_ref, pk_ref, pv_ref, k1_ref, k2_ref, v1_ref, v2_ref, ok_ref, ov_ref):
    _compress_one(xk_ref, pk_ref, k1_ref, k2_ref, ok_ref)
    _compress_one(xv_ref, pv_ref, v1_ref, v2_ref, ov_ref)


def _compress(xk, xv, pk, pv, k1, k2, v1, v2):
    chunk_w = CMP_STRIDE * HEAD_DIM
    x_spec = pl.BlockSpec((1, 1, N_CMP_PAD, chunk_w), lambda b, j: (b, j, 0, 0))
    pe_spec = pl.BlockSpec((2, chunk_w), lambda b, j: (0, 0))
    w1_spec = pl.BlockSpec((2, chunk_w, CMP_HIDDEN), lambda b, j: (0, 0, 0))
    w2_spec = pl.BlockSpec((CMP_HIDDEN, LANES), lambda b, j: (0, 0))
    o_spec = pl.BlockSpec((1, 1, N_CMP_PAD, LANES), lambda b, j: (b, j, 0, 0))
    o_shape = jax.ShapeDtypeStruct((BATCH, NSA_KV_HEADS, N_CMP_PAD, LANES), BF16)
    return pl.pallas_call(
        _compress_kernel,
        out_shape=(o_shape, o_shape),
        grid=(BATCH, NSA_KV_HEADS),
        in_specs=[x_spec, x_spec, pe_spec, pe_spec, w1_spec, w2_spec, w1_spec, w2_spec],
        out_specs=(o_spec, o_spec),
        compiler_params=_params("parallel", "parallel"),
        name="nsa_compress",
    )(xk, xv, pk, pv, k1, k2, v1, v2)


NSA_QC = 256
NSA_KT = 512
NSA_G = NSA_HEADS // NSA_KV_HEADS
NSA_WIN_TILES = NSA_WINDOW // NSA_QC + 1


def _nsa_kernel(qa_ref, qb_ref, kc_ref, vc_ref, ks_ref, vs_ref, kw_ref, vw_ref, gl_ref, ovt_ref,
                o_ref, vct_ref, vst_ref, vwt_ref, bias_ref):
    c = pl.program_id(2)
    qc = NSA_QC
    q0 = c * qc
    lanes_of = lambda t, i: t[:, i * qc:(i + 1) * qc]

    @pl.when(c == 0)
    def _():
        vct_ref[...] = _transpose_bf16(vc_ref[0, 0])
        for t in range(SEQ // NSA_KT):
            vst_ref[t] = _transpose_aug(vs_ref[0, t * NSA_KT:(t + 1) * NSA_KT, :])
        for t in range(SEQ // qc):
            vwt_ref[t] = _transpose_aug(vw_ref[0, t * qc:(t + 1) * qc, :])

    qs = _stack_heads(qa_ref[0], qb_ref[0])

    sc_t = _nt(kc_ref[0, 0], qs)
    t0 = jnp.maximum(c - NSA_WINDOW // qc, 0)
    start = pl.multiple_of(t0 * qc, qc)
    sw_t = _nt(kw_ref[0, pl.ds(start, NSA_WIN_TILES * qc), :], qs)

    n_idx = lax.broadcasted_iota(jnp.int32, (N_CMP_PAD, qc), 0)
    q_idx = lax.broadcasted_iota(jnp.int32, (N_CMP_PAD, qc), 1)
    ok = (n_idx * CMP_STRIDE + (CMP_LEN - 1)) <= (q0 + q_idx)
    p_heads = []
    for i in range(NSA_G):
        s_i = jnp.where(ok, lanes_of(sc_t, i), NEG)
        e_i = jnp.where(ok, jnp.exp2(s_i - _reduce_keys(s_i, jnp.maximum, jnp.max)), 0.0)
        l_i = _reduce_keys(e_i, jnp.add, jnp.sum)
        p_heads.append((e_i / jnp.where(l_i > 0.0, l_i, 1.0)).astype(BF16))
    p_ct = jnp.concatenate(p_heads, axis=1)
    imp4 = _nn(ovt_ref[...], p_ct)
    ocmp_t = _nn(vct_ref[...], p_ct)

    band = _band_bias_t(NSA_WIN_TILES * qc, qc, q0 - start, NSA_WINDOW - 1)
    owin_t, _ = _softmax_block_t(sw_t + _tile_lanes(band, NSA_G), _pv_tiles(vwt_ref, t0, NSA_WIN_TILES, qc))
    gate_t = jnp.transpose(jax.nn.sigmoid(gl_ref[...]))
    gate = lambda i, r: gate_t[3 * i + r:3 * i + r + 1, :]
    partial_out = [gate(i, 0) * lanes_of(ocmp_t, i) + gate(i, 2) * lanes_of(owin_t, i) for i in range(NSA_G)]

    imp = lanes_of(imp4, 0) + lanes_of(imp4, 1) + lanes_of(imp4, 2) + lanes_of(imp4, 3)
    blk = lax.broadcasted_iota(jnp.int32, imp.shape, 0)
    cur = (q0 + lax.broadcasted_iota(jnp.int32, imp.shape, 1)) >> 6
    valid = blk <= cur
    forced = valid & ((blk == 0) | (blk > cur - SLC_LOCAL))
    rank = _rank_rows(jnp.where(forced, FORCE, jnp.where(valid, imp, NEG)), N_SLC)
    bias_ref[...] = jnp.where(valid & (rank < SLC_TOPK), 0.0, NEG)

    key_row = lax.broadcasted_iota(jnp.int32, (NSA_KT, qc), 0)
    qpos = lax.broadcasted_iota(jnp.int32, (NSA_KT, qc), 1) + q0
    per_tile = NSA_KT // SLC_BLOCK

    def scores(t):
        ks0 = pl.multiple_of(t * NSA_KT, NSA_KT)
        blocks = [jnp.broadcast_to(bias_ref[pl.ds(t * per_tile + j, 1), :], (SLC_BLOCK, qc))
                  for j in range(per_tile)]
        bias = jnp.where(key_row + ks0 <= qpos, jnp.concatenate(blocks, axis=0), NEG)
        kt = ks_ref[0, pl.ds(ks0, NSA_KT), :]
        return [_nt(kt, qs[i * qc:(i + 1) * qc]) + bias for i in range(NSA_G)]

    slc = _flash_tiles(c // (NSA_KT // qc) + 1, SEQ // NSA_KT - 1, [_online_init_t(qc)] * NSA_G, scores,
                       lambda t: _pv_tiles(vst_ref, t, 1, NSA_KT))
    outs = [partial_out[i] + gate(i, 1) * _normalise(slc[i][1])[0] for i in range(NSA_G)]
    o_ref[0, :, 0:LANES] = jnp.transpose(_merge_pair_t(outs[0], outs[1])).astype(o_ref.dtype)
    o_ref[0, :, LANES:2 * LANES] = jnp.transpose(_merge_pair_t(outs[2], outs[3])).astype(o_ref.dtype)


def _nsa(z_rot, z_pl, k_cmp, v_cmp, gate_logits, ovt):
    qc = NSA_QC
    seq_spec = lambda base: pl.BlockSpec((1, SEQ, LANES), lambda b, j, c: (b, 0, base + j))
    cmp_spec = pl.BlockSpec((1, 1, N_CMP_PAD, LANES), lambda b, j, c: (b, j, 0, 0))
    return pl.pallas_call(
        _nsa_kernel,
        out_shape=jax.ShapeDtypeStruct((BATCH, SEQ, NSA_HEADS * HEAD_DIM), BF16),
        grid=(BATCH, NSA_KV_HEADS, SEQ // qc),
        in_specs=[
            pl.BlockSpec((1, qc, LANES), lambda b, j, c: (b, c, ROT_NQ + 2 * j)),
            pl.BlockSpec((1, qc, LANES), lambda b, j, c: (b, c, ROT_NQ + 2 * j + 1)),
            cmp_spec, cmp_spec,
            seq_spec(ROT_NKS), seq_spec(PL_NVS), seq_spec(ROT_NKW), seq_spec(PL_NVW),
            pl.BlockSpec((qc, LANES), lambda b, j, c: (b * (SEQ // qc) + c, j)),
            pl.BlockSpec(ovt.shape, lambda b, j, c: (0, 0)),
        ],
        out_specs=pl.BlockSpec((1, qc, 2 * LANES), lambda b, j, c: (b, c, j)),
        scratch_shapes=[pltpu.VMEM((LANES, N_CMP_PAD), BF16),
                        pltpu.VMEM((SEQ // NSA_KT, VT_ROWS, NSA_KT), BF16),
                        pltpu.VMEM((SEQ // qc, VT_ROWS, qc), BF16),
                        pltpu.VMEM((N_SLC, qc), F32)],
        compiler_params=_params("parallel", "parallel", "arbitrary"),
        name="nsa",
    )(z_rot, z_rot, k_cmp, v_cmp, z_rot, z_pl, z_rot, z_pl, gate_logits, ovt)


DIL_QC = 128
DIL_STEPS = SEQ // DIL_QC
DIL_UNROLL = 8


def _dil_group(q_ref, k_ref, v_ref, og_ref, lg_ref, gi):
    window, dil = DIL_CONFIGS[gi]
    qc = DIL_QC
    m = SEQ // dil
    n_back = window // dil
    nk = min(m, qc + -(-n_back // qc) * qc)
    chunks = m // qc

    def rows(first, n):
        return pl.ds(first, n) if dil == 1 else pl.ds(first, n, stride=dil)

    def place(idx):
        r = idx // chunks
        q0 = (idx % chunks) * qc
        start = jnp.maximum(q0 - (nk - qc), 0)
        return rows(r + dil * q0, qc), rows(r + dil * start, nk), q0 - start

    def body(i, _):
        at = [place(i * DIL_UNROLL + u) for u in range(DIL_UNROLL)]
        s = [_nt(k_ref[0, k_rows, :].astype(BF16), _stack_heads(q_ref[0, q_rows, :].astype(BF16)))
             for q_rows, k_rows, _ in at]
        m, p = [], []
        for u, (_, _, off) in enumerate(at):
            s_u = s[u] + _tile_lanes(_band_bias_t(nk, qc, off, n_back), 2)
            m.append(_reduce_keys(s_u, jnp.maximum, jnp.max))
            p.append(_probs(s_u, m[u]))
        acc = [_nn(_transpose_aug(v_ref[0, k_rows, :]), p[u]) for u, (_, k_rows, _) in enumerate(at)]
        for u, (q_rows, _, _) in enumerate(at):
            o_t, l = _normalise(acc[u])
            lse_b = jnp.broadcast_to(m[u] + jnp.log(l) * LOG2_E, (LANES, 2 * qc))
            og_ref[gi, q_rows, :] = jnp.transpose(_merge_pair_t(o_t[:, :qc], o_t[:, qc:]))
            lg_ref[gi, q_rows, :] = jnp.transpose(_merge_pair_t(lse_b[:, :qc], lse_b[:, qc:]))
        return 0

    lax.fori_loop(0, DIL_STEPS // DIL_UNROLL, body, 0)


def _dil_kernel(q_ref, k_ref, v_ref, o_ref, og_ref, lg_ref):
    g = pl.program_id(2)
    n_groups = len(DIL_CONFIGS)
    for gi in range(n_groups):
        pl.when(g == gi)(functools.partial(_dil_group, q_ref, k_ref, v_ref, og_ref, lg_ref, gi))

    @pl.when(g == n_groups - 1)
    def _():
        rows = 512

        def body(i, _):
            sl = pl.ds(pl.multiple_of(i * rows, rows), rows)
            l0, l1, l2 = lg_ref[0, sl, :], lg_ref[1, sl, :], lg_ref[2, sl, :]
            mx = jnp.maximum(jnp.maximum(l0, l1), l2)
            e0, e1, e2 = jnp.exp2(l0 - mx), jnp.exp2(l1 - mx), jnp.exp2(l2 - mx)
            den = e0 + e1 + e2
            out = (e0 / den) * og_ref[0, sl, :] + (e1 / den) * og_ref[1, sl, :] + (e2 / den) * og_ref[2, sl, :]
            o_ref[0, sl, :] = out.astype(o_ref.dtype)
            return 0

        lax.fori_loop(0, SEQ // rows, body, 0)


def _dilated(zd_rot, zd_pl):
    n_groups = len(DIL_CONFIGS)
    width = DIL_HEADS_PER_GROUP * HEAD_DIM
    col = lambda base: (lambda b, p, g: (b, 0, base + 2 * g + p))
    blk = lambda base: pl.BlockSpec((1, SEQ, LANES), col(base))
    return pl.pallas_call(
        _dil_kernel,
        out_shape=jax.ShapeDtypeStruct((BATCH, SEQ, width), BF16),
        grid=(BATCH, 2, n_groups),
        in_specs=[blk(0), blk(DIL_BLOCKS), blk(0)],
        out_specs=pl.BlockSpec((1, SEQ, LANES), lambda b, p, g: (b, 0, p)),
        scratch_shapes=[pltpu.VMEM((n_groups, SEQ, LANES), F32), pltpu.VMEM((n_groups, SEQ, LANES), F32)],
        compiler_params=_params("parallel", "parallel", "arbitrary"),
        name="dilated",
    )(zd_rot, zd_rot, zd_pl)


def _out_proj_kernel(oa_ref, ob_ref, oc_ref, h_ref, wa_ref, wb_ref, wc_ref, g_ref, b_ref, h1_ref, h1b_ref):
    y = _nn(oa_ref[...], wa_ref[...]) + _nn(ob_ref[...], wb_ref[...]) + _nn(oc_ref[...], wc_ref[...])
    h1 = _layer_norm(DEEPNORM_ALPHA * h_ref[...] + y, g_ref[...], b_ref[...])
    h1_ref[...] = h1
    h1b_ref[...] = h1.astype(BF16)


def _out_proj(oa, ob, oc, h, wa, wb, wc, g, b):
    tm = 512
    rows = lambda w: pl.BlockSpec((tm, w), lambda i: (i, 0))
    full = lambda a: pl.BlockSpec(a.shape, lambda i: (0, 0))
    return pl.pallas_call(
        _out_proj_kernel,
        out_shape=(jax.ShapeDtypeStruct((TOKENS, D_MODEL), F32),
                   jax.ShapeDtypeStruct((TOKENS, D_MODEL), BF16)),
        grid=(TOKENS // tm,),
        in_specs=[rows(oa.shape[1]), rows(ob.shape[1]), rows(oc.shape[1]), rows(D_MODEL),
                  full(wa), full(wb), full(wc), full(g), full(b)],
        out_specs=(rows(D_MODEL), rows(D_MODEL)),
        compiler_params=_params("parallel"),
        name="out_proj_ln",
    )(oa, ob, oc, h, wa, wb, wc, g, b)


def _router_kernel(hb_ref, rw_ref, rb_ref, comb_ref, sel_ref):
    logits = _nt(rw_ref[...], hb_ref[...]) + rb_ref[...]
    mx = jnp.max(logits, axis=0, keepdims=True)
    ex = jnp.exp(logits - mx)
    probs = ex / jnp.sum(ex, axis=0, keepdims=True)
    p = [probs[e:e + 1, :] for e in range(N_EXPERTS)]
    best, g_sel = None, None
    for g in range(N_GROUPS):
        a, b, c, d = p[4 * g:4 * g + 4]
        hi1, lo1, hi2, lo2 = jnp.maximum(a, b), jnp.minimum(a, b), jnp.maximum(c, d), jnp.minimum(c, d)
        top2 = jnp.maximum(hi1, hi2) + jnp.maximum(jnp.minimum(hi1, hi2), jnp.maximum(lo1, lo2))
        if g == 0:
            best, g_sel = top2, jnp.zeros_like(top2)
        else:
            better = top2 > best
            best = jnp.where(better, top2, best)
            g_sel = jnp.where(better, float(g), g_sel)
    chosen, picked = [], []
    for e in range(N_EXPERTS):
        g = e // EXPERTS_PER_GROUP
        rank = jnp.zeros_like(best)
        for o in range(4 * g, 4 * g + 4):
            if o < e:
                rank = rank + jnp.where(p[o] >= p[e], 1.0, 0.0)
            elif o > e:
                rank = rank + jnp.where(p[o] > p[e], 1.0, 0.0)
        chosen.append(jnp.where((g_sel == float(g)) & (rank < 2.0), 1.0, 0.0))
        picked.append(chosen[e] * p[e])
    total = picked[0]
    for e in range(1, N_EXPERTS):
        total = total + picked[e]
    comb_ref[...] = jnp.concatenate(picked, axis=0) / total
    sel_ref[...] = jnp.concatenate(chosen, axis=0)


def _router(hb, rw_t, rb):
    tm = 1024
    out = jax.ShapeDtypeStruct((N_EXPERTS, TOKENS), F32)
    o_spec = pl.BlockSpec((N_EXPERTS, tm), lambda i: (0, i))
    return pl.pallas_call(
        _router_kernel,
        out_shape=(out, out),
        grid=(TOKENS // tm,),
        in_specs=[pl.BlockSpec((tm, D_MODEL), lambda i: (i, 0)),
                  pl.BlockSpec((N_EXPERTS, D_MODEL), lambda i: (0, 0)),
                  pl.BlockSpec((N_EXPERTS, 1), lambda i: (0, 0))],
        out_specs=(o_spec, o_spec),
        compiler_params=_params("parallel"),
        name="router",
    )(hb, rw_t, rb)


def _routing_tables(comb_t, sel_t):
    sel = sel_t > 0.5
    cnt = jnp.sum(sel, axis=1, dtype=jnp.int32)
    cnt_pad = (cnt + (MOE_TILE - 1)) // MOE_TILE * MOE_TILE
    ends = jnp.cumsum(cnt_pad)
    rank = jnp.cumsum(sel.astype(jnp.int32), axis=1) - 1
    pos = (ends - cnt_pad)[:, None] + rank
    pos_lo = jnp.min(jnp.where(sel, pos, MOE_ROWS), axis=0)
    pos_hi = jnp.max(jnp.where(sel, pos, -1), axis=0)
    w_lo = jnp.sum(jnp.where(sel & (pos == pos_lo), comb_t, 0.0), axis=0)
    w_hi = jnp.sum(jnp.where(sel & (pos == pos_hi), comb_t, 0.0), axis=0)
    w = jnp.zeros((TOKENS, LANES), F32).at[:, 0].set(w_lo).at[:, 1].set(w_hi)
    n_tiles = ends[-1] // MOE_TILE
    tile_start = jnp.arange(MOE_TILES, dtype=jnp.int32) * MOE_TILE
    tile_start = jnp.minimum(tile_start, ends[-1] - MOE_TILE)
    tile_expert = jnp.sum((ends[None, :] <= tile_start[:, None]).astype(jnp.int32), axis=1)
    return jnp.stack([pos_lo, pos_hi]).astype(jnp.int32), w, tile_expert, n_tiles.reshape(1).astype(jnp.int32)


MOE_TILE = 256
MOE_TILES = 2 * TOKENS // MOE_TILE + N_EXPERTS
MOE_ROWS = MOE_TILES * MOE_TILE
SLAB = D_MODEL // LANES


def _to_slabs(ref, x, rows):
    for j in range(SLAB):
        ref[pl.ds(j, rows, stride=SLAB), :] = x[:, j * LANES:(j + 1) * LANES]


def _from_slabs(ref, rows):
    return jnp.concatenate([ref[pl.ds(j, rows, stride=SLAB), :] for j in range(SLAB)], axis=1)


def _slab_rows(row, n=SLAB):
    return pl.ds(pl.multiple_of(row * n, n), n)


XSLAB = SLAB // 2
U32 = jnp.uint32


def _to_packed_slabs(ref, x, rows):
    bits = lambda t: lax.bitcast_convert_type(t.astype(BF16).astype(F32), U32)
    for j in range(XSLAB):
        hi = bits(x[:, 2 * j * LANES:(2 * j + 1) * LANES])
        lo = bits(x[:, (2 * j + 1) * LANES:(2 * j + 2) * LANES])
        ref[pl.ds(j, rows, stride=XSLAB), :] = hi | (lo >> 16)


def _from_packed_slabs(ref, rows):
    parts = []
    for j in range(XSLAB):
        u = ref[pl.ds(j, rows, stride=XSLAB), :]
        parts.append(lax.bitcast_convert_type(u & jnp.uint32(0xFFFF0000), F32).astype(BF16))
        parts.append(lax.bitcast_convert_type(u << 16, F32).astype(BF16))
    return jnp.concatenate(parts, axis=1)


def _dispatch_kernel(pos_ref, h_ref, init_ref, xs_ref, slab_ref, sem):
    del init_ref
    tm = h_ref.shape[0]
    base = pl.program_id(0) * tm
    _to_packed_slabs(slab_ref, h_ref[...], tm)

    def copy(t, which):
        return pltpu.make_async_copy(slab_ref.at[_slab_rows(t, XSLAB), :],
                                     xs_ref.at[_slab_rows(pos_ref[which, base + t], XSLAB), :], sem)

    def start(t, _):
        copy(t, 0).start()
        copy(t, 1).start()
        return 0

    lax.fori_loop(0, tm, start, 0, unroll=8)
    whole = pltpu.make_async_copy(slab_ref, xs_ref.at[pl.ds(0, tm * XSLAB), :], sem)
    whole.wait()
    whole.wait()


def _dispatch(pos, h):
    tm = 256
    grid_spec = pltpu.PrefetchScalarGridSpec(
        num_scalar_prefetch=1,
        grid=(TOKENS // tm,),
        in_specs=[pl.BlockSpec((tm, D_MODEL), lambda i, pos: (i, 0)),
                  pl.BlockSpec(memory_space=pl.ANY)],
        out_specs=pl.BlockSpec(memory_space=pl.ANY),
        scratch_shapes=[pltpu.VMEM((tm * XSLAB, LANES), U32), pltpu.SemaphoreType.DMA],
    )
    return pl.pallas_call(
        _dispatch_kernel,
        out_shape=jax.ShapeDtypeStruct((MOE_ROWS * XSLAB, LANES), U32),
        grid_spec=grid_spec,
        input_output_aliases={2: 0},
        compiler_params=_params("arbitrary"),
        name="moe_dispatch",
    )(pos, h, jnp.zeros((MOE_ROWS * XSLAB, LANES), U32))


def _experts_kernel(te_ref, nt_ref, xs_ref, wg_ref, wu_ref, wd_ref, ys_ref, wgb_ref, wub_ref, wdb_ref):
    k = pl.program_id(0)
    e = te_ref[k]
    e_prev = te_ref[jnp.maximum(k - 1, 0)]

    @pl.when((k == 0) | (e != e_prev))
    def _():
        wgb_ref[...] = wg_ref[0, 0].astype(BF16)
        wub_ref[...] = wu_ref[0, 0].astype(BF16)
        wdb_ref[...] = wd_ref[0, 0].astype(BF16)

    @pl.when(k < nt_ref[0])
    def _():
        x = _from_packed_slabs(xs_ref, MOE_TILE)
        hid = jax.nn.silu(_nn(x, wgb_ref[...])) * _nn(x, wub_ref[...])
        _to_slabs(ys_ref, _nn(hid.astype(BF16), wdb_ref[...]), MOE_TILE)

    @pl.when(k >= nt_ref[0])
    def _():
        ys_ref[...] = jnp.zeros(ys_ref.shape, F32)


def _experts(tile_expert, n_tiles, xs, wg, wu, wd, layer):
    w_in_spec = pl.BlockSpec((1, 1, D_MODEL, EXPERT_HIDDEN), lambda k, te, nt: (layer, te[k], 0, 0))
    grid_spec = pltpu.PrefetchScalarGridSpec(
        num_scalar_prefetch=2,
        grid=(MOE_TILES,),
        in_specs=[pl.BlockSpec((MOE_TILE * XSLAB, LANES), lambda k, te, nt: (jnp.minimum(k, nt[0] - 1), 0)),
                  w_in_spec, w_in_spec,
                  pl.BlockSpec((1, 1, EXPERT_HIDDEN, D_MODEL), lambda k, te, nt: (layer, te[k], 0, 0))],
        out_specs=pl.BlockSpec((MOE_TILE * SLAB, LANES), lambda k, te, nt: (k, 0)),
        scratch_shapes=[pltpu.VMEM((D_MODEL, EXPERT_HIDDEN), BF16), pltpu.VMEM((D_MODEL, EXPERT_HIDDEN), BF16),
                        pltpu.VMEM((EXPERT_HIDDEN, D_MODEL), BF16)],
    )
    return pl.pallas_call(
        _experts_kernel,
        out_shape=jax.ShapeDtypeStruct((MOE_ROWS * SLAB, LANES), F32),
        grid_spec=grid_spec,
        compiler_params=_params("arbitrary"),
        name="moe_experts",
    )(tile_expert, n_tiles, xs, wg, wu, wd)


def _ple_ln_kernel(pos_ref, hb_ref, h_ref, ys_ref, w_ref, p_ref, gw_ref, gb_ref, pw_ref, g_ref, b_ref,
                   h2_ref, h2b_ref, lo_ref, hi_ref, sem):
    tm = h_ref.shape[0]
    i = pl.program_id(0)
    slot = i & 1
    bufs = (lo_ref, hi_ref)

    def fetch(tile, into):
        def start(t, _):
            for which in range(2):
                pltpu.make_async_copy(ys_ref.at[_slab_rows(pos_ref[which, tile * tm + t]), :],
                                      bufs[which].at[into, _slab_rows(t), :], sem.at[into]).start()
            return 0
        lax.fori_loop(0, tm, start, 0, unroll=8)

    @pl.when(i == 0)
    def _():
        fetch(0, 0)

    @pl.when(i + 1 < pl.num_programs(0))
    def _():
        fetch(i + 1, 1 - slot)

    gate = jax.nn.sigmoid(_nn(hb_ref[...], gw_ref[...]) + gb_ref[...])
    ple = gate * _nn(p_ref[...].astype(BF16), pw_ref[...])
    for which in range(2):
        pltpu.make_async_copy(ys_ref.at[pl.ds(0, tm * SLAB), :], bufs[which].at[slot], sem.at[slot]).wait()
    w = w_ref[...]
    ffn = w[:, 0:1] * _from_slabs(lo_ref.at[slot], tm) + w[:, 1:2] * _from_slabs(hi_ref.at[slot], tm)
    h2 = _layer_norm(DEEPNORM_ALPHA * h_ref[...] + ffn + ple, g_ref[...], b_ref[...])
    h2_ref[...] = h2
    h2b_ref[...] = h2.astype(BF16)


def _ple_ln(pos, hb, h, ys, w, p, layer, gw, gb, pw, g, b):
    tm = 256
    p_spec = pl.BlockSpec((tm, PLE_DIM), lambda i, pos: (layer * (TOKENS // tm) + i, 0))
    rows = lambda width: pl.BlockSpec((tm, width), lambda i, pos: (i, 0))
    full = lambda a: pl.BlockSpec(a.shape, lambda i, pos: (0, 0))
    grid_spec = pltpu.PrefetchScalarGridSpec(
        num_scalar_prefetch=1,
        grid=(TOKENS // tm,),
        in_specs=[rows(D_MODEL), rows(D_MODEL), pl.BlockSpec(memory_space=pl.ANY), rows(LANES), p_spec,
                  full(gw), full(gb), full(pw), full(g), full(b)],
        out_specs=(rows(D_MODEL), rows(D_MODEL)),
        scratch_shapes=[pltpu.VMEM((2, tm * SLAB, LANES), F32), pltpu.VMEM((2, tm * SLAB, LANES), F32),
                        pltpu.SemaphoreType.DMA((2,))],
    )
    return pl.pallas_call(
        _ple_ln_kernel,
        out_shape=(jax.ShapeDtypeStruct((TOKENS, D_MODEL), F32),
                   jax.ShapeDtypeStruct((TOKENS, D_MODEL), BF16)),
        grid_spec=grid_spec,
        compiler_params=_params("arbitrary"),
        name="ple_ln",
    )(pos, hb, h, ys, w, p, gw, gb, pw, g, b)


def _rope_tables(positions):
    half = ROT_DIM // 2
    inv_freq = jnp.exp(jnp.arange(half, dtype=F32) * (-2.0 * math.log(ROPE_THETA) / ROT_DIM))
    ang = positions.astype(F32)[:, :, None] * inv_freq
    cos, sin = jnp.cos(ang), jnp.sin(ang)
    zeros = jnp.zeros_like(cos)
    rest = HEAD_DIM - ROT_DIM
    pad = lambda v: jnp.broadcast_to(jnp.asarray(v, F32), cos.shape[:2] + (rest,))
    c = jnp.concatenate([cos, cos, pad(1.0)], axis=-1)
    s1 = jnp.concatenate([-sin, zeros, pad(0.0)], axis=-1)
    s2 = jnp.concatenate([zeros, sin, pad(0.0)], axis=-1)
    tile = lambda t: jnp.concatenate([t, t], axis=-1).reshape(TOKENS, LANES)
    return tile(c), tile(s1), tile(s2)


def _split_w_in(w):
    mw, nq, nkv, dw = MOBA_HEADS * HEAD_DIM, NSA_HEADS * HEAD_DIM, NSA_KV_HEADS * HEAD_DIM, DIL_HEADS * HEAD_DIM
    widths = (mw, mw, mw, nq) + (nkv,) * 6 + (NSA_HEADS * 3, dw, dw, dw)
    offs = np.concatenate([[0], np.cumsum(widths)])
    qa, ka, va, qb, kbc, vbc, kbs, vbs, kbw, vbw, gb, qc, kc, vc = (
        w[:, int(offs[i]):int(offs[i + 1])] for i in range(len(widths)))

    def dup(t):
        t = t.reshape(D_MODEL, NSA_KV_HEADS, 1, HEAD_DIM)
        return jnp.broadcast_to(t, (D_MODEL, NSA_KV_HEADS, 2, HEAD_DIM)).reshape(D_MODEL, NSA_KV_HEADS * LANES)

    zpad = lambda n: jnp.zeros((D_MODEL, n * LANES), w.dtype)
    w_rot = jnp.concatenate([qa * Q_SCALE, ka, qb * Q_SCALE, dup(kbc), dup(kbs), dup(kbw), zpad(1)], axis=1)
    w_pl = jnp.concatenate([va, dup(vbc), dup(vbs), dup(vbw), zpad(3)], axis=1)
    gpad = jnp.zeros((D_MODEL, NSA_KV_HEADS, LANES - 12), w.dtype)
    w_gl = jnp.concatenate([gb.reshape(D_MODEL, NSA_KV_HEADS, 12), gpad], axis=-1).reshape(D_MODEL, -1)
    w_dil_rot = jnp.concatenate([qc * Q_SCALE, kc], axis=1)
    return tuple(t.astype(BF16) for t in (w_rot, w_pl, w_gl, w_dil_rot, vc))


def _overlap_table():
    starts = np.arange(N_CMP) * CMP_STRIDE
    slc = np.arange(N_SLC) * SLC_BLOCK
    ov = ((starts[:, None] < slc[None, :] + SLC_BLOCK) & (starts[:, None] + CMP_LEN > slc[None, :]))
    ovt = np.zeros((N_SLC, N_CMP_PAD), np.float32)
    ovt[:, :N_CMP] = ov.T
    return jnp.asarray(ovt, BF16)


def _cmp_chunks(z, base):
    nblk = z.shape[-1] // LANES
    t = z.reshape(BATCH, SEQ // CMP_STRIDE, CMP_STRIDE, nblk, LANES)[:, :, :, base:base + NSA_KV_HEADS, :HEAD_DIM]
    return t.transpose(0, 3, 1, 2, 4).reshape(BATCH, NSA_KV_HEADS, SEQ // CMP_STRIDE, CMP_STRIDE * HEAD_DIM)


def kernel(x, p, positions, ln_in_g, ln_in_b, w_in, w_out, nsa_ck1, nsa_ck2, nsa_pe_k, nsa_cv1, nsa_cv2, nsa_pe_v, ln1_g, ln1_b, router_w, router_b, w_gate, w_up, w_down, ple_proj, ple_gate_w, ple_gate_b, ln2_g, ln2_b):
    rope = _rope_tables(positions)
    ovt = _overlap_table()
    rw_t = router_w.T.astype(BF16)
    rb = router_b.reshape(N_EXPERTS, 1).astype(F32)
    chunk_w = CMP_STRIDE * HEAD_DIM
    vec = lambda v: v.reshape(1, -1)
    seq3 = lambda t: t.reshape(BATCH, SEQ, t.shape[-1])
    flat = lambda t: t.reshape(TOKENS, t.shape[-1])

    h, hb = _ln_in(x.reshape(TOKENS, D_MODEL), ln_in_g, ln_in_b)
    for i in range(DEPTH):
        w_rot, w_pl, w_gl, w_dil_rot, w_dil_pl = _split_w_in(w_in[i])
        z_rot = seq3(_project(hb, w_rot, BF16, 768, rope=rope))
        z_pl = seq3(_project(hb, w_pl, BF16, 1024))
        gate_logits = _project(hb, w_gl, F32, NSA_KV_HEADS * LANES)
        zd_rot = seq3(_project(hb, w_dil_rot, F32, 768, rope=rope))
        zd_pl = seq3(_project(hb, w_dil_pl, F32, 768))

        o_a = _moba(z_rot, z_pl)

        dup2 = lambda w2: jnp.concatenate([w2, w2], axis=1).astype(BF16)
        k_cmp, v_cmp = _compress(
            _cmp_chunks(z_rot, ROT_NKC), _cmp_chunks(z_pl, PL_NVC),
            nsa_pe_k[i].reshape(2, chunk_w), nsa_pe_v[i].reshape(2, chunk_w),
            nsa_ck1[i].reshape(2, chunk_w, CMP_HIDDEN).astype(BF16), dup2(nsa_ck2[i]),
            nsa_cv1[i].reshape(2, chunk_w, CMP_HIDDEN).astype(BF16), dup2(nsa_cv2[i]))
        o_b = _nsa(z_rot, z_pl, k_cmp, v_cmp, gate_logits, ovt)

        o_c = _dilated(zd_rot, zd_pl)

        wo = w_out[i].astype(BF16)
        a_w, b_w = MOBA_HEADS * HEAD_DIM, NSA_HEADS * HEAD_DIM
        h, hb = _out_proj(flat(o_a), flat(o_b), flat(o_c), h,
                          wo[:a_w], wo[a_w:a_w + b_w], wo[a_w + b_w:], vec(ln1_g[i]), vec(ln1_b[i]))

        pos, w_tok, tile_expert, n_tiles = _routing_tables(*_router(hb, rw_t, rb))
        xs = _dispatch(pos, h)
        ys = _experts(tile_expert, n_tiles, xs, w_gate, w_up, w_down, i)
        h, hb = _ple_ln(pos, hb, h, ys, w_tok, p.reshape(DEPTH * TOKENS, PLE_DIM), i, ple_gate_w[i].astype(BF16),
                        vec(ple_gate_b[i]), ple_proj[i].astype(BF16), vec(ln2_g[i]), vec(ln2_b[i]))
    return h.reshape(BATCH, SEQ, D_MODEL)
```

```python
import functools
import math

import numpy as np
import jax
import jax.numpy as jnp
from jax import lax
from jax.experimental import pallas as pl
from jax.experimental.pallas import tpu as pltpu

F32 = jnp.float32
BF16 = jnp.bfloat16

D_MODEL = 2048
BATCH = 2
SEQ = 4096
DEPTH = 4
TOKENS = BATCH * SEQ
HEAD_DIM = 64
ROT_DIM = HEAD_DIM // 4
ROPE_THETA = 500000.0
NEG = -1e30
FORCE = 1e30
LN_EPS = 1e-5
SCALE = HEAD_DIM ** -0.5
LOG2_E = math.log2(math.e)
Q_SCALE = SCALE * LOG2_E

MOBA_HEADS = 8
MOBA_BLOCK = 256
MOBA_TOPK = 3
MOBA_NB = SEQ // MOBA_BLOCK

NSA_HEADS = 12
NSA_KV_HEADS = 3
CMP_LEN = 32
CMP_STRIDE = 16
CMP_HIDDEN = 128
N_CMP = (SEQ - CMP_LEN) // CMP_STRIDE + 1
N_CMP_PAD = 256
SLC_BLOCK = 64
SLC_TOPK = 16
SLC_LOCAL = 2
N_SLC = SEQ // SLC_BLOCK
NSA_WINDOW = 512

DIL_CONFIGS = ((128, 1), (512, 4), (2048, 16))
DIL_HEADS_PER_GROUP = 4
DIL_HEADS = DIL_HEADS_PER_GROUP * len(DIL_CONFIGS)

N_EXPERTS = 16
N_GROUPS = 4
EXPERTS_PER_GROUP = 4
EXPERT_HIDDEN = D_MODEL // 4
PLE_DIM = 256

DEEPNORM_ALPHA = (2 * DEPTH) ** 0.25

LANES = 128
VMEM_LIMIT = 56 * 1024 * 1024

ROT_MQ, ROT_MK, ROT_NQ, ROT_NKC, ROT_NKS, ROT_NKW = 0, 4, 8, 14, 17, 20
ROT_BLOCKS = 24
PL_MV, PL_NVC, PL_NVS, PL_NVW = 0, 4, 7, 10
PL_BLOCKS = 13
DIL_BLOCKS = DIL_HEADS // 2

NT_DIMS = (((1,), (1,)), ((), ()))


def _nt(a, b):
    return lax.dot_general(a, b, NT_DIMS, preferred_element_type=F32)


def _nn(a, b):
    return jnp.dot(a, b, preferred_element_type=F32)


def _params(*sem):
    return pltpu.CompilerParams(dimension_semantics=sem, vmem_limit_bytes=VMEM_LIMIT)


def _layer_norm(y, g, b):
    mu = jnp.mean(y, axis=-1, keepdims=True)
    yc = y - mu
    var = jnp.mean(yc * yc, axis=-1, keepdims=True)
    return yc * lax.rsqrt(var + LN_EPS) * g + b


def _ln_kernel(x_ref, g_ref, b_ref, h_ref, hb_ref):
    h = _layer_norm(x_ref[...], g_ref[...], b_ref[...])
    h_ref[...] = h
    hb_ref[...] = h.astype(BF16)


def _ln_in(x, g, b):
    tm = 512
    row = pl.BlockSpec((tm, D_MODEL), lambda i: (i, 0))
    vec = pl.BlockSpec((1, D_MODEL), lambda i: (0, 0))
    return pl.pallas_call(
        _ln_kernel,
        out_shape=(jax.ShapeDtypeStruct((TOKENS, D_MODEL), F32),
                   jax.ShapeDtypeStruct((TOKENS, D_MODEL), BF16)),
        grid=(TOKENS // tm,),
        in_specs=[row, vec, vec],
        out_specs=(row, row),
        compiler_params=_params("parallel"),
        name="ln_in",
    )(x, g.reshape(1, -1), b.reshape(1, -1))


def _proj_kernel(x_ref, w_ref, o_ref):
    o_ref[...] = _nn(x_ref[...], w_ref[...]).astype(o_ref.dtype)


def _proj_rot_kernel(x_ref, w_ref, c_ref, s1_ref, s2_ref, o_ref):
    x = x_ref[...]
    c, s1, s2 = c_ref[...], s1_ref[...], s2_ref[...]
    half = ROT_DIM // 2
    for j0 in range(0, o_ref.shape[1], 2 * LANES):
        z = _nn(x, w_ref[:, j0:j0 + 2 * LANES])
        for j in range(j0, j0 + 2 * LANES, LANES):
            zc = z[:, j - j0:j - j0 + LANES]
            r = zc * c + pltpu.roll(zc, LANES - half, 1) * s1 + pltpu.roll(zc, half, 1) * s2
            o_ref[:, j:j + LANES] = r.astype(o_ref.dtype)


def _project(hb, w, out_dtype, tn, rope=None):
    tm = 1024
    n = w.shape[1]
    x_spec = pl.BlockSpec((tm, D_MODEL), lambda i, j: (i, 0))
    w_spec = pl.BlockSpec((D_MODEL, tn), lambda i, j: (0, j))
    o_spec = pl.BlockSpec((tm, tn), lambda i, j: (i, j))
    if rope is None:
        kern, extra, extra_specs = _proj_kernel, (), []
    else:
        t_spec = pl.BlockSpec((tm, LANES), lambda i, j: (i, 0))
        kern, extra, extra_specs = _proj_rot_kernel, rope, [t_spec] * 3
    return pl.pallas_call(
        kern,
        out_shape=jax.ShapeDtypeStruct((TOKENS, n), out_dtype),
        grid=(TOKENS // tm, n // tn),
        in_specs=[x_spec, w_spec] + extra_specs,
        out_specs=o_spec,
        compiler_params=_params("parallel", "arbitrary"),
        name="in_proj_rot" if rope is not None else "in_proj",
    )(hb, w, *extra)


def _stack_heads(*q_blocks):
    parts = []
    for q in q_blocks:
        lane = lax.broadcasted_iota(jnp.int32, q.shape, 1)
        zero = jnp.zeros_like(q)
        parts += [jnp.where(lane < HEAD_DIM, q, zero), jnp.where(lane >= HEAD_DIM, q, zero)]
    return jnp.concatenate(parts, axis=0)


def _merge_pair_t(lo, hi):
    sub = lax.broadcasted_iota(jnp.int32, lo.shape, 0)
    return jnp.where(sub < HEAD_DIM, lo, hi)


def _band_bias_t(nk, qc, offset, n_back):
    key = lax.broadcasted_iota(jnp.int32, (nk, qc), 0)
    qry = lax.broadcasted_iota(jnp.int32, (nk, qc), 1)
    diff = offset + qry - key
    return jnp.where((diff >= 0) & (diff <= n_back), 0.0, NEG)


def _tile_lanes(x, n):
    return jnp.concatenate([x] * n, axis=1)


def _transpose_bf16(v):
    return jnp.transpose(v.astype(F32)).astype(BF16)


def _tree(x, op):
    n = x.shape[0]
    if n == 8:
        return x
    if n % 16 == 0:
        return op(_tree(x[:n // 2], op), _tree(x[n // 2:], op))
    acc = x[:8]
    for i in range(1, n // 8):
        acc = op(acc, x[8 * i:8 * i + 8])
    return acc


def _reduce_keys(x, op, final):
    return final(_tree(x, op), axis=0, keepdims=True)


VT_ROWS = LANES + 16


def _transpose_aug(v):
    vt = jnp.transpose(v.astype(F32))
    sub = lax.broadcasted_iota(jnp.int32, (VT_ROWS - LANES, v.shape[0]), 0)
    return jnp.concatenate([vt, jnp.where(sub == 0, 1.0, 0.0)], axis=0).astype(BF16)


def _probs(s_t, m):
    return jnp.exp2((s_t - m).astype(BF16))


def _normalise(acc):
    l = acc[LANES:LANES + 1]
    return acc[:LANES] / l, l


def _softmax_block_t(s_t, pv):
    m = _reduce_keys(s_t, jnp.maximum, jnp.max)
    out, l = _normalise(pv(_probs(s_t, m)))
    return out, m + jnp.log(l) * LOG2_E


def _online_step_t(carry, s_t, m_t, pv):
    m, acc = carry
    m_new = jnp.maximum(m, m_t)
    acc = jnp.exp2(m - m_new) * acc + pv(_probs(s_t, m_new))
    return m_new, acc


def _flash_tiles(n_tiles, last_tile, init, scores, pv_of):
    strips = range(len(init))

    def produce(t):
        s = tuple(scores(t))
        return s, tuple(_reduce_keys(s_i, jnp.maximum, jnp.max) for s_i in s)

    def body(t, carry):
        state, s_t, m_t = carry
        s_next, m_next = produce(jnp.minimum(t + 1, last_tile))
        pv = pv_of(t)
        return tuple(_online_step_t(state[i], s_t[i], m_t[i], pv) for i in strips), s_next, m_next

    state, _, _ = lax.fori_loop(0, n_tiles, body, (tuple(init),) + produce(0))
    return state


def _pv_tiles(vt_ref, first, n, rows):
    def pv(p):
        acc = _nn(vt_ref[first], p[:rows])
        for j in range(1, n):
            acc = acc + _nn(vt_ref[first + j], p[j * rows:(j + 1) * rows])
        return acc
    return pv


def _online_init_t(r):
    return (jnp.full((1, r), NEG, F32), jnp.zeros((VT_ROWS, r), F32))


def _rank_rows(g, n_rows):
    sub = lax.broadcasted_iota(jnp.int32, (8, g.shape[1]), 0)
    rank = jnp.zeros(g.shape, F32)
    for m in range(n_rows):
        gm = g[m:m + 1, :]
        b = m // 8 * 8
        mid = g[b:b + 8]
        parts = [jnp.where(gm > mid, 1.0, jnp.where((gm == mid) & (sub > m - b), 1.0, 0.0))]
        if b > 0:
            parts.insert(0, jnp.where(gm > g[:b], 1.0, 0.0))
        if b + 8 < n_rows:
            parts.append(jnp.where(gm >= g[b + 8:], 1.0, 0.0))
        rank = rank + jnp.concatenate(parts, axis=0)
    return rank


MOBA_QC = 256
MOBA_KT = 2 * MOBA_BLOCK


def _moba_kernel(q_ref, k_ref, v_ref, o_ref, kmean_ref, vt_ref, bias_ref):
    c = pl.program_id(2)
    qc = MOBA_QC
    r = 2 * qc

    @pl.when(c == 0)
    def _():
        row = lax.broadcasted_iota(jnp.int32, (MOBA_NB, SEQ), 0)
        col = lax.broadcasted_iota(jnp.int32, (MOBA_NB, SEQ), 1)
        avg = jnp.where((col >> 8) == row, 1.0 / MOBA_BLOCK, 0.0).astype(BF16)
        kmean_ref[...] = _nn(avg, k_ref[0])
        for t in range(MOBA_NB):
            vt_ref[t] = _transpose_aug(v_ref[0, t * MOBA_BLOCK:(t + 1) * MOBA_BLOCK, :])

    qs = _stack_heads(q_ref[0])

    ks = pl.multiple_of(c * MOBA_BLOCK, MOBA_BLOCK)
    heads = [qs[:qc], qs[qc:]]
    causal = _band_bias_t(MOBA_BLOCK, qc, 0, MOBA_BLOCK)
    s_own = [_nt(k_ref[0, pl.ds(ks, MOBA_BLOCK), :], q_h) + causal for q_h in heads]
    state = [_online_step_t(_online_init_t(qc), s_h, _reduce_keys(s_h, jnp.maximum, jnp.max),
                            _pv_tiles(vt_ref, c, 1, MOBA_BLOCK)) for s_h in s_own]

    gate = _nt(kmean_ref[...].astype(BF16), qs)
    blk = lax.broadcasted_iota(jnp.int32, gate.shape, 0)
    past = blk < c
    rank = _rank_rows(jnp.where(past, gate, NEG), MOBA_NB)
    bias_ref[...] = jnp.where(past & (rank < MOBA_TOPK), 0.0, NEG)

    per_tile = MOBA_KT // MOBA_BLOCK

    def scores(t):
        ks = pl.multiple_of(t * MOBA_KT, MOBA_KT)
        kt = k_ref[0, pl.ds(ks, MOBA_KT), :]
        rows = [bias_ref[pl.ds(t * per_tile + j, 1), :] for j in range(per_tile)]
        out = []
        for h in range(2):
            blocks = [jnp.broadcast_to(row[:, h * qc:(h + 1) * qc], (MOBA_BLOCK, qc)) for row in rows]
            out.append(_nt(kt, heads[h]) + jnp.concatenate(blocks, axis=0))
        return out

    state = _flash_tiles((c + per_tile - 1) // per_tile, SEQ // MOBA_KT - 1, state, scores,
                         lambda t: _pv_tiles(vt_ref, t * per_tile, per_tile, MOBA_BLOCK))
    o_lo, o_hi = (_normalise(acc)[0] for _, acc in state)
    o_ref[0] = jnp.transpose(_merge_pair_t(o_lo, o_hi)).astype(o_ref.dtype)


def _moba(z_rot, z_pl):
    qc = MOBA_QC
    grid = (BATCH, MOBA_HEADS // 2, SEQ // qc)
    return pl.pallas_call(
        _moba_kernel,
        out_shape=jax.ShapeDtypeStruct((BATCH, SEQ, MOBA_HEADS * HEAD_DIM), BF16),
        grid=grid,
        in_specs=[
            pl.BlockSpec((1, qc, LANES), lambda b, p, c: (b, c, ROT_MQ + p)),
            pl.BlockSpec((1, SEQ, LANES), lambda b, p, c: (b, 0, ROT_MK + p)),
            pl.BlockSpec((1, SEQ, LANES), lambda b, p, c: (b, 0, PL_MV + p)),
        ],
        out_specs=pl.BlockSpec((1, qc, LANES), lambda b, p, c: (b, c, p)),
        scratch_shapes=[pltpu.VMEM((MOBA_NB, LANES), F32),
                        pltpu.VMEM((MOBA_NB, VT_ROWS, MOBA_BLOCK), BF16),
                        pltpu.VMEM((MOBA_NB, 2 * qc), F32)],
        compiler_params=_params("parallel", "parallel", "arbitrary"),
        name="moba",
    )(z_rot, z_rot, z_pl)


def _compress_one(x_ref, pe_ref, w1_ref, w2_ref, o_ref):
    x = x_ref[0, 0].astype(F32)
    top = (x + pe_ref[0:1, :]).astype(BF16)
    bot = (x + pe_ref[1:2, :]).astype(BF16)
    a = _nn(top, w1_ref[0])
    bm = _nn(bot, w1_ref[1])
    pre = a + pltpu.roll(bm, N_CMP_PAD - 1, 0)
    hid = jax.nn.gelu(pre)
    out = _nn(hid.astype(BF16), w2_ref[...])
    row = lax.broadcasted_iota(jnp.int32, out.shape, 0)
    o_ref[0, 0] = jnp.where(row < N_CMP, out, 0.0).astype(o_ref.dtype)


def _compress_kernel(xk_ref, xv_ref, pk_ref, pv_ref, k1_ref, k2_ref, v1_ref, v2_ref, ok_ref, ov_ref):
    _compress_one(xk_ref, pk_ref, k1_ref, k2_ref, ok_ref)
    _compress_one(xv_ref, pv_ref, v1_ref, v2_ref, ov_ref)


def _compress(xk, xv, pk, pv, k1, k2, v1, v2):
    chunk_w = CMP_STRIDE * HEAD_DIM
    x_spec = pl.BlockSpec((1, 1, N_CMP_PAD, chunk_w), lambda b, j: (b, j, 0, 0))
    pe_spec = pl.BlockSpec((2, chunk_w), lambda b, j: (0, 0))
    w1_spec = pl.BlockSpec((2, chunk_w, CMP_HIDDEN), lambda b, j: (0, 0, 0))
    w2_spec = pl.BlockSpec((CMP_HIDDEN, LANES), lambda b, j: (0, 0))
    o_spec = pl.BlockSpec((1, 1, N_CMP_PAD, LANES), lambda b, j: (b, j, 0, 0))
    o_shape = jax.ShapeDtypeStruct((BATCH, NSA_KV_HEADS, N_CMP_PAD, LANES), BF16)
    return pl.pallas_call(
        _compress_kernel,
        out_shape=(o_shape, o_shape),
        grid=(BATCH, NSA_KV_HEADS),
        in_specs=[x_spec, x_spec, pe_spec, pe_spec, w1_spec, w2_spec, w1_spec, w2_spec],
        out_specs=(o_spec, o_spec),
        compiler_params=_params("parallel", "parallel"),
        name="nsa_compress",
    )(xk, xv, pk, pv, k1, k2, v1, v2)


NSA_QC = 256
NSA_KT = 512
NSA_G = NSA_HEADS // NSA_KV_HEADS
NSA_WIN_TILES = NSA_WINDOW // NSA_QC + 1


def _nsa_kernel(qa_ref, qb_ref, kc_ref, vc_ref, ks_ref, vs_ref, kw_ref, vw_ref, gl_ref, ovt_ref,
                o_ref, vct_ref, vst_ref, vwt_ref, bias_ref, gt_ref):
    c = pl.program_id(2)
    qc = NSA_QC
    q0 = c * qc
    lanes_of = lambda t, i: t[:, i * qc:(i + 1) * qc]

    @pl.when(c == 0)
    def _():
        vct_ref[...] = _transpose_bf16(vc_ref[0, 0])
        for t in range(SEQ // NSA_KT):
            vst_ref[t] = _transpose_aug(vs_ref[0, t * NSA_KT:(t + 1) * NSA_KT, :])
        for t in range(SEQ // qc):
            vwt_ref[t] = _transpose_aug(vw_ref[0, t * qc:(t + 1) * qc, :])

    qs = _stack_heads(qa_ref[0], qb_ref[0])

    sc_t = _nt(kc_ref[0, 0], qs)
    t0 = jnp.maximum(c - NSA_WINDOW // qc, 0)
    start = pl.multiple_of(t0 * qc, qc)
    sw_t = _nt(kw_ref[0, pl.ds(start, NSA_WIN_TILES * qc), :], qs)

    n_idx = lax.broadcasted_iota(jnp.int32, (N_CMP_PAD, qc), 0)
    q_idx = lax.broadcasted_iota(jnp.int32, (N_CMP_PAD, qc), 1)
    ok = (n_idx * CMP_STRIDE + (CMP_LEN - 1)) <= (q0 + q_idx)
    p_heads = []
    for i in range(NSA_G):
        s_i = jnp.where(ok, lanes_of(sc_t, i), NEG)
        e_i = jnp.where(ok, jnp.exp2(s_i - _reduce_keys(s_i, jnp.maximum, jnp.max)), 0.0)
        l_i = _reduce_keys(e_i, jnp.add, jnp.sum)
        p_heads.append((e_i / jnp.where(l_i > 0.0, l_i, 1.0)).astype(BF16))
    p_ct = jnp.concatenate(p_heads, axis=1)
    imp4 = _nn(ovt_ref[...], p_ct)
    ocmp_t = _nn(vct_ref[...], p_ct)

    band = _band_bias_t(NSA_WIN_TILES * qc, qc, q0 - start, NSA_WINDOW - 1)
    owin_t, _ = _softmax_block_t(sw_t + _tile_lanes(band, NSA_G), _pv_tiles(vwt_ref, t0, NSA_WIN_TILES, qc))
    gt_ref[...] = jnp.transpose(jax.nn.sigmoid(gl_ref[...]))
    first_head = pl.program_id(1) * NSA_G
    gate = lambda i, r: gt_ref[pl.ds(3 * (first_head + i) + r, 1), :]
    partial_out = [gate(i, 0) * lanes_of(ocmp_t, i) + gate(i, 2) * lanes_of(owin_t, i) for i in range(NSA_G)]

    imp = lanes_of(imp4, 0) + lanes_of(imp4, 1) + lanes_of(imp4, 2) + lanes_of(imp4, 3)
    blk = lax.broadcasted_iota(jnp.int32, imp.shape, 0)
    cur = (q0 + lax.broadcasted_iota(jnp.int32, imp.shape, 1)) >> 6
    valid = blk <= cur
    forced = valid & ((blk == 0) | (blk > cur - SLC_LOCAL))
    rank = _rank_rows(jnp.where(forced, FORCE, jnp.where(valid, imp, NEG)), N_SLC)
    bias_ref[...] = jnp.where(valid & (rank < SLC_TOPK), 0.0, NEG)

    key_row = lax.broadcasted_iota(jnp.int32, (NSA_KT, qc), 0)
    qpos = lax.broadcasted_iota(jnp.int32, (NSA_KT, qc), 1) + q0
    per_tile = NSA_KT // SLC_BLOCK

    def scores(t):
        ks0 = pl.multiple_of(t * NSA_KT, NSA_KT)
        blocks = [jnp.broadcast_to(bias_ref[pl.ds(t * per_tile + j, 1), :], (SLC_BLOCK, qc))
                  for j in range(per_tile)]
        bias = jnp.where(key_row + ks0 <= qpos, jnp.concatenate(blocks, axis=0), NEG)
        kt = ks_ref[0, pl.ds(ks0, NSA_KT), :]
        return [_nt(kt, qs[i * qc:(i + 1) * qc]) + bias for i in range(NSA_G)]

    slc = _flash_tiles(c // (NSA_KT // qc) + 1, SEQ // NSA_KT - 1, [_online_init_t(qc)] * NSA_G, scores,
                       lambda t: _pv_tiles(vst_ref, t, 1, NSA_KT))
    outs = [partial_out[i] + gate(i, 1) * _normalise(slc[i][1])[0] for i in range(NSA_G)]
    o_ref[0, :, 0:LANES] = jnp.transpose(_merge_pair_t(outs[0], outs[1])).astype(o_ref.dtype)
    o_ref[0, :, LANES:2 * LANES] = jnp.transpose(_merge_pair_t(outs[2], outs[3])).astype(o_ref.dtype)


def _nsa(z_rot, z_pl, k_cmp, v_cmp, gate_logits, ovt):
    qc = NSA_QC
    seq_spec = lambda base: pl.BlockSpec((1, SEQ, LANES), lambda b, j, c: (b, 0, base + j))
    cmp_spec = pl.BlockSpec((1, 1, N_CMP_PAD, LANES), lambda b, j, c: (b, j, 0, 0))
    return pl.pallas_call(
        _nsa_kernel,
        out_shape=jax.ShapeDtypeStruct((BATCH, SEQ, NSA_HEADS * HEAD_DIM), BF16),
        grid=(BATCH, NSA_KV_HEADS, SEQ // qc),
        in_specs=[
            pl.BlockSpec((1, qc, LANES), lambda b, j, c: (b, c, ROT_NQ + 2 * j)),
            pl.BlockSpec((1, qc, LANES), lambda b, j, c: (b, c, ROT_NQ + 2 * j + 1)),
            cmp_spec, cmp_spec,
            seq_spec(ROT_NKS), seq_spec(PL_NVS), seq_spec(ROT_NKW), seq_spec(PL_NVW),
            pl.BlockSpec((qc, LANES), lambda b, j, c: (b * (SEQ // qc) + c, 0)),
            pl.BlockSpec(ovt.shape, lambda b, j, c: (0, 0)),
        ],
        out_specs=pl.BlockSpec((1, qc, 2 * LANES), lambda b, j, c: (b, c, j)),
        scratch_shapes=[pltpu.VMEM((LANES, N_CMP_PAD), BF16),
                        pltpu.VMEM((SEQ // NSA_KT, VT_ROWS, NSA_KT), BF16),
                        pltpu.VMEM((SEQ // qc, VT_ROWS, qc), BF16),
                        pltpu.VMEM((N_SLC, qc), F32),
                        pltpu.VMEM((LANES, qc), F32)],
        compiler_params=_params("parallel", "parallel", "arbitrary"),
        name="nsa",
    )(z_rot, z_rot, k_cmp, v_cmp, z_rot, z_pl, z_rot, z_pl, gate_logits, ovt)


DIL_QC = 128
DIL_STEPS = SEQ // DIL_QC
DIL_UNROLL = 8


def _dil_group(q_ref, k_ref, v_ref, og_ref, lg_ref, gi):
    window, dil = DIL_CONFIGS[gi]
    qc = DIL_QC
    m = SEQ // dil
    n_back = window // dil
    nk = min(m, qc + -(-n_back // qc) * qc)
    chunks = m // qc

    def rows(first, n):
        return pl.ds(first, n) if dil == 1 else pl.ds(first, n, stride=dil)

    def place(idx):
        r = idx // chunks
        q0 = (idx % chunks) * qc
        start = jnp.maximum(q0 - (nk - qc), 0)
        return rows(r + dil * q0, qc), rows(r + dil * start, nk), q0 - start

    def body(i, _):
        at = [place(i * DIL_UNROLL + u) for u in range(DIL_UNROLL)]
        s = [_nt(k_ref[0, k_rows, :].astype(BF16), _stack_heads(q_ref[0, q_rows, :].astype(BF16)))
             for q_rows, k_rows, _ in at]
        m, p = [], []
        for u, (_, _, off) in enumerate(at):
            s_u = s[u] + _tile_lanes(_band_bias_t(nk, qc, off, n_back), 2)
            m.append(_reduce_keys(s_u, jnp.maximum, jnp.max))
            p.append(_probs(s_u, m[u]))
        acc = [_nn(_transpose_aug(v_ref[0, k_rows, :]), p[u]) for u, (_, k_rows, _) in enumerate(at)]
        for u, (q_rows, _, _) in enumerate(at):
            o_t, l = _normalise(acc[u])
            lse_b = jnp.broadcast_to(m[u] + jnp.log(l) * LOG2_E, (LANES, 2 * qc))
            og_ref[gi, q_rows, :] = jnp.transpose(_merge_pair_t(o_t[:, :qc], o_t[:, qc:]))
            lg_ref[gi, q_rows, :] = jnp.transpose(_merge_pair_t(lse_b[:, :qc], lse_b[:, qc:]))
        return 0

    lax.fori_loop(0, DIL_STEPS // DIL_UNROLL, body, 0)


def _dil_kernel(q_ref, k_ref, v_ref, o_ref, og_ref, lg_ref):
    g = pl.program_id(2)
    n_groups = len(DIL_CONFIGS)
    for gi in range(n_groups):
        pl.when(g == gi)(functools.partial(_dil_group, q_ref, k_ref, v_ref, og_ref, lg_ref, gi))

    @pl.when(g == n_groups - 1)
    def _():
        rows = 512

        def body(i, _):
            sl = pl.ds(pl.multiple_of(i * rows, rows), rows)
            l0, l1, l2 = lg_ref[0, sl, :], lg_ref[1, sl, :], lg_ref[2, sl, :]
            mx = jnp.maximum(jnp.maximum(l0, l1), l2)
            e0, e1, e2 = jnp.exp2(l0 - mx), jnp.exp2(l1 - mx), jnp.exp2(l2 - mx)
            den = e0 + e1 + e2
            out = (e0 / den) * og_ref[0, sl, :] + (e1 / den) * og_ref[1, sl, :] + (e2 / den) * og_ref[2, sl, :]
            o_ref[0, sl, :] = out.astype(o_ref.dtype)
            return 0

        lax.fori_loop(0, SEQ // rows, body, 0)


def _dilated(zd_rot, zd_pl):
    n_groups = len(DIL_CONFIGS)
    width = DIL_HEADS_PER_GROUP * HEAD_DIM
    col = lambda base: (lambda b, p, g: (b, 0, base + 2 * g + p))
    blk = lambda base: pl.BlockSpec((1, SEQ, LANES), col(base))
    return pl.pallas_call(
        _dil_kernel,
        out_shape=jax.ShapeDtypeStruct((BATCH, SEQ, width), BF16),
        grid=(BATCH, 2, n_groups),
        in_specs=[blk(0), blk(DIL_BLOCKS), blk(0)],
        out_specs=pl.BlockSpec((1, SEQ, LANES), lambda b, p, g: (b, 0, p)),
        scratch_shapes=[pltpu.VMEM((n_groups, SEQ, LANES), F32), pltpu.VMEM((n_groups, SEQ, LANES), F32)],
        compiler_params=_params("parallel", "parallel", "arbitrary"),
        name="dilated",
    )(zd_rot, zd_rot, zd_pl)


def _out_proj_kernel(oa_ref, ob_ref, oc_ref, h_ref, wa_ref, wb_ref, wc_ref, g_ref, b_ref, h1_ref, h1b_ref):
    n_rows = h_ref.shape[0] // 2
    halves = [pl.ds(i * n_rows, n_rows) for i in range(2)]
    y = [_nn(oa_ref[r, :], wa_ref[...]) + _nn(ob_ref[r, :], wb_ref[...]) + _nn(oc_ref[r, :], wc_ref[...])
         for r in halves]
    for r, y_r in zip(halves, y):
        h1 = _layer_norm(DEEPNORM_ALPHA * h_ref[r, :] + y_r, g_ref[...], b_ref[...])
        h1_ref[r, :] = h1
        h1b_ref[r, :] = h1.astype(BF16)


def _out_proj(oa, ob, oc, h, wa, wb, wc, g, b):
    tm = 512
    rows = lambda w: pl.BlockSpec((tm, w), lambda i: (i, 0))
    full = lambda a: pl.BlockSpec(a.shape, lambda i: (0, 0))
    return pl.pallas_call(
        _out_proj_kernel,
        out_shape=(jax.ShapeDtypeStruct((TOKENS, D_MODEL), F32),
                   jax.ShapeDtypeStruct((TOKENS, D_MODEL), BF16)),
        grid=(TOKENS // tm,),
        in_specs=[rows(oa.shape[1]), rows(ob.shape[1]), rows(oc.shape[1]), rows(D_MODEL),
                  full(wa), full(wb), full(wc), full(g), full(b)],
        out_specs=(rows(D_MODEL), rows(D_MODEL)),
        compiler_params=_params("parallel"),
        name="out_proj_ln",
    )(oa, ob, oc, h, wa, wb, wc, g, b)


def _router_kernel(hb_ref, rw_ref, rb_ref, comb_ref, sel_ref):
    logits = _nt(rw_ref[...], hb_ref[...]) + rb_ref[...]
    mx = jnp.max(logits, axis=0, keepdims=True)
    ex = jnp.exp(logits - mx)
    probs = ex / jnp.sum(ex, axis=0, keepdims=True)
    p = [probs[e:e + 1, :] for e in range(N_EXPERTS)]
    best, g_sel = None, None
    for g in range(N_GROUPS):
        a, b, c, d = p[4 * g:4 * g + 4]
        hi1, lo1, hi2, lo2 = jnp.maximum(a, b), jnp.minimum(a, b), jnp.maximum(c, d), jnp.minimum(c, d)
        top2 = jnp.maximum(hi1, hi2) + jnp.maximum(jnp.minimum(hi1, hi2), jnp.maximum(lo1, lo2))
        if g == 0:
            best, g_sel = top2, jnp.zeros_like(top2)
        else:
            better = top2 > best
            best = jnp.where(better, top2, best)
            g_sel = jnp.where(better, float(g), g_sel)
    chosen, picked = [], []
    for e in range(N_EXPERTS):
        g = e // EXPERTS_PER_GROUP
        rank = jnp.zeros_like(best)
        for o in range(4 * g, 4 * g + 4):
            if o < e:
                rank = rank + jnp.where(p[o] >= p[e], 1.0, 0.0)
            elif o > e:
                rank = rank + jnp.where(p[o] > p[e], 1.0, 0.0)
        chosen.append(jnp.where((g_sel == float(g)) & (rank < 2.0), 1.0, 0.0))
        picked.append(chosen[e] * p[e])
    total = picked[0]
    for e in range(1, N_EXPERTS):
        total = total + picked[e]
    comb_ref[...] = jnp.concatenate(picked, axis=0) / total
    sel_ref[...] = jnp.concatenate(chosen, axis=0)


def _router(hb, rw_t, rb):
    tm = 1024
    out = jax.ShapeDtypeStruct((N_EXPERTS, TOKENS), F32)
    o_spec = pl.BlockSpec((N_EXPERTS, tm), lambda i: (0, i))
    return pl.pallas_call(
        _router_kernel,
        out_shape=(out, out),
        grid=(TOKENS // tm,),
        in_specs=[pl.BlockSpec((tm, D_MODEL), lambda i: (i, 0)),
                  pl.BlockSpec((N_EXPERTS, D_MODEL), lambda i: (0, 0)),
                  pl.BlockSpec((N_EXPERTS, 1), lambda i: (0, 0))],
        out_specs=(o_spec, o_spec),
        compiler_params=_params("parallel"),
        name="router",
    )(hb, rw_t, rb)


def _routing_tables(comb_t, sel_t):
    sel = sel_t > 0.5
    cnt = jnp.sum(sel, axis=1, dtype=jnp.int32)
    cnt_pad = (cnt + (MOE_TILE - 1)) // MOE_TILE * MOE_TILE
    ends = jnp.cumsum(cnt_pad)
    rank = jnp.cumsum(sel.astype(jnp.int32), axis=1) - 1
    pos = (ends - cnt_pad)[:, None] + rank
    pos_lo = jnp.min(jnp.where(sel, pos, MOE_ROWS), axis=0)
    pos_hi = jnp.max(jnp.where(sel, pos, -1), axis=0)
    w_lo = jnp.sum(jnp.where(sel & (pos == pos_lo), comb_t, 0.0), axis=0)
    w_hi = jnp.sum(jnp.where(sel & (pos == pos_hi), comb_t, 0.0), axis=0)
    w = jnp.zeros((TOKENS, LANES), F32).at[:, 0].set(w_lo).at[:, 1].set(w_hi)
    n_tiles = ends[-1] // MOE_TILE
    tile_start = jnp.arange(MOE_TILES, dtype=jnp.int32) * MOE_TILE
    tile_start = jnp.minimum(tile_start, ends[-1] - MOE_TILE)
    tile_expert = jnp.sum((ends[None, :] <= tile_start[:, None]).astype(jnp.int32), axis=1)
    return jnp.stack([pos_lo, pos_hi]).astype(jnp.int32), w, tile_expert, n_tiles.reshape(1).astype(jnp.int32)


MOE_TILE = 256
MOE_TILES = 2 * TOKENS // MOE_TILE + N_EXPERTS
MOE_ROWS = MOE_TILES * MOE_TILE
SLAB = D_MODEL // LANES


def _to_slabs(ref, x, rows, first=0):
    for j in range(SLAB):
        ref[pl.ds(first * SLAB + j, rows, stride=SLAB), :] = x[:, j * LANES:(j + 1) * LANES]


def _from_slabs(ref, rows, first=0):
    return jnp.concatenate([ref[pl.ds(first * SLAB + j, rows, stride=SLAB), :] for j in range(SLAB)], axis=1)


def _slab_rows(row, n=SLAB):
    return pl.ds(pl.multiple_of(row * n, n), n)


XSLAB = SLAB // 2
U32 = jnp.uint32


def _to_packed_slabs(ref, x, rows):
    bits = lambda t: lax.bitcast_convert_type(t.astype(BF16).astype(F32), U32)
    for j in range(XSLAB):
        hi = bits(x[:, 2 * j * LANES:(2 * j + 1) * LANES])
        lo = bits(x[:, (2 * j + 1) * LANES:(2 * j + 2) * LANES])
        ref[pl.ds(j, rows, stride=XSLAB), :] = hi | (lo >> 16)


def _from_packed_slabs(ref, rows, first=0):
    parts = []
    for j in range(XSLAB):
        u = ref[pl.ds(first * XSLAB + j, rows, stride=XSLAB), :]
        parts.append(lax.bitcast_convert_type(u & jnp.uint32(0xFFFF0000), F32).astype(BF16))
        parts.append(lax.bitcast_convert_type(u << 16, F32).astype(BF16))
    return jnp.concatenate(parts, axis=1)


def _dispatch_kernel(pos_ref, h_ref, init_ref, xs_ref, slab_ref, sem):
    del init_ref
    tm = h_ref.shape[0]
    base = pl.program_id(0) * tm
    _to_packed_slabs(slab_ref, h_ref[...], tm)

    def copy(t, which):
        return pltpu.make_async_copy(slab_ref.at[_slab_rows(t, XSLAB), :],
                                     xs_ref.at[_slab_rows(pos_ref[which, base + t], XSLAB), :], sem)

    def start(t, _):
        copy(t, 0).start()
        copy(t, 1).start()
        return 0

    lax.fori_loop(0, tm, start, 0, unroll=8)
    whole = pltpu.make_async_copy(slab_ref, xs_ref.at[pl.ds(0, tm * XSLAB), :], sem)
    whole.wait()
    whole.wait()


def _dispatch(pos, h):
    tm = 256
    grid_spec = pltpu.PrefetchScalarGridSpec(
        num_scalar_prefetch=1,
        grid=(TOKENS // tm,),
        in_specs=[pl.BlockSpec((tm, D_MODEL), lambda i, pos: (i, 0)),
                  pl.BlockSpec(memory_space=pl.ANY)],
        out_specs=pl.BlockSpec(memory_space=pl.ANY),
        scratch_shapes=[pltpu.VMEM((tm * XSLAB, LANES), U32), pltpu.SemaphoreType.DMA],
    )
    return pl.pallas_call(
        _dispatch_kernel,
        out_shape=jax.ShapeDtypeStruct((MOE_ROWS * XSLAB, LANES), U32),
        grid_spec=grid_spec,
        input_output_aliases={2: 0},
        compiler_params=_params("arbitrary"),
        name="moe_dispatch",
    )(pos, h, jnp.zeros((MOE_ROWS * XSLAB, LANES), U32))


def _experts_kernel(te_ref, nt_ref, xs_ref, wg_ref, wu_ref, wd_ref, ys_ref, wgb_ref, wub_ref, wdb_ref):
    k = pl.program_id(0)
    e = te_ref[k]
    e_prev = te_ref[jnp.maximum(k - 1, 0)]

    @pl.when((k == 0) | (e != e_prev))
    def _():
        wgb_ref[...] = wg_ref[0, 0].astype(BF16)
        wub_ref[...] = wu_ref[0, 0].astype(BF16)
        wdb_ref[...] = wd_ref[0, 0].astype(BF16)

    @pl.when(k < nt_ref[0])
    def _():
        half = MOE_TILE // 2
        x = [_from_packed_slabs(xs_ref, half, h * half) for h in range(2)]
        gate = [_nn(x_h, wgb_ref[...]) for x_h in x]
        up = [_nn(x_h, wub_ref[...]) for x_h in x]
        hid = [(jax.nn.silu(g) * u).astype(BF16) for g, u in zip(gate, up)]
        y = [_nn(h_h, wdb_ref[...]) for h_h in hid]
        for h in range(2):
            _to_slabs(ys_ref, y[h], half, h * half)

    @pl.when(k >= nt_ref[0])
    def _():
        ys_ref[...] = jnp.zeros(ys_ref.shape, F32)


def _experts(tile_expert, n_tiles, xs, wg, wu, wd, layer):
    w_in_spec = pl.BlockSpec((1, 1, D_MODEL, EXPERT_HIDDEN), lambda k, te, nt: (layer, te[k], 0, 0))
    grid_spec = pltpu.PrefetchScalarGridSpec(
        num_scalar_prefetch=2,
        grid=(MOE_TILES,),
        in_specs=[pl.BlockSpec((MOE_TILE * XSLAB, LANES), lambda k, te, nt: (jnp.minimum(k, nt[0] - 1), 0)),
                  w_in_spec, w_in_spec,
                  pl.BlockSpec((1, 1, EXPERT_HIDDEN, D_MODEL), lambda k, te, nt: (layer, te[k], 0, 0))],
        out_specs=pl.BlockSpec((MOE_TILE * SLAB, LANES), lambda k, te, nt: (k, 0)),
        scratch_shapes=[pltpu.VMEM((D_MODEL, EXPERT_HIDDEN), BF16), pltpu.VMEM((D_MODEL, EXPERT_HIDDEN), BF16),
                        pltpu.VMEM((EXPERT_HIDDEN, D_MODEL), BF16)],
    )
    return pl.pallas_call(
        _experts_kernel,
        out_shape=jax.ShapeDtypeStruct((MOE_ROWS * SLAB, LANES), F32),
        grid_spec=grid_spec,
        compiler_params=_params("arbitrary"),
        name="moe_experts",
    )(tile_expert, n_tiles, xs, wg, wu, wd)


def _ple_ln_kernel(pos_ref, hb_ref, h_ref, ys_ref, w_ref, p_ref, gw_ref, gb_ref, pw_ref, g_ref, b_ref,
                   h2_ref, h2b_ref, lo_ref, hi_ref, sem):
    tm = h_ref.shape[0]
    i = pl.program_id(0)
    slot = i & 1
    bufs = (lo_ref, hi_ref)

    def fetch(tile, into):
        def start(t, _):
            for which in range(2):
                pltpu.make_async_copy(ys_ref.at[_slab_rows(pos_ref[which, tile * tm + t]), :],
                                      bufs[which].at[into, _slab_rows(t), :], sem.at[into]).start()
            return 0
        lax.fori_loop(0, tm, start, 0, unroll=8)

    @pl.when(i == 0)
    def _():
        fetch(0, 0)

    @pl.when(i + 1 < pl.num_programs(0))
    def _():
        fetch(i + 1, 1 - slot)

    for which in range(2):
        pltpu.make_async_copy(ys_ref.at[pl.ds(0, tm * SLAB), :], bufs[which].at[slot], sem.at[slot]).wait()
    n_rows = tm // 2
    halves = [pl.ds(i * n_rows, n_rows) for i in range(2)]
    gate = [_nn(hb_ref[r, :], gw_ref[...]) for r in halves]
    emb = [_nn(p_ref[r, :].astype(BF16), pw_ref[...]) for r in halves]
    for i, r in enumerate(halves):
        w = w_ref[r, :]
        ffn = (w[:, 0:1] * _from_slabs(lo_ref.at[slot], n_rows, i * n_rows)
               + w[:, 1:2] * _from_slabs(hi_ref.at[slot], n_rows, i * n_rows))
        ple = jax.nn.sigmoid(gate[i] + gb_ref[...]) * emb[i]
        h2 = _layer_norm(DEEPNORM_ALPHA * h_ref[r, :] + ffn + ple, g_ref[...], b_ref[...])
        h2_ref[r, :] = h2
        h2b_ref[r, :] = h2.astype(BF16)


def _ple_ln(pos, hb, h, ys, w, p, layer, gw, gb, pw, g, b):
    tm = 256
    p_spec = pl.BlockSpec((tm, PLE_DIM), lambda i, pos: (layer * (TOKENS // tm) + i, 0))
    rows = lambda width: pl.BlockSpec((tm, width), lambda i, pos: (i, 0))
    full = lambda a: pl.BlockSpec(a.shape, lambda i, pos: (0, 0))
    grid_spec = pltpu.PrefetchScalarGridSpec(
        num_scalar_prefetch=1,
        grid=(TOKENS // tm,),
        in_specs=[rows(D_MODEL), rows(D_MODEL), pl.BlockSpec(memory_space=pl.ANY), rows(LANES), p_spec,
                  full(gw), full(gb), full(pw), full(g), full(b)],
        out_specs=(rows(D_MODEL), rows(D_MODEL)),
        scratch_shapes=[pltpu.VMEM((2, tm * SLAB, LANES), F32), pltpu.VMEM((2, tm * SLAB, LANES), F32),
                        pltpu.SemaphoreType.DMA((2,))],
    )
    return pl.pallas_call(
        _ple_ln_kernel,
        out_shape=(jax.ShapeDtypeStruct((TOKENS, D_MODEL), F32),
                   jax.ShapeDtypeStruct((TOKENS, D_MODEL), BF16)),
        grid_spec=grid_spec,
        compiler_params=_params("arbitrary"),
        name="ple_ln",
    )(pos, hb, h, ys, w, p, gw, gb, pw, g, b)


def _rope_tables(positions):
    half = ROT_DIM // 2
    inv_freq = jnp.exp(jnp.arange(half, dtype=F32) * (-2.0 * math.log(ROPE_THETA) / ROT_DIM))
    ang = positions.astype(F32)[:, :, None] * inv_freq
    cos, sin = jnp.cos(ang), jnp.sin(ang)
    zeros = jnp.zeros_like(cos)
    rest = HEAD_DIM - ROT_DIM
    pad = lambda v: jnp.broadcast_to(jnp.asarray(v, F32), cos.shape[:2] + (rest,))
    c = jnp.concatenate([cos, cos, pad(1.0)], axis=-1)
    s1 = jnp.concatenate([-sin, zeros, pad(0.0)], axis=-1)
    s2 = jnp.concatenate([zeros, sin, pad(0.0)], axis=-1)
    tile = lambda t: jnp.concatenate([t, t], axis=-1).reshape(TOKENS, LANES)
    return tile(c), tile(s1), tile(s2)


def _split_w_in(w):
    mw, nq, nkv, dw = MOBA_HEADS * HEAD_DIM, NSA_HEADS * HEAD_DIM, NSA_KV_HEADS * HEAD_DIM, DIL_HEADS * HEAD_DIM
    widths = (mw, mw, mw, nq) + (nkv,) * 6 + (NSA_HEADS * 3, dw, dw, dw)
    offs = np.concatenate([[0], np.cumsum(widths)])
    qa, ka, va, qb, kbc, vbc, kbs, vbs, kbw, vbw, gb, qc, kc, vc = (
        w[:, int(offs[i]):int(offs[i + 1])] for i in range(len(widths)))

    def dup(t):
        t = t.reshape(D_MODEL, NSA_KV_HEADS, 1, HEAD_DIM)
        return jnp.broadcast_to(t, (D_MODEL, NSA_KV_HEADS, 2, HEAD_DIM)).reshape(D_MODEL, NSA_KV_HEADS * LANES)

    zpad = lambda n: jnp.zeros((D_MODEL, n * LANES), w.dtype)
    w_rot = jnp.concatenate([qa * Q_SCALE, ka, qb * Q_SCALE, dup(kbc), dup(kbs), dup(kbw), zpad(1)], axis=1)
    w_pl = jnp.concatenate([va, dup(vbc), dup(vbs), dup(vbw)], axis=1)
    w_gl = jnp.concatenate([gb, jnp.zeros((D_MODEL, LANES - NSA_HEADS * 3), w.dtype)], axis=1)
    w_dil_rot = jnp.concatenate([qc * Q_SCALE, kc], axis=1)
    return tuple(t.astype(BF16) for t in (w_rot, w_pl, w_gl, w_dil_rot, vc))


def _overlap_table():
    starts = np.arange(N_CMP) * CMP_STRIDE
    slc = np.arange(N_SLC) * SLC_BLOCK
    ov = ((starts[:, None] < slc[None, :] + SLC_BLOCK) & (starts[:, None] + CMP_LEN > slc[None, :]))
    ovt = np.zeros((N_SLC, N_CMP_PAD), np.float32)
    ovt[:, :N_CMP] = ov.T
    return jnp.asarray(ovt, BF16)


def _cmp_chunks(z, base):
    nblk = z.shape[-1] // LANES
    t = z.reshape(BATCH, SEQ // CMP_STRIDE, CMP_STRIDE, nblk, LANES)[:, :, :, base:base + NSA_KV_HEADS, :HEAD_DIM]
    return t.transpose(0, 3, 1, 2, 4).reshape(BATCH, NSA_KV_HEADS, SEQ // CMP_STRIDE, CMP_STRIDE * HEAD_DIM)


def kernel(x, p, positions, ln_in_g, ln_in_b, w_in, w_out, nsa_ck1, nsa_ck2, nsa_pe_k, nsa_cv1, nsa_cv2, nsa_pe_v, ln1_g, ln1_b, router_w, router_b, w_gate, w_up, w_down, ple_proj, ple_gate_w, ple_gate_b, ln2_g, ln2_b):
    rope = _rope_tables(positions)
    ovt = _overlap_table()
    rw_t = router_w.T.astype(BF16)
    rb = router_b.reshape(N_EXPERTS, 1).astype(F32)
    chunk_w = CMP_STRIDE * HEAD_DIM
    vec = lambda v: v.reshape(1, -1)
    seq3 = lambda t: t.reshape(BATCH, SEQ, t.shape[-1])
    flat = lambda t: t.reshape(TOKENS, t.shape[-1])

    h, hb = _ln_in(x.reshape(TOKENS, D_MODEL), ln_in_g, ln_in_b)
    for i in range(DEPTH):
        w_rot, w_pl, w_gl, w_dil_rot, w_dil_pl = _split_w_in(w_in[i])
        z_rot = seq3(_project(hb, w_rot, BF16, 768, rope=rope))
        z_pl = seq3(_project(hb, w_pl, BF16, PL_BLOCKS * LANES))
        gate_logits = _project(hb, w_gl, F32, LANES)
        zd_rot = seq3(_project(hb, w_dil_rot, F32, 768, rope=rope))
        zd_pl = seq3(_project(hb, w_dil_pl, F32, 768))

        o_a = _moba(z_rot, z_pl)

        dup2 = lambda w2: jnp.concatenate([w2, w2], axis=1).astype(BF16)
        k_cmp, v_cmp = _compress(
            _cmp_chunks(z_rot, ROT_NKC), _cmp_chunks(z_pl, PL_NVC),
            nsa_pe_k[i].reshape(2, chunk_w), nsa_pe_v[i].reshape(2, chunk_w),
            nsa_ck1[i].reshape(2, chunk_w, CMP_HIDDEN).astype(BF16), dup2(nsa_ck2[i]),
            nsa_cv1[i].reshape(2, chunk_w, CMP_HIDDEN).astype(BF16), dup2(nsa_cv2[i]))
        o_b = _nsa(z_rot, z_pl, k_cmp, v_cmp, gate_logits, ovt)

        o_c = _dilated(zd_rot, zd_pl)

        wo = w_out[i].astype(BF16)
        a_w, b_w = MOBA_HEADS * HEAD_DIM, NSA_HEADS * HEAD_DIM
        h, hb = _out_proj(flat(o_a), flat(o_b), flat(o_c), h,
                          wo[:a_w], wo[a_w:a_w + b_w], wo[a_w + b_w:], vec(ln1_g[i]), vec(ln1_b[i]))

        pos, w_tok, tile_expert, n_tiles = _routing_tables(*_router(hb, rw_t, rb))
        xs = _dispatch(pos, h)
        ys = _experts(tile_expert, n_tiles, xs, w_gate, w_up, w_down, i)
        h, hb = _ple_ln(pos, hb, h, ys, w_tok, p.reshape(DEPTH * TOKENS, PLE_DIM), i, ple_gate_w[i].astype(BF16),
                        vec(ple_gate_b[i]), ple_proj[i].astype(BF16), vec(ln2_g[i]), vec(ln2_b[i]))
    return h.reshape(BATCH, SEQ, D_MODEL)
```

```python
import functools
import math

import numpy as np
import jax
import jax.numpy as jnp
from jax import lax
from jax.experimental import pallas as pl
from jax.experimental.pallas import tpu as pltpu

F32 = jnp.float32
BF16 = jnp.bfloat16

D_MODEL = 2048
BATCH = 2
SEQ = 4096
DEPTH = 4
TOKENS = BATCH * SEQ
HEAD_DIM = 64
ROT_DIM = HEAD_DIM // 4
ROPE_THETA = 500000.0
NEG = -1e30
FORCE = 1e30
LN_EPS = 1e-5
SCALE = HEAD_DIM ** -0.5
LOG2_E = math.log2(math.e)
Q_SCALE = SCALE * LOG2_E

MOBA_HEADS = 8
MOBA_BLOCK = 256
MOBA_TOPK = 3
MOBA_NB = SEQ // MOBA_BLOCK

NSA_HEADS = 12
NSA_KV_HEADS = 3
CMP_LEN = 32
CMP_STRIDE = 16
CMP_HIDDEN = 128
N_CMP = (SEQ - CMP_LEN) // CMP_STRIDE + 1
N_CMP_PAD = 256
SLC_BLOCK = 64
SLC_TOPK = 16
SLC_LOCAL = 2
N_SLC = SEQ // SLC_BLOCK
NSA_WINDOW = 512

DIL_CONFIGS = ((128, 1), (512, 4), (2048, 16))
DIL_HEADS_PER_GROUP = 4
DIL_HEADS = DIL_HEADS_PER_GROUP * len(DIL_CONFIGS)

N_EXPERTS = 16
N_GROUPS = 4
EXPERTS_PER_GROUP = 4
EXPERT_HIDDEN = D_MODEL // 4
PLE_DIM = 256

DEEPNORM_ALPHA = (2 * DEPTH) ** 0.25

LANES = 128
VMEM_LIMIT = 56 * 1024 * 1024

ROT_MQ, ROT_MK, ROT_NQ, ROT_NKC, ROT_NKS, ROT_NKW = 0, 4, 8, 14, 17, 20
ROT_BLOCKS = 24
PL_MV, PL_NVC, PL_NVS, PL_NVW = 0, 4, 7, 10
PL_BLOCKS = 16
DIL_BLOCKS = DIL_HEADS // 2

NT_DIMS = (((1,), (1,)), ((), ()))


def _nt(a, b):
    return lax.dot_general(a, b, NT_DIMS, preferred_element_type=F32)


def _nn(a, b):
    return jnp.dot(a, b, preferred_element_type=F32)


def _params(*sem):
    return pltpu.CompilerParams(dimension_semantics=sem, vmem_limit_bytes=VMEM_LIMIT)


def _layer_norm(y, g, b):
    mu = jnp.mean(y, axis=-1, keepdims=True)
    yc = y - mu
    var = jnp.mean(yc * yc, axis=-1, keepdims=True)
    return yc * lax.rsqrt(var + LN_EPS) * g + b


def _ln_kernel(x_ref, g_ref, b_ref, h_ref, hb_ref):
    h = _layer_norm(x_ref[...], g_ref[...], b_ref[...])
    h_ref[...] = h
    hb_ref[...] = h.astype(BF16)


def _ln_in(x, g, b):
    tm = 512
    row = pl.BlockSpec((tm, D_MODEL), lambda i: (i, 0))
    vec = pl.BlockSpec((1, D_MODEL), lambda i: (0, 0))
    return pl.pallas_call(
        _ln_kernel,
        out_shape=(jax.ShapeDtypeStruct((TOKENS, D_MODEL), F32),
                   jax.ShapeDtypeStruct((TOKENS, D_MODEL), BF16)),
        grid=(TOKENS // tm,),
        in_specs=[row, vec, vec],
        out_specs=(row, row),
        compiler_params=_params("parallel"),
        name="ln_in",
    )(x, g.reshape(1, -1), b.reshape(1, -1))


def _proj_kernel(x_ref, w_ref, o_ref):
    o_ref[...] = _nn(x_ref[...], w_ref[...]).astype(o_ref.dtype)


def _proj_rot_kernel(x_ref, w_ref, c_ref, s1_ref, s2_ref, o_ref):
    x = x_ref[...]
    c, s1, s2 = c_ref[...], s1_ref[...], s2_ref[...]
    half = ROT_DIM // 2
    for j0 in range(0, o_ref.shape[1], 2 * LANES):
        z = _nn(x, w_ref[:, j0:j0 + 2 * LANES])
        for j in range(j0, j0 + 2 * LANES, LANES):
            zc = z[:, j - j0:j - j0 + LANES]
            r = zc * c + pltpu.roll(zc, LANES - half, 1) * s1 + pltpu.roll(zc, half, 1) * s2
            o_ref[:, j:j + LANES] = r.astype(o_ref.dtype)


def _project(hb, w, out_dtype, tn, rope=None):
    tm = 1024
    n = w.shape[1]
    x_spec = pl.BlockSpec((tm, D_MODEL), lambda i, j: (i, 0))
    w_spec = pl.BlockSpec((D_MODEL, tn), lambda i, j: (0, j))
    o_spec = pl.BlockSpec((tm, tn), lambda i, j: (i, j))
    if rope is None:
        kern, extra, extra_specs = _proj_kernel, (), []
    else:
        t_spec = pl.BlockSpec((tm, LANES), lambda i, j: (i, 0))
        kern, extra, extra_specs = _proj_rot_kernel, rope, [t_spec] * 3
    return pl.pallas_call(
        kern,
        out_shape=jax.ShapeDtypeStruct((TOKENS, n), out_dtype),
        grid=(TOKENS // tm, n // tn),
        in_specs=[x_spec, w_spec] + extra_specs,
        out_specs=o_spec,
        compiler_params=_params("parallel", "arbitrary"),
        name="in_proj_rot" if rope is not None else "in_proj",
    )(hb, w, *extra)


def _stack_heads(*q_blocks):
    parts = []
    for q in q_blocks:
        lane = lax.broadcasted_iota(jnp.int32, q.shape, 1)
        zero = jnp.zeros_like(q)
        parts += [jnp.where(lane < HEAD_DIM, q, zero), jnp.where(lane >= HEAD_DIM, q, zero)]
    return jnp.concatenate(parts, axis=0)


def _merge_pair_t(lo, hi):
    sub = lax.broadcasted_iota(jnp.int32, lo.shape, 0)
    return jnp.where(sub < HEAD_DIM, lo, hi)


def _band_bias_t(nk, qc, offset, n_back):
    key = lax.broadcasted_iota(jnp.int32, (nk, qc), 0)
    qry = lax.broadcasted_iota(jnp.int32, (nk, qc), 1)
    diff = offset + qry - key
    return jnp.where((diff >= 0) & (diff <= n_back), 0.0, NEG)


def _tile_lanes(x, n):
    return jnp.concatenate([x] * n, axis=1)


def _transpose_bf16(v):
    return jnp.transpose(v.astype(F32)).astype(BF16)


def _tree(x, op):
    n = x.shape[0]
    if n == 8:
        return x
    if n % 16 == 0:
        return op(_tree(x[:n // 2], op), _tree(x[n // 2:], op))
    acc = x[:8]
    for i in range(1, n // 8):
        acc = op(acc, x[8 * i:8 * i + 8])
    return acc


def _reduce_keys(x, op, final):
    return final(_tree(x, op), axis=0, keepdims=True)


VT_ROWS = LANES + 16


def _transpose_aug(v):
    vt = jnp.transpose(v.astype(F32))
    sub = lax.broadcasted_iota(jnp.int32, (VT_ROWS - LANES, v.shape[0]), 0)
    return jnp.concatenate([vt, jnp.where(sub == 0, 1.0, 0.0)], axis=0).astype(BF16)


def _probs(s_t, m):
    return jnp.exp2((s_t - m).astype(BF16))


def _normalise(acc):
    l = acc[LANES:LANES + 1]
    return acc[:LANES] / l, l


def _softmax_block_t(s_t, pv):
    m = _reduce_keys(s_t, jnp.maximum, jnp.max)
    out, l = _normalise(pv(_probs(s_t, m)))
    return out, m + jnp.log(l) * LOG2_E


def _online_step_t(carry, s_t, m_t, pv):
    m, acc = carry
    m_new = jnp.maximum(m, m_t)
    acc = jnp.exp2(m - m_new) * acc + pv(_probs(s_t, m_new))
    return m_new, acc


def _flash_tiles(n_tiles, last_tile, init, scores, pv_of):
    strips = range(len(init))

    def produce(t):
        s = tuple(scores(t))
        return s, tuple(_reduce_keys(s_i, jnp.maximum, jnp.max) for s_i in s)

    def body(t, carry):
        state, s_t, m_t = carry
        s_next, m_next = produce(jnp.minimum(t + 1, last_tile))
        pv = pv_of(t)
        return tuple(_online_step_t(state[i], s_t[i], m_t[i], pv) for i in strips), s_next, m_next

    state, _, _ = lax.fori_loop(0, n_tiles, body, (tuple(init),) + produce(0))
    return state


def _pv_tiles(vt_ref, first, n, rows):
    def pv(p):
        acc = _nn(vt_ref[first], p[:rows])
        for j in range(1, n):
            acc = acc + _nn(vt_ref[first + j], p[j * rows:(j + 1) * rows])
        return acc
    return pv


def _online_init_t(r):
    return (jnp.full((1, r), NEG, F32), jnp.zeros((VT_ROWS, r), F32))


def _rank_rows(g, n_rows):
    sub = lax.broadcasted_iota(jnp.int32, (8, g.shape[1]), 0)
    rank = jnp.zeros(g.shape, F32)
    for m in range(n_rows):
        gm = g[m:m + 1, :]
        b = m // 8 * 8
        mid = g[b:b + 8]
        parts = [jnp.where(gm > mid, 1.0, jnp.where((gm == mid) & (sub > m - b), 1.0, 0.0))]
        if b > 0:
            parts.insert(0, jnp.where(gm > g[:b], 1.0, 0.0))
        if b + 8 < n_rows:
            parts.append(jnp.where(gm >= g[b + 8:], 1.0, 0.0))
        rank = rank + jnp.concatenate(parts, axis=0)
    return rank


MOBA_QC = 256
MOBA_KT = 2 * MOBA_BLOCK


def _moba_kernel(q_ref, k_ref, v_ref, o_ref, kmean_ref, vt_ref, bias_ref):
    c = pl.program_id(2)
    qc = MOBA_QC
    r = 2 * qc

    @pl.when(c == 0)
    def _():
        row = lax.broadcasted_iota(jnp.int32, (MOBA_NB, SEQ), 0)
        col = lax.broadcasted_iota(jnp.int32, (MOBA_NB, SEQ), 1)
        avg = jnp.where((col >> 8) == row, 1.0 / MOBA_BLOCK, 0.0).astype(BF16)
        kmean_ref[...] = _nn(avg, k_ref[0])
        for t in range(MOBA_NB):
            vt_ref[t] = _transpose_aug(v_ref[0, t * MOBA_BLOCK:(t + 1) * MOBA_BLOCK, :])

    qs = _stack_heads(q_ref[0])

    ks = pl.multiple_of(c * MOBA_BLOCK, MOBA_BLOCK)
    heads = [qs[:qc], qs[qc:]]
    causal = _band_bias_t(MOBA_BLOCK, qc, 0, MOBA_BLOCK)
    s_own = [_nt(k_ref[0, pl.ds(ks, MOBA_BLOCK), :], q_h) + causal for q_h in heads]
    state = [_online_step_t(_online_init_t(qc), s_h, _reduce_keys(s_h, jnp.maximum, jnp.max),
                            _pv_tiles(vt_ref, c, 1, MOBA_BLOCK)) for s_h in s_own]

    gate = _nt(kmean_ref[...].astype(BF16), qs)
    blk = lax.broadcasted_iota(jnp.int32, gate.shape, 0)
    past = blk < c
    rank = _rank_rows(jnp.where(past, gate, NEG), MOBA_NB)
    bias_ref[...] = jnp.where(past & (rank < MOBA_TOPK), 0.0, NEG)

    per_tile = MOBA_KT // MOBA_BLOCK

    def scores(t):
        ks = pl.multiple_of(t * MOBA_KT, MOBA_KT)
        kt = k_ref[0, pl.ds(ks, MOBA_KT), :]
        rows = [bias_ref[pl.ds(t * per_tile + j, 1), :] for j in range(per_tile)]
        out = []
        for h in range(2):
            blocks = [jnp.broadcast_to(row[:, h * qc:(h + 1) * qc], (MOBA_BLOCK, qc)) for row in rows]
            out.append(_nt(kt, heads[h]) + jnp.concatenate(blocks, axis=0))
        return out

    state = _flash_tiles((c + per_tile - 1) // per_tile, SEQ // MOBA_KT - 1, state, scores,
                         lambda t: _pv_tiles(vt_ref, t * per_tile, per_tile, MOBA_BLOCK))
    o_lo, o_hi = (_normalise(acc)[0] for _, acc in state)
    o_ref[0] = jnp.transpose(_merge_pair_t(o_lo, o_hi)).astype(o_ref.dtype)


def _moba(z_rot, z_pl):
    qc = MOBA_QC
    grid = (BATCH, MOBA_HEADS // 2, SEQ // qc)
    return pl.pallas_call(
        _moba_kernel,
        out_shape=jax.ShapeDtypeStruct((BATCH, SEQ, MOBA_HEADS * HEAD_DIM), BF16),
        grid=grid,
        in_specs=[
            pl.BlockSpec((1, qc, LANES), lambda b, p, c: (b, c, ROT_MQ + p)),
            pl.BlockSpec((1, SEQ, LANES), lambda b, p, c: (b, 0, ROT_MK + p)),
            pl.BlockSpec((1, SEQ, LANES), lambda b, p, c: (b, 0, PL_MV + p)),
        ],
        out_specs=pl.BlockSpec((1, qc, LANES), lambda b, p, c: (b, c, p)),
        scratch_shapes=[pltpu.VMEM((MOBA_NB, LANES), F32),
                        pltpu.VMEM((MOBA_NB, VT_ROWS, MOBA_BLOCK), BF16),
                        pltpu.VMEM((MOBA_NB, 2 * qc), F32)],
        compiler_params=_params("parallel", "parallel", "arbitrary"),
        name="moba",
    )(z_rot, z_rot, z_pl)


def _compress_one(x_ref, pe_ref, w1_ref, w2_ref, o_ref):
    x = x_ref[0, 0].astype(F32)
    top = (x + pe_ref[0:1, :]).astype(BF16)
    bot = (x + pe_ref[1:2, :]).astype(BF16)
    a = _nn(top, w1_ref[0])
    bm = _nn(bot, w1_ref[1])
    pre = a + pltpu.roll(bm, N_CMP_PAD - 1, 0)
    hid = jax.nn.gelu(pre)
    out = _nn(hid.astype(BF16), w2_ref[...])
    row = lax.broadcasted_iota(jnp.int32, out.shape, 0)
    o_ref[0, 0] = jnp.where(row < N_CMP, out, 0.0).astype(o_ref.dtype)


def _compress_kernel(xk_ref, xv_ref, pk_ref, pv_ref, k1_ref, k2_ref, v1_ref, v2_ref, ok_ref, ov_ref):
    _compress_one(xk_ref, pk_ref, k1_ref, k2_ref, ok_ref)
    _compress_one(xv_ref, pv_ref, v1_ref, v2_ref, ov_ref)


def _compress(xk, xv, pk, pv, k1, k2, v1, v2):
    chunk_w = CMP_STRIDE * HEAD_DIM
    x_spec = pl.BlockSpec((1, 1, N_CMP_PAD, chunk_w), lambda b, j: (b, j, 0, 0))
    pe_spec = pl.BlockSpec((2, chunk_w), lambda b, j: (0, 0))
    w1_spec = pl.BlockSpec((2, chunk_w, CMP_HIDDEN), lambda b, j: (0, 0, 0))
    w2_spec = pl.BlockSpec((CMP_HIDDEN, LANES), lambda b, j: (0, 0))
    o_spec = pl.BlockSpec((1, 1, N_CMP_PAD, LANES), lambda b, j: (b, j, 0, 0))
    o_shape = jax.ShapeDtypeStruct((BATCH, NSA_KV_HEADS, N_CMP_PAD, LANES), BF16)
    return pl.pallas_call(
        _compress_kernel,
        out_shape=(o_shape, o_shape),
        grid=(BATCH, NSA_KV_HEADS),
        in_specs=[x_spec, x_spec, pe_spec, pe_spec, w1_spec, w2_spec, w1_spec, w2_spec],
        out_specs=(o_spec, o_spec),
        compiler_params=_params("parallel", "parallel"),
        name="nsa_compress",
    )(xk, xv, pk, pv, k1, k2, v1, v2)


NSA_QC = 256
NSA_KT = 512
NSA_G = NSA_HEADS // NSA_KV_HEADS
NSA_WIN_TILES = NSA_WINDOW // NSA_QC + 1


def _nsa_kernel(qa_ref, qb_ref, kc_ref, vc_ref, ks_ref, vs_ref, kw_ref, vw_ref, gl_ref, ovt_ref,
                o_ref, vct_ref, vst_ref, vwt_ref, bias_ref):
    c = pl.program_id(2)
    qc = NSA_QC
    q0 = c * qc
    lanes_of = lambda t, i: t[:, i * qc:(i + 1) * qc]

    @pl.when(c == 0)
    def _():
        vct_ref[...] = _transpose_bf16(vc_ref[0, 0])
        for t in range(SEQ // NSA_KT):
            vst_ref[t] = _transpose_aug(vs_ref[0, t * NSA_KT:(t + 1) * NSA_KT, :])
        for t in range(SEQ // qc):
            vwt_ref[t] = _transpose_aug(vw_ref[0, t * qc:(t + 1) * qc, :])

    qs = _stack_heads(qa_ref[0], qb_ref[0])

    sc_t = _nt(kc_ref[0, 0], qs)
    t0 = jnp.maximum(c - NSA_WINDOW // qc, 0)
    start = pl.multiple_of(t0 * qc, qc)
    sw_t = _nt(kw_ref[0, pl.ds(start, NSA_WIN_TILES * qc), :], qs)

    n_idx = lax.broadcasted_iota(jnp.int32, (N_CMP_PAD, qc), 0)
    q_idx = lax.broadcasted_iota(jnp.int32, (N_CMP_PAD, qc), 1)
    ok = (n_idx * CMP_STRIDE + (CMP_LEN - 1)) <= (q0 + q_idx)
    p_heads = []
    for i in range(NSA_G):
        s_i = jnp.where(ok, lanes_of(sc_t, i), NEG)
        e_i = jnp.where(ok, jnp.exp2(s_i - _reduce_keys(s_i, jnp.maximum, jnp.max)), 0.0)
        l_i = _reduce_keys(e_i, jnp.add, jnp.sum)
        p_heads.append((e_i / jnp.where(l_i > 0.0, l_i, 1.0)).astype(BF16))
    p_ct = jnp.concatenate(p_heads, axis=1)
    imp4 = _nn(ovt_ref[...], p_ct)
    ocmp_t = _nn(vct_ref[...], p_ct)

    band = _band_bias_t(NSA_WIN_TILES * qc, qc, q0 - start, NSA_WINDOW - 1)
    owin_t, _ = _softmax_block_t(sw_t + _tile_lanes(band, NSA_G), _pv_tiles(vwt_ref, t0, NSA_WIN_TILES, qc))
    gate_t = jnp.transpose(jax.nn.sigmoid(gl_ref[...]))
    gate = lambda i, r: gate_t[3 * i + r:3 * i + r + 1, :]
    partial_out = [gate(i, 0) * lanes_of(ocmp_t, i) + gate(i, 2) * lanes_of(owin_t, i) for i in range(NSA_G)]

    imp = lanes_of(imp4, 0) + lanes_of(imp4, 1) + lanes_of(imp4, 2) + lanes_of(imp4, 3)
    blk = lax.broadcasted_iota(jnp.int32, imp.shape, 0)
    cur = (q0 + lax.broadcasted_iota(jnp.int32, imp.shape, 1)) >> 6
    valid = blk <= cur
    forced = valid & ((blk == 0) | (blk > cur - SLC_LOCAL))
    rank = _rank_rows(jnp.where(forced, FORCE, jnp.where(valid, imp, NEG)), N_SLC)
    bias_ref[...] = jnp.where(valid & (rank < SLC_TOPK), 0.0, NEG)

    key_row = lax.broadcasted_iota(jnp.int32, (NSA_KT, qc), 0)
    qpos = lax.broadcasted_iota(jnp.int32, (NSA_KT, qc), 1) + q0
    per_tile = NSA_KT // SLC_BLOCK

    def scores(t):
        ks0 = pl.multiple_of(t * NSA_KT, NSA_KT)
        blocks = [jnp.broadcast_to(bias_ref[pl.ds(t * per_tile + j, 1), :], (SLC_BLOCK, qc))
                  for j in range(per_tile)]
        bias = jnp.where(key_row + ks0 <= qpos, jnp.concatenate(blocks, axis=0), NEG)
        kt = ks_ref[0, pl.ds(ks0, NSA_KT), :]
        return [_nt(kt, qs[i * qc:(i + 1) * qc]) + bias for i in range(NSA_G)]

    slc = _flash_tiles(c // (NSA_KT // qc) + 1, SEQ // NSA_KT - 1, [_online_init_t(qc)] * NSA_G, scores,
                       lambda t: _pv_tiles(vst_ref, t, 1, NSA_KT))
    outs = [partial_out[i] + gate(i, 1) * _normalise(slc[i][1])[0] for i in range(NSA_G)]
    o_ref[0, :, 0:LANES] = jnp.transpose(_merge_pair_t(outs[0], outs[1])).astype(o_ref.dtype)
    o_ref[0, :, LANES:2 * LANES] = jnp.transpose(_merge_pair_t(outs[2], outs[3])).astype(o_ref.dtype)


def _nsa(z_rot, z_pl, k_cmp, v_cmp, gate_logits, ovt):
    qc = NSA_QC
    seq_spec = lambda base: pl.BlockSpec((1, SEQ, LANES), lambda b, j, c: (b, 0, base + j))
    cmp_spec = pl.BlockSpec((1, 1, N_CMP_PAD, LANES), lambda b, j, c: (b, j, 0, 0))
    return pl.pallas_call(
        _nsa_kernel,
        out_shape=jax.ShapeDtypeStruct((BATCH, SEQ, NSA_HEADS * HEAD_DIM), BF16),
        grid=(BATCH, NSA_KV_HEADS, SEQ // qc),
        in_specs=[
            pl.BlockSpec((1, qc, LANES), lambda b, j, c: (b, c, ROT_NQ + 2 * j)),
            pl.BlockSpec((1, qc, LANES), lambda b, j, c: (b, c, ROT_NQ + 2 * j + 1)),
            cmp_spec, cmp_spec,
            seq_spec(ROT_NKS), seq_spec(PL_NVS), seq_spec(ROT_NKW), seq_spec(PL_NVW),
            pl.BlockSpec((qc, LANES), lambda b, j, c: (b * (SEQ // qc) + c, j)),
            pl.BlockSpec(ovt.shape, lambda b, j, c: (0, 0)),
        ],
        out_specs=pl.BlockSpec((1, qc, 2 * LANES), lambda b, j, c: (b, c, j)),
        scratch_shapes=[pltpu.VMEM((LANES, N_CMP_PAD), BF16),
                        pltpu.VMEM((SEQ // NSA_KT, VT_ROWS, NSA_KT), BF16),
                        pltpu.VMEM((SEQ // qc, VT_ROWS, qc), BF16),
                        pltpu.VMEM((N_SLC, qc), F32)],
        compiler_params=_params("parallel", "parallel", "arbitrary"),
        name="nsa",
    )(z_rot, z_rot, k_cmp, v_cmp, z_rot, z_pl, z_rot, z_pl, gate_logits, ovt)


DIL_QC = 128
DIL_STEPS = SEQ // DIL_QC
DIL_UNROLL = 8


def _dil_group(q_ref, k_ref, v_ref, og_ref, lg_ref, gi):
    window, dil = DIL_CONFIGS[gi]
    qc = DIL_QC
    m = SEQ // dil
    n_back = window // dil
    nk = min(m, qc + -(-n_back // qc) * qc)
    chunks = m // qc

    def rows(first, n):
        return pl.ds(first, n) if dil == 1 else pl.ds(first, n, stride=dil)

    def place(idx):
        r = idx // chunks
        q0 = (idx % chunks) * qc
        start = jnp.maximum(q0 - (nk - qc), 0)
        return rows(r + dil * q0, qc), rows(r + dil * start, nk), q0 - start

    def body(i, _):
        at = [place(i * DIL_UNROLL + u) for u in range(DIL_UNROLL)]
        s = [_nt(k_ref[0, k_rows, :].astype(BF16), _stack_heads(q_ref[0, q_rows, :].astype(BF16)))
             for q_rows, k_rows, _ in at]
        m, p = [], []
        for u, (_, _, off) in enumerate(at):
            s_u = s[u] + _tile_lanes(_band_bias_t(nk, qc, off, n_back), 2)
            m.append(_reduce_keys(s_u, jnp.maximum, jnp.max))
            p.append(_probs(s_u, m[u]))
        acc = [_nn(_transpose_aug(v_ref[0, k_rows, :]), p[u]) for u, (_, k_rows, _) in enumerate(at)]
        for u, (q_rows, _, _) in enumerate(at):
            o_t, l = _normalise(acc[u])
            lse_b = jnp.broadcast_to(m[u] + jnp.log(l) * LOG2_E, (LANES, 2 * qc))
            og_ref[gi, q_rows, :] = jnp.transpose(_merge_pair_t(o_t[:, :qc], o_t[:, qc:]))
            lg_ref[gi, q_rows, :] = jnp.transpose(_merge_pair_t(lse_b[:, :qc], lse_b[:, qc:]))
        return 0

    lax.fori_loop(0, DIL_STEPS // DIL_UNROLL, body, 0)


def _dil_kernel(q_ref, k_ref, v_ref, o_ref, og_ref, lg_ref):
    g = pl.program_id(2)
    n_groups = len(DIL_CONFIGS)
    for gi in range(n_groups):
        pl.when(g == gi)(functools.partial(_dil_group, q_ref, k_ref, v_ref, og_ref, lg_ref, gi))

    @pl.when(g == n_groups - 1)
    def _():
        rows = 512

        def body(i, _):
            sl = pl.ds(pl.multiple_of(i * rows, rows), rows)
            l0, l1, l2 = lg_ref[0, sl, :], lg_ref[1, sl, :], lg_ref[2, sl, :]
            mx = jnp.maximum(jnp.maximum(l0, l1), l2)
            e0, e1, e2 = jnp.exp2(l0 - mx), jnp.exp2(l1 - mx), jnp.exp2(l2 - mx)
            den = e0 + e1 + e2
            out = (e0 / den) * og_ref[0, sl, :] + (e1 / den) * og_ref[1, sl, :] + (e2 / den) * og_ref[2, sl, :]
            o_ref[0, sl, :] = out.astype(o_ref.dtype)
            return 0

        lax.fori_loop(0, SEQ // rows, body, 0)


def _dilated(zd_rot, zd_pl):
    n_groups = len(DIL_CONFIGS)
    width = DIL_HEADS_PER_GROUP * HEAD_DIM
    col = lambda base: (lambda b, p, g: (b, 0, base + 2 * g + p))
    blk = lambda base: pl.BlockSpec((1, SEQ, LANES), col(base))
    return pl.pallas_call(
        _dil_kernel,
        out_shape=jax.ShapeDtypeStruct((BATCH, SEQ, width), BF16),
        grid=(BATCH, 2, n_groups),
        in_specs=[blk(0), blk(DIL_BLOCKS), blk(0)],
        out_specs=pl.BlockSpec((1, SEQ, LANES), lambda b, p, g: (b, 0, p)),
        scratch_shapes=[pltpu.VMEM((n_groups, SEQ, LANES), F32), pltpu.VMEM((n_groups, SEQ, LANES), F32)],
        compiler_params=_params("parallel", "parallel", "arbitrary"),
        name="dilated",
    )(zd_rot, zd_rot, zd_pl)


def _out_proj_kernel(oa_ref, ob_ref, oc_ref, h_ref, wa_ref, wb_ref, wc_ref, g_ref, b_ref, h1_ref, h1b_ref):
    y = _nn(oa_ref[...], wa_ref[...]) + _nn(ob_ref[...], wb_ref[...]) + _nn(oc_ref[...], wc_ref[...])
    h1 = _layer_norm(DEEPNORM_ALPHA * h_ref[...] + y, g_ref[...], b_ref[...])
    h1_ref[...] = h1
    h1b_ref[...] = h1.astype(BF16)


def _out_proj(oa, ob, oc, h, wa, wb, wc, g, b):
    tm = 512
    rows = lambda w: pl.BlockSpec((tm, w), lambda i: (i, 0))
    full = lambda a: pl.BlockSpec(a.shape, lambda i: (0, 0))
    return pl.pallas_call(
        _out_proj_kernel,
        out_shape=(jax.ShapeDtypeStruct((TOKENS, D_MODEL), F32),
                   jax.ShapeDtypeStruct((TOKENS, D_MODEL), BF16)),
        grid=(TOKENS // tm,),
        in_specs=[rows(oa.shape[1]), rows(ob.shape[1]), rows(oc.shape[1]), rows(D_MODEL),
                  full(wa), full(wb), full(wc), full(g), full(b)],
        out_specs=(rows(D_MODEL), rows(D_MODEL)),
        compiler_params=_params("parallel"),
        name="out_proj_ln",
    )(oa, ob, oc, h, wa, wb, wc, g, b)


def _router_kernel(hb_ref, rw_ref, rb_ref, comb_ref, sel_ref):
    logits = _nt(rw_ref[...], hb_ref[...]) + rb_ref[...]
    mx = jnp.max(logits, axis=0, keepdims=True)
    ex = jnp.exp(logits - mx)
    probs = ex / jnp.sum(ex, axis=0, keepdims=True)
    p = [probs[e:e + 1, :] for e in range(N_EXPERTS)]
    best, g_sel = None, None
    for g in range(N_GROUPS):
        a, b, c, d = p[4 * g:4 * g + 4]
        hi1, lo1, hi2, lo2 = jnp.maximum(a, b), jnp.minimum(a, b), jnp.maximum(c, d), jnp.minimum(c, d)
        top2 = jnp.maximum(hi1, hi2) + jnp.maximum(jnp.minimum(hi1, hi2), jnp.maximum(lo1, lo2))
        if g == 0:
            best, g_sel = top2, jnp.zeros_like(top2)
        else:
            better = top2 > best
            best = jnp.where(better, top2, best)
            g_sel = jnp.where(better, float(g), g_sel)
    chosen, picked = [], []
    for e in range(N_EXPERTS):
        g = e // EXPERTS_PER_GROUP
        rank = jnp.zeros_like(best)
        for o in range(4 * g, 4 * g + 4):
            if o < e:
                rank = rank + jnp.where(p[o] >= p[e], 1.0, 0.0)
            elif o > e:
                rank = rank + jnp.where(p[o] > p[e], 1.0, 0.0)
        chosen.append(jnp.where((g_sel == float(g)) & (rank < 2.0), 1.0, 0.0))
        picked.append(chosen[e] * p[e])
    total = picked[0]
    for e in range(1, N_EXPERTS):
        total = total + picked[e]
    comb_ref[...] = jnp.concatenate(picked, axis=0) / total
    sel_ref[...] = jnp.concatenate(chosen, axis=0)


def _router(hb, rw_t, rb):
    tm = 1024
    out = jax.ShapeDtypeStruct((N_EXPERTS, TOKENS), F32)
    o_spec = pl.BlockSpec((N_EXPERTS, tm), lambda i: (0, i))
    return pl.pallas_call(
        _router_kernel,
        out_shape=(out, out),
        grid=(TOKENS // tm,),
        in_specs=[pl.BlockSpec((tm, D_MODEL), lambda i: (i, 0)),
                  pl.BlockSpec((N_EXPERTS, D_MODEL), lambda i: (0, 0)),
                  pl.BlockSpec((N_EXPERTS, 1), lambda i: (0, 0))],
        out_specs=(o_spec, o_spec),
        compiler_params=_params("parallel"),
        name="router",
    )(hb, rw_t, rb)


def _routing_tables(comb_t, sel_t):
    sel = sel_t > 0.5
    cnt = jnp.sum(sel, axis=1, dtype=jnp.int32)
    cnt_pad = (cnt + (MOE_TILE - 1)) // MOE_TILE * MOE_TILE
    ends = jnp.cumsum(cnt_pad)
    rank = jnp.cumsum(sel.astype(jnp.int32), axis=1) - 1
    pos = (ends - cnt_pad)[:, None] + rank
    pos_lo = jnp.min(jnp.where(sel, pos, MOE_ROWS), axis=0)
    pos_hi = jnp.max(jnp.where(sel, pos, -1), axis=0)
    w_lo = jnp.sum(jnp.where(sel & (pos == pos_lo), comb_t, 0.0), axis=0)
    w_hi = jnp.sum(jnp.where(sel & (pos == pos_hi), comb_t, 0.0), axis=0)
    w = jnp.zeros((TOKENS, LANES), F32).at[:, 0].set(w_lo).at[:, 1].set(w_hi)
    n_tiles = ends[-1] // MOE_TILE
    tile_start = jnp.arange(MOE_TILES, dtype=jnp.int32) * MOE_TILE
    tile_start = jnp.minimum(tile_start, ends[-1] - MOE_TILE)
    tile_expert = jnp.sum((ends[None, :] <= tile_start[:, None]).astype(jnp.int32), axis=1)
    return jnp.stack([pos_lo, pos_hi]).astype(jnp.int32), w, tile_expert, n_tiles.reshape(1).astype(jnp.int32)


MOE_TILE = 256
MOE_TILES = 2 * TOKENS // MOE_TILE + N_EXPERTS
MOE_ROWS = MOE_TILES * MOE_TILE
SLAB = D_MODEL // LANES


def _to_slabs(ref, x, rows):
    for j in range(SLAB):
        ref[pl.ds(j, rows, stride=SLAB), :] = x[:, j * LANES:(j + 1) * LANES]


def _from_slabs(ref, rows, pitch=SLAB):
    return jnp.concatenate([ref[pl.ds(j, rows, stride=pitch), :] for j in range(SLAB)], axis=1)


GATHER_PITCH = SLAB + 8


def _slab_rows(row, n=SLAB):
    return pl.ds(pl.multiple_of(row * n, n), n)


XSLAB = SLAB // 2
U32 = jnp.uint32


def _to_packed_slabs(ref, x, rows):
    bits = lambda t: lax.bitcast_convert_type(t.astype(BF16).astype(F32), U32)
    for j in range(XSLAB):
        hi = bits(x[:, 2 * j * LANES:(2 * j + 1) * LANES])
        lo = bits(x[:, (2 * j + 1) * LANES:(2 * j + 2) * LANES])
        ref[pl.ds(j, rows, stride=XSLAB), :] = hi | (lo >> 16)


def _from_packed_slabs(ref, rows):
    parts = []
    for j in range(XSLAB):
        u = ref[pl.ds(j, rows, stride=XSLAB), :]
        parts.append(lax.bitcast_convert_type(u & jnp.uint32(0xFFFF0000), F32).astype(BF16))
        parts.append(lax.bitcast_convert_type(u << 16, F32).astype(BF16))
    return jnp.concatenate(parts, axis=1)


def _dispatch_kernel(pos_ref, h_ref, init_ref, xs_ref, slab_ref, sem):
    del init_ref
    tm = h_ref.shape[0]
    base = pl.program_id(0) * tm
    _to_packed_slabs(slab_ref, h_ref[...], tm)

    def copy(t, which):
        return pltpu.make_async_copy(slab_ref.at[_slab_rows(t, XSLAB), :],
                                     xs_ref.at[_slab_rows(pos_ref[which, base + t], XSLAB), :], sem)

    def start(t, _):
        copy(t, 0).start(priority=0)
        copy(t, 1).start(priority=1)
        return 0

    lax.fori_loop(0, tm, start, 0, unroll=8)
    whole = pltpu.make_async_copy(slab_ref, xs_ref.at[pl.ds(0, tm * XSLAB), :], sem)
    whole.wait()
    whole.wait()


def _dispatch(pos, h):
    tm = 256
    grid_spec = pltpu.PrefetchScalarGridSpec(
        num_scalar_prefetch=1,
        grid=(TOKENS // tm,),
        in_specs=[pl.BlockSpec((tm, D_MODEL), lambda i, pos: (i, 0)),
                  pl.BlockSpec(memory_space=pl.ANY)],
        out_specs=pl.BlockSpec(memory_space=pl.ANY),
        scratch_shapes=[pltpu.VMEM((tm * XSLAB, LANES), U32), pltpu.SemaphoreType.DMA],
    )
    return pl.pallas_call(
        _dispatch_kernel,
        out_shape=jax.ShapeDtypeStruct((MOE_ROWS * XSLAB, LANES), U32),
        grid_spec=grid_spec,
        input_output_aliases={2: 0},
        compiler_params=_params("arbitrary"),
        name="moe_dispatch",
    )(pos, h, jnp.zeros((MOE_ROWS * XSLAB, LANES), U32))


def _experts_kernel(te_ref, nt_ref, xs_ref, wg_ref, wu_ref, wd_ref, ys_ref, wgb_ref, wub_ref, wdb_ref):
    k = pl.program_id(0)
    e = te_ref[k]
    e_prev = te_ref[jnp.maximum(k - 1, 0)]

    @pl.when((k == 0) | (e != e_prev))
    def _():
        wgb_ref[...] = wg_ref[0, 0].astype(BF16)
        wub_ref[...] = wu_ref[0, 0].astype(BF16)
        wdb_ref[...] = wd_ref[0, 0].astype(BF16)

    @pl.when(k < nt_ref[0])
    def _():
        x = _from_packed_slabs(xs_ref, MOE_TILE)
        hid = jax.nn.silu(_nn(x, wgb_ref[...])) * _nn(x, wub_ref[...])
        _to_slabs(ys_ref, _nn(hid.astype(BF16), wdb_ref[...]), MOE_TILE)

    @pl.when(k >= nt_ref[0])
    def _():
        ys_ref[...] = jnp.zeros(ys_ref.shape, F32)


def _experts(tile_expert, n_tiles, xs, wg, wu, wd, layer):
    w_in_spec = pl.BlockSpec((1, 1, D_MODEL, EXPERT_HIDDEN), lambda k, te, nt: (layer, te[k], 0, 0))
    grid_spec = pltpu.PrefetchScalarGridSpec(
        num_scalar_prefetch=2,
        grid=(MOE_TILES,),
        in_specs=[pl.BlockSpec((MOE_TILE * XSLAB, LANES), lambda k, te, nt: (jnp.minimum(k, nt[0] - 1), 0)),
                  w_in_spec, w_in_spec,
                  pl.BlockSpec((1, 1, EXPERT_HIDDEN, D_MODEL), lambda k, te, nt: (layer, te[k], 0, 0))],
        out_specs=pl.BlockSpec((MOE_TILE * SLAB, LANES), lambda k, te, nt: (k, 0)),
        scratch_shapes=[pltpu.VMEM((D_MODEL, EXPERT_HIDDEN), BF16), pltpu.VMEM((D_MODEL, EXPERT_HIDDEN), BF16),
                        pltpu.VMEM((EXPERT_HIDDEN, D_MODEL), BF16)],
    )
    return pl.pallas_call(
        _experts_kernel,
        out_shape=jax.ShapeDtypeStruct((MOE_ROWS * SLAB, LANES), F32),
        grid_spec=grid_spec,
        compiler_params=_params("arbitrary"),
        name="moe_experts",
    )(tile_expert, n_tiles, xs, wg, wu, wd)


def _ple_ln_kernel(pos_ref, hb_ref, h_ref, ys_ref, w_ref, p_ref, gw_ref, gb_ref, pw_ref, g_ref, b_ref,
                   h2_ref, h2b_ref, lo_ref, hi_ref, sem):
    tm = h_ref.shape[0]
    i = pl.program_id(0)
    slot = i & 1
    bufs = (lo_ref, hi_ref)

    def fetch(tile, into):
        def start(t, _):
            for which in range(2):
                pltpu.make_async_copy(ys_ref.at[_slab_rows(pos_ref[which, tile * tm + t]), :],
                                      bufs[which].at[into, pl.ds(pl.multiple_of(t * GATHER_PITCH, 8), SLAB), :],
                                      sem.at[into]).start(priority=which)
            return 0
        lax.fori_loop(0, tm, start, 0, unroll=8)

    @pl.when(i == 0)
    def _():
        fetch(0, 0)

    @pl.when(i + 1 < pl.num_programs(0))
    def _():
        fetch(i + 1, 1 - slot)

    gate = jax.nn.sigmoid(_nn(hb_ref[...], gw_ref[...]) + gb_ref[...])
    ple = gate * _nn(p_ref[...].astype(BF16), pw_ref[...])
    for which in range(2):
        pltpu.make_async_copy(ys_ref.at[pl.ds(0, tm * SLAB), :], bufs[which].at[slot, pl.ds(0, tm * SLAB), :],
                              sem.at[slot]).wait()
    w = w_ref[...]
    ffn = (w[:, 0:1] * _from_slabs(lo_ref.at[slot], tm, GATHER_PITCH)
           + w[:, 1:2] * _from_slabs(hi_ref.at[slot], tm, GATHER_PITCH))
    h2 = _layer_norm(DEEPNORM_ALPHA * h_ref[...] + ffn + ple, g_ref[...], b_ref[...])
    h2_ref[...] = h2
    h2b_ref[...] = h2.astype(BF16)


def _ple_ln(pos, hb, h, ys, w, p, layer, gw, gb, pw, g, b):
    tm = 256
    p_spec = pl.BlockSpec((tm, PLE_DIM), lambda i, pos: (layer * (TOKENS // tm) + i, 0))
    rows = lambda width: pl.BlockSpec((tm, width), lambda i, pos: (i, 0))
    full = lambda a: pl.BlockSpec(a.shape, lambda i, pos: (0, 0))
    grid_spec = pltpu.PrefetchScalarGridSpec(
        num_scalar_prefetch=1,
        grid=(TOKENS // tm,),
        in_specs=[rows(D_MODEL), rows(D_MODEL), pl.BlockSpec(memory_space=pl.ANY), rows(LANES), p_spec,
                  full(gw), full(gb), full(pw), full(g), full(b)],
        out_specs=(rows(D_MODEL), rows(D_MODEL)),
        scratch_shapes=[pltpu.VMEM((2, tm * GATHER_PITCH, LANES), F32), pltpu.VMEM((2, tm * GATHER_PITCH, LANES), F32),
                        pltpu.SemaphoreType.DMA((2,))],
    )
    return pl.pallas_call(
        _ple_ln_kernel,
        out_shape=(jax.ShapeDtypeStruct((TOKENS, D_MODEL), F32),
                   jax.ShapeDtypeStruct((TOKENS, D_MODEL), BF16)),
        grid_spec=grid_spec,
        compiler_params=_params("arbitrary"),
        name="ple_ln",
    )(pos, hb, h, ys, w, p, gw, gb, pw, g, b)


def _rope_tables(positions):
    half = ROT_DIM // 2
    inv_freq = jnp.exp(jnp.arange(half, dtype=F32) * (-2.0 * math.log(ROPE_THETA) / ROT_DIM))
    ang = positions.astype(F32)[:, :, None] * inv_freq
    cos, sin = jnp.cos(ang), jnp.sin(ang)
    zeros = jnp.zeros_like(cos)
    rest = HEAD_DIM - ROT_DIM
    pad = lambda v: jnp.broadcast_to(jnp.asarray(v, F32), cos.shape[:2] + (rest,))
    c = jnp.concatenate([cos, cos, pad(1.0)], axis=-1)
    s1 = jnp.concatenate([-sin, zeros, pad(0.0)], axis=-1)
    s2 = jnp.concatenate([zeros, sin, pad(0.0)], axis=-1)
    tile = lambda t: jnp.concatenate([t, t], axis=-1).reshape(TOKENS, LANES)
    return tile(c), tile(s1), tile(s2)


def _split_w_in(w):
    mw, nq, nkv, dw = MOBA_HEADS * HEAD_DIM, NSA_HEADS * HEAD_DIM, NSA_KV_HEADS * HEAD_DIM, DIL_HEADS * HEAD_DIM
    widths = (mw, mw, mw, nq) + (nkv,) * 6 + (NSA_HEADS * 3, dw, dw, dw)
    offs = np.concatenate([[0], np.cumsum(widths)])
    qa, ka, va, qb, kbc, vbc, kbs, vbs, kbw, vbw, gb, qc, kc, vc = (
        w[:, int(offs[i]):int(offs[i + 1])] for i in range(len(widths)))

    def dup(t):
        t = t.reshape(D_MODEL, NSA_KV_HEADS, 1, HEAD_DIM)
        return jnp.broadcast_to(t, (D_MODEL, NSA_KV_HEADS, 2, HEAD_DIM)).reshape(D_MODEL, NSA_KV_HEADS * LANES)

    zpad = lambda n: jnp.zeros((D_MODEL, n * LANES), w.dtype)
    w_rot = jnp.concatenate([qa * Q_SCALE, ka, qb * Q_SCALE, dup(kbc), dup(kbs), dup(kbw), zpad(1)], axis=1)
    w_pl = jnp.concatenate([va, dup(vbc), dup(vbs), dup(vbw), zpad(3)], axis=1)
    gpad = jnp.zeros((D_MODEL, NSA_KV_HEADS, LANES - 12), w.dtype)
    w_gl = jnp.concatenate([gb.reshape(D_MODEL, NSA_KV_HEADS, 12), gpad], axis=-1).reshape(D_MODEL, -1)
    w_dil_rot = jnp.concatenate([qc * Q_SCALE, kc], axis=1)
    return tuple(t.astype(BF16) for t in (w_rot, w_pl, w_gl, w_dil_rot, vc))


def _overlap_table():
    starts = np.arange(N_CMP) * CMP_STRIDE
    slc = np.arange(N_SLC) * SLC_BLOCK
    ov = ((starts[:, None] < slc[None, :] + SLC_BLOCK) & (starts[:, None] + CMP_LEN > slc[None, :]))
    ovt = np.zeros((N_SLC, N_CMP_PAD), np.float32)
    ovt[:, :N_CMP] = ov.T
    return jnp.asarray(ovt, BF16)


def _cmp_chunks(z, base):
    nblk = z.shape[-1] // LANES
    t = z.reshape(BATCH, SEQ // CMP_STRIDE, CMP_STRIDE, nblk, LANES)[:, :, :, base:base + NSA_KV_HEADS, :HEAD_DIM]
    return t.transpose(0, 3, 1, 2, 4).reshape(BATCH, NSA_KV_HEADS, SEQ // CMP_STRIDE, CMP_STRIDE * HEAD_DIM)


def kernel(x, p, positions, ln_in_g, ln_in_b, w_in, w_out, nsa_ck1, nsa_ck2, nsa_pe_k, nsa_cv1, nsa_cv2, nsa_pe_v, ln1_g, ln1_b, router_w, router_b, w_gate, w_up, w_down, ple_proj, ple_gate_w, ple_gate_b, ln2_g, ln2_b):
    rope = _rope_tables(positions)
    ovt = _overlap_table()
    rw_t = router_w.T.astype(BF16)
    rb = router_b.reshape(N_EXPERTS, 1).astype(F32)
    chunk_w = CMP_STRIDE * HEAD_DIM
    vec = lambda v: v.reshape(1, -1)
    seq3 = lambda t: t.reshape(BATCH, SEQ, t.shape[-1])
    flat = lambda t: t.reshape(TOKENS, t.shape[-1])

    h, hb = _ln_in(x.reshape(TOKENS, D_MODEL), ln_in_g, ln_in_b)
    for i in range(DEPTH):
        w_rot, w_pl, w_gl, w_dil_rot, w_dil_pl = _split_w_in(w_in[i])
        z_rot = seq3(_project(hb, w_rot, BF16, 768, rope=rope))
        z_pl = seq3(_project(hb, w_pl, BF16, 1024))
        gate_logits = _project(hb, w_gl, F32, NSA_KV_HEADS * LANES)
        zd_rot = seq3(_project(hb, w_dil_rot, F32, 768, rope=rope))
        zd_pl = seq3(_project(hb, w_dil_pl, F32, 768))

        o_a = _moba(z_rot, z_pl)

        dup2 = lambda w2: jnp.concatenate([w2, w2], axis=1).astype(BF16)
        k_cmp, v_cmp = _compress(
            _cmp_chunks(z_rot, ROT_NKC), _cmp_chunks(z_pl, PL_NVC),
            nsa_pe_k[i].reshape(2, chunk_w), nsa_pe_v[i].reshape(2, chunk_w),
            nsa_ck1[i].reshape(2, chunk_w, CMP_HIDDEN).astype(BF16), dup2(nsa_ck2[i]),
            nsa_cv1[i].reshape(2, chunk_w, CMP_HIDDEN).astype(BF16), dup2(nsa_cv2[i]))
        o_b = _nsa(z_rot, z_pl, k_cmp, v_cmp, gate_logits, ovt)

        o_c = _dilated(zd_rot, zd_pl)

        wo = w_out[i].astype(BF16)
        a_w, b_w = MOBA_HEADS * HEAD_DIM, NSA_HEADS * HEAD_DIM
        h, hb = _out_proj(flat(o_a), flat(o_b), flat(o_c), h,
                          wo[:a_w], wo[a_w:a_w + b_w], wo[a_w + b_w:], vec(ln1_g[i]), vec(ln1_b[i]))

        pos, w_tok, tile_expert, n_tiles = _routing_tables(*_router(hb, rw_t, rb))
        xs = _dispatch(pos, h)
        ys = _experts(tile_expert, n_tiles, xs, w_gate, w_up, w_down, i)
        h, hb = _ple_ln(pos, hb, h, ys, w_tok, p.reshape(DEPTH * TOKENS, PLE_DIM), i, ple_gate_w[i].astype(BF16),
                        vec(ple_gate_b[i]), ple_proj[i].astype(BF16), vec(ln2_g[i]), vec(ln2_b[i]))
    return h.reshape(BATCH, SEQ, D_MODEL)
```

```python
import functools
import math

import numpy as np
import jax
import jax.numpy as jnp
from jax import lax
from jax.experimental import pallas as pl
from jax.experimental.pallas import tpu as pltpu

F32 = jnp.float32
BF16 = jnp.bfloat16

D_MODEL = 2048
BATCH = 2
SEQ = 4096
DEPTH = 4
TOKENS = BATCH * SEQ
HEAD_DIM = 64
ROT_DIM = HEAD_DIM // 4
ROPE_THETA = 500000.0
NEG = -1e30
FORCE = 1e30
LN_EPS = 1e-5
SCALE = HEAD_DIM ** -0.5
LOG2_E = math.log2(math.e)
Q_SCALE = SCALE * LOG2_E

MOBA_HEADS = 8
MOBA_BLOCK = 256
MOBA_TOPK = 3
MOBA_NB = SEQ // MOBA_BLOCK

NSA_HEADS = 12
NSA_KV_HEADS = 3
CMP_LEN = 32
CMP_STRIDE = 16
CMP_HIDDEN = 128
N_CMP = (SEQ - CMP_LEN) // CMP_STRIDE + 1
N_CMP_PAD = 256
SLC_BLOCK = 64
SLC_TOPK = 16
SLC_LOCAL = 2
N_SLC = SEQ // SLC_BLOCK
NSA_WINDOW = 512

DIL_CONFIGS = ((128, 1), (512, 4), (2048, 16))
DIL_HEADS_PER_GROUP = 4
DIL_HEADS = DIL_HEADS_PER_GROUP * len(DIL_CONFIGS)

N_EXPERTS = 16
N_GROUPS = 4
EXPERTS_PER_GROUP = 4
EXPERT_HIDDEN = D_MODEL // 4
PLE_DIM = 256

DEEPNORM_ALPHA = (2 * DEPTH) ** 0.25

LANES = 128
VMEM_LIMIT = 56 * 1024 * 1024

ROT_MQ, ROT_MK, ROT_NQ, ROT_NKC, ROT_NKS, ROT_NKW = 0, 4, 8, 14, 17, 20
ROT_BLOCKS = 24
PL_MV, PL_NVC, PL_NVS, PL_NVW = 0, 4, 7, 10
PL_BLOCKS = 16
DIL_BLOCKS = DIL_HEADS // 2

NT_DIMS = (((1,), (1,)), ((), ()))


def _nt(a, b):
    return lax.dot_general(a, b, NT_DIMS, preferred_element_type=F32)


def _nn(a, b):
    return jnp.dot(a, b, preferred_element_type=F32)


def _params(*sem):
    return pltpu.CompilerParams(dimension_semantics=sem, vmem_limit_bytes=VMEM_LIMIT)


def _layer_norm(y, g, b):
    mu = jnp.mean(y, axis=-1, keepdims=True)
    yc = y - mu
    var = jnp.mean(yc * yc, axis=-1, keepdims=True)
    return yc * lax.rsqrt(var + LN_EPS) * g + b


def _ln_kernel(x_ref, g_ref, b_ref, h_ref, hb_ref):
    h = _layer_norm(x_ref[...], g_ref[...], b_ref[...])
    h_ref[...] = h
    hb_ref[...] = h.astype(BF16)


def _ln_in(x, g, b):
    tm = 512
    row = pl.BlockSpec((tm, D_MODEL), lambda i: (i, 0))
    vec = pl.BlockSpec((1, D_MODEL), lambda i: (0, 0))
    return pl.pallas_call(
        _ln_kernel,
        out_shape=(jax.ShapeDtypeStruct((TOKENS, D_MODEL), F32),
                   jax.ShapeDtypeStruct((TOKENS, D_MODEL), BF16)),
        grid=(TOKENS // tm,),
        in_specs=[row, vec, vec],
        out_specs=(row, row),
        compiler_params=_params("parallel"),
        name="ln_in",
    )(x, g.reshape(1, -1), b.reshape(1, -1))


def _proj_kernel(x_ref, w_ref, o_ref):
    o_ref[...] = _nn(x_ref[...], w_ref[...]).astype(o_ref.dtype)


def _proj_rot_kernel(x_ref, w_ref, c_ref, s1_ref, s2_ref, o_ref):
    x = x_ref[...]
    c, s1, s2 = c_ref[...], s1_ref[...], s2_ref[...]
    half = ROT_DIM // 2
    for j0 in range(0, o_ref.shape[1], 2 * LANES):
        z = _nn(x, w_ref[:, j0:j0 + 2 * LANES])
        for j in range(j0, j0 + 2 * LANES, LANES):
            zc = z[:, j - j0:j - j0 + LANES]
            r = zc * c + pltpu.roll(zc, LANES - half, 1) * s1 + pltpu.roll(zc, half, 1) * s2
            o_ref[:, j:j + LANES] = r.astype(o_ref.dtype)


def _project(hb, w, out_dtype, tn, rope=None):
    tm = 1024
    n = w.shape[1]
    x_spec = pl.BlockSpec((tm, D_MODEL), lambda i, j: (i, 0))
    w_spec = pl.BlockSpec((D_MODEL, tn), lambda i, j: (0, j))
    o_spec = pl.BlockSpec((tm, tn), lambda i, j: (i, j))
    if rope is None:
        kern, extra, extra_specs = _proj_kernel, (), []
    else:
        t_spec = pl.BlockSpec((tm, LANES), lambda i, j: (i, 0))
        kern, extra, extra_specs = _proj_rot_kernel, rope, [t_spec] * 3
    return pl.pallas_call(
        kern,
        out_shape=jax.ShapeDtypeStruct((TOKENS, n), out_dtype),
        grid=(TOKENS // tm, n // tn),
        in_specs=[x_spec, w_spec] + extra_specs,
        out_specs=o_spec,
        compiler_params=_params("parallel", "arbitrary"),
        name="in_proj_rot" if rope is not None else "in_proj",
    )(hb, w, *extra)


def _stack_heads(*q_blocks):
    parts = []
    for q in q_blocks:
        lane = lax.broadcasted_iota(jnp.int32, q.shape, 1)
        zero = jnp.zeros_like(q)
        parts += [jnp.where(lane < HEAD_DIM, q, zero), jnp.where(lane >= HEAD_DIM, q, zero)]
    return jnp.concatenate(parts, axis=0)


def _merge_pair_t(lo, hi):
    sub = lax.broadcasted_iota(jnp.int32, lo.shape, 0)
    return jnp.where(sub < HEAD_DIM, lo, hi)


def _band_bias_t(nk, qc, offset, n_back):
    key = lax.broadcasted_iota(jnp.int32, (nk, qc), 0)
    qry = lax.broadcasted_iota(jnp.int32, (nk, qc), 1)
    diff = offset + qry - key
    return jnp.where((diff >= 0) & (diff <= n_back), 0.0, NEG)


def _tile_lanes(x, n):
    return jnp.concatenate([x] * n, axis=1)


def _transpose_bf16(v):
    return jnp.transpose(v.astype(F32)).astype(BF16)


def _tree(x, op):
    n = x.shape[0]
    if n == 8:
        return x
    if n % 16 == 0:
        return op(_tree(x[:n // 2], op), _tree(x[n // 2:], op))
    acc = x[:8]
    for i in range(1, n // 8):
        acc = op(acc, x[8 * i:8 * i + 8])
    return acc


def _reduce_keys(x, op, final):
    return final(_tree(x, op), axis=0, keepdims=True)


VT_ROWS = LANES + 16


def _transpose_aug(v):
    vt = jnp.transpose(v.astype(F32))
    sub = lax.broadcasted_iota(jnp.int32, (VT_ROWS - LANES, v.shape[0]), 0)
    return jnp.concatenate([vt, jnp.where(sub == 0, 1.0, 0.0)], axis=0).astype(BF16)


def _probs(s_t, m):
    return jnp.exp2((s_t - m).astype(BF16))


def _normalise(acc):
    l = acc[LANES:LANES + 1]
    return acc[:LANES] / l, l


def _softmax_block_t(s_t, pv):
    m = _reduce_keys(s_t, jnp.maximum, jnp.max)
    out, l = _normalise(pv(_probs(s_t, m)))
    return out, m + jnp.log(l) * LOG2_E


def _online_step_t(carry, s_t, m_t, pv):
    m, acc = carry
    m_new = jnp.maximum(m, m_t)
    acc = jnp.exp2(m - m_new) * acc + pv(_probs(s_t, m_new))
    return m_new, acc


def _flash_tiles(n_tiles, last_tile, init, scores, pv_of):
    strips = range(len(init))

    def produce(t):
        s = tuple(scores(t))
        return s, tuple(_reduce_keys(s_i, jnp.maximum, jnp.max) for s_i in s)

    def body(t, carry):
        state, s_t, m_t = carry
        s_next, m_next = produce(jnp.minimum(t + 1, last_tile))
        pv = pv_of(t)
        return tuple(_online_step_t(state[i], s_t[i], m_t[i], pv) for i in strips), s_next, m_next

    state, _, _ = lax.fori_loop(0, n_tiles, body, (tuple(init),) + produce(0))
    return state


def _pv_tiles(vt_ref, first, n, rows):
    def pv(p):
        acc = _nn(vt_ref[first], p[:rows])
        for j in range(1, n):
            acc = acc + _nn(vt_ref[first + j], p[j * rows:(j + 1) * rows])
        return acc
    return pv


def _online_init_t(r):
    return (jnp.full((1, r), NEG, F32), jnp.zeros((VT_ROWS, r), F32))


def _rank_rows(g, n_rows):
    sub = lax.broadcasted_iota(jnp.int32, (8, g.shape[1]), 0)
    rank = jnp.zeros(g.shape, F32)
    for m in range(n_rows):
        gm = g[m:m + 1, :]
        b = m // 8 * 8
        mid = g[b:b + 8]
        parts = [jnp.where(gm > mid, 1.0, jnp.where((gm == mid) & (sub > m - b), 1.0, 0.0))]
        if b > 0:
            parts.insert(0, jnp.where(gm > g[:b], 1.0, 0.0))
        if b + 8 < n_rows:
            parts.append(jnp.where(gm >= g[b + 8:], 1.0, 0.0))
        rank = rank + jnp.concatenate(parts, axis=0)
    return rank


MOBA_QC = 256
MOBA_KT = 2 * MOBA_BLOCK


def _moba_kernel(q_ref, k_ref, v_ref, o_ref, kmean_ref, vt_ref, bias_ref):
    c = pl.program_id(2)
    qc = MOBA_QC
    r = 2 * qc

    @pl.when(c == 0)
    def _():
        row = lax.broadcasted_iota(jnp.int32, (MOBA_NB, SEQ), 0)
        col = lax.broadcasted_iota(jnp.int32, (MOBA_NB, SEQ), 1)
        avg = jnp.where((col >> 8) == row, 1.0 / MOBA_BLOCK, 0.0).astype(BF16)
        kmean_ref[...] = _nn(avg, k_ref[0])
        for t in range(MOBA_NB):
            vt_ref[t] = _transpose_aug(v_ref[0, t * MOBA_BLOCK:(t + 1) * MOBA_BLOCK, :])

    qs = _stack_heads(q_ref[0])

    ks = pl.multiple_of(c * MOBA_BLOCK, MOBA_BLOCK)
    heads = [qs[:qc], qs[qc:]]
    causal = _band_bias_t(MOBA_BLOCK, qc, 0, MOBA_BLOCK)
    s_own = [_nt(k_ref[0, pl.ds(ks, MOBA_BLOCK), :], q_h) + causal for q_h in heads]
    state = [_online_step_t(_online_init_t(qc), s_h, _reduce_keys(s_h, jnp.maximum, jnp.max),
                            _pv_tiles(vt_ref, c, 1, MOBA_BLOCK)) for s_h in s_own]

    gate = _nt(kmean_ref[...].astype(BF16), qs)
    blk = lax.broadcasted_iota(jnp.int32, gate.shape, 0)
    past = blk < c
    rank = _rank_rows(jnp.where(past, gate, NEG), MOBA_NB)
    bias_ref[...] = jnp.where(past & (rank < MOBA_TOPK), 0.0, NEG)

    per_tile = MOBA_KT // MOBA_BLOCK

    def scores(t):
        ks = pl.multiple_of(t * MOBA_KT, MOBA_KT)
        kt = k_ref[0, pl.ds(ks, MOBA_KT), :]
        rows = [bias_ref[pl.ds(t * per_tile + j, 1), :] for j in range(per_tile)]
        out = []
        for h in range(2):
            blocks = [jnp.broadcast_to(row[:, h * qc:(h + 1) * qc], (MOBA_BLOCK, qc)) for row in rows]
            out.append(_nt(kt, heads[h]) + jnp.concatenate(blocks, axis=0))
        return out

    state = _flash_tiles((c + per_tile - 1) // per_tile, SEQ // MOBA_KT - 1, state, scores,
                         lambda t: _pv_tiles(vt_ref, t * per_tile, per_tile, MOBA_BLOCK))
    o_lo, o_hi = (_normalise(acc)[0] for _, acc in state)
    o_ref[0] = jnp.transpose(_merge_pair_t(o_lo, o_hi)).astype(o_ref.dtype)


def _moba(z_rot, z_pl):
    qc = MOBA_QC
    grid = (BATCH, MOBA_HEADS // 2, SEQ // qc)
    return pl.pallas_call(
        _moba_kernel,
        out_shape=jax.ShapeDtypeStruct((BATCH, SEQ, MOBA_HEADS * HEAD_DIM), BF16),
        grid=grid,
        in_specs=[
            pl.BlockSpec((1, qc, LANES), lambda b, p, c: (b, c, ROT_MQ + p)),
            pl.BlockSpec((1, SEQ, LANES), lambda b, p, c: (b, 0, ROT_MK + p)),
            pl.BlockSpec((1, SEQ, LANES), lambda b, p, c: (b, 0, PL_MV + p)),
        ],
        out_specs=pl.BlockSpec((1, qc, LANES), lambda b, p, c: (b, c, p)),
        scratch_shapes=[pltpu.VMEM((MOBA_NB, LANES), F32),
                        pltpu.VMEM((MOBA_NB, VT_ROWS, MOBA_BLOCK), BF16),
                        pltpu.VMEM((MOBA_NB, 2 * qc), F32)],
        compiler_params=_params("parallel", "parallel", "arbitrary"),
        name="moba",
    )(z_rot, z_rot, z_pl)


def _compress_one(x_ref, pe_ref, w1_ref, w2_ref, o_ref):
    x = x_ref[0, 0].astype(F32)
    top = (x + pe_ref[0:1, :]).astype(BF16)
    bot = (x + pe_ref[1:2, :]).astype(BF16)
    a = _nn(top, w1_ref[0])
    bm = _nn(bot, w1_ref[1])
    pre = a + pltpu.roll(bm, N_CMP_PAD - 1, 0)
    hid = jax.nn.gelu(pre)
    out = _nn(hid.astype(BF16), w2_ref[...])
    row = lax.broadcasted_iota(jnp.int32, out.shape, 0)
    o_ref[0, 0] = jnp.where(row < N_CMP, out, 0.0).astype(o_ref.dtype)


def _compress_kernel(xk_ref, xv_ref, pk_ref, pv_ref, k1_ref, k2_ref, v1_ref, v2_ref, ok_ref, ov_ref):
    _compress_one(xk_ref, pk_ref, k1_ref, k2_ref, ok_ref)
    _compress_one(xv_ref, pv_ref, v1_ref, v2_ref, ov_ref)


def _compress(xk, xv, pk, pv, k1, k2, v1, v2):
    chunk_w = CMP_STRIDE * HEAD_DIM
    x_spec = pl.BlockSpec((1, 1, N_CMP_PAD, chunk_w), lambda b, j: (b, j, 0, 0))
    pe_spec = pl.BlockSpec((2, chunk_w), lambda b, j: (0, 0))
    w1_spec = pl.BlockSpec((2, chunk_w, CMP_HIDDEN), lambda b, j: (0, 0, 0))
    w2_spec = pl.BlockSpec((CMP_HIDDEN, LANES), lambda b, j: (0, 0))
    o_spec = pl.BlockSpec((1, 1, N_CMP_PAD, LANES), lambda b, j: (b, j, 0, 0))
    o_shape = jax.ShapeDtypeStruct((BATCH, NSA_KV_HEADS, N_CMP_PAD, LANES), BF16)
    return pl.pallas_call(
        _compress_kernel,
        out_shape=(o_shape, o_shape),
        grid=(BATCH, NSA_KV_HEADS),
        in_specs=[x_spec, x_spec, pe_spec, pe_spec, w1_spec, w2_spec, w1_spec, w2_spec],
        out_specs=(o_spec, o_spec),
        compiler_params=_params("parallel", "parallel"),
        name="nsa_compress",
    )(xk, xv, pk, pv, k1, k2, v1, v2)


NSA_QC = 512
NSA_KT = 512
NSA_G = NSA_HEADS // NSA_KV_HEADS
NSA_WIN_TILES = NSA_WINDOW // NSA_QC + 1


def _nsa_kernel(qa_ref, qb_ref, kc_ref, vc_ref, ks_ref, vs_ref, kw_ref, vw_ref, gl_ref, ovt_ref,
                o_ref, vct_ref, vst_ref, vwt_ref, bias_ref):
    c = pl.program_id(2)
    qc = NSA_QC
    q0 = c * qc
    lanes_of = lambda t, i: t[:, i * qc:(i + 1) * qc]

    @pl.when(c == 0)
    def _():
        vct_ref[...] = _transpose_bf16(vc_ref[0, 0])
        for t in range(SEQ // NSA_KT):
            vst_ref[t] = _transpose_aug(vs_ref[0, t * NSA_KT:(t + 1) * NSA_KT, :])
        for t in range(SEQ // qc):
            vwt_ref[t] = _transpose_aug(vw_ref[0, t * qc:(t + 1) * qc, :])

    qs = _stack_heads(qa_ref[0], qb_ref[0])

    sc_t = _nt(kc_ref[0, 0], qs)
    t0 = jnp.maximum(c - NSA_WINDOW // qc, 0)
    start = pl.multiple_of(t0 * qc, qc)
    sw_t = _nt(kw_ref[0, pl.ds(start, NSA_WIN_TILES * qc), :], qs)

    n_idx = lax.broadcasted_iota(jnp.int32, (N_CMP_PAD, qc), 0)
    q_idx = lax.broadcasted_iota(jnp.int32, (N_CMP_PAD, qc), 1)
    ok = (n_idx * CMP_STRIDE + (CMP_LEN - 1)) <= (q0 + q_idx)
    p_heads = []
    for i in range(NSA_G):
        s_i = jnp.where(ok, lanes_of(sc_t, i), NEG)
        e_i = jnp.where(ok, jnp.exp2(s_i - _reduce_keys(s_i, jnp.maximum, jnp.max)), 0.0)
        l_i = _reduce_keys(e_i, jnp.add, jnp.sum)
        p_heads.append((e_i / jnp.where(l_i > 0.0, l_i, 1.0)).astype(BF16))
    p_ct = jnp.concatenate(p_heads, axis=1)
    imp4 = _nn(ovt_ref[...], p_ct)
    ocmp_t = _nn(vct_ref[...], p_ct)

    band = _band_bias_t(NSA_WIN_TILES * qc, qc, q0 - start, NSA_WINDOW - 1)
    owin_t, _ = _softmax_block_t(sw_t + _tile_lanes(band, NSA_G), _pv_tiles(vwt_ref, t0, NSA_WIN_TILES, qc))
    gate_t = jnp.transpose(jax.nn.sigmoid(gl_ref[...]))
    gate = lambda i, r: gate_t[3 * i + r:3 * i + r + 1, :]
    partial_out = [gate(i, 0) * lanes_of(ocmp_t, i) + gate(i, 2) * lanes_of(owin_t, i) for i in range(NSA_G)]

    imp = lanes_of(imp4, 0) + lanes_of(imp4, 1) + lanes_of(imp4, 2) + lanes_of(imp4, 3)
    blk = lax.broadcasted_iota(jnp.int32, imp.shape, 0)
    cur = (q0 + lax.broadcasted_iota(jnp.int32, imp.shape, 1)) >> 6
    valid = blk <= cur
    forced = valid & ((blk == 0) | (blk > cur - SLC_LOCAL))
    rank = _rank_rows(jnp.where(forced, FORCE, jnp.where(valid, imp, NEG)), N_SLC)
    bias_ref[...] = jnp.where(valid & (rank < SLC_TOPK), 0.0, NEG)

    key_row = lax.broadcasted_iota(jnp.int32, (NSA_KT, qc), 0)
    qpos = lax.broadcasted_iota(jnp.int32, (NSA_KT, qc), 1) + q0
    per_tile = NSA_KT // SLC_BLOCK

    def scores(t):
        ks0 = pl.multiple_of(t * NSA_KT, NSA_KT)
        blocks = [jnp.broadcast_to(bias_ref[pl.ds(t * per_tile + j, 1), :], (SLC_BLOCK, qc))
                  for j in range(per_tile)]
        bias = jnp.where(key_row + ks0 <= qpos, jnp.concatenate(blocks, axis=0), NEG)
        kt = ks_ref[0, pl.ds(ks0, NSA_KT), :]
        return [_nt(kt, qs[i * qc:(i + 1) * qc]) + bias for i in range(NSA_G)]

    slc = _flash_tiles(c // (NSA_KT // qc) + 1, SEQ // NSA_KT - 1, [_online_init_t(qc)] * NSA_G, scores,
                       lambda t: _pv_tiles(vst_ref, t, 1, NSA_KT))
    outs = [partial_out[i] + gate(i, 1) * _normalise(slc[i][1])[0] for i in range(NSA_G)]
    o_ref[0, :, 0:LANES] = jnp.transpose(_merge_pair_t(outs[0], outs[1])).astype(o_ref.dtype)
    o_ref[0, :, LANES:2 * LANES] = jnp.transpose(_merge_pair_t(outs[2], outs[3])).astype(o_ref.dtype)


def _nsa(z_rot, z_pl, k_cmp, v_cmp, gate_logits, ovt):
    qc = NSA_QC
    seq_spec = lambda base: pl.BlockSpec((1, SEQ, LANES), lambda b, j, c: (b, 0, base + j))
    cmp_spec = pl.BlockSpec((1, 1, N_CMP_PAD, LANES), lambda b, j, c: (b, j, 0, 0))
    return pl.pallas_call(
        _nsa_kernel,
        out_shape=jax.ShapeDtypeStruct((BATCH, SEQ, NSA_HEADS * HEAD_DIM), BF16),
        grid=(BATCH, NSA_KV_HEADS, SEQ // qc),
        in_specs=[
            pl.BlockSpec((1, qc, LANES), lambda b, j, c: (b, c, ROT_NQ + 2 * j)),
            pl.BlockSpec((1, qc, LANES), lambda b, j, c: (b, c, ROT_NQ + 2 * j + 1)),
            cmp_spec, cmp_spec,
            seq_spec(ROT_NKS), seq_spec(PL_NVS), seq_spec(ROT_NKW), seq_spec(PL_NVW),
            pl.BlockSpec((qc, LANES), lambda b, j, c: (b * (SEQ // qc) + c, j)),
            pl.BlockSpec(ovt.shape, lambda b, j, c: (0, 0)),
        ],
        out_specs=pl.BlockSpec((1, qc, 2 * LANES), lambda b, j, c: (b, c, j)),
        scratch_shapes=[pltpu.VMEM((LANES, N_CMP_PAD), BF16),
                        pltpu.VMEM((SEQ // NSA_KT, VT_ROWS, NSA_KT), BF16),
                        pltpu.VMEM((SEQ // qc, VT_ROWS, qc), BF16),
                        pltpu.VMEM((N_SLC, qc), F32)],
        compiler_params=_params("parallel", "parallel", "arbitrary"),
        name="nsa",
    )(z_rot, z_rot, k_cmp, v_cmp, z_rot, z_pl, z_rot, z_pl, gate_logits, ovt)


DIL_QC = 128
DIL_STEPS = SEQ // DIL_QC
DIL_UNROLL = 8


def _dil_group(q_ref, k_ref, v_ref, og_ref, lg_ref, gi):
    window, dil = DIL_CONFIGS[gi]
    qc = DIL_QC
    m = SEQ // dil
    n_back = window // dil
    nk = min(m, qc + -(-n_back // qc) * qc)
    chunks = m // qc

    def rows(first, n):
        return pl.ds(first, n) if dil == 1 else pl.ds(first, n, stride=dil)

    def place(idx):
        r = idx // chunks
        q0 = (idx % chunks) * qc
        start = jnp.maximum(q0 - (nk - qc), 0)
        return rows(r + dil * q0, qc), rows(r + dil * start, nk), q0 - start

    def body(i, _):
        at = [place(i * DIL_UNROLL + u) for u in range(DIL_UNROLL)]
        s = [_nt(k_ref[0, k_rows, :].astype(BF16), _stack_heads(q_ref[0, q_rows, :].astype(BF16)))
             for q_rows, k_rows, _ in at]
        m, p = [], []
        for u, (_, _, off) in enumerate(at):
            s_u = s[u] + _tile_lanes(_band_bias_t(nk, qc, off, n_back), 2)
            m.append(_reduce_keys(s_u, jnp.maximum, jnp.max))
            p.append(_probs(s_u, m[u]))
        acc = [_nn(_transpose_aug(v_ref[0, k_rows, :]), p[u]) for u, (_, k_rows, _) in enumerate(at)]
        for u, (q_rows, _, _) in enumerate(at):
            o_t, l = _normalise(acc[u])
            lse_b = jnp.broadcast_to(m[u] + jnp.log(l) * LOG2_E, (LANES, 2 * qc))
            og_ref[gi, q_rows, :] = jnp.transpose(_merge_pair_t(o_t[:, :qc], o_t[:, qc:]))
            lg_ref[gi, q_rows, :] = jnp.transpose(_merge_pair_t(lse_b[:, :qc], lse_b[:, qc:]))
        return 0

    lax.fori_loop(0, DIL_STEPS // DIL_UNROLL, body, 0)


def _dil_kernel(q_ref, k_ref, v_ref, o_ref, og_ref, lg_ref):
    g = pl.program_id(2)
    n_groups = len(DIL_CONFIGS)
    for gi in range(n_groups):
        pl.when(g == gi)(functools.partial(_dil_group, q_ref, k_ref, v_ref, og_ref, lg_ref, gi))

    @pl.when(g == n_groups - 1)
    def _():
        rows = 512

        def body(i, _):
            sl = pl.ds(pl.multiple_of(i * rows, rows), rows)
            l0, l1, l2 = lg_ref[0, sl, :], lg_ref[1, sl, :], lg_ref[2, sl, :]
            mx = jnp.maximum(jnp.maximum(l0, l1), l2)
            e0, e1, e2 = jnp.exp2(l0 - mx), jnp.exp2(l1 - mx), jnp.exp2(l2 - mx)
            den = e0 + e1 + e2
            out = (e0 / den) * og_ref[0, sl, :] + (e1 / den) * og_ref[1, sl, :] + (e2 / den) * og_ref[2, sl, :]
            o_ref[0, sl, :] = out.astype(o_ref.dtype)
            return 0

        lax.fori_loop(0, SEQ // rows, body, 0)


def _dilated(zd_rot, zd_pl):
    n_groups = len(DIL_CONFIGS)
    width = DIL_HEADS_PER_GROUP * HEAD_DIM
    col = lambda base: (lambda b, p, g: (b, 0, base + 2 * g + p))
    blk = lambda base: pl.BlockSpec((1, SEQ, LANES), col(base))
    return pl.pallas_call(
        _dil_kernel,
        out_shape=jax.ShapeDtypeStruct((BATCH, SEQ, width), BF16),
        grid=(BATCH, 2, n_groups),
        in_specs=[blk(0), blk(DIL_BLOCKS), blk(0)],
        out_specs=pl.BlockSpec((1, SEQ, LANES), lambda b, p, g: (b, 0, p)),
        scratch_shapes=[pltpu.VMEM((n_groups, SEQ, LANES), F32), pltpu.VMEM((n_groups, SEQ, LANES), F32)],
        compiler_params=_params("parallel", "parallel", "arbitrary"),
        name="dilated",
    )(zd_rot, zd_rot, zd_pl)


def _out_proj_kernel(oa_ref, ob_ref, oc_ref, h_ref, wa_ref, wb_ref, wc_ref, g_ref, b_ref, h1_ref, h1b_ref):
    y = _nn(oa_ref[...], wa_ref[...]) + _nn(ob_ref[...], wb_ref[...]) + _nn(oc_ref[...], wc_ref[...])
    h1 = _layer_norm(DEEPNORM_ALPHA * h_ref[...] + y, g_ref[...], b_ref[...])
    h1_ref[...] = h1
    h1b_ref[...] = h1.astype(BF16)


def _out_proj(oa, ob, oc, h, wa, wb, wc, g, b):
    tm = 512
    rows = lambda w: pl.BlockSpec((tm, w), lambda i: (i, 0))
    full = lambda a: pl.BlockSpec(a.shape, lambda i: (0, 0))
    return pl.pallas_call(
        _out_proj_kernel,
        out_shape=(jax.ShapeDtypeStruct((TOKENS, D_MODEL), F32),
                   jax.ShapeDtypeStruct((TOKENS, D_MODEL), BF16)),
        grid=(TOKENS // tm,),
        in_specs=[rows(oa.shape[1]), rows(ob.shape[1]), rows(oc.shape[1]), rows(D_MODEL),
                  full(wa), full(wb), full(wc), full(g), full(b)],
        out_specs=(rows(D_MODEL), rows(D_MODEL)),
        compiler_params=_params("parallel"),
        name="out_proj_ln",
    )(oa, ob, oc, h, wa, wb, wc, g, b)


def _router_kernel(hb_ref, rw_ref, rb_ref, comb_ref, sel_ref):
    logits = _nt(rw_ref[...], hb_ref[...]) + rb_ref[...]
    mx = jnp.max(logits, axis=0, keepdims=True)
    ex = jnp.exp(logits - mx)
    probs = ex / jnp.sum(ex, axis=0, keepdims=True)
    p = [probs[e:e + 1, :] for e in range(N_EXPERTS)]
    best, g_sel = None, None
    for g in range(N_GROUPS):
        a, b, c, d = p[4 * g:4 * g + 4]
        hi1, lo1, hi2, lo2 = jnp.maximum(a, b), jnp.minimum(a, b), jnp.maximum(c, d), jnp.minimum(c, d)
        top2 = jnp.maximum(hi1, hi2) + jnp.maximum(jnp.minimum(hi1, hi2), jnp.maximum(lo1, lo2))
        if g == 0:
            best, g_sel = top2, jnp.zeros_like(top2)
        else:
            better = top2 > best
            best = jnp.where(better, top2, best)
            g_sel = jnp.where(better, float(g), g_sel)
    chosen, picked = [], []
    for e in range(N_EXPERTS):
        g = e // EXPERTS_PER_GROUP
        rank = jnp.zeros_like(best)
        for o in range(4 * g, 4 * g + 4):
            if o < e:
                rank = rank + jnp.where(p[o] >= p[e], 1.0, 0.0)
            elif o > e:
                rank = rank + jnp.where(p[o] > p[e], 1.0, 0.0)
        chosen.append(jnp.where((g_sel == float(g)) & (rank < 2.0), 1.0, 0.0))
        picked.append(chosen[e] * p[e])
    total = picked[0]
    for e in range(1, N_EXPERTS):
        total = total + picked[e]
    comb_ref[...] = jnp.concatenate(picked, axis=0) / total
    sel_ref[...] = jnp.concatenate(chosen, axis=0)


def _router(hb, rw_t, rb):
    tm = 1024
    out = jax.ShapeDtypeStruct((N_EXPERTS, TOKENS), F32)
    o_spec = pl.BlockSpec((N_EXPERTS, tm), lambda i: (0, i))
    return pl.pallas_call(
        _router_kernel,
        out_shape=(out, out),
        grid=(TOKENS // tm,),
        in_specs=[pl.BlockSpec((tm, D_MODEL), lambda i: (i, 0)),
                  pl.BlockSpec((N_EXPERTS, D_MODEL), lambda i: (0, 0)),
                  pl.BlockSpec((N_EXPERTS, 1), lambda i: (0, 0))],
        out_specs=(o_spec, o_spec),
        compiler_params=_params("parallel"),
        name="router",
    )(hb, rw_t, rb)


def _routing_tables(comb_t, sel_t):
    sel = sel_t > 0.5
    cnt = jnp.sum(sel, axis=1, dtype=jnp.int32)
    cnt_pad = (cnt + (MOE_TILE - 1)) // MOE_TILE * MOE_TILE
    ends = jnp.cumsum(cnt_pad)
    rank = jnp.cumsum(sel.astype(jnp.int32), axis=1) - 1
    pos = (ends - cnt_pad)[:, None] + rank
    pos_lo = jnp.min(jnp.where(sel, pos, MOE_ROWS), axis=0)
    pos_hi = jnp.max(jnp.where(sel, pos, -1), axis=0)
    w_lo = jnp.sum(jnp.where(sel & (pos == pos_lo), comb_t, 0.0), axis=0)
    w_hi = jnp.sum(jnp.where(sel & (pos == pos_hi), comb_t, 0.0), axis=0)
    w = jnp.zeros((TOKENS, LANES), F32).at[:, 0].set(w_lo).at[:, 1].set(w_hi)
    n_tiles = ends[-1] // MOE_TILE
    tile_start = jnp.arange(MOE_TILES, dtype=jnp.int32) * MOE_TILE
    tile_start = jnp.minimum(tile_start, ends[-1] - MOE_TILE)
    tile_expert = jnp.sum((ends[None, :] <= tile_start[:, None]).astype(jnp.int32), axis=1)
    return jnp.stack([pos_lo, pos_hi]).astype(jnp.int32), w, tile_expert, n_tiles.reshape(1).astype(jnp.int32)


MOE_TILE = 256
MOE_TILES = 2 * TOKENS // MOE_TILE + N_EXPERTS
MOE_ROWS = MOE_TILES * MOE_TILE
SLAB = D_MODEL // LANES


def _to_slabs(ref, x, rows):
    for j in range(SLAB):
        ref[pl.ds(j, rows, stride=SLAB), :] = x[:, j * LANES:(j + 1) * LANES]


def _from_slabs(ref, rows, pitch=SLAB):
    return jnp.concatenate([ref[pl.ds(j, rows, stride=pitch), :] for j in range(SLAB)], axis=1)


GATHER_PITCH = SLAB + 8


def _slab_rows(row, n=SLAB):
    return pl.ds(pl.multiple_of(row * n, n), n)


XSLAB = SLAB // 2
U32 = jnp.uint32


def _to_packed_slabs(ref, x, rows):
    bits = lambda t: lax.bitcast_convert_type(t.astype(BF16).astype(F32), U32)
    for j in range(XSLAB):
        hi = bits(x[:, 2 * j * LANES:(2 * j + 1) * LANES])
        lo = bits(x[:, (2 * j + 1) * LANES:(2 * j + 2) * LANES])
        ref[pl.ds(j, rows, stride=XSLAB), :] = hi | (lo >> 16)


def _from_packed_slabs(ref, rows):
    parts = []
    for j in range(XSLAB):
        u = ref[pl.ds(j, rows, stride=XSLAB), :]
        parts.append(lax.bitcast_convert_type(u & jnp.uint32(0xFFFF0000), F32).astype(BF16))
        parts.append(lax.bitcast_convert_type(u << 16, F32).astype(BF16))
    return jnp.concatenate(parts, axis=1)


def _dispatch_kernel(pos_ref, h_ref, init_ref, xs_ref, slab_ref, sem):
    del init_ref
    tm = h_ref.shape[0]
    base = pl.program_id(0) * tm
    _to_packed_slabs(slab_ref, h_ref[...], tm)

    def copy(t, which):
        return pltpu.make_async_copy(slab_ref.at[_slab_rows(t, XSLAB), :],
                                     xs_ref.at[_slab_rows(pos_ref[which, base + t], XSLAB), :], sem)

    def start(t, _):
        copy(t, 0).start(priority=0)
        copy(t, 1).start(priority=1)
        return 0

    lax.fori_loop(0, tm, start, 0, unroll=8)
    whole = pltpu.make_async_copy(slab_ref, xs_ref.at[pl.ds(0, tm * XSLAB), :], sem)
    whole.wait()
    whole.wait()


def _dispatch(pos, h):
    tm = 256
    grid_spec = pltpu.PrefetchScalarGridSpec(
        num_scalar_prefetch=1,
        grid=(TOKENS // tm,),
        in_specs=[pl.BlockSpec((tm, D_MODEL), lambda i, pos: (i, 0)),
                  pl.BlockSpec(memory_space=pl.ANY)],
        out_specs=pl.BlockSpec(memory_space=pl.ANY),
        scratch_shapes=[pltpu.VMEM((tm * XSLAB, LANES), U32), pltpu.SemaphoreType.DMA],
    )
    return pl.pallas_call(
        _dispatch_kernel,
        out_shape=jax.ShapeDtypeStruct((MOE_ROWS * XSLAB, LANES), U32),
        grid_spec=grid_spec,
        input_output_aliases={2: 0},
        compiler_params=_params("arbitrary"),
        name="moe_dispatch",
    )(pos, h, jnp.zeros((MOE_ROWS * XSLAB, LANES), U32))


def _experts_kernel(te_ref, nt_ref, xs_ref, wg_ref, wu_ref, wd_ref, ys_ref, wgb_ref, wub_ref, wdb_ref):
    k = pl.program_id(0)
    e = te_ref[k]
    e_prev = te_ref[jnp.maximum(k - 1, 0)]

    @pl.when((k == 0) | (e != e_prev))
    def _():
        wgb_ref[...] = wg_ref[0, 0].astype(BF16)
        wub_ref[...] = wu_ref[0, 0].astype(BF16)
        wdb_ref[...] = wd_ref[0, 0].astype(BF16)

    @pl.when(k < nt_ref[0])
    def _():
        x = _from_packed_slabs(xs_ref, MOE_TILE)
        hid = jax.nn.silu(_nn(x, wgb_ref[...])) * _nn(x, wub_ref[...])
        _to_slabs(ys_ref, _nn(hid.astype(BF16), wdb_ref[...]), MOE_TILE)

    @pl.when(k >= nt_ref[0])
    def _():
        ys_ref[...] = jnp.zeros(ys_ref.shape, F32)


def _experts(tile_expert, n_tiles, xs, wg, wu, wd, layer):
    w_in_spec = pl.BlockSpec((1, 1, D_MODEL, EXPERT_HIDDEN), lambda k, te, nt: (layer, te[k], 0, 0))
    grid_spec = pltpu.PrefetchScalarGridSpec(
        num_scalar_prefetch=2,
        grid=(MOE_TILES,),
        in_specs=[pl.BlockSpec((MOE_TILE * XSLAB, LANES), lambda k, te, nt: (jnp.minimum(k, nt[0] - 1), 0)),
                  w_in_spec, w_in_spec,
                  pl.BlockSpec((1, 1, EXPERT_HIDDEN, D_MODEL), lambda k, te, nt: (layer, te[k], 0, 0))],
        out_specs=pl.BlockSpec((MOE_TILE * SLAB, LANES), lambda k, te, nt: (k, 0)),
        scratch_shapes=[pltpu.VMEM((D_MODEL, EXPERT_HIDDEN), BF16), pltpu.VMEM((D_MODEL, EXPERT_HIDDEN), BF16),
                        pltpu.VMEM((EXPERT_HIDDEN, D_MODEL), BF16)],
    )
    return pl.pallas_call(
        _experts_kernel,
        out_shape=jax.ShapeDtypeStruct((MOE_ROWS * SLAB, LANES), F32),
        grid_spec=grid_spec,
        compiler_params=_params("arbitrary"),
        name="moe_experts",
    )(tile_expert, n_tiles, xs, wg, wu, wd)


def _ple_ln_kernel(pos_ref, hb_ref, h_ref, ys_ref, w_ref, p_ref, gw_ref, gb_ref, pw_ref, g_ref, b_ref,
                   h2_ref, h2b_ref, lo_ref, hi_ref, sem):
    tm = h_ref.shape[0]
    i = pl.program_id(0)
    slot = i & 1
    bufs = (lo_ref, hi_ref)

    def fetch(tile, into):
        def start(t, _):
            for which in range(2):
                pltpu.make_async_copy(ys_ref.at[_slab_rows(pos_ref[which, tile * tm + t]), :],
                                      bufs[which].at[into, pl.ds(pl.multiple_of(t * GATHER_PITCH, 8), SLAB), :],
                                      sem.at[into]).start(priority=which)
            return 0
        lax.fori_loop(0, tm, start, 0, unroll=8)

    @pl.when(i == 0)
    def _():
        fetch(0, 0)

    @pl.when(i + 1 < pl.num_programs(0))
    def _():
        fetch(i + 1, 1 - slot)

    gate = jax.nn.sigmoid(_nn(hb_ref[...], gw_ref[...]) + gb_ref[...])
    ple = gate * _nn(p_ref[...].astype(BF16), pw_ref[...])
    for which in range(2):
        pltpu.make_async_copy(ys_ref.at[pl.ds(0, tm * SLAB), :], bufs[which].at[slot, pl.ds(0, tm * SLAB), :],
                              sem.at[slot]).wait()
    w = w_ref[...]
    ffn = (w[:, 0:1] * _from_slabs(lo_ref.at[slot], tm, GATHER_PITCH)
           + w[:, 1:2] * _from_slabs(hi_ref.at[slot], tm, GATHER_PITCH))
    h2 = _layer_norm(DEEPNORM_ALPHA * h_ref[...] + ffn + ple, g_ref[...], b_ref[...])
    h2_ref[...] = h2
    h2b_ref[...] = h2.astype(BF16)


def _ple_ln(pos, hb, h, ys, w, p, layer, gw, gb, pw, g, b):
    tm = 256
    p_spec = pl.BlockSpec((tm, PLE_DIM), lambda i, pos: (layer * (TOKENS // tm) + i, 0))
    rows = lambda width: pl.BlockSpec((tm, width), lambda i, pos: (i, 0))
    full = lambda a: pl.BlockSpec(a.shape, lambda i, pos: (0, 0))
    grid_spec = pltpu.PrefetchScalarGridSpec(
        num_scalar_prefetch=1,
        grid=(TOKENS // tm,),
        in_specs=[rows(D_MODEL), rows(D_MODEL), pl.BlockSpec(memory_space=pl.ANY), rows(LANES), p_spec,
                  full(gw), full(gb), full(pw), full(g), full(b)],
        out_specs=(rows(D_MODEL), rows(D_MODEL)),
        scratch_shapes=[pltpu.VMEM((2, tm * GATHER_PITCH, LANES), F32), pltpu.VMEM((2, tm * GATHER_PITCH, LANES), F32),
                        pltpu.SemaphoreType.DMA((2,))],
    )
    return pl.pallas_call(
        _ple_ln_kernel,
        out_shape=(jax.ShapeDtypeStruct((TOKENS, D_MODEL), F32),
                   jax.ShapeDtypeStruct((TOKENS, D_MODEL), BF16)),
        grid_spec=grid_spec,
        compiler_params=_params("arbitrary"),
        name="ple_ln",
    )(pos, hb, h, ys, w, p, gw, gb, pw, g, b)


def _rope_tables(positions):
    half = ROT_DIM // 2
    inv_freq = jnp.exp(jnp.arange(half, dtype=F32) * (-2.0 * math.log(ROPE_THETA) / ROT_DIM))
    ang = positions.astype(F32)[:, :, None] * inv_freq
    cos, sin = jnp.cos(ang), jnp.sin(ang)
    zeros = jnp.zeros_like(cos)
    rest = HEAD_DIM - ROT_DIM
    pad = lambda v: jnp.broadcast_to(jnp.asarray(v, F32), cos.shape[:2] + (rest,))
    c = jnp.concatenate([cos, cos, pad(1.0)], axis=-1)
    s1 = jnp.concatenate([-sin, zeros, pad(0.0)], axis=-1)
    s2 = jnp.concatenate([zeros, sin, pad(0.0)], axis=-1)
    tile = lambda t: jnp.concatenate([t, t], axis=-1).reshape(TOKENS, LANES)
    return tile(c), tile(s1), tile(s2)


def _split_w_in(w):
    mw, nq, nkv, dw = MOBA_HEADS * HEAD_DIM, NSA_HEADS * HEAD_DIM, NSA_KV_HEADS * HEAD_DIM, DIL_HEADS * HEAD_DIM
    widths = (mw, mw, mw, nq) + (nkv,) * 6 + (NSA_HEADS * 3, dw, dw, dw)
    offs = np.concatenate([[0], np.cumsum(widths)])
    qa, ka, va, qb, kbc, vbc, kbs, vbs, kbw, vbw, gb, qc, kc, vc = (
        w[:, int(offs[i]):int(offs[i + 1])] for i in range(len(widths)))

    def dup(t):
        t = t.reshape(D_MODEL, NSA_KV_HEADS, 1, HEAD_DIM)
        return jnp.broadcast_to(t, (D_MODEL, NSA_KV_HEADS, 2, HEAD_DIM)).reshape(D_MODEL, NSA_KV_HEADS * LANES)

    zpad = lambda n: jnp.zeros((D_MODEL, n * LANES), w.dtype)
    w_rot = jnp.concatenate([qa * Q_SCALE, ka, qb * Q_SCALE, dup(kbc), dup(kbs), dup(kbw), zpad(1)], axis=1)
    w_pl = jnp.concatenate([va, dup(vbc), dup(vbs), dup(vbw), zpad(3)], axis=1)
    gpad = jnp.zeros((D_MODEL, NSA_KV_HEADS, LANES - 12), w.dtype)
    w_gl = jnp.concatenate([gb.reshape(D_MODEL, NSA_KV_HEADS, 12), gpad], axis=-1).reshape(D_MODEL, -1)
    w_dil_rot = jnp.concatenate([qc * Q_SCALE, kc], axis=1)
    return tuple(t.astype(BF16) for t in (w_rot, w_pl, w_gl, w_dil_rot, vc))


def _overlap_table():
    starts = np.arange(N_CMP) * CMP_STRIDE
    slc = np.arange(N_SLC) * SLC_BLOCK
    ov = ((starts[:, None] < slc[None, :] + SLC_BLOCK) & (starts[:, None] + CMP_LEN > slc[None, :]))
    ovt = np.zeros((N_SLC, N_CMP_PAD), np.float32)
    ovt[:, :N_CMP] = ov.T
    return jnp.asarray(ovt, BF16)


def _cmp_chunks(z, base):
    nblk = z.shape[-1] // LANES
    t = z.reshape(BATCH, SEQ // CMP_STRIDE, CMP_STRIDE, nblk, LANES)[:, :, :, base:base + NSA_KV_HEADS, :HEAD_DIM]
    return t.transpose(0, 3, 1, 2, 4).reshape(BATCH, NSA_KV_HEADS, SEQ // CMP_STRIDE, CMP_STRIDE * HEAD_DIM)


def kernel(x, p, positions, ln_in_g, ln_in_b, w_in, w_out, nsa_ck1, nsa_ck2, nsa_pe_k, nsa_cv1, nsa_cv2, nsa_pe_v, ln1_g, ln1_b, router_w, router_b, w_gate, w_up, w_down, ple_proj, ple_gate_w, ple_gate_b, ln2_g, ln2_b):
    rope = _rope_tables(positions)
    ovt = _overlap_table()
    rw_t = router_w.T.astype(BF16)
    rb = router_b.reshape(N_EXPERTS, 1).astype(F32)
    chunk_w = CMP_STRIDE * HEAD_DIM
    vec = lambda v: v.reshape(1, -1)
    seq3 = lambda t: t.reshape(BATCH, SEQ, t.shape[-1])
    flat = lambda t: t.reshape(TOKENS, t.shape[-1])

    h, hb = _ln_in(x.reshape(TOKENS, D_MODEL), ln_in_g, ln_in_b)
    for i in range(DEPTH):
        w_rot, w_pl, w_gl, w_dil_rot, w_dil_pl = _split_w_in(w_in[i])
        z_rot = seq3(_project(hb, w_rot, BF16, 768, rope=rope))
        z_pl = seq3(_project(hb, w_pl, BF16, 1024))
        gate_logits = _project(hb, w_gl, F32, NSA_KV_HEADS * LANES)
        zd_rot = seq3(_project(hb, w_dil_rot, F32, 768, rope=rope))
        zd_pl = seq3(_project(hb, w_dil_pl, F32, 768))

        o_a = _moba(z_rot, z_pl)

        dup2 = lambda w2: jnp.concatenate([w2, w2], axis=1).astype(BF16)
        k_cmp, v_cmp = _compress(
            _cmp_chunks(z_rot, ROT_NKC), _cmp_chunks(z_pl, PL_NVC),
            nsa_pe_k[i].reshape(2, chunk_w), nsa_pe_v[i].reshape(2, chunk_w),
            nsa_ck1[i].reshape(2, chunk_w, CMP_HIDDEN).astype(BF16), dup2(nsa_ck2[i]),
            nsa_cv1[i].reshape(2, chunk_w, CMP_HIDDEN).astype(BF16), dup2(nsa_cv2[i]))
        o_b = _nsa(z_rot, z_pl, k_cmp, v_cmp, gate_logits, ovt)

        o_c = _dilated(zd_rot, zd_pl)

        wo = w_out[i].astype(BF16)
        a_w, b_w = MOBA_HEADS * HEAD_DIM, NSA_HEADS * HEAD_DIM
        h, hb = _out_proj(flat(o_a), flat(o_b), flat(o_c), h,
                          wo[:a_w], wo[a_w:a_w + b_w], wo[a_w + b_w:], vec(ln1_g[i]), vec(ln1_b[i]))

        pos, w_tok, tile_expert, n_tiles = _routing_tables(*_router(hb, rw_t, rb))
        xs = _dispatch(pos, h)
        ys = _experts(tile_expert, n_tiles, xs, w_gate, w_up, w_down, i)
        h, hb = _ple_ln(pos, hb, h, ys, w_tok, p.reshape(DEPTH * TOKENS, PLE_DIM), i, ple_gate_w[i].astype(BF16),
                        vec(ple_gate_b[i]), ple_proj[i].astype(BF16), vec(ln2_g[i]), vec(ln2_b[i]))
    return h.reshape(BATCH, SEQ, D_MODEL)
```

```python
import functools
import math

import numpy as np
import jax
import jax.numpy as jnp
from jax import lax
from jax.experimental import pallas as pl
from jax.experimental.pallas import tpu as pltpu

F32 = jnp.float32
BF16 = jnp.bfloat16

D_MODEL = 2048
BATCH = 2
SEQ = 4096
DEPTH = 4
TOKENS = BATCH * SEQ
HEAD_DIM = 64
ROT_DIM = HEAD_DIM // 4
ROPE_THETA = 500000.0
NEG = -1e30
FORCE = 1e30
LN_EPS = 1e-5
SCALE = HEAD_DIM ** -0.5
LOG2_E = math.log2(math.e)
Q_SCALE = SCALE * LOG2_E

MOBA_HEADS = 8
MOBA_BLOCK = 256
MOBA_TOPK = 3
MOBA_NB = SEQ // MOBA_BLOCK

NSA_HEADS = 12
NSA_KV_HEADS = 3
CMP_LEN = 32
CMP_STRIDE = 16
CMP_HIDDEN = 128
N_CMP = (SEQ - CMP_LEN) // CMP_STRIDE + 1
N_CMP_PAD = 256
SLC_BLOCK = 64
SLC_TOPK = 16
SLC_LOCAL = 2
N_SLC = SEQ // SLC_BLOCK
NSA_WINDOW = 512

DIL_CONFIGS = ((128, 1), (512, 4), (2048, 16))
DIL_HEADS_PER_GROUP = 4
DIL_HEADS = DIL_HEADS_PER_GROUP * len(DIL_CONFIGS)

N_EXPERTS = 16
N_GROUPS = 4
EXPERTS_PER_GROUP = 4
EXPERT_HIDDEN = D_MODEL // 4
PLE_DIM = 256

DEEPNORM_ALPHA = (2 * DEPTH) ** 0.25

LANES = 128
VMEM_LIMIT = 56 * 1024 * 1024

ROT_MQ, ROT_MK, ROT_NQ, ROT_NKC, ROT_NKS, ROT_NKW = 0, 4, 8, 14, 17, 20
ROT_BLOCKS = 24
PL_MV, PL_NVC, PL_NVS, PL_NVW = 0, 4, 7, 10
PL_BLOCKS = 16
DIL_BLOCKS = DIL_HEADS // 2

NT_DIMS = (((1,), (1,)), ((), ()))


def _nt(a, b):
    return lax.dot_general(a, b, NT_DIMS, preferred_element_type=F32)


def _nn(a, b):
    return jnp.dot(a, b, preferred_element_type=F32)


def _params(*sem):
    return pltpu.CompilerParams(dimension_semantics=sem, vmem_limit_bytes=VMEM_LIMIT)


def _layer_norm(y, g, b):
    mu = jnp.mean(y, axis=-1, keepdims=True)
    yc = y - mu
    var = jnp.mean(yc * yc, axis=-1, keepdims=True)
    return yc * lax.rsqrt(var + LN_EPS) * g + b


def _ln_kernel(x_ref, g_ref, b_ref, h_ref, hb_ref):
    h = _layer_norm(x_ref[...], g_ref[...], b_ref[...])
    h_ref[...] = h
    hb_ref[...] = h.astype(BF16)


def _ln_in(x, g, b):
    tm = 512
    row = pl.BlockSpec((tm, D_MODEL), lambda i: (i, 0))
    vec = pl.BlockSpec((1, D_MODEL), lambda i: (0, 0))
    return pl.pallas_call(
        _ln_kernel,
        out_shape=(jax.ShapeDtypeStruct((TOKENS, D_MODEL), F32),
                   jax.ShapeDtypeStruct((TOKENS, D_MODEL), BF16)),
        grid=(TOKENS // tm,),
        in_specs=[row, vec, vec],
        out_specs=(row, row),
        compiler_params=_params("parallel"),
        name="ln_in",
    )(x, g.reshape(1, -1), b.reshape(1, -1))


def _proj_kernel(x_ref, w_ref, o_ref):
    o_ref[...] = _nn(x_ref[...], w_ref[...]).astype(o_ref.dtype)


def _proj_rot_kernel(x_ref, w_ref, c_ref, s1_ref, s2_ref, o_ref):
    x = x_ref[...]
    c, s1, s2 = c_ref[...], s1_ref[...], s2_ref[...]
    half = ROT_DIM // 2
    for j0 in range(0, o_ref.shape[1], 2 * LANES):
        z = _nn(x, w_ref[:, j0:j0 + 2 * LANES])
        for j in range(j0, j0 + 2 * LANES, LANES):
            zc = z[:, j - j0:j - j0 + LANES]
            r = zc * c + pltpu.roll(zc, LANES - half, 1) * s1 + pltpu.roll(zc, half, 1) * s2
            o_ref[:, j:j + LANES] = r.astype(o_ref.dtype)


def _project(hb, w, out_dtype, tn, rope=None):
    tm = 1024
    n = w.shape[1]
    x_spec = pl.BlockSpec((tm, D_MODEL), lambda i, j: (i, 0))
    w_spec = pl.BlockSpec((D_MODEL, tn), lambda i, j: (0, j))
    o_spec = pl.BlockSpec((tm, tn), lambda i, j: (i, j))
    if rope is None:
        kern, extra, extra_specs = _proj_kernel, (), []
    else:
        t_spec = pl.BlockSpec((tm, LANES), lambda i, j: (i, 0))
        kern, extra, extra_specs = _proj_rot_kernel, rope, [t_spec] * 3
    return pl.pallas_call(
        kern,
        out_shape=jax.ShapeDtypeStruct((TOKENS, n), out_dtype),
        grid=(TOKENS // tm, n // tn),
        in_specs=[x_spec, w_spec] + extra_specs,
        out_specs=o_spec,
        compiler_params=_params("parallel", "arbitrary"),
        name="in_proj_rot" if rope is not None else "in_proj",
    )(hb, w, *extra)


def _stack_heads(*q_blocks):
    parts = []
    for q in q_blocks:
        lane = lax.broadcasted_iota(jnp.int32, q.shape, 1)
        zero = jnp.zeros_like(q)
        parts += [jnp.where(lane < HEAD_DIM, q, zero), jnp.where(lane >= HEAD_DIM, q, zero)]
    return jnp.concatenate(parts, axis=0)


def _merge_pair_t(lo, hi):
    sub = lax.broadcasted_iota(jnp.int32, lo.shape, 0)
    return jnp.where(sub < HEAD_DIM, lo, hi)


def _band_bias_t(nk, qc, offset, n_back):
    key = lax.broadcasted_iota(jnp.int32, (nk, qc), 0)
    qry = lax.broadcasted_iota(jnp.int32, (nk, qc), 1)
    diff = offset + qry - key
    return jnp.where((diff >= 0) & (diff <= n_back), 0.0, NEG)


def _tile_lanes(x, n):
    return jnp.concatenate([x] * n, axis=1)


def _transpose_bf16(v):
    return jnp.transpose(v.astype(F32)).astype(BF16)


def _tree(x, op):
    n = x.shape[0]
    if n == 8:
        return x
    if n % 16 == 0:
        return op(_tree(x[:n // 2], op), _tree(x[n // 2:], op))
    acc = x[:8]
    for i in range(1, n // 8):
        acc = op(acc, x[8 * i:8 * i + 8])
    return acc


def _reduce_keys(x, op, final):
    return final(_tree(x, op), axis=0, keepdims=True)


VT_ROWS = LANES + 16


def _transpose_aug(v):
    vt = jnp.transpose(v.astype(F32))
    sub = lax.broadcasted_iota(jnp.int32, (VT_ROWS - LANES, v.shape[0]), 0)
    return jnp.concatenate([vt, jnp.where(sub == 0, 1.0, 0.0)], axis=0).astype(BF16)


def _probs(s_t, m):
    return jnp.exp2((s_t - m).astype(BF16))


def _normalise(acc):
    l = acc[LANES:LANES + 1]
    return acc[:LANES] / l, l


def _softmax_block_t(s_t, pv):
    m = _reduce_keys(s_t, jnp.maximum, jnp.max)
    out, l = _normalise(pv(_probs(s_t, m)))
    return out, m + jnp.log(l) * LOG2_E


def _online_step_t(carry, s_t, m_t, pv):
    m, acc = carry
    m_new = jnp.maximum(m, m_t)
    acc = jnp.exp2(m - m_new) * acc + pv(_probs(s_t, m_new))
    return m_new, acc


def _flash_tiles(n_tiles, last_tile, init, scores, pv_of):
    strips = range(len(init))

    def produce(t):
        s = tuple(scores(t))
        return s, tuple(_reduce_keys(s_i, jnp.maximum, jnp.max) for s_i in s)

    def body(t, carry):
        state, s_t, m_t = carry
        s_next, m_next = produce(jnp.minimum(t + 1, last_tile))
        pv = pv_of(t)
        return tuple(_online_step_t(state[i], s_t[i], m_t[i], pv) for i in strips), s_next, m_next

    state, _, _ = lax.fori_loop(0, n_tiles, body, (tuple(init),) + produce(0))
    return state


def _pv_tiles(vt_ref, first, n, rows):
    def pv(p):
        acc = _nn(vt_ref[first], p[:rows])
        for j in range(1, n):
            acc = acc + _nn(vt_ref[first + j], p[j * rows:(j + 1) * rows])
        return acc
    return pv


def _online_init_t(r):
    return (jnp.full((1, r), NEG, F32), jnp.zeros((VT_ROWS, r), F32))


def _rank_rows(g, n_rows):
    sub = lax.broadcasted_iota(jnp.int32, (8, g.shape[1]), 0)
    rank = jnp.zeros(g.shape, F32)
    for m in range(n_rows):
        gm = g[m:m + 1, :]
        b = m // 8 * 8
        mid = g[b:b + 8]
        parts = [jnp.where(gm > mid, 1.0, jnp.where((gm == mid) & (sub > m - b), 1.0, 0.0))]
        if b > 0:
            parts.insert(0, jnp.where(gm > g[:b], 1.0, 0.0))
        if b + 8 < n_rows:
            parts.append(jnp.where(gm >= g[b + 8:], 1.0, 0.0))
        rank = rank + jnp.concatenate(parts, axis=0)
    return rank


MOBA_KT = 2 * MOBA_BLOCK
MOBA_QC = MOBA_KT


def _moba_kernel(q_ref, k_ref, v_ref, o_ref, kmean_ref, vt_ref, bias_ref):
    c = pl.program_id(2)
    qc = MOBA_QC
    r = 2 * qc

    @pl.when(c == 0)
    def _():
        row = lax.broadcasted_iota(jnp.int32, (MOBA_NB, SEQ), 0)
        col = lax.broadcasted_iota(jnp.int32, (MOBA_NB, SEQ), 1)
        avg = jnp.where((col >> 8) == row, 1.0 / MOBA_BLOCK, 0.0).astype(BF16)
        kmean_ref[...] = _nn(avg, k_ref[0])
        for t in range(MOBA_NB):
            vt_ref[t] = _transpose_aug(v_ref[0, t * MOBA_BLOCK:(t + 1) * MOBA_BLOCK, :])

    qs = _stack_heads(q_ref[0])
    per_tile = MOBA_KT // MOBA_BLOCK

    ks = pl.multiple_of(c * MOBA_KT, MOBA_KT)
    heads = [qs[:qc], qs[qc:]]
    raw_own = [_nt(k_ref[0, pl.ds(ks, MOBA_KT), :], q_h) for q_h in heads]

    gate = _nt(kmean_ref[...].astype(BF16), qs)
    blk = lax.broadcasted_iota(jnp.int32, gate.shape, 0)
    q_idx = lax.broadcasted_iota(jnp.int32, gate.shape, 1) & (qc - 1)
    past = blk < c * per_tile + (q_idx >> 8)
    rank = _rank_rows(jnp.where(past, gate, NEG), MOBA_NB)
    bias_ref[...] = jnp.where(past & (rank < MOBA_TOPK), 0.0, NEG)

    key = lax.broadcasted_iota(jnp.int32, (MOBA_KT, qc), 0)
    qry = lax.broadcasted_iota(jnp.int32, (MOBA_KT, qc), 1)
    first_block = bias_ref[pl.ds(c * per_tile, 1), :]
    state = []
    for h in range(2):
        other = jnp.broadcast_to(first_block[:, h * qc:(h + 1) * qc], (MOBA_KT, qc))
        own_bias = jnp.where(key <= qry, jnp.where((key >> 8) == (qry >> 8), 0.0, other), NEG)
        s_h = raw_own[h] + own_bias
        state.append(_online_step_t(_online_init_t(qc), s_h, _reduce_keys(s_h, jnp.maximum, jnp.max),
                                    _pv_tiles(vt_ref, c * per_tile, per_tile, MOBA_BLOCK)))

    def scores(t):
        ks = pl.multiple_of(t * MOBA_KT, MOBA_KT)
        kt = k_ref[0, pl.ds(ks, MOBA_KT), :]
        rows = [bias_ref[pl.ds(t * per_tile + j, 1), :] for j in range(per_tile)]
        out = []
        for h in range(2):
            blocks = [jnp.broadcast_to(row[:, h * qc:(h + 1) * qc], (MOBA_BLOCK, qc)) for row in rows]
            out.append(_nt(kt, heads[h]) + jnp.concatenate(blocks, axis=0))
        return out

    state = _flash_tiles(c, SEQ // MOBA_KT - 1, state, scores,
                         lambda t: _pv_tiles(vt_ref, t * per_tile, per_tile, MOBA_BLOCK))
    o_lo, o_hi = (_normalise(acc)[0] for _, acc in state)
    o_ref[0] = jnp.transpose(_merge_pair_t(o_lo, o_hi)).astype(o_ref.dtype)


def _moba(z_rot, z_pl):
    qc = MOBA_QC
    grid = (BATCH, MOBA_HEADS // 2, SEQ // qc)
    return pl.pallas_call(
        _moba_kernel,
        out_shape=jax.ShapeDtypeStruct((BATCH, SEQ, MOBA_HEADS * HEAD_DIM), BF16),
        grid=grid,
        in_specs=[
            pl.BlockSpec((1, qc, LANES), lambda b, p, c: (b, c, ROT_MQ + p)),
            pl.BlockSpec((1, SEQ, LANES), lambda b, p, c: (b, 0, ROT_MK + p)),
            pl.BlockSpec((1, SEQ, LANES), lambda b, p, c: (b, 0, PL_MV + p)),
        ],
        out_specs=pl.BlockSpec((1, qc, LANES), lambda b, p, c: (b, c, p)),
        scratch_shapes=[pltpu.VMEM((MOBA_NB, LANES), F32),
                        pltpu.VMEM((MOBA_NB, VT_ROWS, MOBA_BLOCK), BF16),
                        pltpu.VMEM((MOBA_NB, 2 * qc), F32)],
        compiler_params=_params("parallel", "parallel", "arbitrary"),
        name="moba",
    )(z_rot, z_rot, z_pl)


def _compress_one(x_ref, pe_ref, w1_ref, w2_ref, o_ref):
    x = x_ref[0, 0].astype(F32)
    top = (x + pe_ref[0:1, :]).astype(BF16)
    bot = (x + pe_ref[1:2, :]).astype(BF16)
    a = _nn(top, w1_ref[0])
    bm = _nn(bot, w1_ref[1])
    pre = a + pltpu.roll(bm, N_CMP_PAD - 1, 0)
    hid = jax.nn.gelu(pre)
    out = _nn(hid.astype(BF16), w2_ref[...])
    row = lax.broadcasted_iota(jnp.int32, out.shape, 0)
    o_ref[0, 0] = jnp.where(row < N_CMP, out, 0.0).astype(o_ref.dtype)


def _compress_kernel(xk_ref, xv_ref, pk_ref, pv_ref, k1_ref, k2_ref, v1_ref, v2_ref, ok_ref, ov_ref):
    _compress_one(xk_ref, pk_ref, k1_ref, k2_ref, ok_ref)
    _compress_one(xv_ref, pv_ref, v1_ref, v2_ref, ov_ref)


def _compress(xk, xv, pk, pv, k1, k2, v1, v2):
    chunk_w = CMP_STRIDE * HEAD_DIM
    x_spec = pl.BlockSpec((1, 1, N_CMP_PAD, chunk_w), lambda b, j: (b, j, 0, 0))
    pe_spec = pl.BlockSpec((2, chunk_w), lambda b, j: (0, 0))
    w1_spec = pl.BlockSpec((2, chunk_w, CMP_HIDDEN), lambda b, j: (0, 0, 0))
    w2_spec = pl.BlockSpec((CMP_HIDDEN, LANES), lambda b, j: (0, 0))
    o_spec = pl.BlockSpec((1, 1, N_CMP_PAD, LANES), lambda b, j: (b, j, 0, 0))
    o_shape = jax.ShapeDtypeStruct((BATCH, NSA_KV_HEADS, N_CMP_PAD, LANES), BF16)
    return pl.pallas_call(
        _compress_kernel,
        out_shape=(o_shape, o_shape),
        grid=(BATCH, NSA_KV_HEADS),
        in_specs=[x_spec, x_spec, pe_spec, pe_spec, w1_spec, w2_spec, w1_spec, w2_spec],
        out_specs=(o_spec, o_spec),
        compiler_params=_params("parallel", "parallel"),
        name="nsa_compress",
    )(xk, xv, pk, pv, k1, k2, v1, v2)


NSA_QC = 512
NSA_KT = 512
NSA_G = NSA_HEADS // NSA_KV_HEADS
NSA_WIN_TILES = NSA_WINDOW // NSA_QC + 1


def _nsa_kernel(qa_ref, qb_ref, kc_ref, vc_ref, ks_ref, vs_ref, kw_ref, vw_ref, gl_ref, ovt_ref,
                o_ref, vct_ref, vst_ref, vwt_ref, bias_ref):
    c = pl.program_id(2)
    qc = NSA_QC
    q0 = c * qc
    lanes_of = lambda t, i: t[:, i * qc:(i + 1) * qc]

    @pl.when(c == 0)
    def _():
        vct_ref[...] = _transpose_bf16(vc_ref[0, 0])
        for t in range(SEQ // NSA_KT):
            vst_ref[t] = _transpose_aug(vs_ref[0, t * NSA_KT:(t + 1) * NSA_KT, :])
        for t in range(SEQ // qc):
            vwt_ref[t] = _transpose_aug(vw_ref[0, t * qc:(t + 1) * qc, :])

    qs = _stack_heads(qa_ref[0], qb_ref[0])

    sc_t = _nt(kc_ref[0, 0], qs)
    t0 = jnp.maximum(c - NSA_WINDOW // qc, 0)
    start = pl.multiple_of(t0 * qc, qc)
    sw_t = _nt(kw_ref[0, pl.ds(start, NSA_WIN_TILES * qc), :], qs)

    n_idx = lax.broadcasted_iota(jnp.int32, (N_CMP_PAD, qc), 0)
    q_idx = lax.broadcasted_iota(jnp.int32, (N_CMP_PAD, qc), 1)
    ok = (n_idx * CMP_STRIDE + (CMP_LEN - 1)) <= (q0 + q_idx)
    p_heads = []
    for i in range(NSA_G):
        s_i = jnp.where(ok, lanes_of(sc_t, i), NEG)
        e_i = jnp.where(ok, jnp.exp2(s_i - _reduce_keys(s_i, jnp.maximum, jnp.max)), 0.0)
        l_i = _reduce_keys(e_i, jnp.add, jnp.sum)
        p_heads.append((e_i / jnp.where(l_i > 0.0, l_i, 1.0)).astype(BF16))
    p_ct = jnp.concatenate(p_heads, axis=1)
    imp4 = _nn(ovt_ref[...], p_ct)
    ocmp_t = _nn(vct_ref[...], p_ct)

    band = _band_bias_t(NSA_WIN_TILES * qc, qc, q0 - start, NSA_WINDOW - 1)
    owin_t, _ = _softmax_block_t(sw_t + _tile_lanes(band, NSA_G), _pv_tiles(vwt_ref, t0, NSA_WIN_TILES, qc))
    gate_t = jnp.transpose(jax.nn.sigmoid(gl_ref[...]))
    gate = lambda i, r: gate_t[3 * i + r:3 * i + r + 1, :]
    partial_out = [gate(i, 0) * lanes_of(ocmp_t, i) + gate(i, 2) * lanes_of(owin_t, i) for i in range(NSA_G)]

    imp = lanes_of(imp4, 0) + lanes_of(imp4, 1) + lanes_of(imp4, 2) + lanes_of(imp4, 3)
    blk = lax.broadcasted_iota(jnp.int32, imp.shape, 0)
    cur = (q0 + lax.broadcasted_iota(jnp.int32, imp.shape, 1)) >> 6
    valid = blk <= cur
    forced = valid & ((blk == 0) | (blk > cur - SLC_LOCAL))
    rank = _rank_rows(jnp.where(forced, FORCE, jnp.where(valid, imp, NEG)), N_SLC)
    bias_ref[...] = jnp.where(valid & (rank < SLC_TOPK), 0.0, NEG)

    key_row = lax.broadcasted_iota(jnp.int32, (NSA_KT, qc), 0)
    qpos = lax.broadcasted_iota(jnp.int32, (NSA_KT, qc), 1) + q0
    per_tile = NSA_KT // SLC_BLOCK

    def scores(t):
        ks0 = pl.multiple_of(t * NSA_KT, NSA_KT)
        blocks = [jnp.broadcast_to(bias_ref[pl.ds(t * per_tile + j, 1), :], (SLC_BLOCK, qc))
                  for j in range(per_tile)]
        bias = jnp.where(key_row + ks0 <= qpos, jnp.concatenate(blocks, axis=0), NEG)
        kt = ks_ref[0, pl.ds(ks0, NSA_KT), :]
        return [_nt(kt, qs[i * qc:(i + 1) * qc]) + bias for i in range(NSA_G)]

    slc = _flash_tiles(c // (NSA_KT // qc) + 1, SEQ // NSA_KT - 1, [_online_init_t(qc)] * NSA_G, scores,
                       lambda t: _pv_tiles(vst_ref, t, 1, NSA_KT))
    outs = [partial_out[i] + gate(i, 1) * _normalise(slc[i][1])[0] for i in range(NSA_G)]
    o_ref[0, :, 0:LANES] = jnp.transpose(_merge_pair_t(outs[0], outs[1])).astype(o_ref.dtype)
    o_ref[0, :, LANES:2 * LANES] = jnp.transpose(_merge_pair_t(outs[2], outs[3])).astype(o_ref.dtype)


def _nsa(z_rot, z_pl, k_cmp, v_cmp, gate_logits, ovt):
    qc = NSA_QC
    seq_spec = lambda base: pl.BlockSpec((1, SEQ, LANES), lambda b, j, c: (b, 0, base + j))
    cmp_spec = pl.BlockSpec((1, 1, N_CMP_PAD, LANES), lambda b, j, c: (b, j, 0, 0))
    return pl.pallas_call(
        _nsa_kernel,
        out_shape=jax.ShapeDtypeStruct((BATCH, SEQ, NSA_HEADS * HEAD_DIM), BF16),
        grid=(BATCH, NSA_KV_HEADS, SEQ // qc),
        in_specs=[
            pl.BlockSpec((1, qc, LANES), lambda b, j, c: (b, c, ROT_NQ + 2 * j)),
            pl.BlockSpec((1, qc, LANES), lambda b, j, c: (b, c, ROT_NQ + 2 * j + 1)),
            cmp_spec, cmp_spec,
            seq_spec(ROT_NKS), seq_spec(PL_NVS), seq_spec(ROT_NKW), seq_spec(PL_NVW),
            pl.BlockSpec((qc, LANES), lambda b, j, c: (b * (SEQ // qc) + c, j)),
            pl.BlockSpec(ovt.shape, lambda b, j, c: (0, 0)),
        ],
        out_specs=pl.BlockSpec((1, qc, 2 * LANES), lambda b, j, c: (b, c, j)),
        scratch_shapes=[pltpu.VMEM((LANES, N_CMP_PAD), BF16),
                        pltpu.VMEM((SEQ // NSA_KT, VT_ROWS, NSA_KT), BF16),
                        pltpu.VMEM((SEQ // qc, VT_ROWS, qc), BF16),
                        pltpu.VMEM((N_SLC, qc), F32)],
        compiler_params=_params("parallel", "parallel", "arbitrary"),
        name="nsa",
    )(z_rot, z_rot, k_cmp, v_cmp, z_rot, z_pl, z_rot, z_pl, gate_logits, ovt)


DIL_QC = 128
DIL_STEPS = SEQ // DIL_QC
DIL_UNROLL = 8


def _dil_group(q_ref, k_ref, v_ref, og_ref, lg_ref, gi):
    window, dil = DIL_CONFIGS[gi]
    qc = DIL_QC
    m = SEQ // dil
    n_back = window // dil
    nk = min(m, qc + -(-n_back // qc) * qc)
    chunks = m // qc

    def rows(first, n):
        return pl.ds(first, n) if dil == 1 else pl.ds(first, n, stride=dil)

    def place(idx):
        r = idx // chunks
        q0 = (idx % chunks) * qc
        start = jnp.maximum(q0 - (nk - qc), 0)
        return rows(r + dil * q0, qc), rows(r + dil * start, nk), q0 - start

    def body(i, _):
        at = [place(i * DIL_UNROLL + u) for u in range(DIL_UNROLL)]
        s = [_nt(k_ref[0, k_rows, :].astype(BF16), _stack_heads(q_ref[0, q_rows, :].astype(BF16)))
             for q_rows, k_rows, _ in at]
        m, p = [], []
        for u, (_, _, off) in enumerate(at):
            s_u = s[u] + _tile_lanes(_band_bias_t(nk, qc, off, n_back), 2)
            m.append(_reduce_keys(s_u, jnp.maximum, jnp.max))
            p.append(_probs(s_u, m[u]))
        acc = [_nn(_transpose_aug(v_ref[0, k_rows, :]), p[u]) for u, (_, k_rows, _) in enumerate(at)]
        for u, (q_rows, _, _) in enumerate(at):
            o_t, l = _normalise(acc[u])
            lse_b = jnp.broadcast_to(m[u] + jnp.log(l) * LOG2_E, (LANES, 2 * qc))
            og_ref[gi, q_rows, :] = jnp.transpose(_merge_pair_t(o_t[:, :qc], o_t[:, qc:]))
            lg_ref[gi, q_rows, :] = jnp.transpose(_merge_pair_t(lse_b[:, :qc], lse_b[:, qc:]))
        return 0

    lax.fori_loop(0, DIL_STEPS // DIL_UNROLL, body, 0)


def _dil_kernel(q_ref, k_ref, v_ref, o_ref, og_ref, lg_ref):
    g = pl.program_id(2)
    n_groups = len(DIL_CONFIGS)
    for gi in range(n_groups):
        pl.when(g == gi)(functools.partial(_dil_group, q_ref, k_ref, v_ref, og_ref, lg_ref, gi))

    @pl.when(g == n_groups - 1)
    def _():
        rows = 512

        def body(i, _):
            sl = pl.ds(pl.multiple_of(i * rows, rows), rows)
            l0, l1, l2 = lg_ref[0, sl, :], lg_ref[1, sl, :], lg_ref[2, sl, :]
            mx = jnp.maximum(jnp.maximum(l0, l1), l2)
            e0, e1, e2 = jnp.exp2(l0 - mx), jnp.exp2(l1 - mx), jnp.exp2(l2 - mx)
            den = e0 + e1 + e2
            out = (e0 / den) * og_ref[0, sl, :] + (e1 / den) * og_ref[1, sl, :] + (e2 / den) * og_ref[2, sl, :]
            o_ref[0, sl, :] = out.astype(o_ref.dtype)
            return 0

        lax.fori_loop(0, SEQ // rows, body, 0)


def _dilated(zd_rot, zd_pl):
    n_groups = len(DIL_CONFIGS)
    width = DIL_HEADS_PER_GROUP * HEAD_DIM
    col = lambda base: (lambda b, p, g: (b, 0, base + 2 * g + p))
    blk = lambda base: pl.BlockSpec((1, SEQ, LANES), col(base))
    return pl.pallas_call(
        _dil_kernel,
        out_shape=jax.ShapeDtypeStruct((BATCH, SEQ, width), BF16),
        grid=(BATCH, 2, n_groups),
        in_specs=[blk(0), blk(DIL_BLOCKS), blk(0)],
        out_specs=pl.BlockSpec((1, SEQ, LANES), lambda b, p, g: (b, 0, p)),
        scratch_shapes=[pltpu.VMEM((n_groups, SEQ, LANES), F32), pltpu.VMEM((n_groups, SEQ, LANES), F32)],
        compiler_params=_params("parallel", "parallel", "arbitrary"),
        name="dilated",
    )(zd_rot, zd_rot, zd_pl)


def _out_proj_kernel(oa_ref, ob_ref, oc_ref, h_ref, wa_ref, wb_ref, wc_ref, g_ref, b_ref, h1_ref, h1b_ref):
    y = _nn(oa_ref[...], wa_ref[...]) + _nn(ob_ref[...], wb_ref[...]) + _nn(oc_ref[...], wc_ref[...])
    h1 = _layer_norm(DEEPNORM_ALPHA * h_ref[...] + y, g_ref[...], b_ref[...])
    h1_ref[...] = h1
    h1b_ref[...] = h1.astype(BF16)


def _out_proj(oa, ob, oc, h, wa, wb, wc, g, b):
    tm = 512
    rows = lambda w: pl.BlockSpec((tm, w), lambda i: (i, 0))
    full = lambda a: pl.BlockSpec(a.shape, lambda i: (0, 0))
    return pl.pallas_call(
        _out_proj_kernel,
        out_shape=(jax.ShapeDtypeStruct((TOKENS, D_MODEL), F32),
                   jax.ShapeDtypeStruct((TOKENS, D_MODEL), BF16)),
        grid=(TOKENS // tm,),
        in_specs=[rows(oa.shape[1]), rows(ob.shape[1]), rows(oc.shape[1]), rows(D_MODEL),
                  full(wa), full(wb), full(wc), full(g), full(b)],
        out_specs=(rows(D_MODEL), rows(D_MODEL)),
        compiler_params=_params("parallel"),
        name="out_proj_ln",
    )(oa, ob, oc, h, wa, wb, wc, g, b)


def _router_kernel(hb_ref, rw_ref, rb_ref, comb_ref, sel_ref):
    logits = _nt(rw_ref[...], hb_ref[...]) + rb_ref[...]
    mx = jnp.max(logits, axis=0, keepdims=True)
    ex = jnp.exp(logits - mx)
    probs = ex / jnp.sum(ex, axis=0, keepdims=True)
    p = [probs[e:e + 1, :] for e in range(N_EXPERTS)]
    best, g_sel = None, None
    for g in range(N_GROUPS):
        a, b, c, d = p[4 * g:4 * g + 4]
        hi1, lo1, hi2, lo2 = jnp.maximum(a, b), jnp.minimum(a, b), jnp.maximum(c, d), jnp.minimum(c, d)
        top2 = jnp.maximum(hi1, hi2) + jnp.maximum(jnp.minimum(hi1, hi2), jnp.maximum(lo1, lo2))
        if g == 0:
            best, g_sel = top2, jnp.zeros_like(top2)
        else:
            better = top2 > best
            best = jnp.where(better, top2, best)
            g_sel = jnp.where(better, float(g), g_sel)
    chosen, picked = [], []
    for e in range(N_EXPERTS):
        g = e // EXPERTS_PER_GROUP
        rank = jnp.zeros_like(best)
        for o in range(4 * g, 4 * g + 4):
            if o < e:
                rank = rank + jnp.where(p[o] >= p[e], 1.0, 0.0)
            elif o > e:
                rank = rank + jnp.where(p[o] > p[e], 1.0, 0.0)
        chosen.append(jnp.where((g_sel == float(g)) & (rank < 2.0), 1.0, 0.0))
        picked.append(chosen[e] * p[e])
    total = picked[0]
    for e in range(1, N_EXPERTS):
        total = total + picked[e]
    comb_ref[...] = jnp.concatenate(picked, axis=0) / total
    sel_ref[...] = jnp.concatenate(chosen, axis=0)


def _router(hb, rw_t, rb):
    tm = 1024
    out = jax.ShapeDtypeStruct((N_EXPERTS, TOKENS), F32)
    o_spec = pl.BlockSpec((N_EXPERTS, tm), lambda i: (0, i))
    return pl.pallas_call(
        _router_kernel,
        out_shape=(out, out),
        grid=(TOKENS // tm,),
        in_specs=[pl.BlockSpec((tm, D_MODEL), lambda i: (i, 0)),
                  pl.BlockSpec((N_EXPERTS, D_MODEL), lambda i: (0, 0)),
                  pl.BlockSpec((N_EXPERTS, 1), lambda i: (0, 0))],
        out_specs=(o_spec, o_spec),
        compiler_params=_params("parallel"),
        name="router",
    )(hb, rw_t, rb)


def _routing_tables(comb_t, sel_t):
    sel = sel_t > 0.5
    cnt = jnp.sum(sel, axis=1, dtype=jnp.int32)
    cnt_pad = (cnt + (MOE_TILE - 1)) // MOE_TILE * MOE_TILE
    ends = jnp.cumsum(cnt_pad)
    rank = jnp.cumsum(sel.astype(jnp.int32), axis=1) - 1
    pos = (ends - cnt_pad)[:, None] + rank
    pos_lo = jnp.min(jnp.where(sel, pos, MOE_ROWS), axis=0)
    pos_hi = jnp.max(jnp.where(sel, pos, -1), axis=0)
    w_lo = jnp.sum(jnp.where(sel & (pos == pos_lo), comb_t, 0.0), axis=0)
    w_hi = jnp.sum(jnp.where(sel & (pos == pos_hi), comb_t, 0.0), axis=0)
    w = jnp.zeros((TOKENS, LANES), F32).at[:, 0].set(w_lo).at[:, 1].set(w_hi)
    n_tiles = ends[-1] // MOE_TILE
    tile_start = jnp.arange(MOE_TILES, dtype=jnp.int32) * MOE_TILE
    tile_start = jnp.minimum(tile_start, ends[-1] - MOE_TILE)
    tile_expert = jnp.sum((ends[None, :] <= tile_start[:, None]).astype(jnp.int32), axis=1)
    return jnp.stack([pos_lo, pos_hi]).astype(jnp.int32), w, tile_expert, n_tiles.reshape(1).astype(jnp.int32)


MOE_TILE = 256
MOE_TILES = 2 * TOKENS // MOE_TILE + N_EXPERTS
MOE_ROWS = MOE_TILES * MOE_TILE
SLAB = D_MODEL // LANES


def _to_slabs(ref, x, rows):
    for j in range(SLAB):
        ref[pl.ds(j, rows, stride=SLAB), :] = x[:, j * LANES:(j + 1) * LANES]


def _from_slabs(ref, rows, pitch=SLAB):
    return jnp.concatenate([ref[pl.ds(j, rows, stride=pitch), :] for j in range(SLAB)], axis=1)


GATHER_PITCH = SLAB + 8


def _slab_rows(row, n=SLAB):
    return pl.ds(pl.multiple_of(row * n, n), n)


XSLAB = SLAB // 2
U32 = jnp.uint32


def _to_packed_slabs(ref, x, rows):
    bits = lambda t: lax.bitcast_convert_type(t.astype(BF16).astype(F32), U32)
    for j in range(XSLAB):
        hi = bits(x[:, 2 * j * LANES:(2 * j + 1) * LANES])
        lo = bits(x[:, (2 * j + 1) * LANES:(2 * j + 2) * LANES])
        ref[pl.ds(j, rows, stride=XSLAB), :] = hi | (lo >> 16)


def _from_packed_slabs(ref, rows):
    parts = []
    for j in range(XSLAB):
        u = ref[pl.ds(j, rows, stride=XSLAB), :]
        parts.append(lax.bitcast_convert_type(u & jnp.uint32(0xFFFF0000), F32).astype(BF16))
        parts.append(lax.bitcast_convert_type(u << 16, F32).astype(BF16))
    return jnp.concatenate(parts, axis=1)


def _dispatch_kernel(pos_ref, h_ref, init_ref, xs_ref, slab_ref, sem):
    del init_ref
    tm = h_ref.shape[0]
    base = pl.program_id(0) * tm
    _to_packed_slabs(slab_ref, h_ref[...], tm)

    def copy(t, which):
        return pltpu.make_async_copy(slab_ref.at[_slab_rows(t, XSLAB), :],
                                     xs_ref.at[_slab_rows(pos_ref[which, base + t], XSLAB), :], sem)

    def start(t, _):
        copy(t, 0).start(priority=0)
        copy(t, 1).start(priority=1)
        return 0

    lax.fori_loop(0, tm, start, 0, unroll=8)
    whole = pltpu.make_async_copy(slab_ref, xs_ref.at[pl.ds(0, tm * XSLAB), :], sem)
    whole.wait()
    whole.wait()


def _dispatch(pos, h):
    tm = 256
    grid_spec = pltpu.PrefetchScalarGridSpec(
        num_scalar_prefetch=1,
        grid=(TOKENS // tm,),
        in_specs=[pl.BlockSpec((tm, D_MODEL), lambda i, pos: (i, 0)),
                  pl.BlockSpec(memory_space=pl.ANY)],
        out_specs=pl.BlockSpec(memory_space=pl.ANY),
        scratch_shapes=[pltpu.VMEM((tm * XSLAB, LANES), U32), pltpu.SemaphoreType.DMA],
    )
    return pl.pallas_call(
        _dispatch_kernel,
        out_shape=jax.ShapeDtypeStruct((MOE_ROWS * XSLAB, LANES), U32),
        grid_spec=grid_spec,
        input_output_aliases={2: 0},
        compiler_params=_params("arbitrary"),
        name="moe_dispatch",
    )(pos, h, jnp.zeros((MOE_ROWS * XSLAB, LANES), U32))


def _experts_kernel(te_ref, nt_ref, xs_ref, wg_ref, wu_ref, wd_ref, ys_ref, wgb_ref, wub_ref, wdb_ref):
    k = pl.program_id(0)
    e = te_ref[k]
    e_prev = te_ref[jnp.maximum(k - 1, 0)]

    @pl.when((k == 0) | (e != e_prev))
    def _():
        wgb_ref[...] = wg_ref[0, 0].astype(BF16)
        wub_ref[...] = wu_ref[0, 0].astype(BF16)
        wdb_ref[...] = wd_ref[0, 0].astype(BF16)

    @pl.when(k < nt_ref[0])
    def _():
        x = _from_packed_slabs(xs_ref, MOE_TILE)
        hid = jax.nn.silu(_nn(x, wgb_ref[...])) * _nn(x, wub_ref[...])
        _to_slabs(ys_ref, _nn(hid.astype(BF16), wdb_ref[...]), MOE_TILE)

    @pl.when(k >= nt_ref[0])
    def _():
        ys_ref[...] = jnp.zeros(ys_ref.shape, F32)


def _experts(tile_expert, n_tiles, xs, wg, wu, wd, layer):
    w_in_spec = pl.BlockSpec((1, 1, D_MODEL, EXPERT_HIDDEN), lambda k, te, nt: (layer, te[k], 0, 0))
    grid_spec = pltpu.PrefetchScalarGridSpec(
        num_scalar_prefetch=2,
        grid=(MOE_TILES,),
        in_specs=[pl.BlockSpec((MOE_TILE * XSLAB, LANES), lambda k, te, nt: (jnp.minimum(k, nt[0] - 1), 0)),
                  w_in_spec, w_in_spec,
                  pl.BlockSpec((1, 1, EXPERT_HIDDEN, D_MODEL), lambda k, te, nt: (layer, te[k], 0, 0))],
        out_specs=pl.BlockSpec((MOE_TILE * SLAB, LANES), lambda k, te, nt: (k, 0)),
        scratch_shapes=[pltpu.VMEM((D_MODEL, EXPERT_HIDDEN), BF16), pltpu.VMEM((D_MODEL, EXPERT_HIDDEN), BF16),
                        pltpu.VMEM((EXPERT_HIDDEN, D_MODEL), BF16)],
    )
    return pl.pallas_call(
        _experts_kernel,
        out_shape=jax.ShapeDtypeStruct((MOE_ROWS * SLAB, LANES), F32),
        grid_spec=grid_spec,
        compiler_params=_params("arbitrary"),
        name="moe_experts",
    )(tile_expert, n_tiles, xs, wg, wu, wd)


def _ple_ln_kernel(pos_ref, hb_ref, h_ref, ys_ref, w_ref, p_ref, gw_ref, gb_ref, pw_ref, g_ref, b_ref,
                   h2_ref, h2b_ref, lo_ref, hi_ref, sem):
    tm = h_ref.shape[0]
    i = pl.program_id(0)
    slot = i & 1
    bufs = (lo_ref, hi_ref)

    def fetch(tile, into):
        def start(t, _):
            for which in range(2):
                pltpu.make_async_copy(ys_ref.at[_slab_rows(pos_ref[which, tile * tm + t]), :],
                                      bufs[which].at[into, pl.ds(pl.multiple_of(t * GATHER_PITCH, 8), SLAB), :],
                                      sem.at[into]).start(priority=which)
            return 0
        lax.fori_loop(0, tm, start, 0, unroll=8)

    @pl.when(i == 0)
    def _():
        fetch(0, 0)

    @pl.when(i + 1 < pl.num_programs(0))
    def _():
        fetch(i + 1, 1 - slot)

    gate = jax.nn.sigmoid(_nn(hb_ref[...], gw_ref[...]) + gb_ref[...])
    ple = gate * _nn(p_ref[...].astype(BF16), pw_ref[...])
    for which in range(2):
        pltpu.make_async_copy(ys_ref.at[pl.ds(0, tm * SLAB), :], bufs[which].at[slot, pl.ds(0, tm * SLAB), :],
                              sem.at[slot]).wait()
    w = w_ref[...]
    ffn = (w[:, 0:1] * _from_slabs(lo_ref.at[slot], tm, GATHER_PITCH)
           + w[:, 1:2] * _from_slabs(hi_ref.at[slot], tm, GATHER_PITCH))
    h2 = _layer_norm(DEEPNORM_ALPHA * h_ref[...] + ffn + ple, g_ref[...], b_ref[...])
    h2_ref[...] = h2
    h2b_ref[...] = h2.astype(BF16)


def _ple_ln(pos, hb, h, ys, w, p, layer, gw, gb, pw, g, b):
    tm = 256
    p_spec = pl.BlockSpec((tm, PLE_DIM), lambda i, pos: (layer * (TOKENS // tm) + i, 0))
    rows = lambda width: pl.BlockSpec((tm, width), lambda i, pos: (i, 0))
    full = lambda a: pl.BlockSpec(a.shape, lambda i, pos: (0, 0))
    grid_spec = pltpu.PrefetchScalarGridSpec(
        num_scalar_prefetch=1,
        grid=(TOKENS // tm,),
        in_specs=[rows(D_MODEL), rows(D_MODEL), pl.BlockSpec(memory_space=pl.ANY), rows(LANES), p_spec,
                  full(gw), full(gb), full(pw), full(g), full(b)],
        out_specs=(rows(D_MODEL), rows(D_MODEL)),
        scratch_shapes=[pltpu.VMEM((2, tm * GATHER_PITCH, LANES), F32), pltpu.VMEM((2, tm * GATHER_PITCH, LANES), F32),
                        pltpu.SemaphoreType.DMA((2,))],
    )
    return pl.pallas_call(
        _ple_ln_kernel,
        out_shape=(jax.ShapeDtypeStruct((TOKENS, D_MODEL), F32),
                   jax.ShapeDtypeStruct((TOKENS, D_MODEL), BF16)),
        grid_spec=grid_spec,
        compiler_params=_params("arbitrary"),
        name="ple_ln",
    )(pos, hb, h, ys, w, p, gw, gb, pw, g, b)


def _rope_tables(positions):
    half = ROT_DIM // 2
    inv_freq = jnp.exp(jnp.arange(half, dtype=F32) * (-2.0 * math.log(ROPE_THETA) / ROT_DIM))
    ang = positions.astype(F32)[:, :, None] * inv_freq
    cos, sin = jnp.cos(ang), jnp.sin(ang)
    zeros = jnp.zeros_like(cos)
    rest = HEAD_DIM - ROT_DIM
    pad = lambda v: jnp.broadcast_to(jnp.asarray(v, F32), cos.shape[:2] + (rest,))
    c = jnp.concatenate([cos, cos, pad(1.0)], axis=-1)
    s1 = jnp.concatenate([-sin, zeros, pad(0.0)], axis=-1)
    s2 = jnp.concatenate([zeros, sin, pad(0.0)], axis=-1)
    tile = lambda t: jnp.concatenate([t, t], axis=-1).reshape(TOKENS, LANES)
    return tile(c), tile(s1), tile(s2)


def _split_w_in(w):
    mw, nq, nkv, dw = MOBA_HEADS * HEAD_DIM, NSA_HEADS * HEAD_DIM, NSA_KV_HEADS * HEAD_DIM, DIL_HEADS * HEAD_DIM
    widths = (mw, mw, mw, nq) + (nkv,) * 6 + (NSA_HEADS * 3, dw, dw, dw)
    offs = np.concatenate([[0], np.cumsum(widths)])
    qa, ka, va, qb, kbc, vbc, kbs, vbs, kbw, vbw, gb, qc, kc, vc = (
        w[:, int(offs[i]):int(offs[i + 1])] for i in range(len(widths)))

    def dup(t):
        t = t.reshape(D_MODEL, NSA_KV_HEADS, 1, HEAD_DIM)
        return jnp.broadcast_to(t, (D_MODEL, NSA_KV_HEADS, 2, HEAD_DIM)).reshape(D_MODEL, NSA_KV_HEADS * LANES)

    zpad = lambda n: jnp.zeros((D_MODEL, n * LANES), w.dtype)
    w_rot = jnp.concatenate([qa * Q_SCALE, ka, qb * Q_SCALE, dup(kbc), dup(kbs), dup(kbw), zpad(1)], axis=1)
    w_pl = jnp.concatenate([va, dup(vbc), dup(vbs), dup(vbw), zpad(3)], axis=1)
    gpad = jnp.zeros((D_MODEL, NSA_KV_HEADS, LANES - 12), w.dtype)
    w_gl = jnp.concatenate([gb.reshape(D_MODEL, NSA_KV_HEADS, 12), gpad], axis=-1).reshape(D_MODEL, -1)
    w_dil_rot = jnp.concatenate([qc * Q_SCALE, kc], axis=1)
    return tuple(t.astype(BF16) for t in (w_rot, w_pl, w_gl, w_dil_rot, vc))


def _overlap_table():
    starts = np.arange(N_CMP) * CMP_STRIDE
    slc = np.arange(N_SLC) * SLC_BLOCK
    ov = ((starts[:, None] < slc[None, :] + SLC_BLOCK) & (starts[:, None] + CMP_LEN > slc[None, :]))
    ovt = np.zeros((N_SLC, N_CMP_PAD), np.float32)
    ovt[:, :N_CMP] = ov.T
    return jnp.asarray(ovt, BF16)


def _cmp_chunks(z, base):
    nblk = z.shape[-1] // LANES
    t = z.reshape(BATCH, SEQ // CMP_STRIDE, CMP_STRIDE, nblk, LANES)[:, :, :, base:base + NSA_KV_HEADS, :HEAD_DIM]
    return t.transpose(0, 3, 1, 2, 4).reshape(BATCH, NSA_KV_HEADS, SEQ // CMP_STRIDE, CMP_STRIDE * HEAD_DIM)


def kernel(x, p, positions, ln_in_g, ln_in_b, w_in, w_out, nsa_ck1, nsa_ck2, nsa_pe_k, nsa_cv1, nsa_cv2, nsa_pe_v, ln1_g, ln1_b, router_w, router_b, w_gate, w_up, w_down, ple_proj, ple_gate_w, ple_gate_b, ln2_g, ln2_b):
    rope = _rope_tables(positions)
    ovt = _overlap_table()
    rw_t = router_w.T.astype(BF16)
    rb = router_b.reshape(N_EXPERTS, 1).astype(F32)
    chunk_w = CMP_STRIDE * HEAD_DIM
    vec = lambda v: v.reshape(1, -1)
    seq3 = lambda t: t.reshape(BATCH, SEQ, t.shape[-1])
    flat = lambda t: t.reshape(TOKENS, t.shape[-1])

    h, hb = _ln_in(x.reshape(TOKENS, D_MODEL), ln_in_g, ln_in_b)
    for i in range(DEPTH):
        w_rot, w_pl, w_gl, w_dil_rot, w_dil_pl = _split_w_in(w_in[i])
        z_rot = seq3(_project(hb, w_rot, BF16, 768, rope=rope))
        z_pl = seq3(_project(hb, w_pl, BF16, 1024))
        gate_logits = _project(hb, w_gl, F32, NSA_KV_HEADS * LANES)
        zd_rot = seq3(_project(hb, w_dil_rot, F32, 768, rope=rope))
        zd_pl = seq3(_project(hb, w_dil_pl, F32, 768))

        o_a = _moba(z_rot, z_pl)

        dup2 = lambda w2: jnp.concatenate([w2, w2], axis=1).astype(BF16)
        k_cmp, v_cmp = _compress(
            _cmp_chunks(z_rot, ROT_NKC), _cmp_chunks(z_pl, PL_NVC),
            nsa_pe_k[i].reshape(2, chunk_w), nsa_pe_v[i].reshape(2, chunk_w),
            nsa_ck1[i].reshape(2, chunk_w, CMP_HIDDEN).astype(BF16), dup2(nsa_ck2[i]),
            nsa_cv1[i].reshape(2, chunk_w, CMP_HIDDEN).astype(BF16), dup2(nsa_cv2[i]))
        o_b = _nsa(z_rot, z_pl, k_cmp, v_cmp, gate_logits, ovt)

        o_c = _dilated(zd_rot, zd_pl)

        wo = w_out[i].astype(BF16)
        a_w, b_w = MOBA_HEADS * HEAD_DIM, NSA_HEADS * HEAD_DIM
        h, hb = _out_proj(flat(o_a), flat(o_b), flat(o_c), h,
                          wo[:a_w], wo[a_w:a_w + b_w], wo[a_w + b_w:], vec(ln1_g[i]), vec(ln1_b[i]))

        pos, w_tok, tile_expert, n_tiles = _routing_tables(*_router(hb, rw_t, rb))
        xs = _dispatch(pos, h)
        ys = _experts(tile_expert, n_tiles, xs, w_gate, w_up, w_down, i)
        h, hb = _ple_ln(pos, hb, h, ys, w_tok, p.reshape(DEPTH * TOKENS, PLE_DIM), i, ple_gate_w[i].astype(BF16),
                        vec(ple_gate_b[i]), ple_proj[i].astype(BF16), vec(ln2_g[i]), vec(ln2_b[i]))
    return h.reshape(BATCH, SEQ, D_MODEL)
```

```python
import functools
import math

import numpy as np
import jax
import jax.numpy as jnp
from jax import lax
from jax.experimental import pallas as pl
from jax.experimental.pallas import tpu as pltpu

F32 = jnp.float32
BF16 = jnp.bfloat16

D_MODEL = 2048
BATCH = 2
SEQ = 4096
DEPTH = 4
TOKENS = BATCH * SEQ
HEAD_DIM = 64
ROT_DIM = HEAD_DIM // 4
ROPE_THETA = 500000.0
NEG = -1e30
FORCE = 1e30
LN_EPS = 1e-5
SCALE = HEAD_DIM ** -0.5
LOG2_E = math.log2(math.e)
Q_SCALE = SCALE * LOG2_E

MOBA_HEADS = 8
MOBA_BLOCK = 256
MOBA_TOPK = 3
MOBA_NB = SEQ // MOBA_BLOCK

NSA_HEADS = 12
NSA_KV_HEADS = 3
CMP_LEN = 32
CMP_STRIDE = 16
CMP_HIDDEN = 128
N_CMP = (SEQ - CMP_LEN) // CMP_STRIDE + 1
N_CMP_PAD = 256
SLC_BLOCK = 64
SLC_TOPK = 16
SLC_LOCAL = 2
N_SLC = SEQ // SLC_BLOCK
NSA_WINDOW = 512

DIL_CONFIGS = ((128, 1), (512, 4), (2048, 16))
DIL_HEADS_PER_GROUP = 4
DIL_HEADS = DIL_HEADS_PER_GROUP * len(DIL_CONFIGS)

N_EXPERTS = 16
N_GROUPS = 4
EXPERTS_PER_GROUP = 4
EXPERT_HIDDEN = D_MODEL // 4
PLE_DIM = 256

DEEPNORM_ALPHA = (2 * DEPTH) ** 0.25

LANES = 128
VMEM_LIMIT = 56 * 1024 * 1024

ROT_MQ, ROT_MK, ROT_NQ, ROT_NKC, ROT_NKS, ROT_NKW = 0, 4, 8, 14, 17, 20
ROT_BLOCKS = 24
PL_MV, PL_NVC, PL_NVS, PL_NVW = 0, 4, 7, 10
PL_BLOCKS = 16
DIL_BLOCKS = DIL_HEADS // 2

NT_DIMS = (((1,), (1,)), ((), ()))


def _nt(a, b):
    return lax.dot_general(a, b, NT_DIMS, preferred_element_type=F32)


def _nn(a, b):
    return jnp.dot(a, b, preferred_element_type=F32)


def _params(*sem):
    return pltpu.CompilerParams(dimension_semantics=sem, vmem_limit_bytes=VMEM_LIMIT)


def _layer_norm(y, g, b):
    mu = jnp.mean(y, axis=-1, keepdims=True)
    yc = y - mu
    var = jnp.mean(yc * yc, axis=-1, keepdims=True)
    return yc * lax.rsqrt(var + LN_EPS) * g + b


def _ln_kernel(x_ref, g_ref, b_ref, h_ref, hb_ref):
    h = _layer_norm(x_ref[...], g_ref[...], b_ref[...])
    h_ref[...] = h
    hb_ref[...] = h.astype(BF16)


def _ln_in(x, g, b):
    tm = 512
    row = pl.BlockSpec((tm, D_MODEL), lambda i: (i, 0))
    vec = pl.BlockSpec((1, D_MODEL), lambda i: (0, 0))
    return pl.pallas_call(
        _ln_kernel,
        out_shape=(jax.ShapeDtypeStruct((TOKENS, D_MODEL), F32),
                   jax.ShapeDtypeStruct((TOKENS, D_MODEL), BF16)),
        grid=(TOKENS // tm,),
        in_specs=[row, vec, vec],
        out_specs=(row, row),
        compiler_params=_params("parallel"),
        name="ln_in",
    )(x, g.reshape(1, -1), b.reshape(1, -1))


def _proj_kernel(x_ref, w_ref, o_ref):
    o_ref[...] = _nn(x_ref[...], w_ref[...]).astype(o_ref.dtype)


def _proj_rot_kernel(x_ref, w_ref, c_ref, s1_ref, s2_ref, o_ref):
    x = x_ref[...]
    c, s1, s2 = c_ref[...], s1_ref[...], s2_ref[...]
    half = ROT_DIM // 2
    for j0 in range(0, o_ref.shape[1], 2 * LANES):
        z = _nn(x, w_ref[:, j0:j0 + 2 * LANES])
        for j in range(j0, j0 + 2 * LANES, LANES):
            zc = z[:, j - j0:j - j0 + LANES]
            r = zc * c + pltpu.roll(zc, LANES - half, 1) * s1 + pltpu.roll(zc, half, 1) * s2
            o_ref[:, j:j + LANES] = r.astype(o_ref.dtype)


def _project(hb, w, out_dtype, tn, rope=None):
    tm = 1024
    n = w.shape[1]
    x_spec = pl.BlockSpec((tm, D_MODEL), lambda i, j: (i, 0))
    w_spec = pl.BlockSpec((D_MODEL, tn), lambda i, j: (0, j))
    o_spec = pl.BlockSpec((tm, tn), lambda i, j: (i, j))
    if rope is None:
        kern, extra, extra_specs = _proj_kernel, (), []
    else:
        t_spec = pl.BlockSpec((tm, LANES), lambda i, j: (i, 0))
        kern, extra, extra_specs = _proj_rot_kernel, rope, [t_spec] * 3
    return pl.pallas_call(
        kern,
        out_shape=jax.ShapeDtypeStruct((TOKENS, n), out_dtype),
        grid=(TOKENS // tm, n // tn),
        in_specs=[x_spec, w_spec] + extra_specs,
        out_specs=o_spec,
        compiler_params=_params("parallel", "arbitrary"),
        name="in_proj_rot" if rope is not None else "in_proj",
    )(hb, w, *extra)


def _stack_heads(*q_blocks):
    parts = []
    for q in q_blocks:
        lane = lax.broadcasted_iota(jnp.int32, q.shape, 1)
        zero = jnp.zeros_like(q)
        parts += [jnp.where(lane < HEAD_DIM, q, zero), jnp.where(lane >= HEAD_DIM, q, zero)]
    return jnp.concatenate(parts, axis=0)


def _merge_pair_t(lo, hi):
    sub = lax.broadcasted_iota(jnp.int32, lo.shape, 0)
    return jnp.where(sub < HEAD_DIM, lo, hi)


def _band_bias_t(nk, qc, offset, n_back):
    key = lax.broadcasted_iota(jnp.int32, (nk, qc), 0)
    qry = lax.broadcasted_iota(jnp.int32, (nk, qc), 1)
    diff = offset + qry - key
    return jnp.where((diff >= 0) & (diff <= n_back), 0.0, NEG)


def _tile_lanes(x, n):
    return jnp.concatenate([x] * n, axis=1)


def _transpose_bf16(v):
    return jnp.transpose(v.astype(F32)).astype(BF16)


def _tree(x, op):
    n = x.shape[0]
    if n == 8:
        return x
    if n % 16 == 0:
        return op(_tree(x[:n // 2], op), _tree(x[n // 2:], op))
    acc = x[:8]
    for i in range(1, n // 8):
        acc = op(acc, x[8 * i:8 * i + 8])
    return acc


def _reduce_keys(x, op, final):
    return final(_tree(x, op), axis=0, keepdims=True)


VT_ROWS = LANES + 16


def _transpose_aug(v):
    vt = jnp.transpose(v.astype(F32))
    sub = lax.broadcasted_iota(jnp.int32, (VT_ROWS - LANES, v.shape[0]), 0)
    return jnp.concatenate([vt, jnp.where(sub == 0, 1.0, 0.0)], axis=0).astype(BF16)


def _probs(s_t, m):
    return jnp.exp2((s_t - m).astype(BF16))


def _normalise(acc):
    l = acc[LANES:LANES + 1]
    return acc[:LANES] / l, l


def _softmax_block_t(s_t, pv):
    m = _reduce_keys(s_t, jnp.maximum, jnp.max)
    out, l = _normalise(pv(_probs(s_t, m)))
    return out, m + jnp.log(l) * LOG2_E


def _online_step_t(carry, s_t, m_t, pv):
    m, acc = carry
    m_new = jnp.maximum(m, m_t)
    acc = jnp.exp2(m - m_new) * acc + pv(_probs(s_t, m_new))
    return m_new, acc


def _flash_tiles(n_tiles, init, scores, pv_of, may_be_empty=False):
    strips = range(len(init))

    def produce(t):
        s = tuple(scores(t))
        return s, tuple(_reduce_keys(s_i, jnp.maximum, jnp.max) for s_i in s)

    def consume(state, s_t, m_t, t):
        pv = pv_of(t)
        return tuple(_online_step_t(state[i], s_t[i], m_t[i], pv) for i in strips)

    def body(t, carry):
        state, s_t, m_t = carry
        s_next, m_next = produce(t + 1)
        return consume(state, s_t, m_t, t), s_next, m_next

    state, s_last, m_last = lax.fori_loop(0, n_tiles - 1, body, (tuple(init),) + produce(0))
    done = consume(state, s_last, m_last, jnp.maximum(n_tiles - 1, 0))
    if may_be_empty:
        done = jax.tree_util.tree_map(lambda new, old: jnp.where(n_tiles > 0, new, old), done, state)
    return done


def _pv_tiles(vt_ref, first, n, rows):
    def pv(p):
        acc = _nn(vt_ref[first], p[:rows])
        for j in range(1, n):
            acc = acc + _nn(vt_ref[first + j], p[j * rows:(j + 1) * rows])
        return acc
    return pv


def _online_init_t(r):
    return (jnp.full((1, r), NEG, F32), jnp.zeros((VT_ROWS, r), F32))


def _rank_rows(g, n_rows):
    sub = lax.broadcasted_iota(jnp.int32, (8, g.shape[1]), 0)
    rank = jnp.zeros(g.shape, F32)
    for m in range(n_rows):
        gm = g[m:m + 1, :]
        b = m // 8 * 8
        mid = g[b:b + 8]
        parts = [jnp.where(gm > mid, 1.0, jnp.where((gm == mid) & (sub > m - b), 1.0, 0.0))]
        if b > 0:
            parts.insert(0, jnp.where(gm > g[:b], 1.0, 0.0))
        if b + 8 < n_rows:
            parts.append(jnp.where(gm >= g[b + 8:], 1.0, 0.0))
        rank = rank + jnp.concatenate(parts, axis=0)
    return rank


MOBA_KT = 2 * MOBA_BLOCK
MOBA_QC = MOBA_KT


def _moba_kernel(q_ref, k_ref, v_ref, o_ref, kmean_ref, vt_ref, bias_ref):
    c = pl.program_id(2)
    qc = MOBA_QC
    r = 2 * qc

    @pl.when(c == 0)
    def _():
        row = lax.broadcasted_iota(jnp.int32, (MOBA_NB, SEQ), 0)
        col = lax.broadcasted_iota(jnp.int32, (MOBA_NB, SEQ), 1)
        avg = jnp.where((col >> 8) == row, 1.0 / MOBA_BLOCK, 0.0).astype(BF16)
        kmean_ref[...] = _nn(avg, k_ref[0])
        for t in range(MOBA_NB):
            vt_ref[t] = _transpose_aug(v_ref[0, t * MOBA_BLOCK:(t + 1) * MOBA_BLOCK, :])

    qs = _stack_heads(q_ref[0])
    per_tile = MOBA_KT // MOBA_BLOCK

    ks = pl.multiple_of(c * MOBA_KT, MOBA_KT)
    heads = [qs[:qc], qs[qc:]]
    raw_own = [_nt(k_ref[0, pl.ds(ks, MOBA_KT), :], q_h) for q_h in heads]

    gate = _nt(kmean_ref[...].astype(BF16), qs)
    blk = lax.broadcasted_iota(jnp.int32, gate.shape, 0)
    q_idx = lax.broadcasted_iota(jnp.int32, gate.shape, 1) & (qc - 1)
    past = blk < c * per_tile + (q_idx >> 8)
    rank = _rank_rows(jnp.where(past, gate, NEG), MOBA_NB)
    bias_ref[...] = jnp.where(past & (rank < MOBA_TOPK), 0.0, NEG)

    key = lax.broadcasted_iota(jnp.int32, (MOBA_KT, qc), 0)
    qry = lax.broadcasted_iota(jnp.int32, (MOBA_KT, qc), 1)
    first_block = bias_ref[pl.ds(c * per_tile, 1), :]
    state = []
    for h in range(2):
        other = jnp.broadcast_to(first_block[:, h * qc:(h + 1) * qc], (MOBA_KT, qc))
        own_bias = jnp.where(key <= qry, jnp.where((key >> 8) == (qry >> 8), 0.0, other), NEG)
        s_h = raw_own[h] + own_bias
        state.append(_online_step_t(_online_init_t(qc), s_h, _reduce_keys(s_h, jnp.maximum, jnp.max),
                                    _pv_tiles(vt_ref, c * per_tile, per_tile, MOBA_BLOCK)))

    def scores(t):
        ks = pl.multiple_of(t * MOBA_KT, MOBA_KT)
        kt = k_ref[0, pl.ds(ks, MOBA_KT), :]
        rows = [bias_ref[pl.ds(t * per_tile + j, 1), :] for j in range(per_tile)]
        out = []
        for h in range(2):
            blocks = [jnp.broadcast_to(row[:, h * qc:(h + 1) * qc], (MOBA_BLOCK, qc)) for row in rows]
            out.append(_nt(kt, heads[h]) + jnp.concatenate(blocks, axis=0))
        return out

    state = _flash_tiles(c, state, scores,
                         lambda t: _pv_tiles(vt_ref, t * per_tile, per_tile, MOBA_BLOCK), may_be_empty=True)
    o_lo, o_hi = (_normalise(acc)[0] for _, acc in state)
    o_ref[0] = jnp.transpose(_merge_pair_t(o_lo, o_hi)).astype(o_ref.dtype)


def _moba(z_rot, z_pl):
    qc = MOBA_QC
    grid = (BATCH, MOBA_HEADS // 2, SEQ // qc)
    return pl.pallas_call(
        _moba_kernel,
        out_shape=jax.ShapeDtypeStruct((BATCH, SEQ, MOBA_HEADS * HEAD_DIM), BF16),
        grid=grid,
        in_specs=[
            pl.BlockSpec((1, qc, LANES), lambda b, p, c: (b, c, ROT_MQ + p)),
            pl.BlockSpec((1, SEQ, LANES), lambda b, p, c: (b, 0, ROT_MK + p)),
            pl.BlockSpec((1, SEQ, LANES), lambda b, p, c: (b, 0, PL_MV + p)),
        ],
        out_specs=pl.BlockSpec((1, qc, LANES), lambda b, p, c: (b, c, p)),
        scratch_shapes=[pltpu.VMEM((MOBA_NB, LANES), F32),
                        pltpu.VMEM((MOBA_NB, VT_ROWS, MOBA_BLOCK), BF16),
                        pltpu.VMEM((MOBA_NB, 2 * qc), F32)],
        compiler_params=_params("parallel", "parallel", "arbitrary"),
        name="moba",
    )(z_rot, z_rot, z_pl)


def _compress_one(x_ref, pe_ref, w1_ref, w2_ref, o_ref):
    x = x_ref[0, 0].astype(F32)
    top = (x + pe_ref[0:1, :]).astype(BF16)
    bot = (x + pe_ref[1:2, :]).astype(BF16)
    a = _nn(top, w1_ref[0])
    bm = _nn(bot, w1_ref[1])
    pre = a + pltpu.roll(bm, N_CMP_PAD - 1, 0)
    hid = jax.nn.gelu(pre)
    out = _nn(hid.astype(BF16), w2_ref[...])
    row = lax.broadcasted_iota(jnp.int32, out.shape, 0)
    o_ref[0, 0] = jnp.where(row < N_CMP, out, 0.0).astype(o_ref.dtype)


def _compress_kernel(xk_ref, xv_ref, pk_ref, pv_ref, k1_ref, k2_ref, v1_ref, v2_ref, ok_ref, ov_ref):
    _compress_one(xk_ref, pk_ref, k1_ref, k2_ref, ok_ref)
    _compress_one(xv_ref, pv_ref, v1_ref, v2_ref, ov_ref)


def _compress(xk, xv, pk, pv, k1, k2, v1, v2):
    chunk_w = CMP_STRIDE * HEAD_DIM
    x_spec = pl.BlockSpec((1, 1, N_CMP_PAD, chunk_w), lambda b, j: (b, j, 0, 0))
    pe_spec = pl.BlockSpec((2, chunk_w), lambda b, j: (0, 0))
    w1_spec = pl.BlockSpec((2, chunk_w, CMP_HIDDEN), lambda b, j: (0, 0, 0))
    w2_spec = pl.BlockSpec((CMP_HIDDEN, LANES), lambda b, j: (0, 0))
    o_spec = pl.BlockSpec((1, 1, N_CMP_PAD, LANES), lambda b, j: (b, j, 0, 0))
    o_shape = jax.ShapeDtypeStruct((BATCH, NSA_KV_HEADS, N_CMP_PAD, LANES), BF16)
    return pl.pallas_call(
        _compress_kernel,
        out_shape=(o_shape, o_shape),
        grid=(BATCH, NSA_KV_HEADS),
        in_specs=[x_spec, x_spec, pe_spec, pe_spec, w1_spec, w2_spec, w1_spec, w2_spec],
        out_specs=(o_spec, o_spec),
        compiler_params=_params("parallel", "parallel"),
        name="nsa_compress",
    )(xk, xv, pk, pv, k1, k2, v1, v2)


NSA_QC = 512
NSA_KT = 512
NSA_G = NSA_HEADS // NSA_KV_HEADS
NSA_WIN_TILES = NSA_WINDOW // NSA_QC + 1


def _nsa_kernel(qa_ref, qb_ref, kc_ref, vc_ref, ks_ref, vs_ref, kw_ref, vw_ref, gl_ref, ovt_ref,
                o_ref, vct_ref, vst_ref, vwt_ref, bias_ref):
    c = pl.program_id(2)
    qc = NSA_QC
    q0 = c * qc
    lanes_of = lambda t, i: t[:, i * qc:(i + 1) * qc]

    @pl.when(c == 0)
    def _():
        vct_ref[...] = _transpose_bf16(vc_ref[0, 0])
        for t in range(SEQ // NSA_KT):
            vst_ref[t] = _transpose_aug(vs_ref[0, t * NSA_KT:(t + 1) * NSA_KT, :])
        for t in range(SEQ // qc):
            vwt_ref[t] = _transpose_aug(vw_ref[0, t * qc:(t + 1) * qc, :])

    qs = _stack_heads(qa_ref[0], qb_ref[0])

    sc_t = _nt(kc_ref[0, 0], qs)
    t0 = jnp.maximum(c - NSA_WINDOW // qc, 0)
    start = pl.multiple_of(t0 * qc, qc)
    sw_t = _nt(kw_ref[0, pl.ds(start, NSA_WIN_TILES * qc), :], qs)

    n_idx = lax.broadcasted_iota(jnp.int32, (N_CMP_PAD, qc), 0)
    q_idx = lax.broadcasted_iota(jnp.int32, (N_CMP_PAD, qc), 1)
    ok = (n_idx * CMP_STRIDE + (CMP_LEN - 1)) <= (q0 + q_idx)
    p_heads = []
    for i in range(NSA_G):
        s_i = jnp.where(ok, lanes_of(sc_t, i), NEG)
        e_i = jnp.where(ok, jnp.exp2(s_i - _reduce_keys(s_i, jnp.maximum, jnp.max)), 0.0)
        l_i = _reduce_keys(e_i, jnp.add, jnp.sum)
        p_heads.append((e_i / jnp.where(l_i > 0.0, l_i, 1.0)).astype(BF16))
    p_ct = jnp.concatenate(p_heads, axis=1)
    imp4 = _nn(ovt_ref[...], p_ct)
    ocmp_t = _nn(vct_ref[...], p_ct)

    band = _band_bias_t(NSA_WIN_TILES * qc, qc, q0 - start, NSA_WINDOW - 1)
    owin_t, _ = _softmax_block_t(sw_t + _tile_lanes(band, NSA_G), _pv_tiles(vwt_ref, t0, NSA_WIN_TILES, qc))
    gate_t = jnp.transpose(jax.nn.sigmoid(gl_ref[...]))
    gate = lambda i, r: gate_t[3 * i + r:3 * i + r + 1, :]
    partial_out = [gate(i, 0) * lanes_of(ocmp_t, i) + gate(i, 2) * lanes_of(owin_t, i) for i in range(NSA_G)]

    imp = lanes_of(imp4, 0) + lanes_of(imp4, 1) + lanes_of(imp4, 2) + lanes_of(imp4, 3)
    blk = lax.broadcasted_iota(jnp.int32, imp.shape, 0)
    cur = (q0 + lax.broadcasted_iota(jnp.int32, imp.shape, 1)) >> 6
    valid = blk <= cur
    forced = valid & ((blk == 0) | (blk > cur - SLC_LOCAL))
    rank = _rank_rows(jnp.where(forced, FORCE, jnp.where(valid, imp, NEG)), N_SLC)
    bias_ref[...] = jnp.where(valid & (rank < SLC_TOPK), 0.0, NEG)

    key_row = lax.broadcasted_iota(jnp.int32, (NSA_KT, qc), 0)
    qpos = lax.broadcasted_iota(jnp.int32, (NSA_KT, qc), 1) + q0
    per_tile = NSA_KT // SLC_BLOCK

    def scores(t):
        ks0 = pl.multiple_of(t * NSA_KT, NSA_KT)
        blocks = [jnp.broadcast_to(bias_ref[pl.ds(t * per_tile + j, 1), :], (SLC_BLOCK, qc))
                  for j in range(per_tile)]
        bias = jnp.where(key_row + ks0 <= qpos, jnp.concatenate(blocks, axis=0), NEG)
        kt = ks_ref[0, pl.ds(ks0, NSA_KT), :]
        return [_nt(kt, qs[i * qc:(i + 1) * qc]) + bias for i in range(NSA_G)]

    slc = _flash_tiles(c // (NSA_KT // qc) + 1, [_online_init_t(qc)] * NSA_G, scores,
                       lambda t: _pv_tiles(vst_ref, t, 1, NSA_KT))
    outs = [partial_out[i] + gate(i, 1) * _normalise(slc[i][1])[0] for i in range(NSA_G)]
    o_ref[0, :, 0:LANES] = jnp.transpose(_merge_pair_t(outs[0], outs[1])).astype(o_ref.dtype)
    o_ref[0, :, LANES:2 * LANES] = jnp.transpose(_merge_pair_t(outs[2], outs[3])).astype(o_ref.dtype)


def _nsa(z_rot, z_pl, k_cmp, v_cmp, gate_logits, ovt):
    qc = NSA_QC
    seq_spec = lambda base: pl.BlockSpec((1, SEQ, LANES), lambda b, j, c: (b, 0, base + j))
    cmp_spec = pl.BlockSpec((1, 1, N_CMP_PAD, LANES), lambda b, j, c: (b, j, 0, 0))
    return pl.pallas_call(
        _nsa_kernel,
        out_shape=jax.ShapeDtypeStruct((BATCH, SEQ, NSA_HEADS * HEAD_DIM), BF16),
        grid=(BATCH, NSA_KV_HEADS, SEQ // qc),
        in_specs=[
            pl.BlockSpec((1, qc, LANES), lambda b, j, c: (b, c, ROT_NQ + 2 * j)),
            pl.BlockSpec((1, qc, LANES), lambda b, j, c: (b, c, ROT_NQ + 2 * j + 1)),
            cmp_spec, cmp_spec,
            seq_spec(ROT_NKS), seq_spec(PL_NVS), seq_spec(ROT_NKW), seq_spec(PL_NVW),
            pl.BlockSpec((qc, LANES), lambda b, j, c: (b * (SEQ // qc) + c, j)),
            pl.BlockSpec(ovt.shape, lambda b, j, c: (0, 0)),
        ],
        out_specs=pl.BlockSpec((1, qc, 2 * LANES), lambda b, j, c: (b, c, j)),
        scratch_shapes=[pltpu.VMEM((LANES, N_CMP_PAD), BF16),
                        pltpu.VMEM((SEQ // NSA_KT, VT_ROWS, NSA_KT), BF16),
                        pltpu.VMEM((SEQ // qc, VT_ROWS, qc), BF16),
                        pltpu.VMEM((N_SLC, qc), F32)],
        compiler_params=_params("parallel", "parallel", "arbitrary"),
        name="nsa",
    )(z_rot, z_rot, k_cmp, v_cmp, z_rot, z_pl, z_rot, z_pl, gate_logits, ovt)


DIL_QC = 128
DIL_STEPS = SEQ // DIL_QC
DIL_UNROLL = 8


def _dil_group(q_ref, k_ref, v_ref, og_ref, lg_ref, gi):
    window, dil = DIL_CONFIGS[gi]
    qc = DIL_QC
    m = SEQ // dil
    n_back = window // dil
    nk = min(m, qc + -(-n_back // qc) * qc)
    chunks = m // qc

    def rows(first, n):
        return pl.ds(first, n) if dil == 1 else pl.ds(first, n, stride=dil)

    def place(idx):
        r = idx // chunks
        q0 = (idx % chunks) * qc
        start = jnp.maximum(q0 - (nk - qc), 0)
        return rows(r + dil * q0, qc), rows(r + dil * start, nk), q0 - start

    def body(i, _):
        at = [place(i * DIL_UNROLL + u) for u in range(DIL_UNROLL)]
        s = [_nt(k_ref[0, k_rows, :].astype(BF16), _stack_heads(q_ref[0, q_rows, :].astype(BF16)))
             for q_rows, k_rows, _ in at]
        m, p = [], []
        for u, (_, _, off) in enumerate(at):
            s_u = s[u] + _tile_lanes(_band_bias_t(nk, qc, off, n_back), 2)
            m.append(_reduce_keys(s_u, jnp.maximum, jnp.max))
            p.append(_probs(s_u, m[u]))
        acc = [_nn(_transpose_aug(v_ref[0, k_rows, :]), p[u]) for u, (_, k_rows, _) in enumerate(at)]
        for u, (q_rows, _, _) in enumerate(at):
            o_t, l = _normalise(acc[u])
            lse_b = jnp.broadcast_to(m[u] + jnp.log(l) * LOG2_E, (LANES, 2 * qc))
            og_ref[gi, q_rows, :] = jnp.transpose(_merge_pair_t(o_t[:, :qc], o_t[:, qc:]))
            lg_ref[gi, q_rows, :] = jnp.transpose(_merge_pair_t(lse_b[:, :qc], lse_b[:, qc:]))
        return 0

    lax.fori_loop(0, DIL_STEPS // DIL_UNROLL, body, 0)


def _dil_kernel(q_ref, k_ref, v_ref, o_ref, og_ref, lg_ref):
    g = pl.program_id(2)
    n_groups = len(DIL_CONFIGS)
    for gi in range(n_groups):
        pl.when(g == gi)(functools.partial(_dil_group, q_ref, k_ref, v_ref, og_ref, lg_ref, gi))

    @pl.when(g == n_groups - 1)
    def _():
        rows = 512

        def body(i, _):
            sl = pl.ds(pl.multiple_of(i * rows, rows), rows)
            l0, l1, l2 = lg_ref[0, sl, :], lg_ref[1, sl, :], lg_ref[2, sl, :]
            mx = jnp.maximum(jnp.maximum(l0, l1), l2)
            e0, e1, e2 = jnp.exp2(l0 - mx), jnp.exp2(l1 - mx), jnp.exp2(l2 - mx)
            den = e0 + e1 + e2
            out = (e0 / den) * og_ref[0, sl, :] + (e1 / den) * og_ref[1, sl, :] + (e2 / den) * og_ref[2, sl, :]
            o_ref[0, sl, :] = out.astype(o_ref.dtype)
            return 0

        lax.fori_loop(0, SEQ // rows, body, 0)


def _dilated(zd_rot, zd_pl):
    n_groups = len(DIL_CONFIGS)
    width = DIL_HEADS_PER_GROUP * HEAD_DIM
    col = lambda base: (lambda b, p, g: (b, 0, base + 2 * g + p))
    blk = lambda base: pl.BlockSpec((1, SEQ, LANES), col(base))
    return pl.pallas_call(
        _dil_kernel,
        out_shape=jax.ShapeDtypeStruct((BATCH, SEQ, width), BF16),
        grid=(BATCH, 2, n_groups),
        in_specs=[blk(0), blk(DIL_BLOCKS), blk(0)],
        out_specs=pl.BlockSpec((1, SEQ, LANES), lambda b, p, g: (b, 0, p)),
        scratch_shapes=[pltpu.VMEM((n_groups, SEQ, LANES), F32), pltpu.VMEM((n_groups, SEQ, LANES), F32)],
        compiler_params=_params("parallel", "parallel", "arbitrary"),
        name="dilated",
    )(zd_rot, zd_rot, zd_pl)


def _out_proj_kernel(oa_ref, ob_ref, oc_ref, h_ref, wa_ref, wb_ref, wc_ref, g_ref, b_ref, h1_ref, h1b_ref):
    y = _nn(oa_ref[...], wa_ref[...]) + _nn(ob_ref[...], wb_ref[...]) + _nn(oc_ref[...], wc_ref[...])
    h1 = _layer_norm(DEEPNORM_ALPHA * h_ref[...] + y, g_ref[...], b_ref[...])
    h1_ref[...] = h1
    h1b_ref[...] = h1.astype(BF16)


def _out_proj(oa, ob, oc, h, wa, wb, wc, g, b):
    tm = 512
    rows = lambda w: pl.BlockSpec((tm, w), lambda i: (i, 0))
    full = lambda a: pl.BlockSpec(a.shape, lambda i: (0, 0))
    return pl.pallas_call(
        _out_proj_kernel,
        out_shape=(jax.ShapeDtypeStruct((TOKENS, D_MODEL), F32),
                   jax.ShapeDtypeStruct((TOKENS, D_MODEL), BF16)),
        grid=(TOKENS // tm,),
        in_specs=[rows(oa.shape[1]), rows(ob.shape[1]), rows(oc.shape[1]), rows(D_MODEL),
                  full(wa), full(wb), full(wc), full(g), full(b)],
        out_specs=(rows(D_MODEL), rows(D_MODEL)),
        compiler_params=_params("parallel"),
        name="out_proj_ln",
    )(oa, ob, oc, h, wa, wb, wc, g, b)


def _router_kernel(hb_ref, rw_ref, rb_ref, comb_ref, sel_ref):
    logits = _nt(rw_ref[...], hb_ref[...]) + rb_ref[...]
    mx = jnp.max(logits, axis=0, keepdims=True)
    ex = jnp.exp(logits - mx)
    probs = ex / jnp.sum(ex, axis=0, keepdims=True)
    p = [probs[e:e + 1, :] for e in range(N_EXPERTS)]
    best, g_sel = None, None
    for g in range(N_GROUPS):
        a, b, c, d = p[4 * g:4 * g + 4]
        hi1, lo1, hi2, lo2 = jnp.maximum(a, b), jnp.minimum(a, b), jnp.maximum(c, d), jnp.minimum(c, d)
        top2 = jnp.maximum(hi1, hi2) + jnp.maximum(jnp.minimum(hi1, hi2), jnp.maximum(lo1, lo2))
        if g == 0:
            best, g_sel = top2, jnp.zeros_like(top2)
        else:
            better = top2 > best
            best = jnp.where(better, top2, best)
            g_sel = jnp.where(better, float(g), g_sel)
    chosen, picked = [], []
    for e in range(N_EXPERTS):
        g = e // EXPERTS_PER_GROUP
        rank = jnp.zeros_like(best)
        for o in range(4 * g, 4 * g + 4):
            if o < e:
                rank = rank + jnp.where(p[o] >= p[e], 1.0, 0.0)
            elif o > e:
                rank = rank + jnp.where(p[o] > p[e], 1.0, 0.0)
        chosen.append(jnp.where((g_sel == float(g)) & (rank < 2.0), 1.0, 0.0))
        picked.append(chosen[e] * p[e])
    total = picked[0]
    for e in range(1, N_EXPERTS):
        total = total + picked[e]
    comb_ref[...] = jnp.concatenate(picked, axis=0) / total
    sel_ref[...] = jnp.concatenate(chosen, axis=0)


def _router(hb, rw_t, rb):
    tm = 1024
    out = jax.ShapeDtypeStruct((N_EXPERTS, TOKENS), F32)
    o_spec = pl.BlockSpec((N_EXPERTS, tm), lambda i: (0, i))
    return pl.pallas_call(
        _router_kernel,
        out_shape=(out, out),
        grid=(TOKENS // tm,),
        in_specs=[pl.BlockSpec((tm, D_MODEL), lambda i: (i, 0)),
                  pl.BlockSpec((N_EXPERTS, D_MODEL), lambda i: (0, 0)),
                  pl.BlockSpec((N_EXPERTS, 1), lambda i: (0, 0))],
        out_specs=(o_spec, o_spec),
        compiler_params=_params("parallel"),
        name="router",
    )(hb, rw_t, rb)


def _routing_tables(comb_t, sel_t):
    sel = sel_t > 0.5
    cnt = jnp.sum(sel, axis=1, dtype=jnp.int32)
    cnt_pad = (cnt + (MOE_TILE - 1)) // MOE_TILE * MOE_TILE
    ends = jnp.cumsum(cnt_pad)
    rank = jnp.cumsum(sel.astype(jnp.int32), axis=1) - 1
    pos = (ends - cnt_pad)[:, None] + rank
    pos_lo = jnp.min(jnp.where(sel, pos, MOE_ROWS), axis=0)
    pos_hi = jnp.max(jnp.where(sel, pos, -1), axis=0)
    w_lo = jnp.sum(jnp.where(sel & (pos == pos_lo), comb_t, 0.0), axis=0)
    w_hi = jnp.sum(jnp.where(sel & (pos == pos_hi), comb_t, 0.0), axis=0)
    w = jnp.zeros((TOKENS, LANES), F32).at[:, 0].set(w_lo).at[:, 1].set(w_hi)
    n_tiles = ends[-1] // MOE_TILE
    tile_start = jnp.arange(MOE_TILES, dtype=jnp.int32) * MOE_TILE
    tile_start = jnp.minimum(tile_start, ends[-1] - MOE_TILE)
    tile_expert = jnp.sum((ends[None, :] <= tile_start[:, None]).astype(jnp.int32), axis=1)
    return jnp.stack([pos_lo, pos_hi]).astype(jnp.int32), w, tile_expert, n_tiles.reshape(1).astype(jnp.int32)


MOE_TILE = 256
MOE_TILES = 2 * TOKENS // MOE_TILE + N_EXPERTS
MOE_ROWS = MOE_TILES * MOE_TILE
SLAB = D_MODEL // LANES


def _to_slabs(ref, x, rows):
    for j in range(SLAB):
        ref[pl.ds(j, rows, stride=SLAB), :] = x[:, j * LANES:(j + 1) * LANES]


def _from_slabs(ref, rows, pitch=SLAB):
    return jnp.concatenate([ref[pl.ds(j, rows, stride=pitch), :] for j in range(SLAB)], axis=1)


GATHER_PITCH = SLAB + 8


def _slab_rows(row, n=SLAB):
    return pl.ds(pl.multiple_of(row * n, n), n)


XSLAB = SLAB // 2
U32 = jnp.uint32


def _to_packed_slabs(ref, x, rows):
    bits = lambda t: lax.bitcast_convert_type(t.astype(BF16).astype(F32), U32)
    for j in range(XSLAB):
        hi = bits(x[:, 2 * j * LANES:(2 * j + 1) * LANES])
        lo = bits(x[:, (2 * j + 1) * LANES:(2 * j + 2) * LANES])
        ref[pl.ds(j, rows, stride=XSLAB), :] = hi | (lo >> 16)


def _from_packed_slabs(ref, rows):
    parts = []
    for j in range(XSLAB):
        u = ref[pl.ds(j, rows, stride=XSLAB), :]
        parts.append(lax.bitcast_convert_type(u & jnp.uint32(0xFFFF0000), F32).astype(BF16))
        parts.append(lax.bitcast_convert_type(u << 16, F32).astype(BF16))
    return jnp.concatenate(parts, axis=1)


def _dispatch_kernel(pos_ref, h_ref, init_ref, xs_ref, slab_ref, sem):
    del init_ref
    tm = h_ref.shape[0]
    base = pl.program_id(0) * tm
    _to_packed_slabs(slab_ref, h_ref[...], tm)

    def copy(t, which):
        return pltpu.make_async_copy(slab_ref.at[_slab_rows(t, XSLAB), :],
                                     xs_ref.at[_slab_rows(pos_ref[which, base + t], XSLAB), :], sem)

    def start(t, _):
        copy(t, 0).start(priority=0)
        copy(t, 1).start(priority=1)
        return 0

    lax.fori_loop(0, tm, start, 0, unroll=8)
    whole = pltpu.make_async_copy(slab_ref, xs_ref.at[pl.ds(0, tm * XSLAB), :], sem)
    whole.wait()
    whole.wait()


def _dispatch(pos, h):
    tm = 256
    grid_spec = pltpu.PrefetchScalarGridSpec(
        num_scalar_prefetch=1,
        grid=(TOKENS // tm,),
        in_specs=[pl.BlockSpec((tm, D_MODEL), lambda i, pos: (i, 0)),
                  pl.BlockSpec(memory_space=pl.ANY)],
        out_specs=pl.BlockSpec(memory_space=pl.ANY),
        scratch_shapes=[pltpu.VMEM((tm * XSLAB, LANES), U32), pltpu.SemaphoreType.DMA],
    )
    return pl.pallas_call(
        _dispatch_kernel,
        out_shape=jax.ShapeDtypeStruct((MOE_ROWS * XSLAB, LANES), U32),
        grid_spec=grid_spec,
        input_output_aliases={2: 0},
        compiler_params=_params("arbitrary"),
        name="moe_dispatch",
    )(pos, h, jnp.zeros((MOE_ROWS * XSLAB, LANES), U32))


def _experts_kernel(te_ref, nt_ref, xs_ref, wg_ref, wu_ref, wd_ref, ys_ref, wgb_ref, wub_ref, wdb_ref):
    k = pl.program_id(0)
    e = te_ref[k]
    e_prev = te_ref[jnp.maximum(k - 1, 0)]

    @pl.when((k == 0) | (e != e_prev))
    def _():
        wgb_ref[...] = wg_ref[0, 0].astype(BF16)
        wub_ref[...] = wu_ref[0, 0].astype(BF16)
        wdb_ref[...] = wd_ref[0, 0].astype(BF16)

    @pl.when(k < nt_ref[0])
    def _():
        x = _from_packed_slabs(xs_ref, MOE_TILE)
        hid = jax.nn.silu(_nn(x, wgb_ref[...])) * _nn(x, wub_ref[...])
        _to_slabs(ys_ref, _nn(hid.astype(BF16), wdb_ref[...]), MOE_TILE)

    @pl.when(k >= nt_ref[0])
    def _():
        ys_ref[...] = jnp.zeros(ys_ref.shape, F32)


def _experts(tile_expert, n_tiles, xs, wg, wu, wd, layer):
    w_in_spec = pl.BlockSpec((1, 1, D_MODEL, EXPERT_HIDDEN), lambda k, te, nt: (layer, te[k], 0, 0))
    grid_spec = pltpu.PrefetchScalarGridSpec(
        num_scalar_prefetch=2,
        grid=(MOE_TILES,),
        in_specs=[pl.BlockSpec((MOE_TILE * XSLAB, LANES), lambda k, te, nt: (jnp.minimum(k, nt[0] - 1), 0)),
                  w_in_spec, w_in_spec,
                  pl.BlockSpec((1, 1, EXPERT_HIDDEN, D_MODEL), lambda k, te, nt: (layer, te[k], 0, 0))],
        out_specs=pl.BlockSpec((MOE_TILE * SLAB, LANES), lambda k, te, nt: (k, 0)),
        scratch_shapes=[pltpu.VMEM((D_MODEL, EXPERT_HIDDEN), BF16), pltpu.VMEM((D_MODEL, EXPERT_HIDDEN), BF16),
                        pltpu.VMEM((EXPERT_HIDDEN, D_MODEL), BF16)],
    )
    return pl.pallas_call(
        _experts_kernel,
        out_shape=jax.ShapeDtypeStruct((MOE_ROWS * SLAB, LANES), F32),
        grid_spec=grid_spec,
        compiler_params=_params("arbitrary"),
        name="moe_experts",
    )(tile_expert, n_tiles, xs, wg, wu, wd)


def _ple_ln_kernel(pos_ref, hb_ref, h_ref, ys_ref, w_ref, p_ref, gw_ref, gb_ref, pw_ref, g_ref, b_ref,
                   h2_ref, h2b_ref, lo_ref, hi_ref, sem):
    tm = h_ref.shape[0]
    i = pl.program_id(0)
    slot = i & 1
    bufs = (lo_ref, hi_ref)

    def fetch(tile, into):
        def start(t, _):
            for which in range(2):
                pltpu.make_async_copy(ys_ref.at[_slab_rows(pos_ref[which, tile * tm + t]), :],
                                      bufs[which].at[into, pl.ds(pl.multiple_of(t * GATHER_PITCH, 8), SLAB), :],
                                      sem.at[into]).start(priority=which)
            return 0
        lax.fori_loop(0, tm, start, 0, unroll=8)

    @pl.when(i == 0)
    def _():
        fetch(0, 0)

    @pl.when(i + 1 < pl.num_programs(0))
    def _():
        fetch(i + 1, 1 - slot)

    gate = jax.nn.sigmoid(_nn(hb_ref[...], gw_ref[...]) + gb_ref[...])
    ple = gate * _nn(p_ref[...].astype(BF16), pw_ref[...])
    for which in range(2):
        pltpu.make_async_copy(ys_ref.at[pl.ds(0, tm * SLAB), :], bufs[which].at[slot, pl.ds(0, tm * SLAB), :],
                              sem.at[slot]).wait()
    w = w_ref[...]
    ffn = (w[:, 0:1] * _from_slabs(lo_ref.at[slot], tm, GATHER_PITCH)
           + w[:, 1:2] * _from_slabs(hi_ref.at[slot], tm, GATHER_PITCH))
    h2 = _layer_norm(DEEPNORM_ALPHA * h_ref[...] + ffn + ple, g_ref[...], b_ref[...])
    h2_ref[...] = h2
    h2b_ref[...] = h2.astype(BF16)


def _ple_ln(pos, hb, h, ys, w, p, layer, gw, gb, pw, g, b):
    tm = 256
    p_spec = pl.BlockSpec((tm, PLE_DIM), lambda i, pos: (layer * (TOKENS // tm) + i, 0))
    rows = lambda width: pl.BlockSpec((tm, width), lambda i, pos: (i, 0))
    full = lambda a: pl.BlockSpec(a.shape, lambda i, pos: (0, 0))
    grid_spec = pltpu.PrefetchScalarGridSpec(
        num_scalar_prefetch=1,
        grid=(TOKENS // tm,),
        in_specs=[rows(D_MODEL), rows(D_MODEL), pl.BlockSpec(memory_space=pl.ANY), rows(LANES), p_spec,
                  full(gw), full(gb), full(pw), full(g), full(b)],
        out_specs=(rows(D_MODEL), rows(D_MODEL)),
        scratch_shapes=[pltpu.VMEM((2, tm * GATHER_PITCH, LANES), F32), pltpu.VMEM((2, tm * GATHER_PITCH, LANES), F32),
                        pltpu.SemaphoreType.DMA((2,))],
    )
    return pl.pallas_call(
        _ple_ln_kernel,
        out_shape=(jax.ShapeDtypeStruct((TOKENS, D_MODEL), F32),
                   jax.ShapeDtypeStruct((TOKENS, D_MODEL), BF16)),
        grid_spec=grid_spec,
        compiler_params=_params("arbitrary"),
        name="ple_ln",
    )(pos, hb, h, ys, w, p, gw, gb, pw, g, b)


def _rope_tables(positions):
    half = ROT_DIM // 2
    inv_freq = jnp.exp(jnp.arange(half, dtype=F32) * (-2.0 * math.log(ROPE_THETA) / ROT_DIM))
    ang = positions.astype(F32)[:, :, None] * inv_freq
    cos, sin = jnp.cos(ang), jnp.sin(ang)
    zeros = jnp.zeros_like(cos)
    rest = HEAD_DIM - ROT_DIM
    pad = lambda v: jnp.broadcast_to(jnp.asarray(v, F32), cos.shape[:2] + (rest,))
    c = jnp.concatenate([cos, cos, pad(1.0)], axis=-1)
    s1 = jnp.concatenate([-sin, zeros, pad(0.0)], axis=-1)
    s2 = jnp.concatenate([zeros, sin, pad(0.0)], axis=-1)
    tile = lambda t: jnp.concatenate([t, t], axis=-1).reshape(TOKENS, LANES)
    return tile(c), tile(s1), tile(s2)


def _split_w_in(w):
    mw, nq, nkv, dw = MOBA_HEADS * HEAD_DIM, NSA_HEADS * HEAD_DIM, NSA_KV_HEADS * HEAD_DIM, DIL_HEADS * HEAD_DIM
    widths = (mw, mw, mw, nq) + (nkv,) * 6 + (NSA_HEADS * 3, dw, dw, dw)
    offs = np.concatenate([[0], np.cumsum(widths)])
    qa, ka, va, qb, kbc, vbc, kbs, vbs, kbw, vbw, gb, qc, kc, vc = (
        w[:, int(offs[i]):int(offs[i + 1])] for i in range(len(widths)))

    def dup(t):
        t = t.reshape(D_MODEL, NSA_KV_HEADS, 1, HEAD_DIM)
        return jnp.broadcast_to(t, (D_MODEL, NSA_KV_HEADS, 2, HEAD_DIM)).reshape(D_MODEL, NSA_KV_HEADS * LANES)

    zpad = lambda n: jnp.zeros((D_MODEL, n * LANES), w.dtype)
    w_rot = jnp.concatenate([qa * Q_SCALE, ka, qb * Q_SCALE, dup(kbc), dup(kbs), dup(kbw), zpad(1)], axis=1)
    w_pl = jnp.concatenate([va, dup(vbc), dup(vbs), dup(vbw), zpad(3)], axis=1)
    gpad = jnp.zeros((D_MODEL, NSA_KV_HEADS, LANES - 12), w.dtype)
    w_gl = jnp.concatenate([gb.reshape(D_MODEL, NSA_KV_HEADS, 12), gpad], axis=-1).reshape(D_MODEL, -1)
    w_dil_rot = jnp.concatenate([qc * Q_SCALE, kc], axis=1)
    return tuple(t.astype(BF16) for t in (w_rot, w_pl, w_gl, w_dil_rot, vc))


def _overlap_table():
    starts = np.arange(N_CMP) * CMP_STRIDE
    slc = np.arange(N_SLC) * SLC_BLOCK
    ov = ((starts[:, None] < slc[None, :] + SLC_BLOCK) & (starts[:, None] + CMP_LEN > slc[None, :]))
    ovt = np.zeros((N_SLC, N_CMP_PAD), np.float32)
    ovt[:, :N_CMP] = ov.T
    return jnp.asarray(ovt, BF16)


def _cmp_chunks(z, base):
    nblk = z.shape[-1] // LANES
    t = z.reshape(BATCH, SEQ // CMP_STRIDE, CMP_STRIDE, nblk, LANES)[:, :, :, base:base + NSA_KV_HEADS, :HEAD_DIM]
    return t.transpose(0, 3, 1, 2, 4).reshape(BATCH, NSA_KV_HEADS, SEQ // CMP_STRIDE, CMP_STRIDE * HEAD_DIM)


def kernel(x, p, positions, ln_in_g, ln_in_b, w_in, w_out, nsa_ck1, nsa_ck2, nsa_pe_k, nsa_cv1, nsa_cv2, nsa_pe_v, ln1_g, ln1_b, router_w, router_b, w_gate, w_up, w_down, ple_proj, ple_gate_w, ple_gate_b, ln2_g, ln2_b):
    rope = _rope_tables(positions)
    ovt = _overlap_table()
    rw_t = router_w.T.astype(BF16)
    rb = router_b.reshape(N_EXPERTS, 1).astype(F32)
    chunk_w = CMP_STRIDE * HEAD_DIM
    vec = lambda v: v.reshape(1, -1)
    seq3 = lambda t: t.reshape(BATCH, SEQ, t.shape[-1])
    flat = lambda t: t.reshape(TOKENS, t.shape[-1])

    h, hb = _ln_in(x.reshape(TOKENS, D_MODEL), ln_in_g, ln_in_b)
    for i in range(DEPTH):
        w_rot, w_pl, w_gl, w_dil_rot, w_dil_pl = _split_w_in(w_in[i])
        z_rot = seq3(_project(hb, w_rot, BF16, 768, rope=rope))
        z_pl = seq3(_project(hb, w_pl, BF16, 1024))
        gate_logits = _project(hb, w_gl, F32, NSA_KV_HEADS * LANES)
        zd_rot = seq3(_project(hb, w_dil_rot, F32, 768, rope=rope))
        zd_pl = seq3(_project(hb, w_dil_pl, F32, 768))

        o_a = _moba(z_rot, z_pl)

        dup2 = lambda w2: jnp.concatenate([w2, w2], axis=1).astype(BF16)
        k_cmp, v_cmp = _compress(
            _cmp_chunks(z_rot, ROT_NKC), _cmp_chunks(z_pl, PL_NVC),
            nsa_pe_k[i].reshape(2, chunk_w), nsa_pe_v[i].reshape(2, chunk_w),
            nsa_ck1[i].reshape(2, chunk_w, CMP_HIDDEN).astype(BF16), dup2(nsa_ck2[i]),
            nsa_cv1[i].reshape(2, chunk_w, CMP_HIDDEN).astype(BF16), dup2(nsa_cv2[i]))
        o_b = _nsa(z_rot, z_pl, k_cmp, v_cmp, gate_logits, ovt)

        o_c = _dilated(zd_rot, zd_pl)

        wo = w_out[i].astype(BF16)
        a_w, b_w = MOBA_HEADS * HEAD_DIM, NSA_HEADS * HEAD_DIM
        h, hb = _out_proj(flat(o_a), flat(o_b), flat(o_c), h,
                          wo[:a_w], wo[a_w:a_w + b_w], wo[a_w + b_w:], vec(ln1_g[i]), vec(ln1_b[i]))

        pos, w_tok, tile_expert, n_tiles = _routing_tables(*_router(hb, rw_t, rb))
        xs = _dispatch(pos, h)
        ys = _experts(tile_expert, n_tiles, xs, w_gate, w_up, w_down, i)
        h, hb = _ple_ln(pos, hb, h, ys, w_tok, p.reshape(DEPTH * TOKENS, PLE_DIM), i, ple_gate_w[i].astype(BF16),
                        vec(ple_gate_b[i]), ple_proj[i].astype(BF16), vec(ln2_g[i]), vec(ln2_b[i]))
    return h.reshape(BATCH, SEQ, D_MODEL)
```

```python
import functools
import math

import numpy as np
import jax
import jax.numpy as jnp
from jax import lax
from jax.experimental import pallas as pl
from jax.experimental.pallas import tpu as pltpu

F32 = jnp.float32
BF16 = jnp.bfloat16

D_MODEL = 2048
BATCH = 2
SEQ = 4096
DEPTH = 4
TOKENS = BATCH * SEQ
HEAD_DIM = 64
ROT_DIM = HEAD_DIM // 4
ROPE_THETA = 500000.0
NEG = -1e30
FORCE = 1e30
LN_EPS = 1e-5
SCALE = HEAD_DIM ** -0.5
LOG2_E = math.log2(math.e)
Q_SCALE = SCALE * LOG2_E

MOBA_HEADS = 8
MOBA_BLOCK = 256
MOBA_TOPK = 3
MOBA_NB = SEQ // MOBA_BLOCK

NSA_HEADS = 12
NSA_KV_HEADS = 3
CMP_LEN = 32
CMP_STRIDE = 16
CMP_HIDDEN = 128
N_CMP = (SEQ - CMP_LEN) // CMP_STRIDE + 1
N_CMP_PAD = 256
SLC_BLOCK = 64
SLC_TOPK = 16
SLC_LOCAL = 2
N_SLC = SEQ // SLC_BLOCK
NSA_WINDOW = 512

DIL_CONFIGS = ((128, 1), (512, 4), (2048, 16))
DIL_HEADS_PER_GROUP = 4
DIL_HEADS = DIL_HEADS_PER_GROUP * len(DIL_CONFIGS)

N_EXPERTS = 16
N_GROUPS = 4
EXPERTS_PER_GROUP = 4
EXPERT_HIDDEN = D_MODEL // 4
PLE_DIM = 256

DEEPNORM_ALPHA = (2 * DEPTH) ** 0.25

LANES = 128
SUBLANES = 8
BF16_SUBLANES = 16
VMEM_LIMIT = 56 * 1024 * 1024
MOBA_SHIFT = MOBA_BLOCK.bit_length() - 1
SLC_SHIFT = SLC_BLOCK.bit_length() - 1

ROT_MQ, ROT_MK, ROT_NQ, ROT_NKC, ROT_NKS, ROT_NKW = 0, 4, 8, 14, 17, 20
ROT_BLOCKS = 24
PL_MV, PL_NVC, PL_NVS, PL_NVW = 0, 4, 7, 10
PL_BLOCKS = 16
DIL_BLOCKS = DIL_HEADS // 2

NT_DIMS = (((1,), (1,)), ((), ()))


def _nt(a, b):
    return lax.dot_general(a, b, NT_DIMS, preferred_element_type=F32)


def _nn(a, b):
    return jnp.dot(a, b, preferred_element_type=F32)


def _params(*sem):
    return pltpu.CompilerParams(dimension_semantics=sem, vmem_limit_bytes=VMEM_LIMIT)


def _layer_norm(y, g, b):
    mu = jnp.mean(y, axis=-1, keepdims=True)
    yc = y - mu
    var = jnp.mean(yc * yc, axis=-1, keepdims=True)
    return yc * lax.rsqrt(var + LN_EPS) * g + b


def _ln_kernel(x_ref, g_ref, b_ref, h_ref, hb_ref):
    h = _layer_norm(x_ref[...], g_ref[...], b_ref[...])
    h_ref[...] = h
    hb_ref[...] = h.astype(BF16)


def _ln_in(x, g, b):
    tm = 512
    row = pl.BlockSpec((tm, D_MODEL), lambda i: (i, 0))
    vec = pl.BlockSpec((1, D_MODEL), lambda i: (0, 0))
    return pl.pallas_call(
        _ln_kernel,
        out_shape=(jax.ShapeDtypeStruct((TOKENS, D_MODEL), F32),
                   jax.ShapeDtypeStruct((TOKENS, D_MODEL), BF16)),
        grid=(TOKENS // tm,),
        in_specs=[row, vec, vec],
        out_specs=(row, row),
        compiler_params=_params("parallel"),
        name="ln_in",
    )(x, g.reshape(1, -1), b.reshape(1, -1))


def _proj_kernel(x_ref, w_ref, o_ref):
    o_ref[...] = _nn(x_ref[...], w_ref[...]).astype(o_ref.dtype)


def _proj_rot_kernel(x_ref, w_ref, c_ref, s1_ref, s2_ref, o_ref):
    x = x_ref[...]
    c, s1, s2 = c_ref[...], s1_ref[...], s2_ref[...]
    half = ROT_DIM // 2
    for j0 in range(0, o_ref.shape[1], 2 * LANES):
        z = _nn(x, w_ref[:, j0:j0 + 2 * LANES])
        for j in range(j0, j0 + 2 * LANES, LANES):
            zc = z[:, j - j0:j - j0 + LANES]
            r = zc * c + pltpu.roll(zc, LANES - half, 1) * s1 + pltpu.roll(zc, half, 1) * s2
            o_ref[:, j:j + LANES] = r.astype(o_ref.dtype)


def _project(hb, w, out_dtype, tn, rope=None):
    tm = 1024
    n = w.shape[1]
    x_spec = pl.BlockSpec((tm, D_MODEL), lambda i, j: (i, 0))
    w_spec = pl.BlockSpec((D_MODEL, tn), lambda i, j: (0, j))
    o_spec = pl.BlockSpec((tm, tn), lambda i, j: (i, j))
    if rope is None:
        kern, extra, extra_specs = _proj_kernel, (), []
    else:
        t_spec = pl.BlockSpec((tm, LANES), lambda i, j: (i, 0))
        kern, extra, extra_specs = _proj_rot_kernel, rope, [t_spec] * 3
    return pl.pallas_call(
        kern,
        out_shape=jax.ShapeDtypeStruct((TOKENS, n), out_dtype),
        grid=(TOKENS // tm, n // tn),
        in_specs=[x_spec, w_spec] + extra_specs,
        out_specs=o_spec,
        compiler_params=_params("parallel", "arbitrary"),
        name="in_proj_rot" if rope is not None else "in_proj",
    )(hb, w, *extra)


def _stack_heads(*q_blocks):
    parts = []
    for q in q_blocks:
        lane = lax.broadcasted_iota(jnp.int32, q.shape, 1)
        zero = jnp.zeros_like(q)
        parts += [jnp.where(lane < HEAD_DIM, q, zero), jnp.where(lane >= HEAD_DIM, q, zero)]
    return jnp.concatenate(parts, axis=0)


def _merge_pair_t(lo, hi):
    sub = lax.broadcasted_iota(jnp.int32, lo.shape, 0)
    return jnp.where(sub < HEAD_DIM, lo, hi)


def _band_bias_t(nk, qc, offset, n_back):
    key = lax.broadcasted_iota(jnp.int32, (nk, qc), 0)
    qry = lax.broadcasted_iota(jnp.int32, (nk, qc), 1)
    diff = offset + qry - key
    return jnp.where((diff >= 0) & (diff <= n_back), 0.0, NEG)


def _tile_lanes(x, n):
    return jnp.concatenate([x] * n, axis=1)


def _transpose_bf16(v):
    return jnp.transpose(v.astype(F32)).astype(BF16)


def _tree(x, op):
    n = x.shape[0]
    if n == SUBLANES:
        return x
    if n % (2 * SUBLANES) == 0:
        return op(_tree(x[:n // 2], op), _tree(x[n // 2:], op))
    acc = x[:SUBLANES]
    for i in range(1, n // SUBLANES):
        acc = op(acc, x[SUBLANES * i:SUBLANES * (i + 1)])
    return acc


def _reduce_keys(x, op, final):
    return final(_tree(x, op), axis=0, keepdims=True)


VT_ROWS = LANES + BF16_SUBLANES


def _transpose_aug(v):
    vt = jnp.transpose(v.astype(F32))
    sub = lax.broadcasted_iota(jnp.int32, (VT_ROWS - LANES, v.shape[0]), 0)
    return jnp.concatenate([vt, jnp.where(sub == 0, 1.0, 0.0)], axis=0).astype(BF16)


def _probs(s_t, m):
    return jnp.exp2((s_t - m).astype(BF16))


def _normalise(acc):
    l = acc[LANES:LANES + 1]
    return acc[:LANES] / l, l


def _softmax_block_t(s_t, pv):
    m = _reduce_keys(s_t, jnp.maximum, jnp.max)
    out, l = _normalise(pv(_probs(s_t, m)))
    return out, m + jnp.log(l) * LOG2_E


def _online_step_t(carry, s_t, m_t, pv):
    m, acc = carry
    m_new = jnp.maximum(m, m_t)
    acc = jnp.exp2(m - m_new) * acc + pv(_probs(s_t, m_new))
    return m_new, acc


def _flash_tiles(n_tiles, init, scores, pv_of, may_be_empty=False):
    strips = range(len(init))

    def produce(t):
        s = tuple(scores(t))
        return s, tuple(_reduce_keys(s_i, jnp.maximum, jnp.max) for s_i in s)

    def consume(state, s_t, m_t, t):
        pv = pv_of(t)
        return tuple(_online_step_t(state[i], s_t[i], m_t[i], pv) for i in strips)

    def body(t, carry):
        state, s_t, m_t = carry
        s_next, m_next = produce(t + 1)
        return consume(state, s_t, m_t, t), s_next, m_next

    state, s_last, m_last = lax.fori_loop(0, n_tiles - 1, body, (tuple(init),) + produce(0))
    done = consume(state, s_last, m_last, jnp.maximum(n_tiles - 1, 0))
    if may_be_empty:
        done = jax.tree_util.tree_map(lambda new, old: jnp.where(n_tiles > 0, new, old), done, state)
    return done


def _pv_tiles(vt_ref, first, n, rows):
    def pv(p):
        acc = _nn(vt_ref[first], p[:rows])
        for j in range(1, n):
            acc = acc + _nn(vt_ref[first + j], p[j * rows:(j + 1) * rows])
        return acc
    return pv


def _online_init_t(r):
    return (jnp.full((1, r), NEG, F32), jnp.zeros((VT_ROWS, r), F32))


def _rank_rows(g, n_rows):
    sub = lax.broadcasted_iota(jnp.int32, (SUBLANES, g.shape[1]), 0)
    rank = jnp.zeros(g.shape, F32)
    for m in range(n_rows):
        gm = g[m:m + 1, :]
        b = m // SUBLANES * SUBLANES
        mid = g[b:b + SUBLANES]
        parts = [jnp.where(gm > mid, 1.0, jnp.where((gm == mid) & (sub > m - b), 1.0, 0.0))]
        if b > 0:
            parts.insert(0, jnp.where(gm > g[:b], 1.0, 0.0))
        if b + SUBLANES < n_rows:
            parts.append(jnp.where(gm >= g[b + SUBLANES:], 1.0, 0.0))
        rank = rank + jnp.concatenate(parts, axis=0)
    return rank


MOBA_KT = 2 * MOBA_BLOCK
MOBA_QC = MOBA_KT


def _moba_kernel(q_ref, k_ref, v_ref, o_ref, kmean_ref, vt_ref, bias_ref):
    c = pl.program_id(2)
    qc = MOBA_QC

    @pl.when(c == 0)
    def _():
        row = lax.broadcasted_iota(jnp.int32, (MOBA_NB, SEQ), 0)
        col = lax.broadcasted_iota(jnp.int32, (MOBA_NB, SEQ), 1)
        avg = jnp.where((col >> MOBA_SHIFT) == row, 1.0 / MOBA_BLOCK, 0.0).astype(BF16)
        kmean_ref[...] = _nn(avg, k_ref[0])
        for t in range(MOBA_NB):
            vt_ref[t] = _transpose_aug(v_ref[0, t * MOBA_BLOCK:(t + 1) * MOBA_BLOCK, :])

    qs = _stack_heads(q_ref[0])
    per_tile = MOBA_KT // MOBA_BLOCK

    ks = pl.multiple_of(c * MOBA_KT, MOBA_KT)
    heads = [qs[:qc], qs[qc:]]
    raw_own = [_nt(k_ref[0, pl.ds(ks, MOBA_KT), :], q_h) for q_h in heads]

    gate = _nt(kmean_ref[...].astype(BF16), qs)
    blk = lax.broadcasted_iota(jnp.int32, gate.shape, 0)
    q_idx = lax.broadcasted_iota(jnp.int32, gate.shape, 1) & (qc - 1)
    past = blk < c * per_tile + (q_idx >> MOBA_SHIFT)
    rank = _rank_rows(jnp.where(past, gate, NEG), MOBA_NB)
    bias_ref[...] = jnp.where(past & (rank < MOBA_TOPK), 0.0, NEG)

    key = lax.broadcasted_iota(jnp.int32, (MOBA_KT, qc), 0)
    qry = lax.broadcasted_iota(jnp.int32, (MOBA_KT, qc), 1)
    first_block = bias_ref[pl.ds(c * per_tile, 1), :]
    state = []
    for h in range(2):
        other = jnp.broadcast_to(first_block[:, h * qc:(h + 1) * qc], (MOBA_KT, qc))
        same_block = (key >> MOBA_SHIFT) == (qry >> MOBA_SHIFT)
        own_bias = jnp.where(key <= qry, jnp.where(same_block, 0.0, other), NEG)
        s_h = raw_own[h] + own_bias
        state.append(_online_step_t(_online_init_t(qc), s_h, _reduce_keys(s_h, jnp.maximum, jnp.max),
                                    _pv_tiles(vt_ref, c * per_tile, per_tile, MOBA_BLOCK)))

    def scores(t):
        ks = pl.multiple_of(t * MOBA_KT, MOBA_KT)
        kt = k_ref[0, pl.ds(ks, MOBA_KT), :]
        rows = [bias_ref[pl.ds(t * per_tile + j, 1), :] for j in range(per_tile)]
        out = []
        for h in range(2):
            blocks = [jnp.broadcast_to(row[:, h * qc:(h + 1) * qc], (MOBA_BLOCK, qc)) for row in rows]
            out.append(_nt(kt, heads[h]) + jnp.concatenate(blocks, axis=0))
        return out

    state = _flash_tiles(c, state, scores,
                         lambda t: _pv_tiles(vt_ref, t * per_tile, per_tile, MOBA_BLOCK), may_be_empty=True)
    o_lo, o_hi = (_normalise(acc)[0] for _, acc in state)
    o_ref[0] = jnp.transpose(_merge_pair_t(o_lo, o_hi)).astype(o_ref.dtype)


def _moba(z_rot, z_pl):
    qc = MOBA_QC
    grid = (BATCH, MOBA_HEADS // 2, SEQ // qc)
    return pl.pallas_call(
        _moba_kernel,
        out_shape=jax.ShapeDtypeStruct((BATCH, SEQ, MOBA_HEADS * HEAD_DIM), BF16),
        grid=grid,
        in_specs=[
            pl.BlockSpec((1, qc, LANES), lambda b, p, c: (b, c, ROT_MQ + p)),
            pl.BlockSpec((1, SEQ, LANES), lambda b, p, c: (b, 0, ROT_MK + p)),
            pl.BlockSpec((1, SEQ, LANES), lambda b, p, c: (b, 0, PL_MV + p)),
        ],
        out_specs=pl.BlockSpec((1, qc, LANES), lambda b, p, c: (b, c, p)),
        scratch_shapes=[pltpu.VMEM((MOBA_NB, LANES), F32),
                        pltpu.VMEM((MOBA_NB, VT_ROWS, MOBA_BLOCK), BF16),
                        pltpu.VMEM((MOBA_NB, 2 * qc), F32)],
        compiler_params=_params("parallel", "parallel", "arbitrary"),
        name="moba",
    )(z_rot, z_rot, z_pl)


def _compress_one(x_ref, pe_ref, w1_ref, w2_ref, o_ref):
    x = x_ref[0, 0].astype(F32)
    top = (x + pe_ref[0:1, :]).astype(BF16)
    bot = (x + pe_ref[1:2, :]).astype(BF16)
    a = _nn(top, w1_ref[0])
    bm = _nn(bot, w1_ref[1])
    pre = a + pltpu.roll(bm, N_CMP_PAD - 1, 0)
    hid = jax.nn.gelu(pre)
    out = _nn(hid.astype(BF16), w2_ref[...])
    row = lax.broadcasted_iota(jnp.int32, out.shape, 0)
    o_ref[0, 0] = jnp.where(row < N_CMP, out, 0.0).astype(o_ref.dtype)


def _compress_kernel(xk_ref, xv_ref, pk_ref, pv_ref, k1_ref, k2_ref, v1_ref, v2_ref, ok_ref, ov_ref):
    _compress_one(xk_ref, pk_ref, k1_ref, k2_ref, ok_ref)
    _compress_one(xv_ref, pv_ref, v1_ref, v2_ref, ov_ref)


def _compress(xk, xv, pk, pv, k1, k2, v1, v2):
    chunk_w = CMP_STRIDE * HEAD_DIM
    x_spec = pl.BlockSpec((1, 1, N_CMP_PAD, chunk_w), lambda b, j: (b, j, 0, 0))
    pe_spec = pl.BlockSpec((2, chunk_w), lambda b, j: (0, 0))
    w1_spec = pl.BlockSpec((2, chunk_w, CMP_HIDDEN), lambda b, j: (0, 0, 0))
    w2_spec = pl.BlockSpec((CMP_HIDDEN, LANES), lambda b, j: (0, 0))
    o_spec = pl.BlockSpec((1, 1, N_CMP_PAD, LANES), lambda b, j: (b, j, 0, 0))
    o_shape = jax.ShapeDtypeStruct((BATCH, NSA_KV_HEADS, N_CMP_PAD, LANES), BF16)
    return pl.pallas_call(
        _compress_kernel,
        out_shape=(o_shape, o_shape),
        grid=(BATCH, NSA_KV_HEADS),
        in_specs=[x_spec, x_spec, pe_spec, pe_spec, w1_spec, w2_spec, w1_spec, w2_spec],
        out_specs=(o_spec, o_spec),
        compiler_params=_params("parallel", "parallel"),
        name="nsa_compress",
    )(xk, xv, pk, pv, k1, k2, v1, v2)


NSA_QC = 512
NSA_KT = 512
NSA_G = NSA_HEADS // NSA_KV_HEADS
NSA_WIN_TILES = NSA_WINDOW // NSA_QC + 1


def _nsa_kernel(qa_ref, qb_ref, kc_ref, vc_ref, ks_ref, vs_ref, kw_ref, vw_ref, gl_ref, ovt_ref,
                o_ref, vct_ref, vst_ref, vwt_ref, bias_ref):
    c = pl.program_id(2)
    qc = NSA_QC
    q0 = c * qc
    lanes_of = lambda t, i: t[:, i * qc:(i + 1) * qc]

    @pl.when(c == 0)
    def _():
        vct_ref[...] = _transpose_bf16(vc_ref[0, 0])
        for t in range(SEQ // NSA_KT):
            vst_ref[t] = _transpose_aug(vs_ref[0, t * NSA_KT:(t + 1) * NSA_KT, :])
        for t in range(SEQ // qc):
            vwt_ref[t] = _transpose_aug(vw_ref[0, t * qc:(t + 1) * qc, :])

    qs = _stack_heads(qa_ref[0], qb_ref[0])

    sc_t = _nt(kc_ref[0, 0], qs)
    t0 = jnp.maximum(c - NSA_WINDOW // qc, 0)
    start = pl.multiple_of(t0 * qc, qc)
    sw_t = _nt(kw_ref[0, pl.ds(start, NSA_WIN_TILES * qc), :], qs)

    n_idx = lax.broadcasted_iota(jnp.int32, (N_CMP_PAD, qc), 0)
    q_idx = lax.broadcasted_iota(jnp.int32, (N_CMP_PAD, qc), 1)
    ok = (n_idx * CMP_STRIDE + (CMP_LEN - 1)) <= (q0 + q_idx)
    p_heads = []
    for i in range(NSA_G):
        s_i = jnp.where(ok, lanes_of(sc_t, i), NEG)
        e_i = jnp.where(ok, jnp.exp2(s_i - _reduce_keys(s_i, jnp.maximum, jnp.max)), 0.0)
        l_i = _reduce_keys(e_i, jnp.add, jnp.sum)
        p_heads.append((e_i / jnp.where(l_i > 0.0, l_i, 1.0)).astype(BF16))
    p_ct = jnp.concatenate(p_heads, axis=1)
    imp4 = _nn(ovt_ref[...], p_ct)
    ocmp_t = _nn(vct_ref[...], p_ct)

    band = _band_bias_t(NSA_WIN_TILES * qc, qc, q0 - start, NSA_WINDOW - 1)
    owin_t, _ = _softmax_block_t(sw_t + _tile_lanes(band, NSA_G), _pv_tiles(vwt_ref, t0, NSA_WIN_TILES, qc))
    gate_t = jnp.transpose(jax.nn.sigmoid(gl_ref[...]))
    gate = lambda i, r: gate_t[3 * i + r:3 * i + r + 1, :]
    partial_out = [gate(i, 0) * lanes_of(ocmp_t, i) + gate(i, 2) * lanes_of(owin_t, i) for i in range(NSA_G)]

    imp = lanes_of(imp4, 0) + lanes_of(imp4, 1) + lanes_of(imp4, 2) + lanes_of(imp4, 3)
    blk = lax.broadcasted_iota(jnp.int32, imp.shape, 0)
    cur = (q0 + lax.broadcasted_iota(jnp.int32, imp.shape, 1)) >> SLC_SHIFT
    valid = blk <= cur
    forced = valid & ((blk == 0) | (blk > cur - SLC_LOCAL))
    rank = _rank_rows(jnp.where(forced, FORCE, jnp.where(valid, imp, NEG)), N_SLC)
    bias_ref[...] = jnp.where(valid & (rank < SLC_TOPK), 0.0, NEG)

    key_row = lax.broadcasted_iota(jnp.int32, (NSA_KT, qc), 0)
    qpos = lax.broadcasted_iota(jnp.int32, (NSA_KT, qc), 1) + q0
    per_tile = NSA_KT // SLC_BLOCK

    def scores(t):
        ks0 = pl.multiple_of(t * NSA_KT, NSA_KT)
        blocks = [jnp.broadcast_to(bias_ref[pl.ds(t * per_tile + j, 1), :], (SLC_BLOCK, qc))
                  for j in range(per_tile)]
        bias = jnp.where(key_row + ks0 <= qpos, jnp.concatenate(blocks, axis=0), NEG)
        kt = ks_ref[0, pl.ds(ks0, NSA_KT), :]
        return [_nt(kt, qs[i * qc:(i + 1) * qc]) + bias for i in range(NSA_G)]

    slc = _flash_tiles(c // (NSA_KT // qc) + 1, [_online_init_t(qc)] * NSA_G, scores,
                       lambda t: _pv_tiles(vst_ref, t, 1, NSA_KT))
    outs = [partial_out[i] + gate(i, 1) * _normalise(slc[i][1])[0] for i in range(NSA_G)]
    o_ref[0, :, 0:LANES] = jnp.transpose(_merge_pair_t(outs[0], outs[1])).astype(o_ref.dtype)
    o_ref[0, :, LANES:2 * LANES] = jnp.transpose(_merge_pair_t(outs[2], outs[3])).astype(o_ref.dtype)


def _nsa(z_rot, z_pl, k_cmp, v_cmp, gate_logits, ovt):
    qc = NSA_QC
    seq_spec = lambda base: pl.BlockSpec((1, SEQ, LANES), lambda b, j, c: (b, 0, base + j))
    cmp_spec = pl.BlockSpec((1, 1, N_CMP_PAD, LANES), lambda b, j, c: (b, j, 0, 0))
    return pl.pallas_call(
        _nsa_kernel,
        out_shape=jax.ShapeDtypeStruct((BATCH, SEQ, NSA_HEADS * HEAD_DIM), BF16),
        grid=(BATCH, NSA_KV_HEADS, SEQ // qc),
        in_specs=[
            pl.BlockSpec((1, qc, LANES), lambda b, j, c: (b, c, ROT_NQ + 2 * j)),
            pl.BlockSpec((1, qc, LANES), lambda b, j, c: (b, c, ROT_NQ + 2 * j + 1)),
            cmp_spec, cmp_spec,
            seq_spec(ROT_NKS), seq_spec(PL_NVS), seq_spec(ROT_NKW), seq_spec(PL_NVW),
            pl.BlockSpec((qc, LANES), lambda b, j, c: (b * (SEQ // qc) + c, j)),
            pl.BlockSpec(ovt.shape, lambda b, j, c: (0, 0)),
        ],
        out_specs=pl.BlockSpec((1, qc, 2 * LANES), lambda b, j, c: (b, c, j)),
        scratch_shapes=[pltpu.VMEM((LANES, N_CMP_PAD), BF16),
                        pltpu.VMEM((SEQ // NSA_KT, VT_ROWS, NSA_KT), BF16),
                        pltpu.VMEM((SEQ // qc, VT_ROWS, qc), BF16),
                        pltpu.VMEM((N_SLC, qc), F32)],
        compiler_params=_params("parallel", "parallel", "arbitrary"),
        name="nsa",
    )(z_rot, z_rot, k_cmp, v_cmp, z_rot, z_pl, z_rot, z_pl, gate_logits, ovt)


DIL_QC = 128
DIL_STEPS = SEQ // DIL_QC
DIL_UNROLL = 8


def _dil_group(q_ref, k_ref, v_ref, og_ref, lg_ref, gi):
    window, dil = DIL_CONFIGS[gi]
    qc = DIL_QC
    m = SEQ // dil
    n_back = window // dil
    nk = min(m, qc + -(-n_back // qc) * qc)
    chunks = m // qc

    def rows(first, n):
        return pl.ds(first, n) if dil == 1 else pl.ds(first, n, stride=dil)

    def place(idx):
        r = idx // chunks
        q0 = (idx % chunks) * qc
        start = jnp.maximum(q0 - (nk - qc), 0)
        return rows(r + dil * q0, qc), rows(r + dil * start, nk), q0 - start

    def body(i, _):
        at = [place(i * DIL_UNROLL + u) for u in range(DIL_UNROLL)]
        s = [_nt(k_ref[0, k_rows, :].astype(BF16), _stack_heads(q_ref[0, q_rows, :].astype(BF16)))
             for q_rows, k_rows, _ in at]
        m, p = [], []
        for u, (_, _, off) in enumerate(at):
            s_u = s[u] + _tile_lanes(_band_bias_t(nk, qc, off, n_back), 2)
            m.append(_reduce_keys(s_u, jnp.maximum, jnp.max))
            p.append(_probs(s_u, m[u]))
        acc = [_nn(_transpose_aug(v_ref[0, k_rows, :]), p[u]) for u, (_, k_rows, _) in enumerate(at)]
        for u, (q_rows, _, _) in enumerate(at):
            o_t, l = _normalise(acc[u])
            lse_b = jnp.broadcast_to(m[u] + jnp.log(l) * LOG2_E, (LANES, 2 * qc))
            og_ref[gi, q_rows, :] = jnp.transpose(_merge_pair_t(o_t[:, :qc], o_t[:, qc:]))
            lg_ref[gi, q_rows, :] = jnp.transpose(_merge_pair_t(lse_b[:, :qc], lse_b[:, qc:]))
        return 0

    lax.fori_loop(0, DIL_STEPS // DIL_UNROLL, body, 0)


def _dil_kernel(q_ref, k_ref, v_ref, o_ref, og_ref, lg_ref):
    g = pl.program_id(2)
    n_groups = len(DIL_CONFIGS)
    for gi in range(n_groups):
        pl.when(g == gi)(functools.partial(_dil_group, q_ref, k_ref, v_ref, og_ref, lg_ref, gi))

    @pl.when(g == n_groups - 1)
    def _():
        rows = 512

        def body(i, _):
            sl = pl.ds(pl.multiple_of(i * rows, rows), rows)
            l0, l1, l2 = lg_ref[0, sl, :], lg_ref[1, sl, :], lg_ref[2, sl, :]
            mx = jnp.maximum(jnp.maximum(l0, l1), l2)
            e0, e1, e2 = jnp.exp2(l0 - mx), jnp.exp2(l1 - mx), jnp.exp2(l2 - mx)
            den = e0 + e1 + e2
            out = (e0 / den) * og_ref[0, sl, :] + (e1 / den) * og_ref[1, sl, :] + (e2 / den) * og_ref[2, sl, :]
            o_ref[0, sl, :] = out.astype(o_ref.dtype)
            return 0

        lax.fori_loop(0, SEQ // rows, body, 0)


def _dilated(zd_rot, zd_pl):
    n_groups = len(DIL_CONFIGS)
    width = DIL_HEADS_PER_GROUP * HEAD_DIM
    col = lambda base: (lambda b, p, g: (b, 0, base + 2 * g + p))
    blk = lambda base: pl.BlockSpec((1, SEQ, LANES), col(base))
    return pl.pallas_call(
        _dil_kernel,
        out_shape=jax.ShapeDtypeStruct((BATCH, SEQ, width), BF16),
        grid=(BATCH, 2, n_groups),
        in_specs=[blk(0), blk(DIL_BLOCKS), blk(0)],
        out_specs=pl.BlockSpec((1, SEQ, LANES), lambda b, p, g: (b, 0, p)),
        scratch_shapes=[pltpu.VMEM((n_groups, SEQ, LANES), F32), pltpu.VMEM((n_groups, SEQ, LANES), F32)],
        compiler_params=_params("parallel", "parallel", "arbitrary"),
        name="dilated",
    )(zd_rot, zd_rot, zd_pl)


def _out_proj_kernel(oa_ref, ob_ref, oc_ref, h_ref, wa_ref, wb_ref, wc_ref, g_ref, b_ref, h1_ref, h1b_ref):
    y = _nn(oa_ref[...], wa_ref[...]) + _nn(ob_ref[...], wb_ref[...]) + _nn(oc_ref[...], wc_ref[...])
    h1 = _layer_norm(DEEPNORM_ALPHA * h_ref[...] + y, g_ref[...], b_ref[...])
    h1_ref[...] = h1
    h1b_ref[...] = h1.astype(BF16)


def _out_proj(oa, ob, oc, h, wa, wb, wc, g, b):
    tm = 512
    rows = lambda w: pl.BlockSpec((tm, w), lambda i: (i, 0))
    full = lambda a: pl.BlockSpec(a.shape, lambda i: (0, 0))
    return pl.pallas_call(
        _out_proj_kernel,
        out_shape=(jax.ShapeDtypeStruct((TOKENS, D_MODEL), F32),
                   jax.ShapeDtypeStruct((TOKENS, D_MODEL), BF16)),
        grid=(TOKENS // tm,),
        in_specs=[rows(oa.shape[1]), rows(ob.shape[1]), rows(oc.shape[1]), rows(D_MODEL),
                  full(wa), full(wb), full(wc), full(g), full(b)],
        out_specs=(rows(D_MODEL), rows(D_MODEL)),
        compiler_params=_params("parallel"),
        name="out_proj_ln",
    )(oa, ob, oc, h, wa, wb, wc, g, b)


def _router_kernel(hb_ref, rw_ref, rb_ref, comb_ref, sel_ref):
    logits = _nt(rw_ref[...], hb_ref[...]) + rb_ref[...]
    mx = jnp.max(logits, axis=0, keepdims=True)
    ex = jnp.exp(logits - mx)
    probs = ex / jnp.sum(ex, axis=0, keepdims=True)
    p = [probs[e:e + 1, :] for e in range(N_EXPERTS)]
    best, g_sel = None, None
    for g in range(N_GROUPS):
        a, b, c, d = p[4 * g:4 * g + 4]
        hi1, lo1, hi2, lo2 = jnp.maximum(a, b), jnp.minimum(a, b), jnp.maximum(c, d), jnp.minimum(c, d)
        top2 = jnp.maximum(hi1, hi2) + jnp.maximum(jnp.minimum(hi1, hi2), jnp.maximum(lo1, lo2))
        if g == 0:
            best, g_sel = top2, jnp.zeros_like(top2)
        else:
            better = top2 > best
            best = jnp.where(better, top2, best)
            g_sel = jnp.where(better, float(g), g_sel)
    chosen, picked = [], []
    for e in range(N_EXPERTS):
        g = e // EXPERTS_PER_GROUP
        rank = jnp.zeros_like(best)
        for o in range(4 * g, 4 * g + 4):
            if o < e:
                rank = rank + jnp.where(p[o] >= p[e], 1.0, 0.0)
            elif o > e:
                rank = rank + jnp.where(p[o] > p[e], 1.0, 0.0)
        chosen.append(jnp.where((g_sel == float(g)) & (rank < 2.0), 1.0, 0.0))
        picked.append(chosen[e] * p[e])
    total = picked[0]
    for e in range(1, N_EXPERTS):
        total = total + picked[e]
    comb_ref[...] = jnp.concatenate(picked, axis=0) / total
    sel_ref[...] = jnp.concatenate(chosen, axis=0)


def _router(hb, rw_t, rb):
    tm = 1024
    out = jax.ShapeDtypeStruct((N_EXPERTS, TOKENS), F32)
    o_spec = pl.BlockSpec((N_EXPERTS, tm), lambda i: (0, i))
    return pl.pallas_call(
        _router_kernel,
        out_shape=(out, out),
        grid=(TOKENS // tm,),
        in_specs=[pl.BlockSpec((tm, D_MODEL), lambda i: (i, 0)),
                  pl.BlockSpec((N_EXPERTS, D_MODEL), lambda i: (0, 0)),
                  pl.BlockSpec((N_EXPERTS, 1), lambda i: (0, 0))],
        out_specs=(o_spec, o_spec),
        compiler_params=_params("parallel"),
        name="router",
    )(hb, rw_t, rb)


def _routing_tables(comb_t, sel_t):
    sel = sel_t > 0.5
    cnt = jnp.sum(sel, axis=1, dtype=jnp.int32)
    cnt_pad = (cnt + (MOE_TILE - 1)) // MOE_TILE * MOE_TILE
    ends = jnp.cumsum(cnt_pad)
    rank = jnp.cumsum(sel.astype(jnp.int32), axis=1) - 1
    pos = (ends - cnt_pad)[:, None] + rank
    pos_lo = jnp.min(jnp.where(sel, pos, MOE_ROWS), axis=0)
    pos_hi = jnp.max(jnp.where(sel, pos, -1), axis=0)
    w_lo = jnp.sum(jnp.where(sel & (pos == pos_lo), comb_t, 0.0), axis=0)
    w_hi = jnp.sum(jnp.where(sel & (pos == pos_hi), comb_t, 0.0), axis=0)
    w = jnp.zeros((TOKENS, LANES), F32).at[:, 0].set(w_lo).at[:, 1].set(w_hi)
    n_tiles = ends[-1] // MOE_TILE
    tile_start = jnp.arange(MOE_TILES, dtype=jnp.int32) * MOE_TILE
    tile_start = jnp.minimum(tile_start, ends[-1] - MOE_TILE)
    tile_expert = jnp.sum((ends[None, :] <= tile_start[:, None]).astype(jnp.int32), axis=1)
    return jnp.stack([pos_lo, pos_hi]).astype(jnp.int32), w, tile_expert, n_tiles.reshape(1).astype(jnp.int32)


MOE_TILE = 256
MOE_TILES = 2 * TOKENS // MOE_TILE + N_EXPERTS
MOE_ROWS = MOE_TILES * MOE_TILE
SLAB = D_MODEL // LANES


def _to_slabs(ref, x, rows):
    for j in range(SLAB):
        ref[pl.ds(j, rows, stride=SLAB), :] = x[:, j * LANES:(j + 1) * LANES]


def _from_slabs(ref, rows, pitch=SLAB):
    return jnp.concatenate([ref[pl.ds(j, rows, stride=pitch), :] for j in range(SLAB)], axis=1)


GATHER_PITCH = SLAB + SUBLANES


def _slab_rows(row, n=SLAB):
    return pl.ds(pl.multiple_of(row * n, n), n)


XSLAB = SLAB // 2
U32 = jnp.uint32


def _to_packed_slabs(ref, x, rows):
    bits = lambda t: lax.bitcast_convert_type(t.astype(BF16).astype(F32), U32)
    for j in range(XSLAB):
        hi = bits(x[:, 2 * j * LANES:(2 * j + 1) * LANES])
        lo = bits(x[:, (2 * j + 1) * LANES:(2 * j + 2) * LANES])
        ref[pl.ds(j, rows, stride=XSLAB), :] = hi | (lo >> 16)


def _from_packed_slabs(ref, rows):
    parts = []
    for j in range(XSLAB):
        u = ref[pl.ds(j, rows, stride=XSLAB), :]
        parts.append(lax.bitcast_convert_type(u & jnp.uint32(0xFFFF0000), F32).astype(BF16))
        parts.append(lax.bitcast_convert_type(u << 16, F32).astype(BF16))
    return jnp.concatenate(parts, axis=1)


def _dispatch_kernel(pos_ref, h_ref, init_ref, xs_ref, slab_ref, sem):
    del init_ref
    tm = h_ref.shape[0]
    base = pl.program_id(0) * tm
    _to_packed_slabs(slab_ref, h_ref[...], tm)

    def copy(t, which):
        return pltpu.make_async_copy(slab_ref.at[_slab_rows(t, XSLAB), :],
                                     xs_ref.at[_slab_rows(pos_ref[which, base + t], XSLAB), :], sem)

    def start(t, _):
        copy(t, 0).start(priority=0)
        copy(t, 1).start(priority=1)
        return 0

    lax.fori_loop(0, tm, start, 0, unroll=8)
    whole = pltpu.make_async_copy(slab_ref, xs_ref.at[pl.ds(0, tm * XSLAB), :], sem)
    whole.wait()
    whole.wait()


def _dispatch(pos, h):
    tm = 256
    grid_spec = pltpu.PrefetchScalarGridSpec(
        num_scalar_prefetch=1,
        grid=(TOKENS // tm,),
        in_specs=[pl.BlockSpec((tm, D_MODEL), lambda i, pos: (i, 0)),
                  pl.BlockSpec(memory_space=pl.ANY)],
        out_specs=pl.BlockSpec(memory_space=pl.ANY),
        scratch_shapes=[pltpu.VMEM((tm * XSLAB, LANES), U32), pltpu.SemaphoreType.DMA],
    )
    return pl.pallas_call(
        _dispatch_kernel,
        out_shape=jax.ShapeDtypeStruct((MOE_ROWS * XSLAB, LANES), U32),
        grid_spec=grid_spec,
        input_output_aliases={2: 0},
        compiler_params=_params("arbitrary"),
        name="moe_dispatch",
    )(pos, h, jnp.zeros((MOE_ROWS * XSLAB, LANES), U32))


def _experts_kernel(te_ref, nt_ref, xs_ref, wg_ref, wu_ref, wd_ref, ys_ref, wgb_ref, wub_ref, wdb_ref):
    k = pl.program_id(0)
    e = te_ref[k]
    e_prev = te_ref[jnp.maximum(k - 1, 0)]

    @pl.when((k == 0) | (e != e_prev))
    def _():
        wgb_ref[...] = wg_ref[0, 0].astype(BF16)
        wub_ref[...] = wu_ref[0, 0].astype(BF16)
        wdb_ref[...] = wd_ref[0, 0].astype(BF16)

    @pl.when(k < nt_ref[0])
    def _():
        x = _from_packed_slabs(xs_ref, MOE_TILE)
        hid = jax.nn.silu(_nn(x, wgb_ref[...])) * _nn(x, wub_ref[...])
        _to_slabs(ys_ref, _nn(hid.astype(BF16), wdb_ref[...]), MOE_TILE)

    @pl.when(k >= nt_ref[0])
    def _():
        ys_ref[...] = jnp.zeros(ys_ref.shape, F32)


def _experts(tile_expert, n_tiles, xs, wg, wu, wd, layer):
    w_in_spec = pl.BlockSpec((1, 1, D_MODEL, EXPERT_HIDDEN), lambda k, te, nt: (layer, te[k], 0, 0))
    grid_spec = pltpu.PrefetchScalarGridSpec(
        num_scalar_prefetch=2,
        grid=(MOE_TILES,),
        in_specs=[pl.BlockSpec((MOE_TILE * XSLAB, LANES), lambda k, te, nt: (jnp.minimum(k, nt[0] - 1), 0)),
                  w_in_spec, w_in_spec,
                  pl.BlockSpec((1, 1, EXPERT_HIDDEN, D_MODEL), lambda k, te, nt: (layer, te[k], 0, 0))],
        out_specs=pl.BlockSpec((MOE_TILE * SLAB, LANES), lambda k, te, nt: (k, 0)),
        scratch_shapes=[pltpu.VMEM((D_MODEL, EXPERT_HIDDEN), BF16), pltpu.VMEM((D_MODEL, EXPERT_HIDDEN), BF16),
                        pltpu.VMEM((EXPERT_HIDDEN, D_MODEL), BF16)],
    )
    return pl.pallas_call(
        _experts_kernel,
        out_shape=jax.ShapeDtypeStruct((MOE_ROWS * SLAB, LANES), F32),
        grid_spec=grid_spec,
        compiler_params=_params("arbitrary"),
        name="moe_experts",
    )(tile_expert, n_tiles, xs, wg, wu, wd)


def _ple_ln_kernel(pos_ref, hb_ref, h_ref, ys_ref, w_ref, p_ref, gw_ref, gb_ref, pw_ref, g_ref, b_ref,
                   h2_ref, h2b_ref, lo_ref, hi_ref, sem):
    tm = h_ref.shape[0]
    i = pl.program_id(0)
    slot = i & 1
    bufs = (lo_ref, hi_ref)

    def fetch(tile, into):
        def start(t, _):
            for which in range(2):
                pltpu.make_async_copy(ys_ref.at[_slab_rows(pos_ref[which, tile * tm + t]), :],
                                      bufs[which].at[into, pl.ds(pl.multiple_of(t * GATHER_PITCH, SUBLANES), SLAB), :],
                                      sem.at[into]).start(priority=which)
            return 0
        lax.fori_loop(0, tm, start, 0, unroll=8)

    @pl.when(i == 0)
    def _():
        fetch(0, 0)

    @pl.when(i + 1 < pl.num_programs(0))
    def _():
        fetch(i + 1, 1 - slot)

    gate = jax.nn.sigmoid(_nn(hb_ref[...], gw_ref[...]) + gb_ref[...])
    ple = gate * _nn(p_ref[...].astype(BF16), pw_ref[...])
    for which in range(2):
        pltpu.make_async_copy(ys_ref.at[pl.ds(0, tm * SLAB), :], bufs[which].at[slot, pl.ds(0, tm * SLAB), :],
                              sem.at[slot]).wait()
    w = w_ref[...]
    ffn = (w[:, 0:1] * _from_slabs(lo_ref.at[slot], tm, GATHER_PITCH)
           + w[:, 1:2] * _from_slabs(hi_ref.at[slot], tm, GATHER_PITCH))
    h2 = _layer_norm(DEEPNORM_ALPHA * h_ref[...] + ffn + ple, g_ref[...], b_ref[...])
    h2_ref[...] = h2
    h2b_ref[...] = h2.astype(BF16)


def _ple_ln(pos, hb, h, ys, w, p, layer, gw, gb, pw, g, b):
    tm = 256
    p_spec = pl.BlockSpec((tm, PLE_DIM), lambda i, pos: (layer * (TOKENS // tm) + i, 0))
    rows = lambda width: pl.BlockSpec((tm, width), lambda i, pos: (i, 0))
    full = lambda a: pl.BlockSpec(a.shape, lambda i, pos: (0, 0))
    grid_spec = pltpu.PrefetchScalarGridSpec(
        num_scalar_prefetch=1,
        grid=(TOKENS // tm,),
        in_specs=[rows(D_MODEL), rows(D_MODEL), pl.BlockSpec(memory_space=pl.ANY), rows(LANES), p_spec,
                  full(gw), full(gb), full(pw), full(g), full(b)],
        out_specs=(rows(D_MODEL), rows(D_MODEL)),
        scratch_shapes=[pltpu.VMEM((2, tm * GATHER_PITCH, LANES), F32), pltpu.VMEM((2, tm * GATHER_PITCH, LANES), F32),
                        pltpu.SemaphoreType.DMA((2,))],
    )
    return pl.pallas_call(
        _ple_ln_kernel,
        out_shape=(jax.ShapeDtypeStruct((TOKENS, D_MODEL), F32),
                   jax.ShapeDtypeStruct((TOKENS, D_MODEL), BF16)),
        grid_spec=grid_spec,
        compiler_params=_params("arbitrary"),
        name="ple_ln",
    )(pos, hb, h, ys, w, p, gw, gb, pw, g, b)


def _rope_tables(positions):
    half = ROT_DIM // 2
    inv_freq = jnp.exp(jnp.arange(half, dtype=F32) * (-2.0 * math.log(ROPE_THETA) / ROT_DIM))
    ang = positions.astype(F32)[:, :, None] * inv_freq
    cos, sin = jnp.cos(ang), jnp.sin(ang)
    zeros = jnp.zeros_like(cos)
    rest = HEAD_DIM - ROT_DIM
    pad = lambda v: jnp.broadcast_to(jnp.asarray(v, F32), cos.shape[:2] + (rest,))
    c = jnp.concatenate([cos, cos, pad(1.0)], axis=-1)
    s1 = jnp.concatenate([-sin, zeros, pad(0.0)], axis=-1)
    s2 = jnp.concatenate([zeros, sin, pad(0.0)], axis=-1)
    tile = lambda t: jnp.concatenate([t, t], axis=-1).reshape(TOKENS, LANES)
    return tile(c), tile(s1), tile(s2)


def _split_w_in(w):
    mw, nq, nkv, dw = MOBA_HEADS * HEAD_DIM, NSA_HEADS * HEAD_DIM, NSA_KV_HEADS * HEAD_DIM, DIL_HEADS * HEAD_DIM
    widths = (mw, mw, mw, nq) + (nkv,) * 6 + (NSA_HEADS * 3, dw, dw, dw)
    offs = np.concatenate([[0], np.cumsum(widths)])
    qa, ka, va, qb, kbc, vbc, kbs, vbs, kbw, vbw, gb, qc, kc, vc = (
        w[:, int(offs[i]):int(offs[i + 1])] for i in range(len(widths)))

    def dup(t):
        t = t.reshape(D_MODEL, NSA_KV_HEADS, 1, HEAD_DIM)
        return jnp.broadcast_to(t, (D_MODEL, NSA_KV_HEADS, 2, HEAD_DIM)).reshape(D_MODEL, NSA_KV_HEADS * LANES)

    zpad = lambda n: jnp.zeros((D_MODEL, n * LANES), w.dtype)
    w_rot = jnp.concatenate([qa * Q_SCALE, ka, qb * Q_SCALE, dup(kbc), dup(kbs), dup(kbw), zpad(1)], axis=1)
    w_pl = jnp.concatenate([va, dup(vbc), dup(vbs), dup(vbw), zpad(3)], axis=1)
    n_gates = 3 * NSA_HEADS // NSA_KV_HEADS
    gpad = jnp.zeros((D_MODEL, NSA_KV_HEADS, LANES - n_gates), w.dtype)
    w_gl = jnp.concatenate([gb.reshape(D_MODEL, NSA_KV_HEADS, n_gates), gpad], axis=-1).reshape(D_MODEL, -1)
    w_dil_rot = jnp.concatenate([qc * Q_SCALE, kc], axis=1)
    return tuple(t.astype(BF16) for t in (w_rot, w_pl, w_gl, w_dil_rot, vc))


def _overlap_table():
    starts = np.arange(N_CMP) * CMP_STRIDE
    slc = np.arange(N_SLC) * SLC_BLOCK
    ov = ((starts[:, None] < slc[None, :] + SLC_BLOCK) & (starts[:, None] + CMP_LEN > slc[None, :]))
    ovt = np.zeros((N_SLC, N_CMP_PAD), np.float32)
    ovt[:, :N_CMP] = ov.T
    return jnp.asarray(ovt, BF16)


def _cmp_chunks(z, base):
    nblk = z.shape[-1] // LANES
    t = z.reshape(BATCH, SEQ // CMP_STRIDE, CMP_STRIDE, nblk, LANES)[:, :, :, base:base + NSA_KV_HEADS, :HEAD_DIM]
    return t.transpose(0, 3, 1, 2, 4).reshape(BATCH, NSA_KV_HEADS, SEQ // CMP_STRIDE, CMP_STRIDE * HEAD_DIM)


def kernel(x, p, positions, ln_in_g, ln_in_b, w_in, w_out, nsa_ck1, nsa_ck2, nsa_pe_k, nsa_cv1, nsa_cv2, nsa_pe_v, ln1_g, ln1_b, router_w, router_b, w_gate, w_up, w_down, ple_proj, ple_gate_w, ple_gate_b, ln2_g, ln2_b):
    rope = _rope_tables(positions)
    ovt = _overlap_table()
    rw_t = router_w.T.astype(BF16)
    rb = router_b.reshape(N_EXPERTS, 1).astype(F32)
    chunk_w = CMP_STRIDE * HEAD_DIM
    vec = lambda v: v.reshape(1, -1)
    seq3 = lambda t: t.reshape(BATCH, SEQ, t.shape[-1])
    flat = lambda t: t.reshape(TOKENS, t.shape[-1])

    h, hb = _ln_in(x.reshape(TOKENS, D_MODEL), ln_in_g, ln_in_b)
    for i in range(DEPTH):
        w_rot, w_pl, w_gl, w_dil_rot, w_dil_pl = _split_w_in(w_in[i])
        z_rot = seq3(_project(hb, w_rot, BF16, 768, rope=rope))
        z_pl = seq3(_project(hb, w_pl, BF16, 1024))
        gate_logits = _project(hb, w_gl, F32, NSA_KV_HEADS * LANES)
        zd_rot = seq3(_project(hb, w_dil_rot, F32, 768, rope=rope))
        zd_pl = seq3(_project(hb, w_dil_pl, F32, 768))

        o_a = _moba(z_rot, z_pl)

        dup2 = lambda w2: jnp.concatenate([w2, w2], axis=1).astype(BF16)
        k_cmp, v_cmp = _compress(
            _cmp_chunks(z_rot, ROT_NKC), _cmp_chunks(z_pl, PL_NVC),
            nsa_pe_k[i].reshape(2, chunk_w), nsa_pe_v[i].reshape(2, chunk_w),
            nsa_ck1[i].reshape(2, chunk_w, CMP_HIDDEN).astype(BF16), dup2(nsa_ck2[i]),
            nsa_cv1[i].reshape(2, chunk_w, CMP_HIDDEN).astype(BF16), dup2(nsa_cv2[i]))
        o_b = _nsa(z_rot, z_pl, k_cmp, v_cmp, gate_logits, ovt)

        o_c = _dilated(zd_rot, zd_pl)

        wo = w_out[i].astype(BF16)
        a_w, b_w = MOBA_HEADS * HEAD_DIM, NSA_HEADS * HEAD_DIM
        h, hb = _out_proj(flat(o_a), flat(o_b), flat(o_c), h,
                          wo[:a_w], wo[a_w:a_w + b_w], wo[a_w + b_w:], vec(ln1_g[i]), vec(ln1_b[i]))

        pos, w_tok, tile_expert, n_tiles = _routing_tables(*_router(hb, rw_t, rb))
        xs = _dispatch(pos, h)
        ys = _experts(tile_expert, n_tiles, xs, w_gate, w_up, w_down, i)
        h, hb = _ple_ln(pos, hb, h, ys, w_tok, p.reshape(DEPTH * TOKENS, PLE_DIM), i, ple_gate_w[i].astype(BF16),
                        vec(ple_gate_b[i]), ple_proj[i].astype(BF16), vec(ln2_g[i]), vec(ln2_b[i]))
    return h.reshape(BATCH, SEQ, D_MODEL)
```

```python
import functools
import math

import numpy as np
import jax
import jax.numpy as jnp
from jax import lax
from jax.experimental import pallas as pl
from jax.experimental.pallas import tpu as pltpu

F32 = jnp.float32
BF16 = jnp.bfloat16

D_MODEL = 2048
BATCH = 2
SEQ = 4096
DEPTH = 4
TOKENS = BATCH * SEQ
HEAD_DIM = 64
ROT_DIM = HEAD_DIM // 4
ROPE_THETA = 500000.0
NEG = -1e30
FORCE = 1e30
LN_EPS = 1e-5
SCALE = HEAD_DIM ** -0.5
LOG2_E = math.log2(math.e)
Q_SCALE = SCALE * LOG2_E

MOBA_HEADS = 8
MOBA_BLOCK = 256
MOBA_TOPK = 3
MOBA_NB = SEQ // MOBA_BLOCK

NSA_HEADS = 12
NSA_KV_HEADS = 3
CMP_LEN = 32
CMP_STRIDE = 16
CMP_HIDDEN = 128
N_CMP = (SEQ - CMP_LEN) // CMP_STRIDE + 1
N_CMP_PAD = 256
SLC_BLOCK = 64
SLC_TOPK = 16
SLC_LOCAL = 2
N_SLC = SEQ // SLC_BLOCK
NSA_WINDOW = 512

DIL_CONFIGS = ((128, 1), (512, 4), (2048, 16))
DIL_HEADS_PER_GROUP = 4
DIL_HEADS = DIL_HEADS_PER_GROUP * len(DIL_CONFIGS)

N_EXPERTS = 16
N_GROUPS = 4
EXPERTS_PER_GROUP = 4
EXPERT_HIDDEN = D_MODEL // 4
PLE_DIM = 256

DEEPNORM_ALPHA = (2 * DEPTH) ** 0.25

LANES = 128
SUBLANES = 8
BF16_SUBLANES = 16
VMEM_LIMIT = 56 * 1024 * 1024
MOBA_SHIFT = MOBA_BLOCK.bit_length() - 1
SLC_SHIFT = SLC_BLOCK.bit_length() - 1

ROT_MQ, ROT_MK, ROT_NQ, ROT_NKC, ROT_NKS, ROT_NKW = 0, 4, 8, 14, 17, 20
ROT_BLOCKS = 24
PL_MV, PL_NVC, PL_NVS, PL_NVW = 0, 4, 7, 10
PL_BLOCKS = 16
DIL_BLOCKS = DIL_HEADS // 2

NT_DIMS = (((1,), (1,)), ((), ()))


def _nt(a, b):
    return lax.dot_general(a, b, NT_DIMS, preferred_element_type=F32)


def _nn(a, b):
    return jnp.dot(a, b, preferred_element_type=F32)


def _params(*sem):
    return pltpu.CompilerParams(dimension_semantics=sem, vmem_limit_bytes=VMEM_LIMIT)


def _layer_norm(y, g, b):
    mu = jnp.mean(y, axis=-1, keepdims=True)
    yc = y - mu
    var = jnp.mean(yc * yc, axis=-1, keepdims=True)
    return yc * lax.rsqrt(var + LN_EPS) * g + b


def _ln_kernel(x_ref, g_ref, b_ref, h_ref, hb_ref):
    h = _layer_norm(x_ref[...], g_ref[...], b_ref[...])
    h_ref[...] = h
    hb_ref[...] = h.astype(BF16)


def _ln_in(x, g, b):
    tm = 512
    row = pl.BlockSpec((tm, D_MODEL), lambda i: (i, 0))
    vec = pl.BlockSpec((1, D_MODEL), lambda i: (0, 0))
    return pl.pallas_call(
        _ln_kernel,
        out_shape=(jax.ShapeDtypeStruct((TOKENS, D_MODEL), F32),
                   jax.ShapeDtypeStruct((TOKENS, D_MODEL), BF16)),
        grid=(TOKENS // tm,),
        in_specs=[row, vec, vec],
        out_specs=(row, row),
        compiler_params=_params("parallel"),
        name="ln_in",
    )(x, g.reshape(1, -1), b.reshape(1, -1))


def _proj_kernel(x_ref, w_ref, o_ref):
    o_ref[...] = _nn(x_ref[...], w_ref[...]).astype(o_ref.dtype)


def _proj_rot_kernel(x_ref, w_ref, c_ref, s1_ref, s2_ref, o_ref):
    x = x_ref[...]
    c, s1, s2 = c_ref[...], s1_ref[...], s2_ref[...]
    half = ROT_DIM // 2
    for j0 in range(0, o_ref.shape[1], 2 * LANES):
        z = _nn(x, w_ref[:, j0:j0 + 2 * LANES])
        for j in range(j0, j0 + 2 * LANES, LANES):
            zc = z[:, j - j0:j - j0 + LANES]
            r = zc * c + pltpu.roll(zc, LANES - half, 1) * s1 + pltpu.roll(zc, half, 1) * s2
            o_ref[:, j:j + LANES] = r.astype(o_ref.dtype)


def _project(hb, w, out_dtype, tn, rope=None):
    tm = 1024
    n = w.shape[1]
    x_spec = pl.BlockSpec((tm, D_MODEL), lambda i, j: (i, 0))
    w_spec = pl.BlockSpec((D_MODEL, tn), lambda i, j: (0, j))
    o_spec = pl.BlockSpec((tm, tn), lambda i, j: (i, j))
    if rope is None:
        kern, extra, extra_specs = _proj_kernel, (), []
    else:
        t_spec = pl.BlockSpec((tm, LANES), lambda i, j: (i, 0))
        kern, extra, extra_specs = _proj_rot_kernel, rope, [t_spec] * 3
    return pl.pallas_call(
        kern,
        out_shape=jax.ShapeDtypeStruct((TOKENS, n), out_dtype),
        grid=(TOKENS // tm, n // tn),
        in_specs=[x_spec, w_spec] + extra_specs,
        out_specs=o_spec,
        compiler_params=_params("parallel", "arbitrary"),
        name="in_proj_rot" if rope is not None else "in_proj",
    )(hb, w, *extra)


def _stack_heads(*q_blocks):
    parts = []
    for q in q_blocks:
        lane = lax.broadcasted_iota(jnp.int32, q.shape, 1)
        zero = jnp.zeros_like(q)
        parts += [jnp.where(lane < HEAD_DIM, q, zero), jnp.where(lane >= HEAD_DIM, q, zero)]
    return jnp.concatenate(parts, axis=0)


def _merge_pair_t(lo, hi):
    sub = lax.broadcasted_iota(jnp.int32, lo.shape, 0)
    return jnp.where(sub < HEAD_DIM, lo, hi)


def _band_bias_t(nk, qc, offset, n_back):
    key = lax.broadcasted_iota(jnp.int32, (nk, qc), 0)
    qry = lax.broadcasted_iota(jnp.int32, (nk, qc), 1)
    diff = offset + qry - key
    return jnp.where((diff >= 0) & (diff <= n_back), 0.0, NEG)


def _tile_lanes(x, n):
    return jnp.concatenate([x] * n, axis=1)


def _transpose_bf16(v):
    return jnp.transpose(v.astype(F32)).astype(BF16)


def _tree(x, op):
    n = x.shape[0]
    if n == SUBLANES:
        return x
    if n % (2 * SUBLANES) == 0:
        return op(_tree(x[:n // 2], op), _tree(x[n // 2:], op))
    acc = x[:SUBLANES]
    for i in range(1, n // SUBLANES):
        acc = op(acc, x[SUBLANES * i:SUBLANES * (i + 1)])
    return acc


def _reduce_keys(x, op, final):
    return final(_tree(x, op), axis=0, keepdims=True)


VT_ROWS = LANES + BF16_SUBLANES


def _transpose_aug(v):
    vt = jnp.transpose(v.astype(F32))
    sub = lax.broadcasted_iota(jnp.int32, (VT_ROWS - LANES, v.shape[0]), 0)
    return jnp.concatenate([vt, jnp.where(sub == 0, 1.0, 0.0)], axis=0).astype(BF16)


def _probs(s_t, m):
    return jnp.exp2((s_t - m).astype(BF16))


def _normalise(acc):
    l = acc[LANES:LANES + 1]
    return acc[:LANES] / l, l


def _softmax_block_t(s_t, pv):
    m = _reduce_keys(s_t, jnp.maximum, jnp.max)
    out, l = _normalise(pv(_probs(s_t, m)))
    return out, m + jnp.log(l) * LOG2_E


def _online_step_t(carry, s_t, m_t, pv):
    m, acc = carry
    m_new = jnp.maximum(m, m_t)
    acc = jnp.exp2(m - m_new) * acc + pv(_probs(s_t, m_new))
    return m_new, acc


def _flash_tiles(n_tiles, init, scores, pv_of, may_be_empty=False):
    strips = range(len(init))

    def produce(t):
        s = tuple(scores(t))
        return s, tuple(_reduce_keys(s_i, jnp.maximum, jnp.max) for s_i in s)

    def consume(state, s_t, m_t, t):
        pv = pv_of(t)
        return tuple(_online_step_t(state[i], s_t[i], m_t[i], pv) for i in strips)

    def body(t, carry):
        state, s_t, m_t = carry
        s_next, m_next = produce(t + 1)
        return consume(state, s_t, m_t, t), s_next, m_next

    state, s_last, m_last = lax.fori_loop(0, n_tiles - 1, body, (tuple(init),) + produce(0))
    done = consume(state, s_last, m_last, jnp.maximum(n_tiles - 1, 0))
    if may_be_empty:
        done = jax.tree_util.tree_map(lambda new, old: jnp.where(n_tiles > 0, new, old), done, state)
    return done


def _pv_tiles(vt_ref, first, n, rows):
    def pv(p):
        acc = _nn(vt_ref[first], p[:rows])
        for j in range(1, n):
            acc = acc + _nn(vt_ref[first + j], p[j * rows:(j + 1) * rows])
        return acc
    return pv


def _online_init_t(r):
    return (jnp.full((1, r), NEG, F32), jnp.zeros((VT_ROWS, r), F32))


def _rank_rows(g, n_rows):
    sub = lax.broadcasted_iota(jnp.int32, (SUBLANES, g.shape[1]), 0)
    rank = jnp.zeros(g.shape, F32)
    for m in range(n_rows):
        gm = g[m:m + 1, :]
        b = m // SUBLANES * SUBLANES
        mid = g[b:b + SUBLANES]
        parts = [jnp.where(gm > mid, 1.0, jnp.where((gm == mid) & (sub > m - b), 1.0, 0.0))]
        if b > 0:
            parts.insert(0, jnp.where(gm > g[:b], 1.0, 0.0))
        if b + SUBLANES < n_rows:
            parts.append(jnp.where(gm >= g[b + SUBLANES:], 1.0, 0.0))
        rank = rank + jnp.concatenate(parts, axis=0)
    return rank


MOBA_KT = 2 * MOBA_BLOCK
MOBA_QC = MOBA_KT


def _moba_kernel(q_ref, k_ref, v_ref, o_ref, kmean_ref, vt_ref, bias_ref):
    c = pl.program_id(2)
    qc = MOBA_QC

    @pl.when(c == 0)
    def _():
        row = lax.broadcasted_iota(jnp.int32, (MOBA_NB, SEQ), 0)
        col = lax.broadcasted_iota(jnp.int32, (MOBA_NB, SEQ), 1)
        avg = jnp.where((col >> MOBA_SHIFT) == row, 1.0 / MOBA_BLOCK, 0.0).astype(BF16)
        kmean_ref[...] = _nn(avg, k_ref[0])
        for t in range(MOBA_NB):
            vt_ref[t] = _transpose_aug(v_ref[0, t * MOBA_BLOCK:(t + 1) * MOBA_BLOCK, :])

    qs = _stack_heads(q_ref[0])
    per_tile = MOBA_KT // MOBA_BLOCK

    ks = pl.multiple_of(c * MOBA_KT, MOBA_KT)
    heads = [qs[:qc], qs[qc:]]
    raw_own = [_nt(k_ref[0, pl.ds(ks, MOBA_KT), :], q_h) for q_h in heads]

    gate = _nt(kmean_ref[...].astype(BF16), qs)
    blk = lax.broadcasted_iota(jnp.int32, gate.shape, 0)
    q_idx = lax.broadcasted_iota(jnp.int32, gate.shape, 1) & (qc - 1)
    past = blk < c * per_tile + (q_idx >> MOBA_SHIFT)
    rank = _rank_rows(jnp.where(past, gate, NEG), MOBA_NB)
    bias_ref[...] = jnp.where(past & (rank < MOBA_TOPK), 0.0, NEG)

    key = lax.broadcasted_iota(jnp.int32, (MOBA_KT, qc), 0)
    qry = lax.broadcasted_iota(jnp.int32, (MOBA_KT, qc), 1)
    first_block = bias_ref[pl.ds(c * per_tile, 1), :]
    state = []
    for h in range(2):
        other = jnp.broadcast_to(first_block[:, h * qc:(h + 1) * qc], (MOBA_KT, qc))
        same_block = (key >> MOBA_SHIFT) == (qry >> MOBA_SHIFT)
        own_bias = jnp.where(key <= qry, jnp.where(same_block, 0.0, other), NEG)
        s_h = raw_own[h] + own_bias
        state.append(_online_step_t(_online_init_t(qc), s_h, _reduce_keys(s_h, jnp.maximum, jnp.max),
                                    _pv_tiles(vt_ref, c * per_tile, per_tile, MOBA_BLOCK)))

    def scores(t):
        ks = pl.multiple_of(t * MOBA_KT, MOBA_KT)
        kt = k_ref[0, pl.ds(ks, MOBA_KT), :]
        rows = [bias_ref[pl.ds(t * per_tile + j, 1), :] for j in range(per_tile)]
        out = []
        for h in range(2):
            blocks = [jnp.broadcast_to(row[:, h * qc:(h + 1) * qc], (MOBA_BLOCK, qc)) for row in rows]
            out.append(_nt(kt, heads[h]) + jnp.concatenate(blocks, axis=0))
        return out

    state = _flash_tiles(c, state, scores,
                         lambda t: _pv_tiles(vt_ref, t * per_tile, per_tile, MOBA_BLOCK), may_be_empty=True)
    o_lo, o_hi = (_normalise(acc)[0] for _, acc in state)
    o_ref[0] = jnp.transpose(_merge_pair_t(o_lo, o_hi)).astype(o_ref.dtype)


def _moba(z_rot, z_pl):
    qc = MOBA_QC
    grid = (BATCH, MOBA_HEADS // 2, SEQ // qc)
    return pl.pallas_call(
        _moba_kernel,
        out_shape=jax.ShapeDtypeStruct((BATCH, SEQ, MOBA_HEADS * HEAD_DIM), BF16),
        grid=grid,
        in_specs=[
            pl.BlockSpec((1, qc, LANES), lambda b, p, c: (b, c, ROT_MQ + p)),
            pl.BlockSpec((1, SEQ, LANES), lambda b, p, c: (b, 0, ROT_MK + p)),
            pl.BlockSpec((1, SEQ, LANES), lambda b, p, c: (b, 0, PL_MV + p)),
        ],
        out_specs=pl.BlockSpec((1, qc, LANES), lambda b, p, c: (b, c, p)),
        scratch_shapes=[pltpu.VMEM((MOBA_NB, LANES), F32),
                        pltpu.VMEM((MOBA_NB, VT_ROWS, MOBA_BLOCK), BF16),
                        pltpu.VMEM((MOBA_NB, 2 * qc), F32)],
        compiler_params=_params("parallel", "parallel", "arbitrary"),
        name="moba",
    )(z_rot, z_rot, z_pl)


def _compress_one(x_ref, pe_ref, w1_ref, w2_ref, o_ref):
    x = x_ref[0, 0].astype(F32)
    top = (x + pe_ref[0:1, :]).astype(BF16)
    bot = (x + pe_ref[1:2, :]).astype(BF16)
    a = _nn(top, w1_ref[0])
    bm = _nn(bot, w1_ref[1])
    pre = a + pltpu.roll(bm, N_CMP_PAD - 1, 0)
    hid = jax.nn.gelu(pre)
    out = _nn(hid.astype(BF16), w2_ref[...])
    row = lax.broadcasted_iota(jnp.int32, out.shape, 0)
    o_ref[0, 0] = jnp.where(row < N_CMP, out, 0.0).astype(o_ref.dtype)


def _compress_kernel(xk_ref, xv_ref, pk_ref, pv_ref, k1_ref, k2_ref, v1_ref, v2_ref, ok_ref, ov_ref):
    _compress_one(xk_ref, pk_ref, k1_ref, k2_ref, ok_ref)
    _compress_one(xv_ref, pv_ref, v1_ref, v2_ref, ov_ref)


def _compress(xk, xv, pk, pv, k1, k2, v1, v2):
    chunk_w = CMP_STRIDE * HEAD_DIM
    x_spec = pl.BlockSpec((1, 1, N_CMP_PAD, chunk_w), lambda b, j: (b, j, 0, 0))
    pe_spec = pl.BlockSpec((2, chunk_w), lambda b, j: (0, 0))
    w1_spec = pl.BlockSpec((2, chunk_w, CMP_HIDDEN), lambda b, j: (0, 0, 0))
    w2_spec = pl.BlockSpec((CMP_HIDDEN, LANES), lambda b, j: (0, 0))
    o_spec = pl.BlockSpec((1, 1, N_CMP_PAD, LANES), lambda b, j: (b, j, 0, 0))
    o_shape = jax.ShapeDtypeStruct((BATCH, NSA_KV_HEADS, N_CMP_PAD, LANES), BF16)
    return pl.pallas_call(
        _compress_kernel,
        out_shape=(o_shape, o_shape),
        grid=(BATCH, NSA_KV_HEADS),
        in_specs=[x_spec, x_spec, pe_spec, pe_spec, w1_spec, w2_spec, w1_spec, w2_spec],
        out_specs=(o_spec, o_spec),
        compiler_params=_params("parallel", "parallel"),
        name="nsa_compress",
    )(xk, xv, pk, pv, k1, k2, v1, v2)


NSA_QC = 512
NSA_KT = 512
NSA_G = NSA_HEADS // NSA_KV_HEADS
NSA_WIN_TILES = NSA_WINDOW // NSA_QC + 1


def _nsa_kernel(qa_ref, qb_ref, kc_ref, vc_ref, ks_ref, vs_ref, kw_ref, vw_ref, gl_ref, ovt_ref,
                o_ref, vct_ref, vst_ref, vwt_ref, bias_ref):
    c = pl.program_id(2)
    qc = NSA_QC
    q0 = c * qc
    lanes_of = lambda t, i: t[:, i * qc:(i + 1) * qc]

    @pl.when(c == 0)
    def _():
        vct_ref[...] = _transpose_bf16(vc_ref[0, 0])
        for t in range(SEQ // NSA_KT):
            vst_ref[t] = _transpose_aug(vs_ref[0, t * NSA_KT:(t + 1) * NSA_KT, :])
        for t in range(SEQ // qc):
            vwt_ref[t] = _transpose_aug(vw_ref[0, t * qc:(t + 1) * qc, :])

    qs = _stack_heads(qa_ref[0], qb_ref[0])

    sc_t = _nt(kc_ref[0, 0], qs)
    t0 = jnp.maximum(c - NSA_WINDOW // qc, 0)
    start = pl.multiple_of(t0 * qc, qc)
    sw_t = _nt(kw_ref[0, pl.ds(start, NSA_WIN_TILES * qc), :], qs)

    n_idx = lax.broadcasted_iota(jnp.int32, (N_CMP_PAD, qc), 0)
    q_idx = lax.broadcasted_iota(jnp.int32, (N_CMP_PAD, qc), 1)
    ok = (n_idx * CMP_STRIDE + (CMP_LEN - 1)) <= (q0 + q_idx)
    p_heads = []
    for i in range(NSA_G):
        s_i = jnp.where(ok, lanes_of(sc_t, i), NEG)
        e_i = jnp.where(ok, jnp.exp2(s_i - _reduce_keys(s_i, jnp.maximum, jnp.max)), 0.0)
        l_i = _reduce_keys(e_i, jnp.add, jnp.sum)
        p_heads.append((e_i / jnp.where(l_i > 0.0, l_i, 1.0)).astype(BF16))
    p_ct = jnp.concatenate(p_heads, axis=1)
    imp4 = _nn(ovt_ref[...], p_ct)
    ocmp_t = _nn(vct_ref[...], p_ct)

    band = _band_bias_t(NSA_WIN_TILES * qc, qc, q0 - start, NSA_WINDOW - 1)
    owin_t, _ = _softmax_block_t(sw_t + _tile_lanes(band, NSA_G), _pv_tiles(vwt_ref, t0, NSA_WIN_TILES, qc))
    gate_t = jnp.transpose(jax.nn.sigmoid(gl_ref[...]))
    gate = lambda i, r: gate_t[3 * i + r:3 * i + r + 1, :]
    partial_out = [gate(i, 0) * lanes_of(ocmp_t, i) + gate(i, 2) * lanes_of(owin_t, i) for i in range(NSA_G)]

    imp = lanes_of(imp4, 0) + lanes_of(imp4, 1) + lanes_of(imp4, 2) + lanes_of(imp4, 3)
    blk = lax.broadcasted_iota(jnp.int32, imp.shape, 0)
    cur = (q0 + lax.broadcasted_iota(jnp.int32, imp.shape, 1)) >> SLC_SHIFT
    valid = blk <= cur
    forced = valid & ((blk == 0) | (blk > cur - SLC_LOCAL))
    rank = _rank_rows(jnp.where(forced, FORCE, jnp.where(valid, imp, NEG)), N_SLC)
    bias_ref[...] = jnp.where(valid & (rank < SLC_TOPK), 0.0, NEG)

    key_row = lax.broadcasted_iota(jnp.int32, (NSA_KT, qc), 0)
    qpos = lax.broadcasted_iota(jnp.int32, (NSA_KT, qc), 1) + q0
    per_tile = NSA_KT // SLC_BLOCK

    def scores(t):
        ks0 = pl.multiple_of(t * NSA_KT, NSA_KT)
        blocks = [jnp.broadcast_to(bias_ref[pl.ds(t * per_tile + j, 1), :], (SLC_BLOCK, qc))
                  for j in range(per_tile)]
        bias = jnp.where(key_row + ks0 <= qpos, jnp.concatenate(blocks, axis=0), NEG)
        kt = ks_ref[0, pl.ds(ks0, NSA_KT), :]
        return [_nt(kt, qs[i * qc:(i + 1) * qc]) + bias for i in range(NSA_G)]

    slc = _flash_tiles(c // (NSA_KT // qc) + 1, [_online_init_t(qc)] * NSA_G, scores,
                       lambda t: _pv_tiles(vst_ref, t, 1, NSA_KT))
    outs = [partial_out[i] + gate(i, 1) * _normalise(slc[i][1])[0] for i in range(NSA_G)]
    o_ref[0, :, 0:LANES] = jnp.transpose(_merge_pair_t(outs[0], outs[1])).astype(o_ref.dtype)
    o_ref[0, :, LANES:2 * LANES] = jnp.transpose(_merge_pair_t(outs[2], outs[3])).astype(o_ref.dtype)


def _nsa(z_rot, z_pl, k_cmp, v_cmp, gate_logits, ovt):
    qc = NSA_QC
    seq_spec = lambda base: pl.BlockSpec((1, SEQ, LANES), lambda b, j, c: (b, 0, base + j))
    cmp_spec = pl.BlockSpec((1, 1, N_CMP_PAD, LANES), lambda b, j, c: (b, j, 0, 0))
    return pl.pallas_call(
        _nsa_kernel,
        out_shape=jax.ShapeDtypeStruct((BATCH, SEQ, NSA_HEADS * HEAD_DIM), BF16),
        grid=(BATCH, NSA_KV_HEADS, SEQ // qc),
        in_specs=[
            pl.BlockSpec((1, qc, LANES), lambda b, j, c: (b, c, ROT_NQ + 2 * j)),
            pl.BlockSpec((1, qc, LANES), lambda b, j, c: (b, c, ROT_NQ + 2 * j + 1)),
            cmp_spec, cmp_spec,
            seq_spec(ROT_NKS), seq_spec(PL_NVS), seq_spec(ROT_NKW), seq_spec(PL_NVW),
            pl.BlockSpec((qc, LANES), lambda b, j, c: (b * (SEQ // qc) + c, j)),
            pl.BlockSpec(ovt.shape, lambda b, j, c: (0, 0)),
        ],
        out_specs=pl.BlockSpec((1, qc, 2 * LANES), lambda b, j, c: (b, c, j)),
        scratch_shapes=[pltpu.VMEM((LANES, N_CMP_PAD), BF16),
                        pltpu.VMEM((SEQ // NSA_KT, VT_ROWS, NSA_KT), BF16),
                        pltpu.VMEM((SEQ // qc, VT_ROWS, qc), BF16),
                        pltpu.VMEM((N_SLC, qc), F32)],
        compiler_params=_params("parallel", "parallel", "arbitrary"),
        name="nsa",
    )(z_rot, z_rot, k_cmp, v_cmp, z_rot, z_pl, z_rot, z_pl, gate_logits, ovt)


DIL_QC = 128
DIL_STEPS = SEQ // DIL_QC
DIL_UNROLL = 8


def _dil_group(q_ref, k_ref, v_ref, og_ref, lg_ref, gi):
    window, dil = DIL_CONFIGS[gi]
    qc = DIL_QC
    m = SEQ // dil
    n_back = window // dil
    nk = min(m, qc + -(-n_back // qc) * qc)
    chunks = m // qc

    def rows(first, n):
        return pl.ds(first, n) if dil == 1 else pl.ds(first, n, stride=dil)

    def place(idx):
        r = idx // chunks
        q0 = (idx % chunks) * qc
        start = jnp.maximum(q0 - (nk - qc), 0)
        return rows(r + dil * q0, qc), rows(r + dil * start, nk), q0 - start

    def body(i, _):
        at = [place(i * DIL_UNROLL + u) for u in range(DIL_UNROLL)]
        s = [_nt(k_ref[0, k_rows, :].astype(BF16), _stack_heads(q_ref[0, q_rows, :].astype(BF16)))
             for q_rows, k_rows, _ in at]
        m, p = [], []
        for u, (_, _, off) in enumerate(at):
            s_u = s[u] + _tile_lanes(_band_bias_t(nk, qc, off, n_back), 2)
            m.append(_reduce_keys(s_u, jnp.maximum, jnp.max))
            p.append(_probs(s_u, m[u]))
        acc = [_nn(_transpose_aug(v_ref[0, k_rows, :]), p[u]) for u, (_, k_rows, _) in enumerate(at)]
        for u, (q_rows, _, _) in enumerate(at):
            o_t, l = _normalise(acc[u])
            lse_b = jnp.broadcast_to(m[u] + jnp.log(l) * LOG2_E, (LANES, 2 * qc))
            og_ref[gi, q_rows, :] = jnp.transpose(_merge_pair_t(o_t[:, :qc], o_t[:, qc:]))
            lg_ref[gi, q_rows, :] = jnp.transpose(_merge_pair_t(lse_b[:, :qc], lse_b[:, qc:]))
        return 0

    lax.fori_loop(0, DIL_STEPS // DIL_UNROLL, body, 0)


def _dil_kernel(q_ref, k_ref, v_ref, o_ref, og_ref, lg_ref):
    g = pl.program_id(2)
    n_groups = len(DIL_CONFIGS)
    for gi in range(n_groups):
        pl.when(g == gi)(functools.partial(_dil_group, q_ref, k_ref, v_ref, og_ref, lg_ref, gi))

    @pl.when(g == n_groups - 1)
    def _():
        rows = 512

        def body(i, _):
            sl = pl.ds(pl.multiple_of(i * rows, rows), rows)
            l0, l1, l2 = lg_ref[0, sl, :], lg_ref[1, sl, :], lg_ref[2, sl, :]
            mx = jnp.maximum(jnp.maximum(l0, l1), l2)
            e0, e1, e2 = jnp.exp2(l0 - mx), jnp.exp2(l1 - mx), jnp.exp2(l2 - mx)
            den = e0 + e1 + e2
            out = (e0 / den) * og_ref[0, sl, :] + (e1 / den) * og_ref[1, sl, :] + (e2 / den) * og_ref[2, sl, :]
            o_ref[0, sl, :] = out.astype(o_ref.dtype)
            return 0

        lax.fori_loop(0, SEQ // rows, body, 0)


def _dilated(zd_rot, zd_pl):
    n_groups = len(DIL_CONFIGS)
    width = DIL_HEADS_PER_GROUP * HEAD_DIM
    col = lambda base: (lambda b, p, g: (b, 0, base + 2 * g + p))
    blk = lambda base: pl.BlockSpec((1, SEQ, LANES), col(base))
    return pl.pallas_call(
        _dil_kernel,
        out_shape=jax.ShapeDtypeStruct((BATCH, SEQ, width), BF16),
        grid=(BATCH, 2, n_groups),
        in_specs=[blk(0), blk(DIL_BLOCKS), blk(0)],
        out_specs=pl.BlockSpec((1, SEQ, LANES), lambda b, p, g: (b, 0, p)),
        scratch_shapes=[pltpu.VMEM((n_groups, SEQ, LANES), F32), pltpu.VMEM((n_groups, SEQ, LANES), F32)],
        compiler_params=_params("parallel", "parallel", "arbitrary"),
        name="dilated",
    )(zd_rot, zd_rot, zd_pl)


def _out_proj_kernel(oa_ref, ob_ref, oc_ref, h_ref, wa_ref, wb_ref, wc_ref, g_ref, b_ref, rw_ref, rb_ref,
                     h1_ref, h1b_ref, comb_ref, sel_ref):
    y = _nn(oa_ref[...], wa_ref[...]) + _nn(ob_ref[...], wb_ref[...]) + _nn(oc_ref[...], wc_ref[...])
    h1 = _layer_norm(DEEPNORM_ALPHA * h_ref[...] + y, g_ref[...], b_ref[...])
    h1b = h1.astype(BF16)
    h1_ref[...] = h1
    h1b_ref[...] = h1b
    comb_ref[...], sel_ref[...] = _route(h1b, rw_ref[...], rb_ref[...])


def _out_proj(oa, ob, oc, h, wa, wb, wc, g, b, rw_t, rb):
    tm = 512
    rows = lambda w: pl.BlockSpec((tm, w), lambda i: (i, 0))
    full = lambda a: pl.BlockSpec(a.shape, lambda i: (0, 0))
    route_shape = jax.ShapeDtypeStruct((N_EXPERTS, TOKENS), F32)
    route_spec = pl.BlockSpec((N_EXPERTS, tm), lambda i: (0, i))
    return pl.pallas_call(
        _out_proj_kernel,
        out_shape=(jax.ShapeDtypeStruct((TOKENS, D_MODEL), F32),
                   jax.ShapeDtypeStruct((TOKENS, D_MODEL), BF16), route_shape, route_shape),
        grid=(TOKENS // tm,),
        in_specs=[rows(oa.shape[1]), rows(ob.shape[1]), rows(oc.shape[1]), rows(D_MODEL),
                  full(wa), full(wb), full(wc), full(g), full(b), full(rw_t), full(rb)],
        out_specs=(rows(D_MODEL), rows(D_MODEL), route_spec, route_spec),
        compiler_params=_params("parallel"),
        name="out_proj_ln",
    )(oa, ob, oc, h, wa, wb, wc, g, b, rw_t, rb)


def _route(hb, rw, rb):
    logits = _nt(rw, hb) + rb
    mx = jnp.max(logits, axis=0, keepdims=True)
    ex = jnp.exp(logits - mx)
    probs = ex / jnp.sum(ex, axis=0, keepdims=True)
    p = [probs[e:e + 1, :] for e in range(N_EXPERTS)]
    best, g_sel = None, None
    for g in range(N_GROUPS):
        a, b, c, d = p[4 * g:4 * g + 4]
        hi1, lo1, hi2, lo2 = jnp.maximum(a, b), jnp.minimum(a, b), jnp.maximum(c, d), jnp.minimum(c, d)
        top2 = jnp.maximum(hi1, hi2) + jnp.maximum(jnp.minimum(hi1, hi2), jnp.maximum(lo1, lo2))
        if g == 0:
            best, g_sel = top2, jnp.zeros_like(top2)
        else:
            better = top2 > best
            best = jnp.where(better, top2, best)
            g_sel = jnp.where(better, float(g), g_sel)
    chosen, picked = [], []
    for e in range(N_EXPERTS):
        g = e // EXPERTS_PER_GROUP
        rank = jnp.zeros_like(best)
        for o in range(4 * g, 4 * g + 4):
            if o < e:
                rank = rank + jnp.where(p[o] >= p[e], 1.0, 0.0)
            elif o > e:
                rank = rank + jnp.where(p[o] > p[e], 1.0, 0.0)
        chosen.append(jnp.where((g_sel == float(g)) & (rank < 2.0), 1.0, 0.0))
        picked.append(chosen[e] * p[e])
    total = picked[0]
    for e in range(1, N_EXPERTS):
        total = total + picked[e]
    return jnp.concatenate(picked, axis=0) / total, jnp.concatenate(chosen, axis=0)


def _routing_tables(comb_t, sel_t):
    sel = sel_t > 0.5
    cnt = jnp.sum(sel, axis=1, dtype=jnp.int32)
    cnt_pad = (cnt + (MOE_TILE - 1)) // MOE_TILE * MOE_TILE
    ends = jnp.cumsum(cnt_pad)
    rank = jnp.cumsum(sel.astype(jnp.int32), axis=1) - 1
    pos = (ends - cnt_pad)[:, None] + rank
    pos_lo = jnp.min(jnp.where(sel, pos, MOE_ROWS), axis=0)
    pos_hi = jnp.max(jnp.where(sel, pos, -1), axis=0)
    w_lo = jnp.sum(jnp.where(sel & (pos == pos_lo), comb_t, 0.0), axis=0)
    w_hi = jnp.sum(jnp.where(sel & (pos == pos_hi), comb_t, 0.0), axis=0)
    w = jnp.zeros((TOKENS, LANES), F32).at[:, 0].set(w_lo).at[:, 1].set(w_hi)
    n_tiles = ends[-1] // MOE_TILE
    tile_start = jnp.arange(MOE_TILES, dtype=jnp.int32) * MOE_TILE
    tile_start = jnp.minimum(tile_start, ends[-1] - MOE_TILE)
    tile_expert = jnp.sum((ends[None, :] <= tile_start[:, None]).astype(jnp.int32), axis=1)
    return jnp.stack([pos_lo, pos_hi]).astype(jnp.int32), w, tile_expert, n_tiles.reshape(1).astype(jnp.int32)


MOE_TILE = 256
MOE_TILES = 2 * TOKENS // MOE_TILE + N_EXPERTS
MOE_ROWS = MOE_TILES * MOE_TILE
SLAB = D_MODEL // LANES


def _to_slabs(ref, x, rows):
    for j in range(SLAB):
        ref[pl.ds(j, rows, stride=SLAB), :] = x[:, j * LANES:(j + 1) * LANES]


def _from_slabs(ref, rows, pitch=SLAB):
    return jnp.concatenate([ref[pl.ds(j, rows, stride=pitch), :] for j in range(SLAB)], axis=1)


GATHER_PITCH = SLAB + SUBLANES


def _slab_rows(row, n=SLAB):
    return pl.ds(pl.multiple_of(row * n, n), n)


XSLAB = SLAB // 2
U32 = jnp.uint32


def _to_packed_slabs(ref, x, rows):
    bits = lambda t: lax.bitcast_convert_type(t.astype(BF16).astype(F32), U32)
    for j in range(XSLAB):
        hi = bits(x[:, 2 * j * LANES:(2 * j + 1) * LANES])
        lo = bits(x[:, (2 * j + 1) * LANES:(2 * j + 2) * LANES])
        ref[pl.ds(j, rows, stride=XSLAB), :] = hi | (lo >> 16)


def _from_packed_slabs(ref, rows):
    parts = []
    for j in range(XSLAB):
        u = ref[pl.ds(j, rows, stride=XSLAB), :]
        parts.append(lax.bitcast_convert_type(u & jnp.uint32(0xFFFF0000), F32).astype(BF16))
        parts.append(lax.bitcast_convert_type(u << 16, F32).astype(BF16))
    return jnp.concatenate(parts, axis=1)


def _dispatch_kernel(pos_ref, h_ref, init_ref, xs_ref, slab_ref, sem):
    del init_ref
    tm = h_ref.shape[0]
    base = pl.program_id(0) * tm
    _to_packed_slabs(slab_ref, h_ref[...], tm)

    def copy(t, which):
        return pltpu.make_async_copy(slab_ref.at[_slab_rows(t, XSLAB), :],
                                     xs_ref.at[_slab_rows(pos_ref[which, base + t], XSLAB), :], sem)

    def start(t, _):
        copy(t, 0).start(priority=0)
        copy(t, 1).start(priority=1)
        return 0

    lax.fori_loop(0, tm, start, 0, unroll=8)
    whole = pltpu.make_async_copy(slab_ref, xs_ref.at[pl.ds(0, tm * XSLAB), :], sem)
    whole.wait()
    whole.wait()


def _dispatch(pos, h):
    tm = 256
    grid_spec = pltpu.PrefetchScalarGridSpec(
        num_scalar_prefetch=1,
        grid=(TOKENS // tm,),
        in_specs=[pl.BlockSpec((tm, D_MODEL), lambda i, pos: (i, 0)),
                  pl.BlockSpec(memory_space=pl.ANY)],
        out_specs=pl.BlockSpec(memory_space=pl.ANY),
        scratch_shapes=[pltpu.VMEM((tm * XSLAB, LANES), U32), pltpu.SemaphoreType.DMA],
    )
    return pl.pallas_call(
        _dispatch_kernel,
        out_shape=jax.ShapeDtypeStruct((MOE_ROWS * XSLAB, LANES), U32),
        grid_spec=grid_spec,
        input_output_aliases={2: 0},
        compiler_params=_params("arbitrary"),
        name="moe_dispatch",
    )(pos, h, jnp.zeros((MOE_ROWS * XSLAB, LANES), U32))


def _experts_kernel(te_ref, nt_ref, xs_ref, wg_ref, wu_ref, wd_ref, ys_ref, wgb_ref, wub_ref, wdb_ref):
    k = pl.program_id(0)
    e = te_ref[k]
    e_prev = te_ref[jnp.maximum(k - 1, 0)]

    @pl.when((k == 0) | (e != e_prev))
    def _():
        wgb_ref[...] = wg_ref[0, 0].astype(BF16)
        wub_ref[...] = wu_ref[0, 0].astype(BF16)
        wdb_ref[...] = wd_ref[0, 0].astype(BF16)

    @pl.when(k < nt_ref[0])
    def _():
        x = _from_packed_slabs(xs_ref, MOE_TILE)
        hid = jax.nn.silu(_nn(x, wgb_ref[...])) * _nn(x, wub_ref[...])
        _to_slabs(ys_ref, _nn(hid.astype(BF16), wdb_ref[...]), MOE_TILE)

    @pl.when(k >= nt_ref[0])
    def _():
        ys_ref[...] = jnp.zeros(ys_ref.shape, F32)


def _experts(tile_expert, n_tiles, xs, wg, wu, wd, layer):
    w_in_spec = pl.BlockSpec((1, 1, D_MODEL, EXPERT_HIDDEN), lambda k, te, nt: (layer, te[k], 0, 0))
    grid_spec = pltpu.PrefetchScalarGridSpec(
        num_scalar_prefetch=2,
        grid=(MOE_TILES,),
        in_specs=[pl.BlockSpec((MOE_TILE * XSLAB, LANES), lambda k, te, nt: (jnp.minimum(k, nt[0] - 1), 0)),
                  w_in_spec, w_in_spec,
                  pl.BlockSpec((1, 1, EXPERT_HIDDEN, D_MODEL), lambda k, te, nt: (layer, te[k], 0, 0))],
        out_specs=pl.BlockSpec((MOE_TILE * SLAB, LANES), lambda k, te, nt: (k, 0)),
        scratch_shapes=[pltpu.VMEM((D_MODEL, EXPERT_HIDDEN), BF16), pltpu.VMEM((D_MODEL, EXPERT_HIDDEN), BF16),
                        pltpu.VMEM((EXPERT_HIDDEN, D_MODEL), BF16)],
    )
    return pl.pallas_call(
        _experts_kernel,
        out_shape=jax.ShapeDtypeStruct((MOE_ROWS * SLAB, LANES), F32),
        grid_spec=grid_spec,
        compiler_params=_params("arbitrary"),
        name="moe_experts",
    )(tile_expert, n_tiles, xs, wg, wu, wd)


def _ple_ln_kernel(pos_ref, hb_ref, h_ref, ys_ref, w_ref, p_ref, gw_ref, gb_ref, pw_ref, g_ref, b_ref,
                   h2_ref, h2b_ref, lo_ref, hi_ref, sem):
    tm = h_ref.shape[0]
    i = pl.program_id(0)
    slot = i & 1
    bufs = (lo_ref, hi_ref)

    def fetch(tile, into):
        def start(t, _):
            for which in range(2):
                pltpu.make_async_copy(ys_ref.at[_slab_rows(pos_ref[which, tile * tm + t]), :],
                                      bufs[which].at[into, pl.ds(pl.multiple_of(t * GATHER_PITCH, SUBLANES), SLAB), :],
                                      sem.at[into]).start(priority=which)
            return 0
        lax.fori_loop(0, tm, start, 0, unroll=8)

    @pl.when(i == 0)
    def _():
        fetch(0, 0)

    @pl.when(i + 1 < pl.num_programs(0))
    def _():
        fetch(i + 1, 1 - slot)

    gate = jax.nn.sigmoid(_nn(hb_ref[...], gw_ref[...]) + gb_ref[...])
    ple = gate * _nn(p_ref[...].astype(BF16), pw_ref[...])
    for which in range(2):
        pltpu.make_async_copy(ys_ref.at[pl.ds(0, tm * SLAB), :], bufs[which].at[slot, pl.ds(0, tm * SLAB), :],
                              sem.at[slot]).wait()
    w = w_ref[...]
    ffn = (w[:, 0:1] * _from_slabs(lo_ref.at[slot], tm, GATHER_PITCH)
           + w[:, 1:2] * _from_slabs(hi_ref.at[slot], tm, GATHER_PITCH))
    h2 = _layer_norm(DEEPNORM_ALPHA * h_ref[...] + ffn + ple, g_ref[...], b_ref[...])
    h2_ref[...] = h2
    h2b_ref[...] = h2.astype(BF16)


def _ple_ln(pos, hb, h, ys, w, p, layer, gw, gb, pw, g, b):
    tm = 256
    p_spec = pl.BlockSpec((tm, PLE_DIM), lambda i, pos: (layer * (TOKENS // tm) + i, 0))
    rows = lambda width: pl.BlockSpec((tm, width), lambda i, pos: (i, 0))
    full = lambda a: pl.BlockSpec(a.shape, lambda i, pos: (0, 0))
    grid_spec = pltpu.PrefetchScalarGridSpec(
        num_scalar_prefetch=1,
        grid=(TOKENS // tm,),
        in_specs=[rows(D_MODEL), rows(D_MODEL), pl.BlockSpec(memory_space=pl.ANY), rows(LANES), p_spec,
                  full(gw), full(gb), full(pw), full(g), full(b)],
        out_specs=(rows(D_MODEL), rows(D_MODEL)),
        scratch_shapes=[pltpu.VMEM((2, tm * GATHER_PITCH, LANES), F32), pltpu.VMEM((2, tm * GATHER_PITCH, LANES), F32),
                        pltpu.SemaphoreType.DMA((2,))],
    )
    return pl.pallas_call(
        _ple_ln_kernel,
        out_shape=(jax.ShapeDtypeStruct((TOKENS, D_MODEL), F32),
                   jax.ShapeDtypeStruct((TOKENS, D_MODEL), BF16)),
        grid_spec=grid_spec,
        compiler_params=_params("arbitrary"),
        name="ple_ln",
    )(pos, hb, h, ys, w, p, gw, gb, pw, g, b)


def _rope_tables(positions):
    half = ROT_DIM // 2
    inv_freq = jnp.exp(jnp.arange(half, dtype=F32) * (-2.0 * math.log(ROPE_THETA) / ROT_DIM))
    ang = positions.astype(F32)[:, :, None] * inv_freq
    cos, sin = jnp.cos(ang), jnp.sin(ang)
    zeros = jnp.zeros_like(cos)
    rest = HEAD_DIM - ROT_DIM
    pad = lambda v: jnp.broadcast_to(jnp.asarray(v, F32), cos.shape[:2] + (rest,))
    c = jnp.concatenate([cos, cos, pad(1.0)], axis=-1)
    s1 = jnp.concatenate([-sin, zeros, pad(0.0)], axis=-1)
    s2 = jnp.concatenate([zeros, sin, pad(0.0)], axis=-1)
    tile = lambda t: jnp.concatenate([t, t], axis=-1).reshape(TOKENS, LANES)
    return tile(c), tile(s1), tile(s2)


def _split_w_in(w):
    mw, nq, nkv, dw = MOBA_HEADS * HEAD_DIM, NSA_HEADS * HEAD_DIM, NSA_KV_HEADS * HEAD_DIM, DIL_HEADS * HEAD_DIM
    widths = (mw, mw, mw, nq) + (nkv,) * 6 + (NSA_HEADS * 3, dw, dw, dw)
    offs = np.concatenate([[0], np.cumsum(widths)])
    qa, ka, va, qb, kbc, vbc, kbs, vbs, kbw, vbw, gb, qc, kc, vc = (
        w[:, int(offs[i]):int(offs[i + 1])] for i in range(len(widths)))

    def dup(t):
        t = t.reshape(D_MODEL, NSA_KV_HEADS, 1, HEAD_DIM)
        return jnp.broadcast_to(t, (D_MODEL, NSA_KV_HEADS, 2, HEAD_DIM)).reshape(D_MODEL, NSA_KV_HEADS * LANES)

    zpad = lambda n: jnp.zeros((D_MODEL, n * LANES), w.dtype)
    w_rot = jnp.concatenate([qa * Q_SCALE, ka, qb * Q_SCALE, dup(kbc), dup(kbs), dup(kbw), zpad(1)], axis=1)
    w_pl = jnp.concatenate([va, dup(vbc), dup(vbs), dup(vbw), zpad(3)], axis=1)
    n_gates = 3 * NSA_HEADS // NSA_KV_HEADS
    gpad = jnp.zeros((D_MODEL, NSA_KV_HEADS, LANES - n_gates), w.dtype)
    w_gl = jnp.concatenate([gb.reshape(D_MODEL, NSA_KV_HEADS, n_gates), gpad], axis=-1).reshape(D_MODEL, -1)
    w_dil_rot = jnp.concatenate([qc * Q_SCALE, kc], axis=1)
    return tuple(t.astype(BF16) for t in (w_rot, w_pl, w_gl, w_dil_rot, vc))


def _overlap_table():
    starts = np.arange(N_CMP) * CMP_STRIDE
    slc = np.arange(N_SLC) * SLC_BLOCK
    ov = ((starts[:, None] < slc[None, :] + SLC_BLOCK) & (starts[:, None] + CMP_LEN > slc[None, :]))
    ovt = np.zeros((N_SLC, N_CMP_PAD), np.float32)
    ovt[:, :N_CMP] = ov.T
    return jnp.asarray(ovt, BF16)


def _cmp_chunks(z, base):
    nblk = z.shape[-1] // LANES
    t = z.reshape(BATCH, SEQ // CMP_STRIDE, CMP_STRIDE, nblk, LANES)[:, :, :, base:base + NSA_KV_HEADS, :HEAD_DIM]
    return t.transpose(0, 3, 1, 2, 4).reshape(BATCH, NSA_KV_HEADS, SEQ // CMP_STRIDE, CMP_STRIDE * HEAD_DIM)


def kernel(x, p, positions, ln_in_g, ln_in_b, w_in, w_out, nsa_ck1, nsa_ck2, nsa_pe_k, nsa_cv1, nsa_cv2, nsa_pe_v, ln1_g, ln1_b, router_w, router_b, w_gate, w_up, w_down, ple_proj, ple_gate_w, ple_gate_b, ln2_g, ln2_b):
    rope = _rope_tables(positions)
    ovt = _overlap_table()
    rw_t = router_w.T.astype(BF16)
    rb = router_b.reshape(N_EXPERTS, 1).astype(F32)
    chunk_w = CMP_STRIDE * HEAD_DIM
    vec = lambda v: v.reshape(1, -1)
    seq3 = lambda t: t.reshape(BATCH, SEQ, t.shape[-1])
    flat = lambda t: t.reshape(TOKENS, t.shape[-1])

    h, hb = _ln_in(x.reshape(TOKENS, D_MODEL), ln_in_g, ln_in_b)
    for i in range(DEPTH):
        w_rot, w_pl, w_gl, w_dil_rot, w_dil_pl = _split_w_in(w_in[i])
        z_rot = seq3(_project(hb, w_rot, BF16, 768, rope=rope))
        z_pl = seq3(_project(hb, w_pl, BF16, 1024))
        gate_logits = _project(hb, w_gl, F32, NSA_KV_HEADS * LANES)
        zd_rot = seq3(_project(hb, w_dil_rot, F32, 768, rope=rope))
        zd_pl = seq3(_project(hb, w_dil_pl, F32, 768))

        o_a = _moba(z_rot, z_pl)

        dup2 = lambda w2: jnp.concatenate([w2, w2], axis=1).astype(BF16)
        k_cmp, v_cmp = _compress(
            _cmp_chunks(z_rot, ROT_NKC), _cmp_chunks(z_pl, PL_NVC),
            nsa_pe_k[i].reshape(2, chunk_w), nsa_pe_v[i].reshape(2, chunk_w),
            nsa_ck1[i].reshape(2, chunk_w, CMP_HIDDEN).astype(BF16), dup2(nsa_ck2[i]),
            nsa_cv1[i].reshape(2, chunk_w, CMP_HIDDEN).astype(BF16), dup2(nsa_cv2[i]))
        o_b = _nsa(z_rot, z_pl, k_cmp, v_cmp, gate_logits, ovt)

        o_c = _dilated(zd_rot, zd_pl)

        wo = w_out[i].astype(BF16)
        a_w, b_w = MOBA_HEADS * HEAD_DIM, NSA_HEADS * HEAD_DIM
        h, hb, comb_t, sel_t = _out_proj(flat(o_a), flat(o_b), flat(o_c), h, wo[:a_w], wo[a_w:a_w + b_w],
                                         wo[a_w + b_w:], vec(ln1_g[i]), vec(ln1_b[i]), rw_t, rb)

        pos, w_tok, tile_expert, n_tiles = _routing_tables(comb_t, sel_t)
        xs = _dispatch(pos, h)
        ys = _experts(tile_expert, n_tiles, xs, w_gate, w_up, w_down, i)
        h, hb = _ple_ln(pos, hb, h, ys, w_tok, p.reshape(DEPTH * TOKENS, PLE_DIM), i, ple_gate_w[i].astype(BF16),
                        vec(ple_gate_b[i]), ple_proj[i].astype(BF16), vec(ln2_g[i]), vec(ln2_b[i]))
    return h.reshape(BATCH, SEQ, D_MODEL)
```

```python
import functools
import math

import numpy as np
import jax
import jax.numpy as jnp
from jax import lax
from jax.experimental import pallas as pl
from jax.experimental.pallas import tpu as pltpu

F32 = jnp.float32
BF16 = jnp.bfloat16

D_MODEL = 2048
BATCH = 2
SEQ = 4096
DEPTH = 4
TOKENS = BATCH * SEQ
HEAD_DIM = 64
ROT_DIM = HEAD_DIM // 4
ROPE_THETA = 500000.0
NEG = -1e30
FORCE = 1e30
LN_EPS = 1e-5
SCALE = HEAD_DIM ** -0.5
LOG2_E = math.log2(math.e)
Q_SCALE = SCALE * LOG2_E

MOBA_HEADS = 8
MOBA_BLOCK = 256
MOBA_TOPK = 3
MOBA_NB = SEQ // MOBA_BLOCK

NSA_HEADS = 12
NSA_KV_HEADS = 3
CMP_LEN = 32
CMP_STRIDE = 16
CMP_HIDDEN = 128
N_CMP = (SEQ - CMP_LEN) // CMP_STRIDE + 1
N_CMP_PAD = 256
SLC_BLOCK = 64
SLC_TOPK = 16
SLC_LOCAL = 2
N_SLC = SEQ // SLC_BLOCK
NSA_WINDOW = 512

DIL_CONFIGS = ((128, 1), (512, 4), (2048, 16))
DIL_HEADS_PER_GROUP = 4
DIL_HEADS = DIL_HEADS_PER_GROUP * len(DIL_CONFIGS)

N_EXPERTS = 16
N_GROUPS = 4
EXPERTS_PER_GROUP = 4
EXPERT_HIDDEN = D_MODEL // 4
PLE_DIM = 256

DEEPNORM_ALPHA = (2 * DEPTH) ** 0.25

LANES = 128
SUBLANES = 8
BF16_SUBLANES = 16
VMEM_LIMIT = 56 * 1024 * 1024
MOBA_SHIFT = MOBA_BLOCK.bit_length() - 1
SLC_SHIFT = SLC_BLOCK.bit_length() - 1

ROT_MQ, ROT_MK, ROT_NQ, ROT_NKC, ROT_NKS, ROT_NKW = 0, 4, 8, 14, 17, 20
ROT_BLOCKS = 24
PL_MV, PL_NVC, PL_NVS, PL_NVW = 0, 4, 7, 10
PL_BLOCKS = 16
DIL_BLOCKS = DIL_HEADS // 2

NT_DIMS = (((1,), (1,)), ((), ()))


def _nt(a, b):
    return lax.dot_general(a, b, NT_DIMS, preferred_element_type=F32)


def _nn(a, b):
    return jnp.dot(a, b, preferred_element_type=F32)


def _params(*sem):
    return pltpu.CompilerParams(dimension_semantics=sem, vmem_limit_bytes=VMEM_LIMIT)


def _layer_norm(y, g, b):
    mu = jnp.mean(y, axis=-1, keepdims=True)
    yc = y - mu
    var = jnp.mean(yc * yc, axis=-1, keepdims=True)
    return yc * lax.rsqrt(var + LN_EPS) * g + b


def _ln_kernel(x_ref, g_ref, b_ref, h_ref, hb_ref):
    h = _layer_norm(x_ref[...], g_ref[...], b_ref[...])
    h_ref[...] = h
    hb_ref[...] = h.astype(BF16)


def _ln_in(x, g, b):
    tm = 512
    row = pl.BlockSpec((tm, D_MODEL), lambda i: (i, 0))
    vec = pl.BlockSpec((1, D_MODEL), lambda i: (0, 0))
    return pl.pallas_call(
        _ln_kernel,
        out_shape=(jax.ShapeDtypeStruct((TOKENS, D_MODEL), F32),
                   jax.ShapeDtypeStruct((TOKENS, D_MODEL), BF16)),
        grid=(TOKENS // tm,),
        in_specs=[row, vec, vec],
        out_specs=(row, row),
        compiler_params=_params("parallel"),
        name="ln_in",
    )(x, g.reshape(1, -1), b.reshape(1, -1))


def _proj_kernel(x_ref, w_ref, o_ref):
    o_ref[...] = _nn(x_ref[...], w_ref[...]).astype(o_ref.dtype)


def _proj_rot_kernel(x_ref, w_ref, c_ref, s1_ref, s2_ref, o_ref):
    x = x_ref[...]
    c, s1, s2 = c_ref[...], s1_ref[...], s2_ref[...]
    half = ROT_DIM // 2
    for j0 in range(0, o_ref.shape[1], 2 * LANES):
        z = _nn(x, w_ref[:, j0:j0 + 2 * LANES])
        for j in range(j0, j0 + 2 * LANES, LANES):
            zc = z[:, j - j0:j - j0 + LANES]
            r = zc * c + pltpu.roll(zc, LANES - half, 1) * s1 + pltpu.roll(zc, half, 1) * s2
            o_ref[:, j:j + LANES] = r.astype(o_ref.dtype)


def _project(hb, w, out_dtype, tn, rope=None):
    tm = 1024
    n = w.shape[1]
    x_spec = pl.BlockSpec((tm, D_MODEL), lambda i, j: (i, 0))
    w_spec = pl.BlockSpec((D_MODEL, tn), lambda i, j: (0, j))
    o_spec = pl.BlockSpec((tm, tn), lambda i, j: (i, j))
    if rope is None:
        kern, extra, extra_specs = _proj_kernel, (), []
    else:
        t_spec = pl.BlockSpec((tm, LANES), lambda i, j: (i, 0))
        kern, extra, extra_specs = _proj_rot_kernel, rope, [t_spec] * 3
    return pl.pallas_call(
        kern,
        out_shape=jax.ShapeDtypeStruct((TOKENS, n), out_dtype),
        grid=(TOKENS // tm, n // tn),
        in_specs=[x_spec, w_spec] + extra_specs,
        out_specs=o_spec,
        compiler_params=_params("parallel", "arbitrary"),
        name="in_proj_rot" if rope is not None else "in_proj",
    )(hb, w, *extra)


def _stack_heads(*q_blocks):
    parts = []
    for q in q_blocks:
        lane = lax.broadcasted_iota(jnp.int32, q.shape, 1)
        zero = jnp.zeros_like(q)
        parts += [jnp.where(lane < HEAD_DIM, q, zero), jnp.where(lane >= HEAD_DIM, q, zero)]
    return jnp.concatenate(parts, axis=0)


def _merge_pair_t(lo, hi):
    sub = lax.broadcasted_iota(jnp.int32, lo.shape, 0)
    return jnp.where(sub < HEAD_DIM, lo, hi)


def _band_bias_t(nk, qc, offset, n_back):
    key = lax.broadcasted_iota(jnp.int32, (nk, qc), 0)
    qry = lax.broadcasted_iota(jnp.int32, (nk, qc), 1)
    diff = offset + qry - key
    return jnp.where((diff >= 0) & (diff <= n_back), 0.0, NEG)


def _tile_lanes(x, n):
    return jnp.concatenate([x] * n, axis=1)


def _transpose_bf16(v):
    return jnp.transpose(v.astype(F32)).astype(BF16)


def _tree(x, op):
    n = x.shape[0]
    if n == SUBLANES:
        return x
    if n % (2 * SUBLANES) == 0:
        return op(_tree(x[:n // 2], op), _tree(x[n // 2:], op))
    acc = x[:SUBLANES]
    for i in range(1, n // SUBLANES):
        acc = op(acc, x[SUBLANES * i:SUBLANES * (i + 1)])
    return acc


def _reduce_keys(x, op, final):
    return final(_tree(x, op), axis=0, keepdims=True)


VT_ROWS = LANES + BF16_SUBLANES


def _transpose_aug(v):
    vt = jnp.transpose(v.astype(F32))
    sub = lax.broadcasted_iota(jnp.int32, (VT_ROWS - LANES, v.shape[0]), 0)
    return jnp.concatenate([vt, jnp.where(sub == 0, 1.0, 0.0)], axis=0).astype(BF16)


def _probs(s_t, m):
    return jnp.exp2((s_t - m).astype(BF16))


def _normalise(acc):
    l = acc[LANES:LANES + 1]
    return acc[:LANES] / l, l


def _softmax_block_t(s_t, pv):
    m = _reduce_keys(s_t, jnp.maximum, jnp.max)
    out, l = _normalise(pv(_probs(s_t, m)))
    return out, m + jnp.log(l) * LOG2_E


def _online_step_t(carry, s_t, m_t, pv):
    m, acc = carry
    m_new = jnp.maximum(m, m_t)
    acc = jnp.exp2(m - m_new) * acc + pv(_probs(s_t, m_new))
    return m_new, acc


def _flash_tiles(n_tiles, init, scores, pv_of, may_be_empty=False):
    strips = range(len(init))

    def produce(t):
        s = tuple(scores(t))
        return s, tuple(_reduce_keys(s_i, jnp.maximum, jnp.max) for s_i in s)

    def consume(state, s_t, m_t, t):
        pv = pv_of(t)
        return tuple(_online_step_t(state[i], s_t[i], m_t[i], pv) for i in strips)

    def body(t, carry):
        state, s_t, m_t = carry
        s_next, m_next = produce(t + 1)
        return consume(state, s_t, m_t, t), s_next, m_next

    state, s_last, m_last = lax.fori_loop(0, n_tiles - 1, body, (tuple(init),) + produce(0))
    done = consume(state, s_last, m_last, jnp.maximum(n_tiles - 1, 0))
    if may_be_empty:
        done = jax.tree_util.tree_map(lambda new, old: jnp.where(n_tiles > 0, new, old), done, state)
    return done


def _pv_tiles(vt_ref, first, n, rows):
    def pv(p):
        acc = _nn(vt_ref[first], p[:rows])
        for j in range(1, n):
            acc = acc + _nn(vt_ref[first + j], p[j * rows:(j + 1) * rows])
        return acc
    return pv


def _online_init_t(r):
    return (jnp.full((1, r), NEG, F32), jnp.zeros((VT_ROWS, r), F32))


def _rank_rows(g, n_rows):
    sub = lax.broadcasted_iota(jnp.int32, (SUBLANES, g.shape[1]), 0)
    rank = jnp.zeros(g.shape, F32)
    for m in range(n_rows):
        gm = g[m:m + 1, :]
        b = m // SUBLANES * SUBLANES
        mid = g[b:b + SUBLANES]
        parts = [jnp.where(gm > mid, 1.0, jnp.where((gm == mid) & (sub > m - b), 1.0, 0.0))]
        if b > 0:
            parts.insert(0, jnp.where(gm > g[:b], 1.0, 0.0))
        if b + SUBLANES < n_rows:
            parts.append(jnp.where(gm >= g[b + SUBLANES:], 1.0, 0.0))
        rank = rank + jnp.concatenate(parts, axis=0)
    return rank


MOBA_KT = 2 * MOBA_BLOCK
MOBA_QC = MOBA_KT


def _moba_kernel(q_ref, k_ref, v_ref, o_ref, kmean_ref, vt_ref, bias_ref):
    c = pl.program_id(2)
    qc = MOBA_QC

    @pl.when(c == 0)
    def _():
        row = lax.broadcasted_iota(jnp.int32, (MOBA_NB, SEQ), 0)
        col = lax.broadcasted_iota(jnp.int32, (MOBA_NB, SEQ), 1)
        avg = jnp.where((col >> MOBA_SHIFT) == row, 1.0 / MOBA_BLOCK, 0.0).astype(BF16)
        kmean_ref[...] = _nn(avg, k_ref[0])
        for t in range(MOBA_NB):
            vt_ref[t] = _transpose_aug(v_ref[0, t * MOBA_BLOCK:(t + 1) * MOBA_BLOCK, :])

    qs = _stack_heads(q_ref[0])
    per_tile = MOBA_KT // MOBA_BLOCK

    ks = pl.multiple_of(c * MOBA_KT, MOBA_KT)
    heads = [qs[:qc], qs[qc:]]
    raw_own = [_nt(k_ref[0, pl.ds(ks, MOBA_KT), :], q_h) for q_h in heads]

    gate = _nt(kmean_ref[...].astype(BF16), qs)
    blk = lax.broadcasted_iota(jnp.int32, gate.shape, 0)
    q_idx = lax.broadcasted_iota(jnp.int32, gate.shape, 1) & (qc - 1)
    past = blk < c * per_tile + (q_idx >> MOBA_SHIFT)
    rank = _rank_rows(jnp.where(past, gate, NEG), MOBA_NB)
    bias_ref[...] = jnp.where(past & (rank < MOBA_TOPK), 0.0, NEG)

    key = lax.broadcasted_iota(jnp.int32, (MOBA_KT, qc), 0)
    qry = lax.broadcasted_iota(jnp.int32, (MOBA_KT, qc), 1)
    first_block = bias_ref[pl.ds(c * per_tile, 1), :]
    state = []
    for h in range(2):
        other = jnp.broadcast_to(first_block[:, h * qc:(h + 1) * qc], (MOBA_KT, qc))
        same_block = (key >> MOBA_SHIFT) == (qry >> MOBA_SHIFT)
        own_bias = jnp.where(key <= qry, jnp.where(same_block, 0.0, other), NEG)
        s_h = raw_own[h] + own_bias
        state.append(_online_step_t(_online_init_t(qc), s_h, _reduce_keys(s_h, jnp.maximum, jnp.max),
                                    _pv_tiles(vt_ref, c * per_tile, per_tile, MOBA_BLOCK)))

    def scores(t):
        ks = pl.multiple_of(t * MOBA_KT, MOBA_KT)
        kt = k_ref[0, pl.ds(ks, MOBA_KT), :]
        rows = [bias_ref[pl.ds(t * per_tile + j, 1), :] for j in range(per_tile)]
        out = []
        for h in range(2):
            blocks = [jnp.broadcast_to(row[:, h * qc:(h + 1) * qc], (MOBA_BLOCK, qc)) for row in rows]
            out.append(_nt(kt, heads[h]) + jnp.concatenate(blocks, axis=0))
        return out

    state = _flash_tiles(c, state, scores,
                         lambda t: _pv_tiles(vt_ref, t * per_tile, per_tile, MOBA_BLOCK), may_be_empty=True)
    o_lo, o_hi = (_normalise(acc)[0] for _, acc in state)
    o_ref[0] = jnp.transpose(_merge_pair_t(o_lo, o_hi)).astype(o_ref.dtype)


def _moba(z_rot, z_pl):
    qc = MOBA_QC
    grid = (BATCH, MOBA_HEADS // 2, SEQ // qc)
    return pl.pallas_call(
        _moba_kernel,
        out_shape=jax.ShapeDtypeStruct((BATCH, SEQ, MOBA_HEADS * HEAD_DIM), BF16),
        grid=grid,
        in_specs=[
            pl.BlockSpec((1, qc, LANES), lambda b, p, c: (b, c, ROT_MQ + p)),
            pl.BlockSpec((1, SEQ, LANES), lambda b, p, c: (b, 0, ROT_MK + p)),
            pl.BlockSpec((1, SEQ, LANES), lambda b, p, c: (b, 0, PL_MV + p)),
        ],
        out_specs=pl.BlockSpec((1, qc, LANES), lambda b, p, c: (b, c, p)),
        scratch_shapes=[pltpu.VMEM((MOBA_NB, LANES), F32),
                        pltpu.VMEM((MOBA_NB, VT_ROWS, MOBA_BLOCK), BF16),
                        pltpu.VMEM((MOBA_NB, 2 * qc), F32)],
        compiler_params=_params("parallel", "parallel", "arbitrary"),
        name="moba",
    )(z_rot, z_rot, z_pl)


def _compress_one(x_ref, pe_ref, w1_ref, w2_ref, o_ref):
    x = x_ref[0, 0].astype(F32)
    top = (x + pe_ref[0:1, :]).astype(BF16)
    bot = (x + pe_ref[1:2, :]).astype(BF16)
    a = _nn(top, w1_ref[0])
    bm = _nn(bot, w1_ref[1])
    pre = a + pltpu.roll(bm, N_CMP_PAD - 1, 0)
    hid = jax.nn.gelu(pre)
    out = _nn(hid.astype(BF16), w2_ref[...])
    row = lax.broadcasted_iota(jnp.int32, out.shape, 0)
    o_ref[0, 0] = jnp.where(row < N_CMP, out, 0.0).astype(o_ref.dtype)


def _compress_kernel(xk_ref, xv_ref, pk_ref, pv_ref, k1_ref, k2_ref, v1_ref, v2_ref, ok_ref, ov_ref):
    _compress_one(xk_ref, pk_ref, k1_ref, k2_ref, ok_ref)
    _compress_one(xv_ref, pv_ref, v1_ref, v2_ref, ov_ref)


def _compress(xk, xv, pk, pv, k1, k2, v1, v2):
    chunk_w = CMP_STRIDE * HEAD_DIM
    x_spec = pl.BlockSpec((1, 1, N_CMP_PAD, chunk_w), lambda b, j: (b, j, 0, 0))
    pe_spec = pl.BlockSpec((2, chunk_w), lambda b, j: (0, 0))
    w1_spec = pl.BlockSpec((2, chunk_w, CMP_HIDDEN), lambda b, j: (0, 0, 0))
    w2_spec = pl.BlockSpec((CMP_HIDDEN, LANES), lambda b, j: (0, 0))
    o_spec = pl.BlockSpec((1, 1, N_CMP_PAD, LANES), lambda b, j: (b, j, 0, 0))
    o_shape = jax.ShapeDtypeStruct((BATCH, NSA_KV_HEADS, N_CMP_PAD, LANES), BF16)
    return pl.pallas_call(
        _compress_kernel,
        out_shape=(o_shape, o_shape),
        grid=(BATCH, NSA_KV_HEADS),
        in_specs=[x_spec, x_spec, pe_spec, pe_spec, w1_spec, w2_spec, w1_spec, w2_spec],
        out_specs=(o_spec, o_spec),
        compiler_params=_params("parallel", "parallel"),
        name="nsa_compress",
    )(xk, xv, pk, pv, k1, k2, v1, v2)


NSA_QC = 512
NSA_KT = 512
NSA_G = NSA_HEADS // NSA_KV_HEADS
NSA_WIN_TILES = NSA_WINDOW // NSA_QC + 1


def _nsa_kernel(qa_ref, qb_ref, kc_ref, vc_ref, ks_ref, vs_ref, kw_ref, vw_ref, gl_ref, ovt_ref,
                o_ref, vct_ref, vst_ref, vwt_ref, bias_ref):
    c = pl.program_id(2)
    qc = NSA_QC
    q0 = c * qc
    lanes_of = lambda t, i: t[:, i * qc:(i + 1) * qc]

    @pl.when(c == 0)
    def _():
        vct_ref[...] = _transpose_bf16(vc_ref[0, 0])
        for t in range(SEQ // NSA_KT):
            vst_ref[t] = _transpose_aug(vs_ref[0, t * NSA_KT:(t + 1) * NSA_KT, :])
        for t in range(SEQ // qc):
            vwt_ref[t] = _transpose_aug(vw_ref[0, t * qc:(t + 1) * qc, :])

    qs = _stack_heads(qa_ref[0], qb_ref[0])

    sc_t = _nt(kc_ref[0, 0], qs)
    t0 = jnp.maximum(c - NSA_WINDOW // qc, 0)
    start = pl.multiple_of(t0 * qc, qc)
    sw_t = _nt(kw_ref[0, pl.ds(start, NSA_WIN_TILES * qc), :], qs)

    n_idx = lax.broadcasted_iota(jnp.int32, (N_CMP_PAD, qc), 0)
    q_idx = lax.broadcasted_iota(jnp.int32, (N_CMP_PAD, qc), 1)
    ok = (n_idx * CMP_STRIDE + (CMP_LEN - 1)) <= (q0 + q_idx)
    p_heads = []
    for i in range(NSA_G):
        s_i = jnp.where(ok, lanes_of(sc_t, i), NEG)
        e_i = jnp.where(ok, jnp.exp2(s_i - _reduce_keys(s_i, jnp.maximum, jnp.max)), 0.0)
        l_i = _reduce_keys(e_i, jnp.add, jnp.sum)
        p_heads.append((e_i / jnp.where(l_i > 0.0, l_i, 1.0)).astype(BF16))
    p_ct = jnp.concatenate(p_heads, axis=1)
    imp4 = _nn(ovt_ref[...], p_ct)
    ocmp_t = _nn(vct_ref[...], p_ct)

    band = _band_bias_t(NSA_WIN_TILES * qc, qc, q0 - start, NSA_WINDOW - 1)
    owin_t, _ = _softmax_block_t(sw_t + _tile_lanes(band, NSA_G), _pv_tiles(vwt_ref, t0, NSA_WIN_TILES, qc))
    gate_t = jnp.transpose(jax.nn.sigmoid(gl_ref[...]))
    gate = lambda i, r: gate_t[3 * i + r:3 * i + r + 1, :]
    partial_out = [gate(i, 0) * lanes_of(ocmp_t, i) + gate(i, 2) * lanes_of(owin_t, i) for i in range(NSA_G)]

    imp = lanes_of(imp4, 0) + lanes_of(imp4, 1) + lanes_of(imp4, 2) + lanes_of(imp4, 3)
    blk = lax.broadcasted_iota(jnp.int32, imp.shape, 0)
    cur = (q0 + lax.broadcasted_iota(jnp.int32, imp.shape, 1)) >> SLC_SHIFT
    valid = blk <= cur
    forced = valid & ((blk == 0) | (blk > cur - SLC_LOCAL))
    rank = _rank_rows(jnp.where(forced, FORCE, jnp.where(valid, imp, NEG)), N_SLC)
    bias_ref[...] = jnp.where(valid & (rank < SLC_TOPK), 0.0, NEG)

    key_row = lax.broadcasted_iota(jnp.int32, (NSA_KT, qc), 0)
    qpos = lax.broadcasted_iota(jnp.int32, (NSA_KT, qc), 1) + q0
    per_tile = NSA_KT // SLC_BLOCK

    def scores(t):
        ks0 = pl.multiple_of(t * NSA_KT, NSA_KT)
        blocks = [jnp.broadcast_to(bias_ref[pl.ds(t * per_tile + j, 1), :], (SLC_BLOCK, qc))
                  for j in range(per_tile)]
        bias = jnp.where(key_row + ks0 <= qpos, jnp.concatenate(blocks, axis=0), NEG)
        kt = ks_ref[0, pl.ds(ks0, NSA_KT), :]
        return [_nt(kt, qs[i * qc:(i + 1) * qc]) + bias for i in range(NSA_G)]

    slc = _flash_tiles(c // (NSA_KT // qc) + 1, [_online_init_t(qc)] * NSA_G, scores,
                       lambda t: _pv_tiles(vst_ref, t, 1, NSA_KT))
    outs = [partial_out[i] + gate(i, 1) * _normalise(slc[i][1])[0] for i in range(NSA_G)]
    o_ref[0, :, 0:LANES] = jnp.transpose(_merge_pair_t(outs[0], outs[1])).astype(o_ref.dtype)
    o_ref[0, :, LANES:2 * LANES] = jnp.transpose(_merge_pair_t(outs[2], outs[3])).astype(o_ref.dtype)


def _nsa(z_rot, z_pl, k_cmp, v_cmp, gate_logits, ovt):
    qc = NSA_QC
    seq_spec = lambda base: pl.BlockSpec((1, SEQ, LANES), lambda b, j, c: (b, 0, base + j))
    cmp_spec = pl.BlockSpec((1, 1, N_CMP_PAD, LANES), lambda b, j, c: (b, j, 0, 0))
    return pl.pallas_call(
        _nsa_kernel,
        out_shape=jax.ShapeDtypeStruct((BATCH, SEQ, NSA_HEADS * HEAD_DIM), BF16),
        grid=(BATCH, NSA_KV_HEADS, SEQ // qc),
        in_specs=[
            pl.BlockSpec((1, qc, LANES), lambda b, j, c: (b, c, ROT_NQ + 2 * j)),
            pl.BlockSpec((1, qc, LANES), lambda b, j, c: (b, c, ROT_NQ + 2 * j + 1)),
            cmp_spec, cmp_spec,
            seq_spec(ROT_NKS), seq_spec(PL_NVS), seq_spec(ROT_NKW), seq_spec(PL_NVW),
            pl.BlockSpec((qc, LANES), lambda b, j, c: (b * (SEQ // qc) + c, DIL_BLOCKS + j)),
            pl.BlockSpec(ovt.shape, lambda b, j, c: (0, 0)),
        ],
        out_specs=pl.BlockSpec((1, qc, 2 * LANES), lambda b, j, c: (b, c, j)),
        scratch_shapes=[pltpu.VMEM((LANES, N_CMP_PAD), BF16),
                        pltpu.VMEM((SEQ // NSA_KT, VT_ROWS, NSA_KT), BF16),
                        pltpu.VMEM((SEQ // qc, VT_ROWS, qc), BF16),
                        pltpu.VMEM((N_SLC, qc), F32)],
        compiler_params=_params("parallel", "parallel", "arbitrary"),
        name="nsa",
    )(z_rot, z_rot, k_cmp, v_cmp, z_rot, z_pl, z_rot, z_pl, gate_logits, ovt)


DIL_QC = 128
DIL_STEPS = SEQ // DIL_QC
DIL_UNROLL = 8


def _dil_group(q_ref, k_ref, v_ref, og_ref, lg_ref, gi):
    window, dil = DIL_CONFIGS[gi]
    qc = DIL_QC
    m = SEQ // dil
    n_back = window // dil
    nk = min(m, qc + -(-n_back // qc) * qc)
    chunks = m // qc

    def rows(first, n):
        return pl.ds(first, n) if dil == 1 else pl.ds(first, n, stride=dil)

    def place(idx):
        r = idx // chunks
        q0 = (idx % chunks) * qc
        start = jnp.maximum(q0 - (nk - qc), 0)
        return rows(r + dil * q0, qc), rows(r + dil * start, nk), q0 - start

    def body(i, _):
        at = [place(i * DIL_UNROLL + u) for u in range(DIL_UNROLL)]
        s = [_nt(k_ref[0, k_rows, :].astype(BF16), _stack_heads(q_ref[0, q_rows, :].astype(BF16)))
             for q_rows, k_rows, _ in at]
        m, p = [], []
        for u, (_, _, off) in enumerate(at):
            s_u = s[u] + _tile_lanes(_band_bias_t(nk, qc, off, n_back), 2)
            m.append(_reduce_keys(s_u, jnp.maximum, jnp.max))
            p.append(_probs(s_u, m[u]))
        acc = [_nn(_transpose_aug(v_ref[0, k_rows, :]), p[u]) for u, (_, k_rows, _) in enumerate(at)]
        for u, (q_rows, _, _) in enumerate(at):
            o_t, l = _normalise(acc[u])
            lse_b = jnp.broadcast_to(m[u] + jnp.log(l) * LOG2_E, (LANES, 2 * qc))
            og_ref[gi, q_rows, :] = jnp.transpose(_merge_pair_t(o_t[:, :qc], o_t[:, qc:]))
            lg_ref[gi, q_rows, :] = jnp.transpose(_merge_pair_t(lse_b[:, :qc], lse_b[:, qc:]))
        return 0

    lax.fori_loop(0, DIL_STEPS // DIL_UNROLL, body, 0)


def _dil_kernel(q_ref, k_ref, v_ref, o_ref, og_ref, lg_ref):
    g = pl.program_id(2)
    n_groups = len(DIL_CONFIGS)
    for gi in range(n_groups):
        pl.when(g == gi)(functools.partial(_dil_group, q_ref, k_ref, v_ref, og_ref, lg_ref, gi))

    @pl.when(g == n_groups - 1)
    def _():
        rows = 512

        def body(i, _):
            sl = pl.ds(pl.multiple_of(i * rows, rows), rows)
            l0, l1, l2 = lg_ref[0, sl, :], lg_ref[1, sl, :], lg_ref[2, sl, :]
            mx = jnp.maximum(jnp.maximum(l0, l1), l2)
            e0, e1, e2 = jnp.exp2(l0 - mx), jnp.exp2(l1 - mx), jnp.exp2(l2 - mx)
            den = e0 + e1 + e2
            out = (e0 / den) * og_ref[0, sl, :] + (e1 / den) * og_ref[1, sl, :] + (e2 / den) * og_ref[2, sl, :]
            o_ref[0, sl, :] = out.astype(o_ref.dtype)
            return 0

        lax.fori_loop(0, SEQ // rows, body, 0)


def _dilated(zd_rot, zd_pl):
    n_groups = len(DIL_CONFIGS)
    width = DIL_HEADS_PER_GROUP * HEAD_DIM
    col = lambda base: (lambda b, p, g: (b, 0, base + 2 * g + p))
    blk = lambda base: pl.BlockSpec((1, SEQ, LANES), col(base))
    return pl.pallas_call(
        _dil_kernel,
        out_shape=jax.ShapeDtypeStruct((BATCH, SEQ, width), BF16),
        grid=(BATCH, 2, n_groups),
        in_specs=[blk(0), blk(DIL_BLOCKS), blk(0)],
        out_specs=pl.BlockSpec((1, SEQ, LANES), lambda b, p, g: (b, 0, p)),
        scratch_shapes=[pltpu.VMEM((n_groups, SEQ, LANES), F32), pltpu.VMEM((n_groups, SEQ, LANES), F32)],
        compiler_params=_params("parallel", "parallel", "arbitrary"),
        name="dilated",
    )(zd_rot, zd_rot, zd_pl)


def _out_proj_kernel(oa_ref, ob_ref, oc_ref, h_ref, wa_ref, wb_ref, wc_ref, g_ref, b_ref, rw_ref, rb_ref,
                     h1_ref, h1b_ref, comb_ref, sel_ref):
    y = _nn(oa_ref[...], wa_ref[...]) + _nn(ob_ref[...], wb_ref[...]) + _nn(oc_ref[...], wc_ref[...])
    h1 = _layer_norm(DEEPNORM_ALPHA * h_ref[...] + y, g_ref[...], b_ref[...])
    h1b = h1.astype(BF16)
    h1_ref[...] = h1
    h1b_ref[...] = h1b
    comb_ref[...], sel_ref[...] = _route(h1b, rw_ref[...], rb_ref[...])


def _out_proj(oa, ob, oc, h, wa, wb, wc, g, b, rw_t, rb):
    tm = 512
    rows = lambda w: pl.BlockSpec((tm, w), lambda i: (i, 0))
    full = lambda a: pl.BlockSpec(a.shape, lambda i: (0, 0))
    route_shape = jax.ShapeDtypeStruct((N_EXPERTS, TOKENS), F32)
    route_spec = pl.BlockSpec((N_EXPERTS, tm), lambda i: (0, i))
    return pl.pallas_call(
        _out_proj_kernel,
        out_shape=(jax.ShapeDtypeStruct((TOKENS, D_MODEL), F32),
                   jax.ShapeDtypeStruct((TOKENS, D_MODEL), BF16), route_shape, route_shape),
        grid=(TOKENS // tm,),
        in_specs=[rows(oa.shape[1]), rows(ob.shape[1]), rows(oc.shape[1]), rows(D_MODEL),
                  full(wa), full(wb), full(wc), full(g), full(b), full(rw_t), full(rb)],
        out_specs=(rows(D_MODEL), rows(D_MODEL), route_spec, route_spec),
        compiler_params=_params("parallel"),
        name="out_proj_ln",
    )(oa, ob, oc, h, wa, wb, wc, g, b, rw_t, rb)


def _route(hb, rw, rb):
    logits = _nt(rw, hb) + rb
    mx = jnp.max(logits, axis=0, keepdims=True)
    ex = jnp.exp(logits - mx)
    probs = ex / jnp.sum(ex, axis=0, keepdims=True)
    p = [probs[e:e + 1, :] for e in range(N_EXPERTS)]
    best, g_sel = None, None
    for g in range(N_GROUPS):
        a, b, c, d = p[4 * g:4 * g + 4]
        hi1, lo1, hi2, lo2 = jnp.maximum(a, b), jnp.minimum(a, b), jnp.maximum(c, d), jnp.minimum(c, d)
        top2 = jnp.maximum(hi1, hi2) + jnp.maximum(jnp.minimum(hi1, hi2), jnp.maximum(lo1, lo2))
        if g == 0:
            best, g_sel = top2, jnp.zeros_like(top2)
        else:
            better = top2 > best
            best = jnp.where(better, top2, best)
            g_sel = jnp.where(better, float(g), g_sel)
    chosen, picked = [], []
    for e in range(N_EXPERTS):
        g = e // EXPERTS_PER_GROUP
        rank = jnp.zeros_like(best)
        for o in range(4 * g, 4 * g + 4):
            if o < e:
                rank = rank + jnp.where(p[o] >= p[e], 1.0, 0.0)
            elif o > e:
                rank = rank + jnp.where(p[o] > p[e], 1.0, 0.0)
        chosen.append(jnp.where((g_sel == float(g)) & (rank < 2.0), 1.0, 0.0))
        picked.append(chosen[e] * p[e])
    total = picked[0]
    for e in range(1, N_EXPERTS):
        total = total + picked[e]
    return jnp.concatenate(picked, axis=0) / total, jnp.concatenate(chosen, axis=0)


def _routing_tables(comb_t, sel_t):
    sel = sel_t > 0.5
    cnt = jnp.sum(sel, axis=1, dtype=jnp.int32)
    cnt_pad = (cnt + (MOE_TILE - 1)) // MOE_TILE * MOE_TILE
    ends = jnp.cumsum(cnt_pad)
    rank = jnp.cumsum(sel.astype(jnp.int32), axis=1) - 1
    pos = (ends - cnt_pad)[:, None] + rank
    pos_lo = jnp.min(jnp.where(sel, pos, MOE_ROWS), axis=0)
    pos_hi = jnp.max(jnp.where(sel, pos, -1), axis=0)
    w_lo = jnp.sum(jnp.where(sel & (pos == pos_lo), comb_t, 0.0), axis=0)
    w_hi = jnp.sum(jnp.where(sel & (pos == pos_hi), comb_t, 0.0), axis=0)
    w = jnp.zeros((TOKENS, LANES), F32).at[:, 0].set(w_lo).at[:, 1].set(w_hi)
    n_tiles = ends[-1] // MOE_TILE
    tile_start = jnp.arange(MOE_TILES, dtype=jnp.int32) * MOE_TILE
    tile_start = jnp.minimum(tile_start, ends[-1] - MOE_TILE)
    tile_expert = jnp.sum((ends[None, :] <= tile_start[:, None]).astype(jnp.int32), axis=1)
    return jnp.stack([pos_lo, pos_hi]).astype(jnp.int32), w, tile_expert, n_tiles.reshape(1).astype(jnp.int32)


MOE_TILE = 256
MOE_TILES = 2 * TOKENS // MOE_TILE + N_EXPERTS
MOE_ROWS = MOE_TILES * MOE_TILE
SLAB = D_MODEL // LANES


def _to_slabs(ref, x, rows):
    for j in range(SLAB):
        ref[pl.ds(j, rows, stride=SLAB), :] = x[:, j * LANES:(j + 1) * LANES]


def _from_slabs(ref, rows, pitch=SLAB):
    return jnp.concatenate([ref[pl.ds(j, rows, stride=pitch), :] for j in range(SLAB)], axis=1)


GATHER_PITCH = SLAB + SUBLANES


def _slab_rows(row, n=SLAB):
    return pl.ds(pl.multiple_of(row * n, n), n)


XSLAB = SLAB // 2
U32 = jnp.uint32


def _to_packed_slabs(ref, x, rows):
    bits = lambda t: lax.bitcast_convert_type(t.astype(BF16).astype(F32), U32)
    for j in range(XSLAB):
        hi = bits(x[:, 2 * j * LANES:(2 * j + 1) * LANES])
        lo = bits(x[:, (2 * j + 1) * LANES:(2 * j + 2) * LANES])
        ref[pl.ds(j, rows, stride=XSLAB), :] = hi | (lo >> 16)


def _from_packed_slabs(ref, rows):
    parts = []
    for j in range(XSLAB):
        u = ref[pl.ds(j, rows, stride=XSLAB), :]
        parts.append(lax.bitcast_convert_type(u & jnp.uint32(0xFFFF0000), F32).astype(BF16))
        parts.append(lax.bitcast_convert_type(u << 16, F32).astype(BF16))
    return jnp.concatenate(parts, axis=1)


def _dispatch_kernel(pos_ref, h_ref, init_ref, xs_ref, slab_ref, sem):
    del init_ref
    tm = h_ref.shape[0]
    base = pl.program_id(0) * tm
    _to_packed_slabs(slab_ref, h_ref[...], tm)

    def copy(t, which):
        return pltpu.make_async_copy(slab_ref.at[_slab_rows(t, XSLAB), :],
                                     xs_ref.at[_slab_rows(pos_ref[which, base + t], XSLAB), :], sem)

    def start(t, _):
        copy(t, 0).start(priority=0)
        copy(t, 1).start(priority=1)
        return 0

    lax.fori_loop(0, tm, start, 0, unroll=8)
    whole = pltpu.make_async_copy(slab_ref, xs_ref.at[pl.ds(0, tm * XSLAB), :], sem)
    whole.wait()
    whole.wait()


def _dispatch(pos, h):
    tm = 256
    grid_spec = pltpu.PrefetchScalarGridSpec(
        num_scalar_prefetch=1,
        grid=(TOKENS // tm,),
        in_specs=[pl.BlockSpec((tm, D_MODEL), lambda i, pos: (i, 0)),
                  pl.BlockSpec(memory_space=pl.ANY)],
        out_specs=pl.BlockSpec(memory_space=pl.ANY),
        scratch_shapes=[pltpu.VMEM((tm * XSLAB, LANES), U32), pltpu.SemaphoreType.DMA],
    )
    return pl.pallas_call(
        _dispatch_kernel,
        out_shape=jax.ShapeDtypeStruct((MOE_ROWS * XSLAB, LANES), U32),
        grid_spec=grid_spec,
        input_output_aliases={2: 0},
        compiler_params=_params("arbitrary"),
        name="moe_dispatch",
    )(pos, h, jnp.zeros((MOE_ROWS * XSLAB, LANES), U32))


def _experts_kernel(te_ref, nt_ref, xs_ref, wg_ref, wu_ref, wd_ref, ys_ref, wgb_ref, wub_ref, wdb_ref):
    k = pl.program_id(0)
    e = te_ref[k]
    e_prev = te_ref[jnp.maximum(k - 1, 0)]

    @pl.when((k == 0) | (e != e_prev))
    def _():
        wgb_ref[...] = wg_ref[0, 0].astype(BF16)
        wub_ref[...] = wu_ref[0, 0].astype(BF16)
        wdb_ref[...] = wd_ref[0, 0].astype(BF16)

    @pl.when(k < nt_ref[0])
    def _():
        x = _from_packed_slabs(xs_ref, MOE_TILE)
        hid = jax.nn.silu(_nn(x, wgb_ref[...])) * _nn(x, wub_ref[...])
        _to_slabs(ys_ref, _nn(hid.astype(BF16), wdb_ref[...]), MOE_TILE)

    @pl.when(k >= nt_ref[0])
    def _():
        ys_ref[...] = jnp.zeros(ys_ref.shape, F32)


def _experts(tile_expert, n_tiles, xs, wg, wu, wd, layer):
    w_in_spec = pl.BlockSpec((1, 1, D_MODEL, EXPERT_HIDDEN), lambda k, te, nt: (layer, te[k], 0, 0))
    grid_spec = pltpu.PrefetchScalarGridSpec(
        num_scalar_prefetch=2,
        grid=(MOE_TILES,),
        in_specs=[pl.BlockSpec((MOE_TILE * XSLAB, LANES), lambda k, te, nt: (jnp.minimum(k, nt[0] - 1), 0)),
                  w_in_spec, w_in_spec,
                  pl.BlockSpec((1, 1, EXPERT_HIDDEN, D_MODEL), lambda k, te, nt: (layer, te[k], 0, 0))],
        out_specs=pl.BlockSpec((MOE_TILE * SLAB, LANES), lambda k, te, nt: (k, 0)),
        scratch_shapes=[pltpu.VMEM((D_MODEL, EXPERT_HIDDEN), BF16), pltpu.VMEM((D_MODEL, EXPERT_HIDDEN), BF16),
                        pltpu.VMEM((EXPERT_HIDDEN, D_MODEL), BF16)],
    )
    return pl.pallas_call(
        _experts_kernel,
        out_shape=jax.ShapeDtypeStruct((MOE_ROWS * SLAB, LANES), F32),
        grid_spec=grid_spec,
        compiler_params=_params("arbitrary"),
        name="moe_experts",
    )(tile_expert, n_tiles, xs, wg, wu, wd)


def _ple_ln_kernel(pos_ref, hb_ref, h_ref, ys_ref, w_ref, p_ref, gw_ref, gb_ref, pw_ref, g_ref, b_ref,
                   h2_ref, h2b_ref, lo_ref, hi_ref, sem):
    tm = h_ref.shape[0]
    i = pl.program_id(0)
    slot = i & 1
    bufs = (lo_ref, hi_ref)

    def fetch(tile, into):
        def start(t, _):
            for which in range(2):
                pltpu.make_async_copy(ys_ref.at[_slab_rows(pos_ref[which, tile * tm + t]), :],
                                      bufs[which].at[into, pl.ds(pl.multiple_of(t * GATHER_PITCH, SUBLANES), SLAB), :],
                                      sem.at[into]).start(priority=which)
            return 0
        lax.fori_loop(0, tm, start, 0, unroll=8)

    @pl.when(i == 0)
    def _():
        fetch(0, 0)

    @pl.when(i + 1 < pl.num_programs(0))
    def _():
        fetch(i + 1, 1 - slot)

    gate = jax.nn.sigmoid(_nn(hb_ref[...], gw_ref[...]) + gb_ref[...])
    ple = gate * _nn(p_ref[...].astype(BF16), pw_ref[...])
    for which in range(2):
        pltpu.make_async_copy(ys_ref.at[pl.ds(0, tm * SLAB), :], bufs[which].at[slot, pl.ds(0, tm * SLAB), :],
                              sem.at[slot]).wait()
    w = w_ref[...]
    ffn = (w[:, 0:1] * _from_slabs(lo_ref.at[slot], tm, GATHER_PITCH)
           + w[:, 1:2] * _from_slabs(hi_ref.at[slot], tm, GATHER_PITCH))
    h2 = _layer_norm(DEEPNORM_ALPHA * h_ref[...] + ffn + ple, g_ref[...], b_ref[...])
    h2_ref[...] = h2
    h2b_ref[...] = h2.astype(BF16)


def _ple_ln(pos, hb, h, ys, w, p, layer, gw, gb, pw, g, b):
    tm = 256
    p_spec = pl.BlockSpec((tm, PLE_DIM), lambda i, pos: (layer * (TOKENS // tm) + i, 0))
    rows = lambda width: pl.BlockSpec((tm, width), lambda i, pos: (i, 0))
    full = lambda a: pl.BlockSpec(a.shape, lambda i, pos: (0, 0))
    grid_spec = pltpu.PrefetchScalarGridSpec(
        num_scalar_prefetch=1,
        grid=(TOKENS // tm,),
        in_specs=[rows(D_MODEL), rows(D_MODEL), pl.BlockSpec(memory_space=pl.ANY), rows(LANES), p_spec,
                  full(gw), full(gb), full(pw), full(g), full(b)],
        out_specs=(rows(D_MODEL), rows(D_MODEL)),
        scratch_shapes=[pltpu.VMEM((2, tm * GATHER_PITCH, LANES), F32), pltpu.VMEM((2, tm * GATHER_PITCH, LANES), F32),
                        pltpu.SemaphoreType.DMA((2,))],
    )
    return pl.pallas_call(
        _ple_ln_kernel,
        out_shape=(jax.ShapeDtypeStruct((TOKENS, D_MODEL), F32),
                   jax.ShapeDtypeStruct((TOKENS, D_MODEL), BF16)),
        grid_spec=grid_spec,
        compiler_params=_params("arbitrary"),
        name="ple_ln",
    )(pos, hb, h, ys, w, p, gw, gb, pw, g, b)


def _rope_tables(positions):
    half = ROT_DIM // 2
    inv_freq = jnp.exp(jnp.arange(half, dtype=F32) * (-2.0 * math.log(ROPE_THETA) / ROT_DIM))
    ang = positions.astype(F32)[:, :, None] * inv_freq
    cos, sin = jnp.cos(ang), jnp.sin(ang)
    zeros = jnp.zeros_like(cos)
    rest = HEAD_DIM - ROT_DIM
    pad = lambda v: jnp.broadcast_to(jnp.asarray(v, F32), cos.shape[:2] + (rest,))
    c = jnp.concatenate([cos, cos, pad(1.0)], axis=-1)
    s1 = jnp.concatenate([-sin, zeros, pad(0.0)], axis=-1)
    s2 = jnp.concatenate([zeros, sin, pad(0.0)], axis=-1)
    tile = lambda t: jnp.concatenate([t, t], axis=-1).reshape(TOKENS, LANES)
    return tile(c), tile(s1), tile(s2)


def _split_w_in(w):
    mw, nq, nkv, dw = MOBA_HEADS * HEAD_DIM, NSA_HEADS * HEAD_DIM, NSA_KV_HEADS * HEAD_DIM, DIL_HEADS * HEAD_DIM
    widths = (mw, mw, mw, nq) + (nkv,) * 6 + (NSA_HEADS * 3, dw, dw, dw)
    offs = np.concatenate([[0], np.cumsum(widths)])
    qa, ka, va, qb, kbc, vbc, kbs, vbs, kbw, vbw, gb, qc, kc, vc = (
        w[:, int(offs[i]):int(offs[i + 1])] for i in range(len(widths)))

    def dup(t):
        t = t.reshape(D_MODEL, NSA_KV_HEADS, 1, HEAD_DIM)
        return jnp.broadcast_to(t, (D_MODEL, NSA_KV_HEADS, 2, HEAD_DIM)).reshape(D_MODEL, NSA_KV_HEADS * LANES)

    zpad = lambda n: jnp.zeros((D_MODEL, n * LANES), w.dtype)
    w_rot = jnp.concatenate([qa * Q_SCALE, ka, qb * Q_SCALE, dup(kbc), dup(kbs), dup(kbw), zpad(1)], axis=1)
    w_pl = jnp.concatenate([va, dup(vbc), dup(vbs), dup(vbw), zpad(3)], axis=1)
    n_gates = 3 * NSA_HEADS // NSA_KV_HEADS
    gpad = jnp.zeros((D_MODEL, NSA_KV_HEADS, LANES - n_gates), w.dtype)
    w_gl = jnp.concatenate([gb.reshape(D_MODEL, NSA_KV_HEADS, n_gates), gpad], axis=-1).reshape(D_MODEL, -1)
    w_dil_rot = jnp.concatenate([qc * Q_SCALE, kc], axis=1)
    w_f32 = jnp.concatenate([vc, w_gl], axis=1)
    return tuple(t.astype(BF16) for t in (w_rot, w_pl, w_dil_rot, w_f32))


def _overlap_table():
    starts = np.arange(N_CMP) * CMP_STRIDE
    slc = np.arange(N_SLC) * SLC_BLOCK
    ov = ((starts[:, None] < slc[None, :] + SLC_BLOCK) & (starts[:, None] + CMP_LEN > slc[None, :]))
    ovt = np.zeros((N_SLC, N_CMP_PAD), np.float32)
    ovt[:, :N_CMP] = ov.T
    return jnp.asarray(ovt, BF16)


def _cmp_chunks(z, base):
    nblk = z.shape[-1] // LANES
    t = z.reshape(BATCH, SEQ // CMP_STRIDE, CMP_STRIDE, nblk, LANES)[:, :, :, base:base + NSA_KV_HEADS, :HEAD_DIM]
    return t.transpose(0, 3, 1, 2, 4).reshape(BATCH, NSA_KV_HEADS, SEQ // CMP_STRIDE, CMP_STRIDE * HEAD_DIM)


def kernel(x, p, positions, ln_in_g, ln_in_b, w_in, w_out, nsa_ck1, nsa_ck2, nsa_pe_k, nsa_cv1, nsa_cv2, nsa_pe_v, ln1_g, ln1_b, router_w, router_b, w_gate, w_up, w_down, ple_proj, ple_gate_w, ple_gate_b, ln2_g, ln2_b):
    rope = _rope_tables(positions)
    ovt = _overlap_table()
    rw_t = router_w.T.astype(BF16)
    rb = router_b.reshape(N_EXPERTS, 1).astype(F32)
    chunk_w = CMP_STRIDE * HEAD_DIM
    vec = lambda v: v.reshape(1, -1)
    seq3 = lambda t: t.reshape(BATCH, SEQ, t.shape[-1])
    flat = lambda t: t.reshape(TOKENS, t.shape[-1])

    h, hb = _ln_in(x.reshape(TOKENS, D_MODEL), ln_in_g, ln_in_b)
    for i in range(DEPTH):
        w_rot, w_pl, w_dil_rot, w_f32 = _split_w_in(w_in[i])
        z_rot = seq3(_project(hb, w_rot, BF16, 768, rope=rope))
        z_pl = seq3(_project(hb, w_pl, BF16, 1024))
        zd_rot = seq3(_project(hb, w_dil_rot, F32, 768, rope=rope))
        gate_logits = _project(hb, w_f32, F32, w_f32.shape[1])
        zd_pl = seq3(gate_logits)

        o_a = _moba(z_rot, z_pl)

        dup2 = lambda w2: jnp.concatenate([w2, w2], axis=1).astype(BF16)
        k_cmp, v_cmp = _compress(
            _cmp_chunks(z_rot, ROT_NKC), _cmp_chunks(z_pl, PL_NVC),
            nsa_pe_k[i].reshape(2, chunk_w), nsa_pe_v[i].reshape(2, chunk_w),
            nsa_ck1[i].reshape(2, chunk_w, CMP_HIDDEN).astype(BF16), dup2(nsa_ck2[i]),
            nsa_cv1[i].reshape(2, chunk_w, CMP_HIDDEN).astype(BF16), dup2(nsa_cv2[i]))
        o_b = _nsa(z_rot, z_pl, k_cmp, v_cmp, gate_logits, ovt)

        o_c = _dilated(zd_rot, zd_pl)

        wo = w_out[i].astype(BF16)
        a_w, b_w = MOBA_HEADS * HEAD_DIM, NSA_HEADS * HEAD_DIM
        h, hb, comb_t, sel_t = _out_proj(flat(o_a), flat(o_b), flat(o_c), h, wo[:a_w], wo[a_w:a_w + b_w],
                                         wo[a_w + b_w:], vec(ln1_g[i]), vec(ln1_b[i]), rw_t, rb)

        pos, w_tok, tile_expert, n_tiles = _routing_tables(comb_t, sel_t)
        xs = _dispatch(pos, h)
        ys = _experts(tile_expert, n_tiles, xs, w_gate, w_up, w_down, i)
        h, hb = _ple_ln(pos, hb, h, ys, w_tok, p.reshape(DEPTH * TOKENS, PLE_DIM), i, ple_gate_w[i].astype(BF16),
                        vec(ple_gate_b[i]), ple_proj[i].astype(BF16), vec(ln2_g[i]), vec(ln2_b[i]))
    return h.reshape(BATCH, SEQ, D_MODEL)
```
